```python
import jax, jax.numpy as jnp
from jax import lax
import numpy as np

D_MODEL = 1024
BATCH = 4
SEQ = 4096
DEPTH = 1
DEC_BATCH = 128
DEC_SEQ = 4
PAST_LEN = 2048
PAGE_SIZE = 128

A_HEADS = 8
A_HEAD_DIM = 64
A_WIDTH = A_HEADS * A_HEAD_DIM
Q_BLOCK = 128
B_HEADS = 4
B_HEAD_DIM = 128
B_WIDTH = B_HEADS * B_HEAD_DIM
CONV_W = 4
MLSTM_CHUNK = 64
D_FF = 4 * D_MODEL
N_MOD = 6
RMS_EPS = 1e-6
NEG_INF = -1e30

kernel_name = 'fox_mlstm_hybrid_step'


def _split_points():
    sizes = (A_WIDTH, A_WIDTH, A_WIDTH, A_HEADS, B_WIDTH, B_WIDTH, B_WIDTH, B_HEADS, B_HEADS, B_WIDTH, D_MODEL, D_MODEL)
    return [int(s) for s in np.cumsum(sizes)[:-1]]


def _n_in():
    return 3 * A_WIDTH + A_HEADS + 4 * B_WIDTH + 2 * B_HEADS + 2 * D_MODEL


def rmsnorm(x, g):
    xf = x.astype(jnp.float32)
    r = lax.rsqrt(jnp.mean(xf * xf, axis=-1, keepdims=True) + RMS_EPS)
    return (xf * r).astype(x.dtype) * g


def fox_attention(q, fq, qpos, k, v, fk, kpos):
    b, l, h, d = q.shape
    qb = min(Q_BLOCK, l)
    nb = l // qb
    scale = d ** -0.5
    fk_t = jnp.transpose(fk, (0, 2, 1))

    def block(args):
        qi, fqi, pi = args
        s = jnp.einsum('bqhd,bkhd->bhqk', qi, k).astype(jnp.float32) * scale
        s = s + jnp.transpose(fqi, (0, 2, 1))[..., None] - fk_t[:, :, None, :]
        s = jnp.where(kpos[None, None, None, :] <= pi[None, None, :, None], s, NEG_INF)
        p = jax.nn.softmax(s, axis=-1).astype(v.dtype)
        return jnp.einsum('bhqk,bkhd->bqhd', p, v)

    qs = jnp.moveaxis(q.reshape(b, nb, qb, h, d), 1, 0)
    fs = jnp.moveaxis(fq.reshape(b, nb, qb, h), 1, 0)
    ps = qpos.reshape(nb, qb)
    out = lax.map(block, (qs, fs, ps))
    return jnp.moveaxis(out, 0, 1).reshape(b, l, h, d)


def mlstm_chunkwise(q, k, v, i_pre, logf, c0, n0, m0):
    f32 = jnp.float32
    b, l, h, dk = q.shape
    dv = v.shape[-1]
    ch = min(MLSTM_CHUNK, l)
    nc = l // ch

    def to_chunks(a):
        return jnp.moveaxis(a.astype(f32).reshape((b, nc, ch) + a.shape[2:]), 1, 0)

    causal = jnp.tril(jnp.ones((ch, ch), dtype=bool))

    def step(carry, xs):
        c, n, m = carry
        qc, kc, vc, ic, fc = xs
        bcum = jnp.cumsum(fc, axis=1)
        dlog = bcum[:, :, None, :] - bcum[:, None, :, :] + ic[:, None, :, :]
        dlog = jnp.where(causal[None, :, :, None], dlog, NEG_INF)
        inter = bcum + m[:, None, :]
        mt = jnp.maximum(inter, jnp.max(dlog, axis=2))
        a = jnp.exp(dlog - mt[:, :, None, :]) * jnp.einsum('bthd,bshd->btsh', qc, kc)
        si = jnp.exp(inter - mt)
        num = jnp.einsum('btsh,bshe->bthe', a, vc) + si[..., None] * jnp.einsum('bthd,bhde->bthe', qc, c)
        den = jnp.sum(a, axis=2) + si * jnp.einsum('bthd,bhd->bth', qc, n)
        hc = num / jnp.maximum(jnp.abs(den), jnp.exp(-mt))[..., None]
        bl = bcum[:, -1, :]
        wlog = bl[:, None, :] - bcum + ic
        m_new = jnp.maximum(bl + m, jnp.max(wlog, axis=1))
        ws = jnp.exp(wlog - m_new[:, None, :])
        decay = jnp.exp(bl + m - m_new)
        c_new = decay[..., None, None] * c + jnp.einsum('bsh,bshd,bshe->bhde', ws, kc, vc)
        n_new = decay[..., None] * n + jnp.einsum('bsh,bshd->bhd', ws, kc)
        return (c_new, n_new, m_new), hc

    xs = (to_chunks(q), to_chunks(k), to_chunks(v), to_chunks(i_pre), to_chunks(logf))
    (c_f, n_f, m_f), hs = lax.scan(step, (c0.astype(f32), n0.astype(f32), m0.astype(f32)), xs)
    return jnp.moveaxis(hs, 0, 1).reshape(b, l, h, dv), c_f, n_f, m_f


def causal_conv(x, prev, w, bias):
    l = x.shape[1]
    xp = jnp.concatenate([prev.astype(x.dtype), x], axis=1)
    y = bias + w[0] * xp[:, 0:l]
    for j in range(1, CONV_W):
        y = y + w[j] * xp[:, j:j + l]
    return jax.nn.silu(y), xp[:, l:]


def hybrid_layer(x, c, k_past, v_past, logf_past, c_state, n_state, m_state, conv_prev, lw):
    (w_ada, b_ada, g_pre_mix, g_post_mix, g_pre_mlp, g_post_mlp, w_in, b_fox_f, b_ml_i, b_ml_f,
     conv_w, conv_b, w_proj_a, w_proj_b, w_out, w_up, w_down) = lw
    f32 = jnp.float32
    bsz, l = x.shape[:2]
    mod = jax.nn.silu(c) @ w_ada + b_ada
    sh_m, sc_m, gt_m, sh_f, sc_f, gt_f = [t[:, None, :] for t in jnp.split(mod, N_MOD, axis=-1)]

    h = rmsnorm(x, g_pre_mix) * (1 + sc_m) + sh_m
    z = h @ w_in
    aq, ak, av, af, bq, bk, bv, bi, bf, bo, ga, gb = jnp.split(z, _split_points(), axis=-1)

    aq = aq.reshape(bsz, l, A_HEADS, A_HEAD_DIM)
    ak = ak.reshape(bsz, l, A_HEADS, A_HEAD_DIM)
    av = av.reshape(bsz, l, A_HEADS, A_HEAD_DIM)
    a_logf = jax.nn.log_sigmoid(af.astype(f32) + b_fox_f)
    if k_past is None:
        p = 0
        keys, vals, lf_all = ak, av, a_logf
    else:
        p = k_past.shape[1]
        keys = jnp.concatenate([k_past.astype(ak.dtype), ak], axis=1)
        vals = jnp.concatenate([v_past.astype(av.dtype), av], axis=1)
        lf_all = jnp.concatenate([logf_past.astype(f32), a_logf], axis=1)
    fk = jnp.cumsum(lf_all, axis=1)
    kpos = jnp.arange(p + l, dtype=jnp.int32)
    ya = fox_attention(aq, fk[:, p:], kpos[p:], keys, vals, fk, kpos)
    ya = ya.reshape(bsz, l, A_WIDTH) @ w_proj_a

    qk_c, conv_new = causal_conv(jnp.concatenate([bq, bk], axis=-1), conv_prev, conv_w, conv_b)
    mq, mk = jnp.split(qk_c, 2, axis=-1)
    mq = mq.reshape(bsz, l, B_HEADS, B_HEAD_DIM)
    mk = mk.reshape(bsz, l, B_HEADS, B_HEAD_DIM) * (B_HEAD_DIM ** -0.5)
    mv = bv.reshape(bsz, l, B_HEADS, B_HEAD_DIM)
    i_pre = bi.astype(f32) + b_ml_i
    b_logf = jax.nn.log_sigmoid(bf.astype(f32) + b_ml_f)
    hb, c_new, n_new, m_new = mlstm_chunkwise(mq, mk, mv, i_pre, b_logf, c_state, n_state, m_state)
    yb = (jax.nn.sigmoid(bo) * hb.reshape(bsz, l, B_WIDTH).astype(x.dtype)) @ w_proj_b

    y = (jax.nn.sigmoid(ga) * ya + jax.nn.sigmoid(gb) * yb) @ w_out
    x = x + gt_m * rmsnorm(y, g_post_mix)

    h2 = rmsnorm(x, g_pre_mlp) * (1 + sc_f) + sh_f
    u = jnp.square(jax.nn.relu(h2 @ w_up))
    x = x + gt_f * rmsnorm(u @ w_down, g_post_mlp)
    return x, ak, av, a_logf, c_new, n_new, m_new, conv_new


def setup_inputs(seed: int = 0) -> dict:
    key = jax.random.key(seed)
    ks = jax.random.split(key, 32)
    f32 = jnp.float32
    n_pages = PAST_LEN // PAGE_SIZE
    n_used = DEC_BATCH * n_pages
    n_phys = n_used + n_used // 4
    n_in = _n_in()

    def nrm(k, shape, s):
        return jax.random.normal(k, shape, f32) * s

    page_table = jax.random.permutation(ks[0], n_phys)[:n_used].reshape(DEC_BATCH, n_pages).astype(jnp.int32)
    return {
        'x_prompt': nrm(ks[1], (BATCH, SEQ, D_MODEL), 1.0),
        'x_sample': nrm(ks[2], (DEC_BATCH, DEC_SEQ, D_MODEL), 1.0),
        'cache_k': nrm(ks[3], (DEPTH, n_phys, PAGE_SIZE, A_HEADS, A_HEAD_DIM), 1.0),
        'cache_v': nrm(ks[4], (DEPTH, n_phys, PAGE_SIZE, A_HEADS, A_HEAD_DIM), 1.0),
        'cache_logf': jax.nn.log_sigmoid(2.0 + nrm(ks[5], (DEPTH, n_phys, PAGE_SIZE, A_HEADS), 1.0)),
        'page_table': page_table,
        'state_C': nrm(ks[6], (DEPTH, DEC_BATCH, B_HEADS, B_HEAD_DIM, B_HEAD_DIM), 0.3),
        'state_n': nrm(ks[7], (DEPTH, DEC_BATCH, B_HEADS, B_HEAD_DIM), 0.3),
        'state_m': nrm(ks[8], (DEPTH, DEC_BATCH, B_HEADS), 1.0),
        'state_conv': nrm(ks[9], (DEPTH, DEC_BATCH, CONV_W - 1, 2 * B_WIDTH), 1.0),
        'c_prompt': nrm(ks[10], (BATCH, D_MODEL), 1.0),
        'c_sample': nrm(ks[11], (DEC_BATCH, D_MODEL), 1.0),
        'w_ada': nrm(ks[12], (DEPTH, D_MODEL, N_MOD * D_MODEL), 0.5 * D_MODEL ** -0.5),
        'b_ada': nrm(ks[13], (DEPTH, N_MOD * D_MODEL), 0.02),
        'g_pre_mix': 1.0 + nrm(ks[14], (DEPTH, D_MODEL), 0.02),
        'g_post_mix': 1.0 + nrm(ks[15], (DEPTH, D_MODEL), 0.02),
        'g_pre_mlp': 1.0 + nrm(ks[16], (DEPTH, D_MODEL), 0.02),
        'g_post_mlp': 1.0 + nrm(ks[17], (DEPTH, D_MODEL), 0.02),
        'w_in': nrm(ks[18], (DEPTH, D_MODEL, n_in), D_MODEL ** -0.5),
        'b_fox_f': 2.0 + nrm(ks[19], (DEPTH, A_HEADS), 0.5),
        'b_ml_i': nrm(ks[20], (DEPTH, B_HEADS), 0.1),
        'b_ml_f': jnp.linspace(3.0, 6.0, B_HEADS, dtype=f32)[None, :] + nrm(ks[21], (DEPTH, B_HEADS), 0.1),
        'conv_w': nrm(ks[22], (DEPTH, CONV_W, 2 * B_WIDTH), CONV_W ** -0.5),
        'conv_b': nrm(ks[23], (DEPTH, 2 * B_WIDTH), 0.02),
        'w_proj_a': nrm(ks[24], (DEPTH, A_WIDTH, D_MODEL), A_WIDTH ** -0.5),
        'w_proj_b': nrm(ks[25], (DEPTH, B_WIDTH, D_MODEL), B_WIDTH ** -0.5),
        'w_out': nrm(ks[26], (DEPTH, D_MODEL, D_MODEL), D_MODEL ** -0.5),
        'w_up': nrm(ks[27], (DEPTH, D_MODEL, D_FF), D_MODEL ** -0.5),
        'w_down': nrm(ks[28], (DEPTH, D_FF, D_MODEL), D_FF ** -0.5),
    }


def reference(x_prompt, x_sample, cache_k, cache_v, cache_logf, page_table, state_C, state_n, state_m, state_conv,
              c_prompt, c_sample, w_ada, b_ada, g_pre_mix, g_post_mix, g_pre_mlp, g_post_mlp, w_in, b_fox_f,
              b_ml_i, b_ml_f, conv_w, conv_b, w_proj_a, w_proj_b, w_out, w_up, w_down):
    f32 = jnp.float32
    bsz = x_prompt.shape[0]
    dbsz = x_sample.shape[0]
    n_pages = page_table.shape[1]
    past = n_pages * PAGE_SIZE
    y_prompt, y_sample = x_prompt, x_sample
    kp, vp, lfp, cp, nps, mp, cvp = [], [], [], [], [], [], []
    ksm, vsm, lfs, cs, nss, ms, cvs = [], [], [], [], [], [], []
    for layer in range(DEPTH):
        lw = (w_ada[layer], b_ada[layer], g_pre_mix[layer], g_post_mix[layer], g_pre_mlp[layer], g_post_mlp[layer],
              w_in[layer], b_fox_f[layer], b_ml_i[layer], b_ml_f[layer], conv_w[layer], conv_b[layer],
              w_proj_a[layer], w_proj_b[layer], w_out[layer], w_up[layer], w_down[layer])
        c0 = jnp.zeros((bsz, B_HEADS, B_HEAD_DIM, B_HEAD_DIM), f32)
        n0 = jnp.zeros((bsz, B_HEADS, B_HEAD_DIM), f32)
        m0 = jnp.zeros((bsz, B_HEADS), f32)
        cv0 = jnp.zeros((bsz, CONV_W - 1, 2 * B_WIDTH), x_prompt.dtype)
        y_prompt, k1, v1, lf1, c1, n1, m1, cv1 = hybrid_layer(y_prompt, c_prompt, None, None, None, c0, n0, m0, cv0, lw)
        k_past = cache_k[layer][page_table].reshape(dbsz, past, A_HEADS, A_HEAD_DIM)
        v_past = cache_v[layer][page_table].reshape(dbsz, past, A_HEADS, A_HEAD_DIM)
        lf_past = cache_logf[layer][page_table].reshape(dbsz, past, A_HEADS)
        y_sample, k2, v2, lf2, c2, n2, m2, cv2 = hybrid_layer(
            y_sample, c_sample, k_past, v_past, lf_past, state_C[layer], state_n[layer], state_m[layer],
            state_conv[layer], lw)
        kp.append(k1); vp.append(v1); lfp.append(lf1); cp.append(c1); nps.append(n1); mp.append(m1); cvp.append(cv1)
        ksm.append(k2); vsm.append(v2); lfs.append(lf2); cs.append(c2); nss.append(n2); ms.append(m2); cvs.append(cv2)
    new_k_prompt = jnp.stack(kp)
    new_v_prompt = jnp.stack(vp)
    new_logf_prompt = jnp.stack(lfp)
    new_C_prompt = jnp.stack(cp)
    new_n_prompt = jnp.stack(nps)
    new_m_prompt = jnp.stack(mp)
    new_conv_prompt = jnp.stack(cvp)
    new_k_sample = jnp.stack(ksm)
    new_v_sample = jnp.stack(vsm)
    new_logf_sample = jnp.stack(lfs)
    new_C_sample = jnp.stack(cs)
    new_n_sample = jnp.stack(nss)
    new_m_sample = jnp.stack(ms)
    new_conv_sample = jnp.stack(cvs)
    return (y_prompt, y_sample, new_k_prompt, new_v_prompt, new_logf_prompt, new_C_prompt, new_n_prompt,
            new_m_prompt, new_conv_prompt, new_k_sample, new_v_sample, new_logf_sample, new_C_sample,
            new_n_sample, new_m_sample, new_conv_sample)
```

```python
import functools

import jax
import jax.numpy as jnp
from jax import lax
from jax.experimental import pallas as pl
from jax.experimental.pallas import tpu as pltpu

F32 = jnp.float32
BF16 = jnp.bfloat16

A_HEADS = 8
A_HEAD_DIM = 64
A_WIDTH = A_HEADS * A_HEAD_DIM
B_HEADS = 4
B_HEAD_DIM = 128
B_WIDTH = B_HEADS * B_HEAD_DIM
CONV_W = 4
N_MOD = 6
RMS_EPS = 1e-6
NEG_INF = -1e30
PAGE_SIZE = 128

LANES = 128
SUBLANES = 8
VMEM_LIMIT_BYTES = 56 * 1024 * 1024

_NT = (((1,), (1,)), ((), ()))
_TN = (((0,), (0,)), ((), ()))


def _params(*sem):
    return pltpu.CompilerParams(dimension_semantics=sem, vmem_limit_bytes=VMEM_LIMIT_BYTES)


def _resident(shape):
    nd = len(shape)
    return pl.BlockSpec(shape, lambda *_: (0,) * nd, pipeline_mode=pl.Buffered(1))


def _rms(x, g):
    r = lax.rsqrt(jnp.mean(x * x, axis=-1, keepdims=True) + RMS_EPS)
    return (x * r) * g


def _log_sigmoid(x):
    return jnp.minimum(x, 0.0) - jnp.log1p(jnp.exp(-jnp.abs(x)))


def _mod_kernel(c_ref, w_ref, b_ref, o_ref):
    c = c_ref[...]
    a = (c * jax.nn.sigmoid(c)).astype(BF16)
    o_ref[...] = jnp.dot(a, w_ref[...].astype(BF16), preferred_element_type=F32) + b_ref[...]


def _adaln(c, w, b, tn=1024):
    r, d = c.shape
    n = w.shape[1]
    return pl.pallas_call(
        _mod_kernel,
        out_shape=jax.ShapeDtypeStruct((r, n), F32),
        grid=(n // tn,),
        in_specs=[pl.BlockSpec((r, d), lambda j: (0, 0)),
                  pl.BlockSpec((d, tn), lambda j: (0, j)),
                  pl.BlockSpec((1, tn), lambda j: (0, j))],
        out_specs=pl.BlockSpec((r, tn), lambda j: (0, j)),
        compiler_params=_params("arbitrary"),
        name="adaln",
    )(c, w, b)


def _mod_spec(arr, tm, tiles_per_group):
    _, r, d = arr.shape
    if r == 1:
        return pl.BlockSpec((1, 1, d), lambda i: (i // tiles_per_group, 0, 0))
    assert r == tm
    return pl.BlockSpec((1, tm, d), lambda i: (i, 0, 0))


def _in_kernel(x_ref, sc_ref, sh_ref, g_ref, wa_ref, wqk_ref, wv_ref, wg_ref, ws_ref, bs_ref,
               qa_ref, k32_ref, v32_ref, ka_ref, va_ref, qkb_ref, vb_ref, og_ref, sm_ref):
    h = _rms(x_ref[...], g_ref[...]) * (1.0 + sc_ref[0]) + sh_ref[0]
    hb = h.astype(BF16)
    za = jnp.dot(hb, wa_ref[...], preferred_element_type=F32)
    qa_ref[...] = za[:, :A_WIDTH].astype(BF16)
    k = za[:, A_WIDTH:2 * A_WIDTH]
    v = za[:, 2 * A_WIDTH:]
    k32_ref[...] = k
    v32_ref[...] = v
    ka_ref[...] = k.astype(BF16)
    va_ref[...] = v.astype(BF16)
    qkb_ref[...] = jnp.dot(hb, wqk_ref[...], preferred_element_type=F32)
    vb_ref[...] = jnp.dot(hb, wv_ref[...], preferred_element_type=F32).astype(BF16)
    og_ref[...] = jax.nn.sigmoid(jnp.dot(hb, wg_ref[...], preferred_element_type=F32)).astype(BF16)
    zs = jnp.dot(hb, ws_ref[...], preferred_element_type=F32) + bs_ref[...]
    lane = lax.broadcasted_iota(jnp.int32, zs.shape, 1)
    is_input_gate = (lane >= A_HEADS) & (lane < A_HEADS + B_HEADS)
    sm_ref[...] = jnp.where(is_input_gate, zs, _log_sigmoid(zs))


def _in_proj(x, sc, sh, g, wa, wqk, wv, wg, ws, bs, tm, tiles_per_group):
    m, d = x.shape
    row = lambda n: pl.BlockSpec((tm, n), lambda i: (i, 0))
    outs = [(A_WIDTH, BF16), (A_WIDTH, F32), (A_WIDTH, F32), (A_WIDTH, BF16), (A_WIDTH, BF16),
            (2 * B_WIDTH, F32), (B_WIDTH, BF16), (wg.shape[1], BF16), (LANES, F32)]
    return pl.pallas_call(
        _in_kernel,
        out_shape=[jax.ShapeDtypeStruct((m, n), dt) for n, dt in outs],
        grid=(m // tm,),
        in_specs=[row(d), _mod_spec(sc, tm, tiles_per_group), _mod_spec(sh, tm, tiles_per_group),
                  _resident(g.shape), _resident(wa.shape), _resident(wqk.shape), _resident(wv.shape),
                  _resident(wg.shape), _resident(ws.shape), _resident(bs.shape)],
        out_specs=[row(n) for n, _ in outs],
        compiler_params=_params("arbitrary"),
        name="in_proj",
    )(x, sc, sh, g, wa, wqk, wv, wg, ws, bs)


def _mix_kernel(ya_ref, hb_ref, og_ref, x_ref, gt_ref, sc_ref, sh_ref, g1_ref, g2_ref,
                wpa_ref, wpb_ref, wo_ref, x1_ref, h2_ref):
    d = x_ref.shape[1]
    og = og_ref[...]
    ya = jnp.dot(ya_ref[...], wpa_ref[...], preferred_element_type=F32)
    yb_in = (og[:, :B_WIDTH].astype(F32) * hb_ref[...].astype(F32)).astype(BF16)
    yb = jnp.dot(yb_in, wpb_ref[...], preferred_element_type=F32)
    y = og[:, B_WIDTH:B_WIDTH + d].astype(F32) * ya + og[:, B_WIDTH + d:].astype(F32) * yb
    yo = jnp.dot(y.astype(BF16), wo_ref[...], preferred_element_type=F32)
    x1 = x_ref[...] + gt_ref[0] * _rms(yo, g1_ref[...])
    x1_ref[...] = x1
    h2_ref[...] = (_rms(x1, g2_ref[...]) * (1.0 + sc_ref[0]) + sh_ref[0]).astype(BF16)


def _mix_out(ya, hb, og, x, gt, sc, sh, g1, g2, wpa, wpb, wo, tm, tiles_per_group):
    m, d = x.shape
    row = lambda n: pl.BlockSpec((tm, n), lambda i: (i, 0))
    ms = lambda a: _mod_spec(a, tm, tiles_per_group)
    return pl.pallas_call(
        _mix_kernel,
        out_shape=[jax.ShapeDtypeStruct((m, d), F32), jax.ShapeDtypeStruct((m, d), BF16)],
        grid=(m // tm,),
        in_specs=[row(ya.shape[1]), row(hb.shape[1]), row(og.shape[1]), row(d), ms(gt), ms(sc), ms(sh),
                  _resident(g1.shape), _resident(g2.shape), _resident(wpa.shape), _resident(wpb.shape),
                  _resident(wo.shape)],
        out_specs=[row(d), row(d)],
        compiler_params=_params("arbitrary"),
        name="mix_out",
    )(ya, hb, og, x, gt, sc, sh, g1, g2, wpa, wpb, wo)


def _mlp_kernel(h2_ref, x1_ref, gt_ref, g_ref, wu_ref, wd_ref, y_ref, *, fc):
    h2 = h2_ref[...]
    acc = jnp.zeros(x1_ref.shape, F32)
    for c in range(wu_ref.shape[1] // fc):
        u = jnp.dot(h2, wu_ref[:, c * fc:(c + 1) * fc], preferred_element_type=F32)
        u = jnp.square(jnp.maximum(u, 0.0)).astype(BF16)
        acc = acc + jnp.dot(u, wd_ref[c * fc:(c + 1) * fc, :], preferred_element_type=F32)
    y_ref[...] = x1_ref[...] + gt_ref[0] * _rms(acc, g_ref[...])


def _mlp(h2, x1, gt, g, wu, wd, tm, tiles_per_group, fc=1024):
    m, d = x1.shape
    row = lambda n: pl.BlockSpec((tm, n), lambda i: (i, 0))
    return pl.pallas_call(
        functools.partial(_mlp_kernel, fc=fc),
        out_shape=jax.ShapeDtypeStruct((m, d), F32),
        grid=(m // tm,),
        in_specs=[row(d), row(d), _mod_spec(gt, tm, tiles_per_group), _resident(g.shape),
                  _resident(wu.shape), _resident(wd.shape)],
        out_specs=row(d),
        compiler_params=_params("arbitrary"),
        name="mlp",
    )(h2, x1, gt, g, wu, wd)


def _lane_scan(x, lane, shifts):
    for s in shifts:
        x = x + jnp.where(lane >= s, pltpu.roll(x, s, axis=1), 0.0)
    return x


def _cumsum_kernel(x_ref, o_ref):
    rows, length = x_ref.shape[1], x_ref.shape[2]
    lane = lax.broadcasted_iota(jnp.int32, (rows, LANES), 1)
    carry = jnp.zeros((rows, 1), F32)
    for c in range(length // LANES):
        x = _lane_scan(x_ref[0, :, c * LANES:(c + 1) * LANES], lane, (1, 2, 4, 8, 16, 32, 64)) + carry
        o_ref[0, :, c * LANES:(c + 1) * LANES] = x
        carry = x[:, LANES - 1:LANES]


def _cumsum_lanes(x):
    b, r, length = x.shape
    spec = pl.BlockSpec((1, r, length), lambda i: (i, 0, 0))
    return pl.pallas_call(
        _cumsum_kernel,
        out_shape=jax.ShapeDtypeStruct(x.shape, F32),
        grid=(b,),
        in_specs=[spec],
        out_specs=spec,
        compiler_params=_params("arbitrary"),
        name="logf_cumsum",
    )(x)


def _attn_kernel(q_ref, k_ref, v_ref, fc_ref, fr_ref, o_ref, *, blk):
    qi = pl.program_id(2)
    q = q_ref[0]
    lane = lax.broadcasted_iota(jnp.int32, (1, LANES), 1)
    first = lane < A_HEAD_DIM
    zero = jnp.zeros_like(q)
    qh = (jnp.where(first, q, zero), jnp.where(first, zero, q))
    fcol = fc_ref[0, 0]
    fcol = (fcol[:, 0:1], fcol[:, 1:2])
    row = lax.broadcasted_iota(jnp.int32, (blk, blk), 0)
    col = lax.broadcasted_iota(jnp.int32, (blk, blk), 1)
    causal = col <= row

    def step(j, carry, diagonal):
        start = pl.multiple_of(j * blk, blk)
        kj = k_ref[0, pl.ds(start, blk), :]
        vj = v_ref[0, pl.ds(start, blk), :]
        fr = fr_ref[0, 0, :, pl.ds(start, blk)]
        new = []
        for hh in range(2):
            m, l, acc = carry[hh]
            s = lax.dot_general(qh[hh], kj, _NT, preferred_element_type=F32)
            s = s + (fcol[hh] - fr[hh:hh + 1, :])
            if diagonal:
                s = jnp.where(causal, s, NEG_INF)
            m_new = jnp.maximum(m, jnp.max(s, axis=1, keepdims=True))
            p = jnp.exp(s - m_new)
            alpha = jnp.exp(m - m_new)
            l = alpha * l + jnp.sum(p, axis=1, keepdims=True)
            acc = alpha * acc + jnp.dot(p.astype(BF16), vj, preferred_element_type=F32)
            new.append((m_new, l, acc))
        return tuple(new)

    init = tuple((jnp.full((blk, 1), NEG_INF, F32), jnp.zeros((blk, 1), F32), jnp.zeros((blk, LANES), F32))
                 for _ in range(2))
    carry = lax.fori_loop(0, qi, lambda j, c: step(j, c, False), init)
    (_, l0, a0), (_, l1, a1) = step(qi, carry, True)
    o_ref[0] = jnp.where(first, a0 / l0, a1 / l1).astype(BF16)


def _attention(q, k, v, fcol, frow, blk=512):
    b, length, width = q.shape
    pairs = width // LANES
    blk = min(blk, length)
    return pl.pallas_call(
        functools.partial(_attn_kernel, blk=blk),
        out_shape=jax.ShapeDtypeStruct(q.shape, BF16),
        grid=(b, pairs, length // blk),
        in_specs=[pl.BlockSpec((1, blk, LANES), lambda i, p, t: (i, t, p)),
                  pl.BlockSpec((1, length, LANES), lambda i, p, t: (i, 0, p)),
                  pl.BlockSpec((1, length, LANES), lambda i, p, t: (i, 0, p)),
                  pl.BlockSpec((1, 1, blk, 2), lambda i, p, t: (i, p, t, 0)),
                  pl.BlockSpec((1, 1, 2, length), lambda i, p, t: (i, p, 0, 0))],
        out_specs=pl.BlockSpec((1, blk, LANES), lambda i, p, t: (i, t, p)),
        compiler_params=_params("arbitrary", "arbitrary", "arbitrary"),
        name="fox_attention",
    )(q, k, v, fcol, frow)


def _conv_kernel(x_ref, prev_ref, init_ref, w_ref, b_ref, o_ref, xp_ref, *, tc):
    halo = jnp.where(pl.program_id(1) == 0, init_ref[0], prev_ref[0])
    xp_ref[0:SUBLANES, :] = halo
    xp_ref[SUBLANES:SUBLANES + tc, :] = x_ref[0]
    y = b_ref[...]
    for j in range(CONV_W):
        off = SUBLANES - (CONV_W - 1) + j
        y = y + w_ref[j:j + 1, :] * xp_ref[off:off + tc, :]
    y = y * jax.nn.sigmoid(y)
    lane = lax.broadcasted_iota(jnp.int32, (1, y.shape[1]), 1)
    y = y * jnp.where(lane < B_WIDTH, 1.0, B_HEAD_DIM ** -0.5)
    o_ref[0] = y.astype(BF16)


def _conv(x, init, w, b, tc):
    bsz, length, c = x.shape
    tpb = tc // SUBLANES
    return pl.pallas_call(
        functools.partial(_conv_kernel, tc=tc),
        out_shape=jax.ShapeDtypeStruct(x.shape, BF16),
        grid=(bsz, length // tc),
        in_specs=[pl.BlockSpec((1, tc, c), lambda i, t: (i, t, 0)),
                  pl.BlockSpec((1, SUBLANES, c), lambda i, t: (i, jnp.maximum(t * tpb - 1, 0), 0)),
                  pl.BlockSpec((1, SUBLANES, c), lambda i, t: (i, 0, 0)),
                  pl.BlockSpec((CONV_W, c), lambda i, t: (0, 0)),
                  pl.BlockSpec((1, c), lambda i, t: (0, 0))],
        out_specs=pl.BlockSpec((1, tc, c), lambda i, t: (i, t, 0)),
        scratch_shapes=[pltpu.VMEM((tc + SUBLANES, c), F32)],
        compiler_params=_params("arbitrary", "arbitrary"),
        name="short_conv",
    )(x, x, init, w, b)


def _mlstm_head(q, k, v, ir, fr, ic, fc, c_st, n_st, m_st):
    t = q.shape[0]
    row = lax.broadcasted_iota(jnp.int32, (t, t), 0)
    col = lax.broadcasted_iota(jnp.int32, (t, t), 1)
    tril = col <= row
    bcum_c = jnp.sum(jnp.where(tril, fr, 0.0), axis=1, keepdims=True)
    bcum_r = jnp.sum(jnp.where(row <= col, fc, 0.0), axis=0, keepdims=True)
    dlog = jnp.where(tril, bcum_c - bcum_r + ir, NEG_INF)
    inter = bcum_c + m_st
    mt = jnp.maximum(inter, jnp.max(dlog, axis=1, keepdims=True))
    a = jnp.exp(dlog - mt) * lax.dot_general(q, k, _NT, preferred_element_type=F32)
    si = jnp.exp(inter - mt)
    num = (jnp.dot(a.astype(BF16), v, preferred_element_type=F32)
           + si * jnp.dot(q, c_st.astype(BF16), preferred_element_type=F32))
    den = jnp.sum(a, axis=1, keepdims=True) + si * jnp.sum(q.astype(F32) * n_st, axis=1, keepdims=True)
    h = num / jnp.maximum(jnp.abs(den), jnp.exp(-mt))
    bl = bcum_r[:, t - 1:t]
    wlog_r = bl - bcum_r + ir
    wlog_c = bl - bcum_c + ic
    m_new = jnp.maximum(bl + m_st, jnp.max(wlog_r, axis=1, keepdims=True))
    kw = jnp.exp(wlog_c - m_new) * k.astype(F32)
    decay = jnp.exp(bl + m_st - m_new)
    c_new = decay * c_st + lax.dot_general(kw.astype(BF16), v, _TN, preferred_element_type=F32)
    n_new = decay * n_st + jnp.sum(kw, axis=0, keepdims=True)
    return h, c_new, n_new, m_new


def _mlstm_kernel(q_ref, k_ref, v_ref, gc_ref, gr_ref, c0_ref, n0_ref, m0_ref,
                  h_ref, c_ref, n_ref, m_ref):
    @pl.when(pl.program_id(1) == 0)
    def _():
        c_ref[...] = c0_ref[...]
        n_ref[...] = n0_ref[...]
        m_ref[...] = m0_ref[...]

    gc = gc_ref[0]
    gr = gr_ref[0]
    for hd in range(B_HEADS):
        sl = slice(hd * B_HEAD_DIM, (hd + 1) * B_HEAD_DIM)
        h, c_new, n_new, m_new = _mlstm_head(
            q_ref[0, :, sl], k_ref[0, :, sl], v_ref[0, :, sl],
            gr[hd:hd + 1, :], gr[B_HEADS + hd:B_HEADS + hd + 1, :],
            gc[:, hd:hd + 1], gc[:, B_HEADS + hd:B_HEADS + hd + 1],
            c_ref[0, hd], n_ref[0, hd], m_ref[0, hd])
        h_ref[0, :, sl] = h.astype(BF16)
        c_ref[0, hd] = c_new
        n_ref[0, hd] = n_new
        m_ref[0, hd] = m_new


def _mlstm(qk, v, gcol, grow, c0, n0, m0, t):
    b, length, _ = v.shape
    st = lambda a: pl.BlockSpec((1,) + a.shape[1:], lambda i, j: (i, 0, 0, 0))
    return pl.pallas_call(
        _mlstm_kernel,
        out_shape=[jax.ShapeDtypeStruct(v.shape, BF16), jax.ShapeDtypeStruct(c0.shape, F32),
                   jax.ShapeDtypeStruct(n0.shape, F32), jax.ShapeDtypeStruct(m0.shape, F32)],
        grid=(b, length // t),
        in_specs=[pl.BlockSpec((1, t, B_WIDTH), lambda i, j: (i, j, 0)),
                  pl.BlockSpec((1, t, B_WIDTH), lambda i, j: (i, j, 1)),
                  pl.BlockSpec((1, t, B_WIDTH), lambda i, j: (i, j, 0)),
                  pl.BlockSpec((1, t, 2 * B_HEADS), lambda i, j: (i, j, 0)),
                  pl.BlockSpec((1, 2 * B_HEADS, t), lambda i, j: (i, 0, j)),
                  st(c0), st(n0), st(m0)],
        out_specs=[pl.BlockSpec((1, t, B_WIDTH), lambda i, j: (i, j, 0)), st(c0), st(n0), st(m0)],
        compiler_params=_params("arbitrary", "arbitrary"),
        name="mlstm",
    )(qk, qk, v, gcol, grow, c0, n0, m0)


def _flat_head_scan(x, lane, sub):
    for s in (8, 16, 32, 64):
        r = pltpu.roll(x, s, axis=1)
        wrapped = jnp.where(sub >= 1, pltpu.roll(r, 1, axis=0), 0.0)
        x = x + jnp.where(lane >= s, r, wrapped)
    for s in (1, 2, 4):
        x = x + jnp.where(sub >= s, pltpu.roll(x, s, axis=0), 0.0)
    return x


def _head_totals(x, lane):
    last = jnp.broadcast_to(x[SUBLANES - 1:SUBLANES, :], x.shape)
    y = jnp.where(lane >= LANES - A_HEADS, last, 0.0)
    for s in (8, 16, 32, 64):
        y = y + pltpu.roll(y, LANES - s, axis=1)
    return y


def _sattn_kernel(pt_ref, q_ref, kn_ref, vn_ref, lfn_ref, *rest, n_pages):
    del pt_ref
    k_refs = rest[:n_pages]
    v_refs = rest[n_pages:2 * n_pages]
    lf_refs = rest[2 * n_pages:3 * n_pages]
    o_ref = rest[3 * n_pages]
    q = q_ref[0]
    nq = q.shape[0]
    page_rows = PAGE_SIZE * A_HEADS
    lane = lax.broadcasted_iota(jnp.int32, (SUBLANES, LANES), 1)
    sub = lax.broadcasted_iota(jnp.int32, (SUBLANES, LANES), 0)
    qrow = lax.broadcasted_iota(jnp.int32, (nq, LANES), 0)
    qlane = lax.broadcasted_iota(jnp.int32, (nq, LANES), 1)
    same_head = (qrow % A_HEADS) == (qlane % A_HEADS)

    m = jnp.full((nq, 1), NEG_INF, F32)
    l = jnp.zeros((nq, 1), F32)
    acc = jnp.zeros((nq, A_HEAD_DIM), F32)
    offset = jnp.zeros((SUBLANES, LANES), F32)

    def fold(state, s, vals):
        m, l, acc = state
        m_new = jnp.maximum(m, jnp.max(s, axis=1, keepdims=True))
        p = jnp.exp(s - m_new)
        alpha = jnp.exp(m - m_new)
        l = alpha * l + jnp.sum(p, axis=1, keepdims=True)
        acc = alpha * acc + jnp.dot(p.astype(BF16), vals, preferred_element_type=F32)
        return m_new, l, acc

    state = (m, l, acc)
    for j in range(n_pages):
        lf = lf_refs[j][...]
        f = _flat_head_scan(lf, lane, sub)
        fpage = f + offset
        offset = offset + _head_totals(f, lane)
        kp = k_refs[j][...].reshape(page_rows, A_HEAD_DIM).astype(BF16)
        vp = v_refs[j][...].reshape(page_rows, A_HEAD_DIM).astype(BF16)
        s = lax.dot_general(q, kp, _NT, preferred_element_type=F32)
        pieces = []
        for c in range(SUBLANES):
            piece = s[:, c * LANES:(c + 1) * LANES] - fpage[c:c + 1, :]
            pieces.append(jnp.where(same_head, piece, NEG_INF))
        state = fold(state, jnp.concatenate(pieces, axis=1), vp)

    fnew = _lane_scan(jnp.broadcast_to(lfn_ref[0], (SUBLANES, LANES)), lane, (8, 16)) + offset
    s = lax.dot_general(q, kn_ref[0], _NT, preferred_element_type=F32) - fnew[0:1, :]
    allowed = same_head & ((qlane // A_HEADS) <= (qrow // A_HEADS))
    m, l, acc = fold(state, jnp.where(allowed, s, NEG_INF), vn_ref[0])
    o_ref[0] = acc / l


def _sample_attention(page_table, q, kn, vn, lfn, cache_k, cache_v, cache_lf):
    b, n_pages = page_table.shape
    nq = q.shape[1]
    page = lambda j: pl.BlockSpec((None, PAGE_SIZE, A_HEADS, A_HEAD_DIM), lambda i, pt: (pt[i, j], 0, 0, 0))
    lfpage = lambda j: pl.BlockSpec((None, SUBLANES, LANES), lambda i, pt: (pt[i, j], 0, 0))
    per_seq = lambda a: pl.BlockSpec((1,) + a.shape[1:], lambda i, pt: (i, 0, 0))
    grid_spec = pltpu.PrefetchScalarGridSpec(
        num_scalar_prefetch=1,
        grid=(b,),
        in_specs=([per_seq(q), per_seq(kn), per_seq(vn), per_seq(lfn)]
                  + [page(j) for j in range(n_pages)] + [page(j) for j in range(n_pages)]
                  + [lfpage(j) for j in range(n_pages)]),
        out_specs=pl.BlockSpec((1, nq, A_HEAD_DIM), lambda i, pt: (i, 0, 0)),
    )
    return pl.pallas_call(
        functools.partial(_sattn_kernel, n_pages=n_pages),
        out_shape=jax.ShapeDtypeStruct((b, nq, A_HEAD_DIM), F32),
        grid_spec=grid_spec,
        compiler_params=_params("arbitrary"),
        name="paged_fox_attention",
    )(page_table, q, kn, vn, lfn, *([cache_k] * n_pages), *([cache_v] * n_pages), *([cache_lf] * n_pages))


def _split_weights(w_in, b_fox_f, b_ml_i, b_ml_f):
    d = w_in.shape[0]
    o = 0
    parts = {}
    for name, n in (("aq", A_WIDTH), ("ak", A_WIDTH), ("av", A_WIDTH), ("af", A_HEADS), ("bq", B_WIDTH),
                    ("bk", B_WIDTH), ("bv", B_WIDTH), ("bi", B_HEADS), ("bf", B_HEADS), ("bo", B_WIDTH),
                    ("ga", d), ("gb", d)):
        parts[name] = w_in[:, o:o + n]
        o += n
    assert o == w_in.shape[1]
    n_small = A_HEADS + 2 * B_HEADS
    wa = jnp.concatenate([parts["aq"] * (A_HEAD_DIM ** -0.5), parts["ak"], parts["av"]], axis=1).astype(BF16)
    wqk = jnp.concatenate([parts["bq"], parts["bk"]], axis=1).astype(BF16)
    wv = parts["bv"].astype(BF16)
    wg = jnp.concatenate([parts["bo"], parts["ga"], parts["gb"]], axis=1).astype(BF16)
    ws = jnp.concatenate([parts["af"], parts["bi"], parts["bf"], jnp.zeros((d, LANES - n_small), F32)],
                         axis=1).astype(BF16)
    bs = jnp.concatenate([b_fox_f, b_ml_i, b_ml_f, jnp.zeros((LANES - n_small,), F32)]).reshape(1, LANES)
    return wa, wqk, wv, wg, ws, bs


def kernel(x_prompt, x_sample, cache_k, cache_v, cache_logf, page_table, state_C, state_n, state_m, state_conv,
           c_prompt, c_sample, w_ada, b_ada, g_pre_mix, g_post_mix, g_pre_mlp, g_post_mlp, w_in, b_fox_f,
           b_ml_i, b_ml_f, conv_w, conv_b, w_proj_a, w_proj_b, w_out, w_up, w_down):
    assert w_in.shape[0] == 1, "one trunk layer"
    bsz, seq, d = x_prompt.shape
    dbsz, dseq, _ = x_sample.shape
    n_phys = cache_k.shape[1]

    wa, wqk, wv, wg, ws, bs = _split_weights(w_in[0], b_fox_f[0], b_ml_i[0], b_ml_f[0])
    wpa, wpb, wo = w_proj_a[0].astype(BF16), w_proj_b[0].astype(BF16), w_out[0].astype(BF16)
    wu, wd = w_up[0].astype(BF16), w_down[0].astype(BF16)
    g1, g2, g3, g4 = g_pre_mix, g_post_mix, g_pre_mlp, g_post_mlp
    cw, cb = conv_w[0], conv_b

    n_c = bsz + dbsz
    c_all = jnp.concatenate([c_prompt, c_sample, jnp.zeros((-n_c % SUBLANES, d), F32)], axis=0)
    mod = _adaln(c_all, w_ada[0], b_ada)

    def token_path(x2, mods, tm, tiles_per_group):
        sh_m, sc_m, gt_m, sh_f, sc_f, gt_f = mods
        return _in_proj(x2, sc_m, sh_m, g1, wa, wqk, wv, wg, ws, bs, tm, tiles_per_group), (gt_m, sc_f, sh_f, gt_f)

    def finish(x2, ya, hb, og, rest, tm, tiles_per_group):
        gt_m, sc_f, sh_f, gt_f = rest
        x1, h2 = _mix_out(ya, hb, og, x2, gt_m, sc_f, sh_f, g2, g3, wpa, wpb, wo, tm, tiles_per_group)
        return _mlp(h2, x1, gt_f, g4, wu, wd, tm, tiles_per_group)

    tm = 256
    m_p = bsz * seq
    x2 = x_prompt.reshape(m_p, d)
    mods = tuple(t.reshape(bsz, 1, d) for t in jnp.split(mod[:bsz], N_MOD, axis=-1))
    (qa, k32, v32, ka, va, qkb, vb, og, sm), rest = token_path(x2, mods, tm, seq // tm)

    lf = sm[:, :A_HEADS].reshape(bsz, seq, A_HEADS)
    frow = _cumsum_lanes(jnp.transpose(lf, (0, 2, 1))).reshape(bsz, A_HEADS // 2, 2, seq)
    fcol = jnp.transpose(frow, (0, 1, 3, 2))
    r3 = lambda a: a.reshape(bsz, seq, a.shape[-1])
    ya = _attention(r3(qa), r3(ka), r3(va), fcol, frow).reshape(m_p, A_WIDTH)

    qkc = _conv(r3(qkb), jnp.zeros((bsz, SUBLANES, 2 * B_WIDTH), F32), cw, cb, tc=512)
    gcol = sm[:, A_HEADS:A_HEADS + 2 * B_HEADS].reshape(bsz, seq, 2 * B_HEADS)
    grow = jnp.transpose(gcol, (0, 2, 1))
    hb, c_p, n_p, m_pr = _mlstm(qkc, r3(vb), gcol, grow,
                                jnp.zeros((bsz, B_HEADS, B_HEAD_DIM, B_HEAD_DIM), F32),
                                jnp.zeros((bsz, B_HEADS, 1, B_HEAD_DIM), F32),
                                jnp.zeros((bsz, B_HEADS, 1, 1), F32), t=128)
    y_prompt = finish(x2, ya, hb.reshape(m_p, B_WIDTH), og, rest, tm, seq // tm).reshape(bsz, seq, d)

    new_k_prompt = k32.reshape(1, bsz, seq, A_HEADS, A_HEAD_DIM)
    new_v_prompt = v32.reshape(1, bsz, seq, A_HEADS, A_HEAD_DIM)
    new_logf_prompt = lf[None]
    new_conv_prompt = r3(qkb)[:, seq - (CONV_W - 1):, :][None]

    m_s = dbsz * dseq
    xs2 = x_sample.reshape(m_s, d)
    mods = tuple(t.reshape(1, m_s, d) for t in jnp.split(jnp.repeat(mod[bsz:n_c], dseq, axis=0), N_MOD, axis=-1))
    (qa, k32, v32, ka, va, qkb, vb, og, sm), rest = token_path(xs2, mods, m_s, 1)

    n_new = LANES // A_HEADS
    pad_tok = lambda a: jnp.pad(a.reshape(dbsz, dseq, -1), ((0, 0), (0, n_new - dseq), (0, 0)))
    lf_s = sm[:, :A_HEADS].reshape(dbsz, dseq, A_HEADS)
    ya = _sample_attention(
        page_table,
        qa.reshape(dbsz, dseq * A_HEADS, A_HEAD_DIM),
        pad_tok(ka).reshape(dbsz, n_new * A_HEADS, A_HEAD_DIM),
        pad_tok(va).reshape(dbsz, n_new * A_HEADS, A_HEAD_DIM),
        pad_tok(lf_s).reshape(dbsz, 1, LANES),
        cache_k[0], cache_v[0], cache_logf[0].reshape(n_phys, SUBLANES, LANES))
    ya = ya.astype(BF16).reshape(m_s, A_WIDTH)

    t_s = 128
    conv_in = jnp.pad(qkb.reshape(dbsz, dseq, 2 * B_WIDTH), ((0, 0), (0, SUBLANES - dseq), (0, 0)))
    conv_init = jnp.pad(state_conv[0], ((0, 0), (SUBLANES - (CONV_W - 1), 0), (0, 0)))
    qkc = _conv(conv_in, conv_init, cw, cb, tc=SUBLANES)
    tok_valid = (jnp.arange(t_s) < dseq)[None, :, None]
    pad_t = lambda a: jnp.pad(a, ((0, 0), (0, t_s - a.shape[1]), (0, 0)))
    qkc = jnp.where(tok_valid, pad_t(qkc), jnp.zeros((), BF16))
    vb_s = pad_t(vb.reshape(dbsz, dseq, B_WIDTH))
    gates = pad_t(sm[:, A_HEADS:A_HEADS + 2 * B_HEADS].reshape(dbsz, dseq, 2 * B_HEADS))
    neutral = jnp.concatenate([jnp.full((B_HEADS,), NEG_INF, F32), jnp.zeros((B_HEADS,), F32)])
    gcol = jnp.where(tok_valid, gates, neutral)
    grow = jnp.transpose(gcol, (0, 2, 1))
    hb, c_s, n_s, m_sm = _mlstm(qkc, vb_s, gcol, grow, state_C[0], state_n[0][:, :, None, :],
                                state_m[0][:, :, None, None], t=t_s)
    hb = hb[:, :dseq, :].reshape(m_s, B_WIDTH)
    y_sample = finish(xs2, ya, hb, og, rest, m_s, 1).reshape(dbsz, dseq, d)

    new_k_sample = k32.reshape(1, dbsz, dseq, A_HEADS, A_HEAD_DIM)
    new_v_sample = v32.reshape(1, dbsz, dseq, A_HEADS, A_HEAD_DIM)
    new_logf_sample = lf_s[None]
    qkb3 = qkb.reshape(dbsz, dseq, 2 * B_WIDTH)
    new_conv_sample = jnp.concatenate([state_conv[0], qkb3], axis=1)[:, dseq:, :][None]

    return (y_prompt, y_sample, new_k_prompt, new_v_prompt, new_logf_prompt,
            c_p[None], n_p[:, :, 0, :][None], m_pr[:, :, 0, 0][None], new_conv_prompt,
            new_k_sample, new_v_sample, new_logf_sample,
            c_s[None], n_s[:, :, 0, :][None], m_sm[:, :, 0, 0][None], new_conv_sample)
```

```python
import functools

import jax
import jax.numpy as jnp
from jax import lax
from jax.experimental import pallas as pl
from jax.experimental.pallas import tpu as pltpu

F32 = jnp.float32
BF16 = jnp.bfloat16

A_HEADS = 8
A_HEAD_DIM = 64
A_WIDTH = A_HEADS * A_HEAD_DIM
B_HEADS = 4
B_HEAD_DIM = 128
B_WIDTH = B_HEADS * B_HEAD_DIM
CONV_W = 4
N_MOD = 6
RMS_EPS = 1e-6
NEG_INF = -1e30
PAGE_SIZE = 128

LANES = 128
SUBLANES = 8
VMEM_LIMIT_BYTES = 56 * 1024 * 1024

_NT = (((1,), (1,)), ((), ()))
_TN = (((0,), (0,)), ((), ()))


def _params(*sem):
    return pltpu.CompilerParams(dimension_semantics=sem, vmem_limit_bytes=VMEM_LIMIT_BYTES)


def _resident(shape):
    nd = len(shape)
    return pl.BlockSpec(shape, lambda *_: (0,) * nd, pipeline_mode=pl.Buffered(1))


def _rms(x, g):
    r = lax.rsqrt(jnp.mean(x * x, axis=-1, keepdims=True) + RMS_EPS)
    return (x * r) * g


def _log_sigmoid(x):
    return jnp.minimum(x, 0.0) - jnp.log1p(jnp.exp(-jnp.abs(x)))


def _mod_kernel(c_ref, w_ref, b_ref, o_ref):
    c = c_ref[...]
    a = (c * jax.nn.sigmoid(c)).astype(BF16)
    o_ref[...] = jnp.dot(a, w_ref[...].astype(BF16), preferred_element_type=F32) + b_ref[...]


def _adaln(c, w, b, tn=1024):
    r, d = c.shape
    n = w.shape[1]
    return pl.pallas_call(
        _mod_kernel,
        out_shape=jax.ShapeDtypeStruct((r, n), F32),
        grid=(n // tn,),
        in_specs=[pl.BlockSpec((r, d), lambda j: (0, 0)),
                  pl.BlockSpec((d, tn), lambda j: (0, j)),
                  pl.BlockSpec((1, tn), lambda j: (0, j))],
        out_specs=pl.BlockSpec((r, tn), lambda j: (0, j)),
        compiler_params=_params("arbitrary"),
        name="adaln",
    )(c, w, b)


def _mod_spec(arr, tm, tiles_per_group):
    _, r, d = arr.shape
    if r == 1:
        return pl.BlockSpec((1, 1, d), lambda i: (i // tiles_per_group, 0, 0))
    assert r == tm
    return pl.BlockSpec((1, tm, d), lambda i: (i, 0, 0))


def _in_kernel(x_ref, sc_ref, sh_ref, g_ref, wa_ref, wqk_ref, wv_ref, wg_ref, ws_ref, bs_ref,
               qa_ref, k32_ref, v32_ref, ka_ref, va_ref, qkb_ref, vb_ref, og_ref, sm_ref):
    h = _rms(x_ref[...], g_ref[...]) * (1.0 + sc_ref[0]) + sh_ref[0]
    hb = h.astype(BF16)
    za = jnp.dot(hb, wa_ref[...], preferred_element_type=F32)
    qa_ref[...] = za[:, :A_WIDTH].astype(BF16)
    k = za[:, A_WIDTH:2 * A_WIDTH]
    v = za[:, 2 * A_WIDTH:]
    k32_ref[...] = k
    v32_ref[...] = v
    ka_ref[...] = k.astype(BF16)
    va_ref[...] = v.astype(BF16)
    qkb_ref[...] = jnp.dot(hb, wqk_ref[...], preferred_element_type=F32)
    vb_ref[...] = jnp.dot(hb, wv_ref[...], preferred_element_type=F32).astype(BF16)
    og_ref[...] = jax.nn.sigmoid(jnp.dot(hb, wg_ref[...], preferred_element_type=F32)).astype(BF16)
    zs = jnp.dot(hb, ws_ref[...], preferred_element_type=F32) + bs_ref[...]
    lane = lax.broadcasted_iota(jnp.int32, zs.shape, 1)
    is_input_gate = (lane >= A_HEADS) & (lane < A_HEADS + B_HEADS)
    sm_ref[...] = jnp.where(is_input_gate, zs, _log_sigmoid(zs))


def _in_proj(x, sc, sh, g, wa, wqk, wv, wg, ws, bs, tm, tiles_per_group):
    m, d = x.shape
    row = lambda n: pl.BlockSpec((tm, n), lambda i: (i, 0))
    outs = [(A_WIDTH, BF16), (A_WIDTH, F32), (A_WIDTH, F32), (A_WIDTH, BF16), (A_WIDTH, BF16),
            (2 * B_WIDTH, F32), (B_WIDTH, BF16), (wg.shape[1], BF16), (LANES, F32)]
    return pl.pallas_call(
        _in_kernel,
        out_shape=[jax.ShapeDtypeStruct((m, n), dt) for n, dt in outs],
        grid=(m // tm,),
        in_specs=[row(d), _mod_spec(sc, tm, tiles_per_group), _mod_spec(sh, tm, tiles_per_group),
                  _resident(g.shape), _resident(wa.shape), _resident(wqk.shape), _resident(wv.shape),
                  _resident(wg.shape), _resident(ws.shape), _resident(bs.shape)],
        out_specs=[row(n) for n, _ in outs],
        compiler_params=_params("arbitrary"),
        name="in_proj",
    )(x, sc, sh, g, wa, wqk, wv, wg, ws, bs)


def _mix_kernel(ya_ref, hb_ref, og_ref, x_ref, gt_ref, sc_ref, sh_ref, g1_ref, g2_ref,
                wpa_ref, wpb_ref, wo_ref, x1_ref, h2_ref):
    d = x_ref.shape[1]
    og = og_ref[...]
    ya = jnp.dot(ya_ref[...], wpa_ref[...], preferred_element_type=F32)
    yb_in = (og[:, :B_WIDTH].astype(F32) * hb_ref[...].astype(F32)).astype(BF16)
    yb = jnp.dot(yb_in, wpb_ref[...], preferred_element_type=F32)
    y = og[:, B_WIDTH:B_WIDTH + d].astype(F32) * ya + og[:, B_WIDTH + d:].astype(F32) * yb
    yo = jnp.dot(y.astype(BF16), wo_ref[...], preferred_element_type=F32)
    x1 = x_ref[...] + gt_ref[0] * _rms(yo, g1_ref[...])
    x1_ref[...] = x1
    h2_ref[...] = (_rms(x1, g2_ref[...]) * (1.0 + sc_ref[0]) + sh_ref[0]).astype(BF16)


def _mix_out(ya, hb, og, x, gt, sc, sh, g1, g2, wpa, wpb, wo, tm, tiles_per_group):
    m, d = x.shape
    row = lambda n: pl.BlockSpec((tm, n), lambda i: (i, 0))
    ms = lambda a: _mod_spec(a, tm, tiles_per_group)
    return pl.pallas_call(
        _mix_kernel,
        out_shape=[jax.ShapeDtypeStruct((m, d), F32), jax.ShapeDtypeStruct((m, d), BF16)],
        grid=(m // tm,),
        in_specs=[row(ya.shape[1]), row(hb.shape[1]), row(og.shape[1]), row(d), ms(gt), ms(sc), ms(sh),
                  _resident(g1.shape), _resident(g2.shape), _resident(wpa.shape), _resident(wpb.shape),
                  _resident(wo.shape)],
        out_specs=[row(d), row(d)],
        compiler_params=_params("arbitrary"),
        name="mix_out",
    )(ya, hb, og, x, gt, sc, sh, g1, g2, wpa, wpb, wo)


def _mlp_kernel(h2_ref, x1_ref, gt_ref, g_ref, wu_ref, wd_ref, y_ref, *, fc):
    h2 = h2_ref[...]
    acc = jnp.zeros(x1_ref.shape, F32)
    for c in range(wu_ref.shape[1] // fc):
        u = jnp.dot(h2, wu_ref[:, c * fc:(c + 1) * fc], preferred_element_type=F32)
        u = jnp.square(jnp.maximum(u, 0.0)).astype(BF16)
        acc = acc + jnp.dot(u, wd_ref[c * fc:(c + 1) * fc, :], preferred_element_type=F32)
    y_ref[...] = x1_ref[...] + gt_ref[0] * _rms(acc, g_ref[...])


def _mlp(h2, x1, gt, g, wu, wd, tm, tiles_per_group, fc=1024):
    m, d = x1.shape
    row = lambda n: pl.BlockSpec((tm, n), lambda i: (i, 0))
    return pl.pallas_call(
        functools.partial(_mlp_kernel, fc=fc),
        out_shape=jax.ShapeDtypeStruct((m, d), F32),
        grid=(m // tm,),
        in_specs=[row(d), row(d), _mod_spec(gt, tm, tiles_per_group), _resident(g.shape),
                  _resident(wu.shape), _resident(wd.shape)],
        out_specs=row(d),
        compiler_params=_params("arbitrary"),
        name="mlp",
    )(h2, x1, gt, g, wu, wd)


def _lane_scan(x, lane, shifts):
    for s in shifts:
        x = x + jnp.where(lane >= s, pltpu.roll(x, s, axis=1), 0.0)
    return x


def _cumsum_kernel(x_ref, o_ref):
    rows, length = x_ref.shape[1], x_ref.shape[2]
    lane = lax.broadcasted_iota(jnp.int32, (rows, LANES), 1)
    carry = jnp.zeros((rows, 1), F32)
    for c in range(length // LANES):
        x = _lane_scan(x_ref[0, :, c * LANES:(c + 1) * LANES], lane, (1, 2, 4, 8, 16, 32, 64)) + carry
        o_ref[0, :, c * LANES:(c + 1) * LANES] = x
        carry = x[:, LANES - 1:LANES]


def _cumsum_lanes(x):
    b, r, length = x.shape
    spec = pl.BlockSpec((1, r, length), lambda i: (i, 0, 0))
    return pl.pallas_call(
        _cumsum_kernel,
        out_shape=jax.ShapeDtypeStruct(x.shape, F32),
        grid=(b,),
        in_specs=[spec],
        out_specs=spec,
        compiler_params=_params("arbitrary"),
        name="logf_cumsum",
    )(x)


def _attn_kernel(q_ref, k_ref, v_ref, fc_ref, fr_ref, o_ref, *, blk):
    qi = pl.program_id(2)
    q = q_ref[0]
    lane = lax.broadcasted_iota(jnp.int32, (1, LANES), 1)
    first = lane < A_HEAD_DIM
    zero = jnp.zeros_like(q)
    qh = (jnp.where(first, q, zero), jnp.where(first, zero, q))
    fcol = fc_ref[0, 0]
    fcol = (fcol[:, 0:1], fcol[:, 1:2])
    row = lax.broadcasted_iota(jnp.int32, (blk, blk), 0)
    col = lax.broadcasted_iota(jnp.int32, (blk, blk), 1)
    causal = col <= row

    def step(j, carry, diagonal):
        start = pl.multiple_of(j * blk, blk)
        kj = k_ref[0, pl.ds(start, blk), :]
        vj = v_ref[0, pl.ds(start, blk), :]
        fr = fr_ref[0, 0, :, pl.ds(start, blk)]
        new = []
        for hh in range(2):
            m, l, acc = carry[hh]
            s = lax.dot_general(qh[hh], kj, _NT, preferred_element_type=F32)
            s = s + (fcol[hh] - fr[hh:hh + 1, :])
            if diagonal:
                s = jnp.where(causal, s, NEG_INF)
            m_new = jnp.maximum(m, jnp.max(s, axis=1, keepdims=True))
            p = jnp.exp(s - m_new)
            alpha = jnp.exp(m - m_new)
            l = alpha * l + jnp.sum(p, axis=1, keepdims=True)
            acc = alpha * acc + jnp.dot(p.astype(BF16), vj, preferred_element_type=F32)
            new.append((m_new, l, acc))
        return tuple(new)

    init = tuple((jnp.full((blk, 1), NEG_INF, F32), jnp.zeros((blk, 1), F32), jnp.zeros((blk, LANES), F32))
                 for _ in range(2))
    carry = lax.fori_loop(0, qi, lambda j, c: step(j, c, False), init)
    (_, l0, a0), (_, l1, a1) = step(qi, carry, True)
    o_ref[0] = jnp.where(first, a0 / l0, a1 / l1).astype(BF16)


def _attention(q, k, v, fcol, frow, blk=512):
    b, length, width = q.shape
    pairs = width // LANES
    blk = min(blk, length)
    return pl.pallas_call(
        functools.partial(_attn_kernel, blk=blk),
        out_shape=jax.ShapeDtypeStruct(q.shape, BF16),
        grid=(b, pairs, length // blk),
        in_specs=[pl.BlockSpec((1, blk, LANES), lambda i, p, t: (i, t, p)),
                  pl.BlockSpec((1, length, LANES), lambda i, p, t: (i, 0, p)),
                  pl.BlockSpec((1, length, LANES), lambda i, p, t: (i, 0, p)),
                  pl.BlockSpec((1, 1, blk, 2), lambda i, p, t: (i, p, t, 0)),
                  pl.BlockSpec((1, 1, 2, length), lambda i, p, t: (i, p, 0, 0))],
        out_specs=pl.BlockSpec((1, blk, LANES), lambda i, p, t: (i, t, p)),
        compiler_params=_params("arbitrary", "arbitrary", "arbitrary"),
        name="fox_attention",
    )(q, k, v, fcol, frow)


def _conv_kernel(x_ref, prev_ref, init_ref, w_ref, b_ref, o_ref, xp_ref, *, tc):
    halo = jnp.where(pl.program_id(1) == 0, init_ref[0], prev_ref[0])
    xp_ref[0:SUBLANES, :] = halo
    xp_ref[SUBLANES:SUBLANES + tc, :] = x_ref[0]
    y = b_ref[...]
    for j in range(CONV_W):
        off = SUBLANES - (CONV_W - 1) + j
        y = y + w_ref[j:j + 1, :] * xp_ref[off:off + tc, :]
    y = y * jax.nn.sigmoid(y)
    lane = lax.broadcasted_iota(jnp.int32, (1, y.shape[1]), 1)
    y = y * jnp.where(lane < B_WIDTH, 1.0, B_HEAD_DIM ** -0.5)
    o_ref[0] = y.astype(BF16)


def _conv(x, init, w, b, tc):
    bsz, length, c = x.shape
    tpb = tc // SUBLANES
    return pl.pallas_call(
        functools.partial(_conv_kernel, tc=tc),
        out_shape=jax.ShapeDtypeStruct(x.shape, BF16),
        grid=(bsz, length // tc),
        in_specs=[pl.BlockSpec((1, tc, c), lambda i, t: (i, t, 0)),
                  pl.BlockSpec((1, SUBLANES, c), lambda i, t: (i, jnp.maximum(t * tpb - 1, 0), 0)),
                  pl.BlockSpec((1, SUBLANES, c), lambda i, t: (i, 0, 0)),
                  pl.BlockSpec((CONV_W, c), lambda i, t: (0, 0)),
                  pl.BlockSpec((1, c), lambda i, t: (0, 0))],
        out_specs=pl.BlockSpec((1, tc, c), lambda i, t: (i, t, 0)),
        scratch_shapes=[pltpu.VMEM((tc + SUBLANES, c), F32)],
        compiler_params=_params("arbitrary", "arbitrary"),
        name="short_conv",
    )(x, x, init, w, b)


def _mlstm_head(q, k, v, ir, fr, ic, fc, c_st, n_st, m_st):
    t = q.shape[0]
    row = lax.broadcasted_iota(jnp.int32, (t, t), 0)
    col = lax.broadcasted_iota(jnp.int32, (t, t), 1)
    tril = col <= row
    bcum_c = jnp.sum(jnp.where(tril, fr, 0.0), axis=1, keepdims=True)
    bcum_r = jnp.sum(jnp.where(row <= col, fc, 0.0), axis=0, keepdims=True)
    dlog = jnp.where(tril, bcum_c - bcum_r + ir, NEG_INF)
    inter = bcum_c + m_st
    mt = jnp.maximum(inter, jnp.max(dlog, axis=1, keepdims=True))
    a = jnp.exp(dlog - mt) * lax.dot_general(q, k, _NT, preferred_element_type=F32)
    si = jnp.exp(inter - mt)
    num = (jnp.dot(a.astype(BF16), v, preferred_element_type=F32)
           + si * jnp.dot(q, c_st.astype(BF16), preferred_element_type=F32))
    den = jnp.sum(a, axis=1, keepdims=True) + si * jnp.sum(q.astype(F32) * n_st, axis=1, keepdims=True)
    h = num / jnp.maximum(jnp.abs(den), jnp.exp(-mt))
    bl = bcum_r[:, t - 1:t]
    wlog_r = bl - bcum_r + ir
    wlog_c = bl - bcum_c + ic
    m_new = jnp.maximum(bl + m_st, jnp.max(wlog_r, axis=1, keepdims=True))
    kw = jnp.exp(wlog_c - m_new) * k.astype(F32)
    decay = jnp.exp(bl + m_st - m_new)
    c_new = decay * c_st + lax.dot_general(kw.astype(BF16), v, _TN, preferred_element_type=F32)
    n_new = decay * n_st + jnp.sum(kw, axis=0, keepdims=True)
    return h, c_new, n_new, m_new


def _mlstm_kernel(q_ref, k_ref, v_ref, gc_ref, gr_ref, c0_ref, n0_ref, m0_ref,
                  h_ref, c_ref, n_ref, m_ref):
    @pl.when(pl.program_id(1) == 0)
    def _():
        c_ref[...] = c0_ref[...]
        n_ref[...] = n0_ref[...]
        m_ref[...] = m0_ref[...]

    gc = gc_ref[0]
    gr = gr_ref[0]
    for hd in range(B_HEADS):
        sl = slice(hd * B_HEAD_DIM, (hd + 1) * B_HEAD_DIM)
        h, c_new, n_new, m_new = _mlstm_head(
            q_ref[0, :, sl], k_ref[0, :, sl], v_ref[0, :, sl],
            gr[hd:hd + 1, :], gr[B_HEADS + hd:B_HEADS + hd + 1, :],
            gc[:, hd:hd + 1], gc[:, B_HEADS + hd:B_HEADS + hd + 1],
            c_ref[0, hd], n_ref[0, hd], m_ref[0, hd])
        h_ref[0, :, sl] = h.astype(BF16)
        c_ref[0, hd] = c_new
        n_ref[0, hd] = n_new
        m_ref[0, hd] = m_new


def _mlstm(qk, v, gcol, grow, c0, n0, m0, t):
    b, length, _ = v.shape
    st = lambda a: pl.BlockSpec((1,) + a.shape[1:], lambda i, j: (i, 0, 0, 0))
    return pl.pallas_call(
        _mlstm_kernel,
        out_shape=[jax.ShapeDtypeStruct(v.shape, BF16), jax.ShapeDtypeStruct(c0.shape, F32),
                   jax.ShapeDtypeStruct(n0.shape, F32), jax.ShapeDtypeStruct(m0.shape, F32)],
        grid=(b, length // t),
        in_specs=[pl.BlockSpec((1, t, B_WIDTH), lambda i, j: (i, j, 0)),
                  pl.BlockSpec((1, t, B_WIDTH), lambda i, j: (i, j, 1)),
                  pl.BlockSpec((1, t, B_WIDTH), lambda i, j: (i, j, 0)),
                  pl.BlockSpec((1, t, 2 * B_HEADS), lambda i, j: (i, j, 0)),
                  pl.BlockSpec((1, 2 * B_HEADS, t), lambda i, j: (i, 0, j)),
                  st(c0), st(n0), st(m0)],
        out_specs=[pl.BlockSpec((1, t, B_WIDTH), lambda i, j: (i, j, 0)), st(c0), st(n0), st(m0)],
        compiler_params=_params("arbitrary", "arbitrary"),
        name="mlstm",
    )(qk, qk, v, gcol, grow, c0, n0, m0)


def _sattn_kernel(pt_ref, q_ref, kn_ref, vn_ref, lfn_ref, *rest, n_pages):
    del pt_ref
    k_refs = rest[:n_pages]
    v_refs = rest[n_pages:2 * n_pages]
    lf_refs = rest[2 * n_pages:3 * n_pages]
    o_ref = rest[3 * n_pages]
    q = q_ref[0]
    nrow, width = q.shape
    nq = nrow // A_HEADS
    lane = lax.broadcasted_iota(jnp.int32, (A_HEADS, LANES), 1)
    per_row = lambda f: jnp.concatenate([f] * nq, axis=0)

    offset = jnp.zeros((A_HEADS, 1), F32)
    scores = []
    for j in range(n_pages):
        f = _lane_scan(lf_refs[j][...], lane, (1, 2, 4, 8, 16, 32, 64)) + offset
        offset = f[:, LANES - 1:LANES]
        kt = k_refs[j][...].reshape(width, PAGE_SIZE).astype(BF16)
        scores.append(jnp.dot(q, kt, preferred_element_type=F32) - per_row(f))
    fnew = _lane_scan(lfn_ref[0], lane, (1, 2, 4, 8, 16, 32, 64)) + offset
    s_new = jnp.dot(q, kn_ref[0], preferred_element_type=F32) - per_row(fnew)
    q_tok = lax.broadcasted_iota(jnp.int32, (nrow, LANES), 0) // A_HEADS
    k_tok = lax.broadcasted_iota(jnp.int32, (nrow, LANES), 1)
    scores.append(jnp.where(k_tok <= q_tok, s_new, NEG_INF))

    m = jnp.max(functools.reduce(jnp.maximum, scores), axis=1, keepdims=True)
    probs = [jnp.exp(s - m) for s in scores]
    l = jnp.sum(functools.reduce(jnp.add, probs), axis=1, keepdims=True)
    acc = lax.dot_general(probs[n_pages].astype(BF16), vn_ref[0], _NT, preferred_element_type=F32)
    for j in range(n_pages):
        vt = v_refs[j][...].reshape(width, PAGE_SIZE).astype(BF16)
        acc = acc + lax.dot_general(probs[j].astype(BF16), vt, _NT, preferred_element_type=F32)
    row_head = lax.broadcasted_iota(jnp.int32, acc.shape, 0) % A_HEADS
    col_head = lax.broadcasted_iota(jnp.int32, acc.shape, 1) // A_HEAD_DIM
    out = jnp.where(row_head == col_head, acc / l, 0.0)
    o_ref[0] = jnp.sum(out.reshape(nq, A_HEADS, width), axis=1)


def _sample_attention(page_table, q, kn, vn, lfn, cache_kt, cache_vt, cache_lf):
    b, n_pages = page_table.shape
    nq = q.shape[1] // A_HEADS
    page = lambda j: pl.BlockSpec((None, A_HEADS, A_HEAD_DIM, PAGE_SIZE), lambda i, pt: (pt[i, j], 0, 0, 0))
    lfpage = lambda j: pl.BlockSpec((None, A_HEADS, PAGE_SIZE), lambda i, pt: (pt[i, j], 0, 0))
    per_seq = lambda a: pl.BlockSpec((1,) + a.shape[1:], lambda i, pt: (i, 0, 0))
    grid_spec = pltpu.PrefetchScalarGridSpec(
        num_scalar_prefetch=1,
        grid=(b,),
        in_specs=([per_seq(q), per_seq(kn), per_seq(vn), per_seq(lfn)]
                  + [page(j) for j in range(n_pages)] + [page(j) for j in range(n_pages)]
                  + [lfpage(j) for j in range(n_pages)]),
        out_specs=pl.BlockSpec((1, nq, A_WIDTH), lambda i, pt: (i, 0, 0)),
    )
    return pl.pallas_call(
        functools.partial(_sattn_kernel, n_pages=n_pages),
        out_shape=jax.ShapeDtypeStruct((b, nq, A_WIDTH), F32),
        grid_spec=grid_spec,
        compiler_params=_params("arbitrary"),
        name="paged_fox_attention",
    )(page_table, q, kn, vn, lfn, *([cache_kt] * n_pages), *([cache_vt] * n_pages), *([cache_lf] * n_pages))


def _split_weights(w_in, b_fox_f, b_ml_i, b_ml_f):
    d = w_in.shape[0]
    o = 0
    parts = {}
    for name, n in (("aq", A_WIDTH), ("ak", A_WIDTH), ("av", A_WIDTH), ("af", A_HEADS), ("bq", B_WIDTH),
                    ("bk", B_WIDTH), ("bv", B_WIDTH), ("bi", B_HEADS), ("bf", B_HEADS), ("bo", B_WIDTH),
                    ("ga", d), ("gb", d)):
        parts[name] = w_in[:, o:o + n]
        o += n
    assert o == w_in.shape[1]
    n_small = A_HEADS + 2 * B_HEADS
    wa = jnp.concatenate([parts["aq"] * (A_HEAD_DIM ** -0.5), parts["ak"], parts["av"]], axis=1).astype(BF16)
    wqk = jnp.concatenate([parts["bq"], parts["bk"]], axis=1).astype(BF16)
    wv = parts["bv"].astype(BF16)
    wg = jnp.concatenate([parts["bo"], parts["ga"], parts["gb"]], axis=1).astype(BF16)
    ws = jnp.concatenate([parts["af"], parts["bi"], parts["bf"], jnp.zeros((d, LANES - n_small), F32)],
                         axis=1).astype(BF16)
    bs = jnp.concatenate([b_fox_f, b_ml_i, b_ml_f, jnp.zeros((LANES - n_small,), F32)]).reshape(1, LANES)
    return wa, wqk, wv, wg, ws, bs


def kernel(x_prompt, x_sample, cache_k, cache_v, cache_logf, page_table, state_C, state_n, state_m, state_conv,
           c_prompt, c_sample, w_ada, b_ada, g_pre_mix, g_post_mix, g_pre_mlp, g_post_mlp, w_in, b_fox_f,
           b_ml_i, b_ml_f, conv_w, conv_b, w_proj_a, w_proj_b, w_out, w_up, w_down):
    assert w_in.shape[0] == 1, "one trunk layer"
    bsz, seq, d = x_prompt.shape
    dbsz, dseq, _ = x_sample.shape

    wa, wqk, wv, wg, ws, bs = _split_weights(w_in[0], b_fox_f[0], b_ml_i[0], b_ml_f[0])
    wpa, wpb, wo = w_proj_a[0].astype(BF16), w_proj_b[0].astype(BF16), w_out[0].astype(BF16)
    wu, wd = w_up[0].astype(BF16), w_down[0].astype(BF16)
    g1, g2, g3, g4 = g_pre_mix, g_post_mix, g_pre_mlp, g_post_mlp
    cw, cb = conv_w[0], conv_b

    n_c = bsz + dbsz
    c_all = jnp.concatenate([c_prompt, c_sample, jnp.zeros((-n_c % SUBLANES, d), F32)], axis=0)
    mod = _adaln(c_all, w_ada[0], b_ada)

    def token_path(x2, mods, tm, tiles_per_group):
        sh_m, sc_m, gt_m, sh_f, sc_f, gt_f = mods
        return _in_proj(x2, sc_m, sh_m, g1, wa, wqk, wv, wg, ws, bs, tm, tiles_per_group), (gt_m, sc_f, sh_f, gt_f)

    def finish(x2, ya, hb, og, rest, tm, tiles_per_group):
        gt_m, sc_f, sh_f, gt_f = rest
        x1, h2 = _mix_out(ya, hb, og, x2, gt_m, sc_f, sh_f, g2, g3, wpa, wpb, wo, tm, tiles_per_group)
        return _mlp(h2, x1, gt_f, g4, wu, wd, tm, tiles_per_group)

    tm = 256
    m_p = bsz * seq
    x2 = x_prompt.reshape(m_p, d)
    mods = tuple(t.reshape(bsz, 1, d) for t in jnp.split(mod[:bsz], N_MOD, axis=-1))
    (qa, k32, v32, ka, va, qkb, vb, og, sm), rest = token_path(x2, mods, tm, seq // tm)

    lf = sm[:, :A_HEADS].reshape(bsz, seq, A_HEADS)
    frow = _cumsum_lanes(jnp.transpose(lf, (0, 2, 1))).reshape(bsz, A_HEADS // 2, 2, seq)
    fcol = jnp.transpose(frow, (0, 1, 3, 2))
    r3 = lambda a: a.reshape(bsz, seq, a.shape[-1])
    ya = _attention(r3(qa), r3(ka), r3(va), fcol, frow).reshape(m_p, A_WIDTH)

    qkc = _conv(r3(qkb), jnp.zeros((bsz, SUBLANES, 2 * B_WIDTH), F32), cw, cb, tc=512)
    gcol = sm[:, A_HEADS:A_HEADS + 2 * B_HEADS].reshape(bsz, seq, 2 * B_HEADS)
    grow = jnp.transpose(gcol, (0, 2, 1))
    hb, c_p, n_p, m_pr = _mlstm(qkc, r3(vb), gcol, grow,
                                jnp.zeros((bsz, B_HEADS, B_HEAD_DIM, B_HEAD_DIM), F32),
                                jnp.zeros((bsz, B_HEADS, 1, B_HEAD_DIM), F32),
                                jnp.zeros((bsz, B_HEADS, 1, 1), F32), t=128)
    y_prompt = finish(x2, ya, hb.reshape(m_p, B_WIDTH), og, rest, tm, seq // tm).reshape(bsz, seq, d)

    new_k_prompt = k32.reshape(1, bsz, seq, A_HEADS, A_HEAD_DIM)
    new_v_prompt = v32.reshape(1, bsz, seq, A_HEADS, A_HEAD_DIM)
    new_logf_prompt = lf[None]
    new_conv_prompt = r3(qkb)[:, seq - (CONV_W - 1):, :][None]

    m_s = dbsz * dseq
    xs2 = x_sample.reshape(m_s, d)
    mods = tuple(t.reshape(1, m_s, d) for t in jnp.split(jnp.repeat(mod[bsz:n_c], dseq, axis=0), N_MOD, axis=-1))
    (qa, k32, v32, ka, va, qkb, vb, og, sm), rest = token_path(xs2, mods, m_s, 1)

    tok_minor = lambda a: jnp.pad(jnp.transpose(a.reshape(dbsz, dseq, -1), (0, 2, 1)),
                                  ((0, 0), (0, 0), (0, LANES - dseq)))
    lf_s = sm[:, :A_HEADS].reshape(dbsz, dseq, A_HEADS)
    q_heads = qa.reshape(dbsz, dseq, A_HEADS, 1, A_HEAD_DIM)
    q_bd = (q_heads * jnp.eye(A_HEADS, dtype=BF16)[None, None, :, :, None]).reshape(dbsz, dseq * A_HEADS, A_WIDTH)
    ya = _sample_attention(
        page_table, q_bd, tok_minor(ka), tok_minor(va), tok_minor(lf_s),
        jnp.transpose(cache_k[0], (0, 2, 3, 1)), jnp.transpose(cache_v[0], (0, 2, 3, 1)),
        jnp.transpose(cache_logf[0], (0, 2, 1)))
    ya = ya.astype(BF16).reshape(m_s, A_WIDTH)

    t_s = 128
    conv_in = jnp.pad(qkb.reshape(dbsz, dseq, 2 * B_WIDTH), ((0, 0), (0, SUBLANES - dseq), (0, 0)))
    conv_init = jnp.pad(state_conv[0], ((0, 0), (SUBLANES - (CONV_W - 1), 0), (0, 0)))
    qkc = _conv(conv_in, conv_init, cw, cb, tc=SUBLANES)
    tok_valid = (jnp.arange(t_s) < dseq)[None, :, None]
    pad_t = lambda a: jnp.pad(a, ((0, 0), (0, t_s - a.shape[1]), (0, 0)))
    qkc = jnp.where(tok_valid, pad_t(qkc), jnp.zeros((), BF16))
    vb_s = pad_t(vb.reshape(dbsz, dseq, B_WIDTH))
    gates = pad_t(sm[:, A_HEADS:A_HEADS + 2 * B_HEADS].reshape(dbsz, dseq, 2 * B_HEADS))
    neutral = jnp.concatenate([jnp.full((B_HEADS,), NEG_INF, F32), jnp.zeros((B_HEADS,), F32)])
    gcol = jnp.where(tok_valid, gates, neutral)
    grow = jnp.transpose(gcol, (0, 2, 1))
    hb, c_s, n_s, m_sm = _mlstm(qkc, vb_s, gcol, grow, state_C[0], state_n[0][:, :, None, :],
                                state_m[0][:, :, None, None], t=t_s)
    hb = hb[:, :dseq, :].reshape(m_s, B_WIDTH)
    y_sample = finish(xs2, ya, hb, og, rest, m_s, 1).reshape(dbsz, dseq, d)

    new_k_sample = k32.reshape(1, dbsz, dseq, A_HEADS, A_HEAD_DIM)
    new_v_sample = v32.reshape(1, dbsz, dseq, A_HEADS, A_HEAD_DIM)
    new_logf_sample = lf_s[None]
    qkb3 = qkb.reshape(dbsz, dseq, 2 * B_WIDTH)
    new_conv_sample = jnp.concatenate([state_conv[0], qkb3], axis=1)[:, dseq:, :][None]

    return (y_prompt, y_sample, new_k_prompt, new_v_prompt, new_logf_prompt,
            c_p[None], n_p[:, :, 0, :][None], m_pr[:, :, 0, 0][None], new_conv_prompt,
            new_k_sample, new_v_sample, new_logf_sample,
            c_s[None], n_s[:, :, 0, :][None], m_sm[:, :, 0, 0][None], new_conv_sample)
```

```python
import functools

import jax
import jax.numpy as jnp
from jax import lax
from jax.experimental import pallas as pl
from jax.experimental.pallas import tpu as pltpu

F32 = jnp.float32
BF16 = jnp.bfloat16

A_HEADS = 8
A_HEAD_DIM = 64
A_WIDTH = A_HEADS * A_HEAD_DIM
B_HEADS = 4
B_HEAD_DIM = 128
B_WIDTH = B_HEADS * B_HEAD_DIM
CONV_W = 4
N_MOD = 6
RMS_EPS = 1e-6
NEG_INF = -1e30
PAGE_SIZE = 128

LANES = 128
SUBLANES = 8
VMEM_LIMIT_BYTES = 56 * 1024 * 1024

_NT = (((1,), (1,)), ((), ()))
_TN = (((0,), (0,)), ((), ()))


def _params(*sem):
    return pltpu.CompilerParams(dimension_semantics=sem, vmem_limit_bytes=VMEM_LIMIT_BYTES)


def _resident(shape):
    nd = len(shape)
    return pl.BlockSpec(shape, lambda *_: (0,) * nd, pipeline_mode=pl.Buffered(1))


def _rms(x, g):
    r = lax.rsqrt(jnp.mean(x * x, axis=-1, keepdims=True) + RMS_EPS)
    return (x * r) * g


def _log_sigmoid(x):
    return jnp.minimum(x, 0.0) - jnp.log1p(jnp.exp(-jnp.abs(x)))


def _mod_kernel(c_ref, w_ref, b_ref, o_ref):
    c = c_ref[...]
    a = (c * jax.nn.sigmoid(c)).astype(BF16)
    o_ref[...] = jnp.dot(a, w_ref[...].astype(BF16), preferred_element_type=F32) + b_ref[...]


def _adaln(c, w, b, tn=1024):
    r, d = c.shape
    n = w.shape[1]
    return pl.pallas_call(
        _mod_kernel,
        out_shape=jax.ShapeDtypeStruct((r, n), F32),
        grid=(n // tn,),
        in_specs=[pl.BlockSpec((r, d), lambda j: (0, 0)),
                  pl.BlockSpec((d, tn), lambda j: (0, j)),
                  pl.BlockSpec((1, tn), lambda j: (0, j))],
        out_specs=pl.BlockSpec((r, tn), lambda j: (0, j)),
        compiler_params=_params("arbitrary"),
        name="adaln",
    )(c, w, b)


def _mod_spec(arr, tm, tiles_per_group):
    _, r, d = arr.shape
    if r == 1:
        return pl.BlockSpec((1, 1, d), lambda i: (i // tiles_per_group, 0, 0))
    assert r == tm
    return pl.BlockSpec((1, tm, d), lambda i: (i, 0, 0))


def _in_kernel(x_ref, sc_ref, sh_ref, g_ref, wa_ref, wqk_ref, wv_ref, wg_ref, ws_ref, bs_ref,
               qa_ref, k32_ref, v32_ref, ka_ref, va_ref, qkb_ref, vb_ref, og_ref, sm_ref):
    h = _rms(x_ref[...], g_ref[...]) * (1.0 + sc_ref[0]) + sh_ref[0]
    hb = h.astype(BF16)
    za = jnp.dot(hb, wa_ref[...], preferred_element_type=F32)
    qa_ref[...] = za[:, :A_WIDTH].astype(BF16)
    k = za[:, A_WIDTH:2 * A_WIDTH]
    v = za[:, 2 * A_WIDTH:]
    k32_ref[...] = k
    v32_ref[...] = v
    ka_ref[...] = k.astype(BF16)
    va_ref[...] = v.astype(BF16)
    qkb_ref[...] = jnp.dot(hb, wqk_ref[...], preferred_element_type=F32)
    vb_ref[...] = jnp.dot(hb, wv_ref[...], preferred_element_type=F32).astype(BF16)
    og_ref[...] = jax.nn.sigmoid(jnp.dot(hb, wg_ref[...], preferred_element_type=F32)).astype(BF16)
    zs = jnp.dot(hb, ws_ref[...], preferred_element_type=F32) + bs_ref[...]
    lane = lax.broadcasted_iota(jnp.int32, zs.shape, 1)
    is_input_gate = (lane >= A_HEADS) & (lane < A_HEADS + B_HEADS)
    sm_ref[...] = jnp.where(is_input_gate, zs, _log_sigmoid(zs))


def _in_proj(x, sc, sh, g, wa, wqk, wv, wg, ws, bs, tm, tiles_per_group):
    m, d = x.shape
    row = lambda n: pl.BlockSpec((tm, n), lambda i: (i, 0))
    outs = [(A_WIDTH, BF16), (A_WIDTH, F32), (A_WIDTH, F32), (A_WIDTH, BF16), (A_WIDTH, BF16),
            (2 * B_WIDTH, F32), (B_WIDTH, BF16), (wg.shape[1], BF16), (LANES, F32)]
    return pl.pallas_call(
        _in_kernel,
        out_shape=[jax.ShapeDtypeStruct((m, n), dt) for n, dt in outs],
        grid=(m // tm,),
        in_specs=[row(d), _mod_spec(sc, tm, tiles_per_group), _mod_spec(sh, tm, tiles_per_group),
                  _resident(g.shape), _resident(wa.shape), _resident(wqk.shape), _resident(wv.shape),
                  _resident(wg.shape), _resident(ws.shape), _resident(bs.shape)],
        out_specs=[row(n) for n, _ in outs],
        compiler_params=_params("arbitrary"),
        name="in_proj",
    )(x, sc, sh, g, wa, wqk, wv, wg, ws, bs)


def _mix_kernel(ya_ref, hb_ref, og_ref, x_ref, gt_ref, sc_ref, sh_ref, g1_ref, g2_ref,
                wpa_ref, wpb_ref, wo_ref, x1_ref, h2_ref):
    d = x_ref.shape[1]
    og = og_ref[...]
    ya = jnp.dot(ya_ref[...], wpa_ref[...], preferred_element_type=F32)
    yb_in = (og[:, :B_WIDTH].astype(F32) * hb_ref[...].astype(F32)).astype(BF16)
    yb = jnp.dot(yb_in, wpb_ref[...], preferred_element_type=F32)
    y = og[:, B_WIDTH:B_WIDTH + d].astype(F32) * ya + og[:, B_WIDTH + d:].astype(F32) * yb
    yo = jnp.dot(y.astype(BF16), wo_ref[...], preferred_element_type=F32)
    x1 = x_ref[...] + gt_ref[0] * _rms(yo, g1_ref[...])
    x1_ref[...] = x1
    h2_ref[...] = (_rms(x1, g2_ref[...]) * (1.0 + sc_ref[0]) + sh_ref[0]).astype(BF16)


def _mix_out(ya, hb, og, x, gt, sc, sh, g1, g2, wpa, wpb, wo, tm, tiles_per_group):
    m, d = x.shape
    row = lambda n: pl.BlockSpec((tm, n), lambda i: (i, 0))
    ms = lambda a: _mod_spec(a, tm, tiles_per_group)
    return pl.pallas_call(
        _mix_kernel,
        out_shape=[jax.ShapeDtypeStruct((m, d), F32), jax.ShapeDtypeStruct((m, d), BF16)],
        grid=(m // tm,),
        in_specs=[row(ya.shape[1]), row(hb.shape[1]), row(og.shape[1]), row(d), ms(gt), ms(sc), ms(sh),
                  _resident(g1.shape), _resident(g2.shape), _resident(wpa.shape), _resident(wpb.shape),
                  _resident(wo.shape)],
        out_specs=[row(d), row(d)],
        compiler_params=_params("arbitrary"),
        name="mix_out",
    )(ya, hb, og, x, gt, sc, sh, g1, g2, wpa, wpb, wo)


def _mlp_kernel(h2_ref, x1_ref, gt_ref, g_ref, wu_ref, wd_ref, y_ref, *, fc):
    h2 = h2_ref[...]
    acc = jnp.zeros(x1_ref.shape, F32)
    for c in range(wu_ref.shape[1] // fc):
        u = jnp.dot(h2, wu_ref[:, c * fc:(c + 1) * fc], preferred_element_type=F32)
        u = jnp.square(jnp.maximum(u, 0.0)).astype(BF16)
        acc = acc + jnp.dot(u, wd_ref[c * fc:(c + 1) * fc, :], preferred_element_type=F32)
    y_ref[...] = x1_ref[...] + gt_ref[0] * _rms(acc, g_ref[...])


def _mlp(h2, x1, gt, g, wu, wd, tm, tiles_per_group, fc=1024):
    m, d = x1.shape
    row = lambda n: pl.BlockSpec((tm, n), lambda i: (i, 0))
    return pl.pallas_call(
        functools.partial(_mlp_kernel, fc=fc),
        out_shape=jax.ShapeDtypeStruct((m, d), F32),
        grid=(m // tm,),
        in_specs=[row(d), row(d), _mod_spec(gt, tm, tiles_per_group), _resident(g.shape),
                  _resident(wu.shape), _resident(wd.shape)],
        out_specs=row(d),
        compiler_params=_params("arbitrary"),
        name="mlp",
    )(h2, x1, gt, g, wu, wd)


def _lane_scan(x, lane, shifts):
    for s in shifts:
        x = x + jnp.where(lane >= s, pltpu.roll(x, s, axis=1), 0.0)
    return x


def _cumsum_kernel(x_ref, o_ref):
    rows, length = x_ref.shape[1], x_ref.shape[2]
    lane = lax.broadcasted_iota(jnp.int32, (rows, LANES), 1)
    carry = jnp.zeros((rows, 1), F32)
    for c in range(length // LANES):
        x = _lane_scan(x_ref[0, :, c * LANES:(c + 1) * LANES], lane, (1, 2, 4, 8, 16, 32, 64)) + carry
        o_ref[0, :, c * LANES:(c + 1) * LANES] = x
        carry = x[:, LANES - 1:LANES]


def _cumsum_lanes(x):
    b, r, length = x.shape
    spec = pl.BlockSpec((1, r, length), lambda i: (i, 0, 0))
    return pl.pallas_call(
        _cumsum_kernel,
        out_shape=jax.ShapeDtypeStruct(x.shape, F32),
        grid=(b,),
        in_specs=[spec],
        out_specs=spec,
        compiler_params=_params("arbitrary"),
        name="logf_cumsum",
    )(x)


def _aug_lane0(hh):
    return A_HEAD_DIM if hh == 0 else 0


def _attn_kernel(q_ref, k_ref, v_ref, fc_ref, o_ref, kaug_ref, vaug_ref, *, blk_q, blk_k, chunk):
    qi = pl.program_id(2)
    length = k_ref.shape[1]
    lane = lax.broadcasted_iota(jnp.int32, (1, LANES), 1)
    own = (lane < A_HEAD_DIM, lane >= A_HEAD_DIM)

    @pl.when(qi == 0)
    def _():
        def build(c, _):
            rows = pl.ds(pl.multiple_of(c * chunk, chunk), chunk)
            k = k_ref[0, rows, :]
            v = v_ref[0, rows, :]
            f = fc_ref[0, 0, rows, :]
            for hh in range(2):
                a0 = _aug_lane0(hh)
                neg_f = -f[:, hh:hh + 1]
                hi = neg_f.astype(BF16).astype(F32)
                mid = (neg_f - hi).astype(BF16).astype(F32)
                lo = (neg_f - hi - mid).astype(BF16).astype(F32)
                bias = jnp.where(lane == a0, hi, jnp.where(lane == a0 + 1, mid, jnp.where(lane == a0 + 2, lo, 0.0)))
                kaug_ref[hh, rows, :] = jnp.where(own[hh], k, bias.astype(BF16))
                vaug_ref[hh, rows, :] = jnp.where(own[hh], v, jnp.where(lane == a0, 1.0, 0.0).astype(BF16))
            return 0
        lax.fori_loop(0, length // chunk, build, 0)

    q = q_ref[0]
    qh = []
    for hh in range(2):
        a0 = _aug_lane0(hh)
        ones = jnp.where((lane >= a0) & (lane < a0 + 3), 1.0, 0.0).astype(BF16)
        qh.append(jnp.where(own[hh], q, ones))
    row = lax.broadcasted_iota(jnp.int32, (blk_q, blk_k), 0)
    col = lax.broadcasted_iota(jnp.int32, (blk_q, blk_k), 1)

    def step(j, carry, first_masked_col):
        start = pl.multiple_of(j * blk_k, blk_k)
        new = []
        for hh in range(2):
            m, acc = carry[hh]
            s = lax.dot_general(qh[hh], kaug_ref[hh, pl.ds(start, blk_k), :], _NT, preferred_element_type=F32)
            if first_masked_col is not None:
                s = jnp.where(col + first_masked_col <= row, s, NEG_INF)
            m_new = jnp.maximum(m, jnp.max(s, axis=1, keepdims=True))
            p = jnp.exp(s - m_new).astype(BF16)
            acc = jnp.exp(m - m_new) * acc + jnp.dot(p, vaug_ref[hh, pl.ds(start, blk_k), :],
                                                     preferred_element_type=F32)
            new.append((m_new, acc))
        return tuple(new)

    per_q = blk_q // blk_k
    init = tuple((jnp.full((blk_q, 1), NEG_INF, F32), jnp.zeros((blk_q, LANES), F32)) for _ in range(2))
    carry = lax.fori_loop(0, qi * per_q, lambda j, c: step(j, c, None), init)
    for d in range(per_q):
        carry = step(qi * per_q + d, carry, d * blk_k)
    (_, a0), (_, a1) = carry
    o_ref[0] = jnp.where(own[0], a0 / a0[:, A_HEAD_DIM:A_HEAD_DIM + 1], a1 / a1[:, 0:1]).astype(BF16)


def _attention(q, k, v, fcol, blk_q=512, blk_k=512):
    b, length, width = q.shape
    pairs = width // LANES
    blk_q, blk_k = min(blk_q, length), min(blk_k, length)
    return pl.pallas_call(
        functools.partial(_attn_kernel, blk_q=blk_q, blk_k=blk_k, chunk=min(512, length)),
        out_shape=jax.ShapeDtypeStruct(q.shape, BF16),
        grid=(b, pairs, length // blk_q),
        in_specs=[pl.BlockSpec((1, blk_q, LANES), lambda i, p, t: (i, t, p)),
                  pl.BlockSpec((1, length, LANES), lambda i, p, t: (i, 0, p)),
                  pl.BlockSpec((1, length, LANES), lambda i, p, t: (i, 0, p)),
                  pl.BlockSpec((1, 1, length, 2), lambda i, p, t: (i, p, 0, 0))],
        out_specs=pl.BlockSpec((1, blk_q, LANES), lambda i, p, t: (i, t, p)),
        scratch_shapes=[pltpu.VMEM((2, length, LANES), BF16), pltpu.VMEM((2, length, LANES), BF16)],
        compiler_params=_params("arbitrary", "arbitrary", "arbitrary"),
        name="fox_attention",
    )(q, k, v, fcol)


def _conv_kernel(x_ref, prev_ref, init_ref, w_ref, b_ref, o_ref, xp_ref, *, tc):
    halo = jnp.where(pl.program_id(1) == 0, init_ref[0], prev_ref[0])
    xp_ref[0:SUBLANES, :] = halo
    xp_ref[SUBLANES:SUBLANES + tc, :] = x_ref[0]
    y = b_ref[...]
    for j in range(CONV_W):
        off = SUBLANES - (CONV_W - 1) + j
        y = y + w_ref[j:j + 1, :] * xp_ref[off:off + tc, :]
    y = y * jax.nn.sigmoid(y)
    lane = lax.broadcasted_iota(jnp.int32, (1, y.shape[1]), 1)
    y = y * jnp.where(lane < B_WIDTH, 1.0, B_HEAD_DIM ** -0.5)
    o_ref[0] = y.astype(BF16)


def _conv(x, init, w, b, tc):
    bsz, length, c = x.shape
    tpb = tc // SUBLANES
    return pl.pallas_call(
        functools.partial(_conv_kernel, tc=tc),
        out_shape=jax.ShapeDtypeStruct(x.shape, BF16),
        grid=(bsz, length // tc),
        in_specs=[pl.BlockSpec((1, tc, c), lambda i, t: (i, t, 0)),
                  pl.BlockSpec((1, SUBLANES, c), lambda i, t: (i, jnp.maximum(t * tpb - 1, 0), 0)),
                  pl.BlockSpec((1, SUBLANES, c), lambda i, t: (i, 0, 0)),
                  pl.BlockSpec((CONV_W, c), lambda i, t: (0, 0)),
                  pl.BlockSpec((1, c), lambda i, t: (0, 0))],
        out_specs=pl.BlockSpec((1, tc, c), lambda i, t: (i, t, 0)),
        scratch_shapes=[pltpu.VMEM((tc + SUBLANES, c), F32)],
        compiler_params=_params("arbitrary", "arbitrary"),
        name="short_conv",
    )(x, x, init, w, b)


_BNT = (((2,), (2,)), ((0,), (0,)))
_BNN = (((2,), (1,)), ((0,), (0,)))
_BTN = (((1,), (1,)), ((0,), (0,)))


def _mlstm_heads(q, k, v, ir, fr, ic, fc, c_st, n_st, m_st):
    n, t, _ = q.shape
    row = lax.broadcasted_iota(jnp.int32, (n, t, t), 1)
    col = lax.broadcasted_iota(jnp.int32, (n, t, t), 2)
    tril = col <= row
    bcum_c = jnp.sum(jnp.where(tril, fr, 0.0), axis=2, keepdims=True)
    bcum_r = jnp.sum(jnp.where(row <= col, fc, 0.0), axis=1, keepdims=True)
    dlog = jnp.where(tril, bcum_c - bcum_r + ir, NEG_INF)
    inter = bcum_c + m_st
    mt = jnp.maximum(inter, jnp.max(dlog, axis=2, keepdims=True))
    a = jnp.exp(dlog - mt) * lax.dot_general(q, k, _BNT, preferred_element_type=F32)
    si = jnp.exp(inter - mt)
    num = (lax.dot_general(a.astype(BF16), v, _BNN, preferred_element_type=F32)
           + si * lax.dot_general(q, c_st.astype(BF16), _BNN, preferred_element_type=F32))
    den = jnp.sum(a, axis=2, keepdims=True) + si * jnp.sum(q.astype(F32) * n_st, axis=2, keepdims=True)
    h = num / jnp.maximum(jnp.abs(den), jnp.exp(-mt))
    bl = bcum_r[:, :, t - 1:t]
    m_new = jnp.maximum(bl + m_st, jnp.max(bl - bcum_r + ir, axis=2, keepdims=True))
    kw = jnp.exp(bl - bcum_c + ic - m_new) * k.astype(F32)
    decay = jnp.exp(bl + m_st - m_new)
    c_new = decay * c_st + lax.dot_general(kw.astype(BF16), v, _BTN, preferred_element_type=F32)
    n_new = decay * n_st + jnp.sum(kw, axis=1, keepdims=True)
    return h, c_new, n_new, m_new


def _mlstm_kernel(q_ref, k_ref, v_ref, gc_ref, gr_ref, c0_ref, n0_ref, m0_ref,
                  h_ref, c_ref, n_ref, m_ref):
    @pl.when(pl.program_id(1) == 0)
    def _():
        c_ref[...] = c0_ref[...]
        n_ref[...] = n0_ref[...]
        m_ref[...] = m0_ref[...]

    nb = q_ref.shape[0]
    where = [(s, hd, slice(hd * B_HEAD_DIM, (hd + 1) * B_HEAD_DIM)) for s in range(nb) for hd in range(B_HEADS)]
    heads = lambda ref: jnp.stack([ref[s, :, sl] for s, _, sl in where])
    gc = [gc_ref[s] for s in range(nb)]
    gr = [gr_ref[s] for s in range(nb)]
    h, c_new, n_new, m_new = _mlstm_heads(
        heads(q_ref), heads(k_ref), heads(v_ref),
        jnp.stack([gr[s][hd:hd + 1, :] for s, hd, _ in where]),
        jnp.stack([gr[s][B_HEADS + hd:B_HEADS + hd + 1, :] for s, hd, _ in where]),
        jnp.stack([gc[s][:, hd:hd + 1] for s, hd, _ in where]),
        jnp.stack([gc[s][:, B_HEADS + hd:B_HEADS + hd + 1] for s, hd, _ in where]),
        c_ref[...].reshape((nb * B_HEADS,) + c_ref.shape[2:]),
        n_ref[...].reshape((nb * B_HEADS,) + n_ref.shape[2:]),
        m_ref[...].reshape((nb * B_HEADS,) + m_ref.shape[2:]))
    for i, (s, _, sl) in enumerate(where):
        h_ref[s, :, sl] = h[i].astype(BF16)
    c_ref[...] = c_new.reshape(c_ref.shape)
    n_ref[...] = n_new.reshape(n_ref.shape)
    m_ref[...] = m_new.reshape(m_ref.shape)


def _mlstm(qk, v, gcol, grow, c0, n0, m0, t, nb=1):
    b, length, _ = v.shape
    st = lambda a: pl.BlockSpec((nb,) + a.shape[1:], lambda i, j: (i, 0, 0, 0))
    return pl.pallas_call(
        _mlstm_kernel,
        out_shape=[jax.ShapeDtypeStruct(v.shape, BF16), jax.ShapeDtypeStruct(c0.shape, F32),
                   jax.ShapeDtypeStruct(n0.shape, F32), jax.ShapeDtypeStruct(m0.shape, F32)],
        grid=(b // nb, length // t),
        in_specs=[pl.BlockSpec((nb, t, B_WIDTH), lambda i, j: (i, j, 0)),
                  pl.BlockSpec((nb, t, B_WIDTH), lambda i, j: (i, j, 1)),
                  pl.BlockSpec((nb, t, B_WIDTH), lambda i, j: (i, j, 0)),
                  pl.BlockSpec((nb, t, 2 * B_HEADS), lambda i, j: (i, j, 0)),
                  pl.BlockSpec((nb, 2 * B_HEADS, t), lambda i, j: (i, 0, j)),
                  st(c0), st(n0), st(m0)],
        out_specs=[pl.BlockSpec((nb, t, B_WIDTH), lambda i, j: (i, j, 0)), st(c0), st(n0), st(m0)],
        compiler_params=_params("arbitrary", "arbitrary"),
        name="mlstm",
    )(qk, qk, v, gcol, grow, c0, n0, m0)


def _sattn_kernel(pt_ref, q_ref, kn_ref, vn_ref, lfn_ref, *rest, n_pages):
    del pt_ref
    k_refs = rest[:n_pages]
    v_refs = rest[n_pages:2 * n_pages]
    lf_refs = rest[2 * n_pages:3 * n_pages]
    o_ref = rest[3 * n_pages]
    q = q_ref[0]
    nrow, width = q.shape
    nq = nrow // A_HEADS
    lane = lax.broadcasted_iota(jnp.int32, (A_HEADS, LANES), 1)
    per_row = lambda f: jnp.concatenate([f] * nq, axis=0)

    offset = jnp.zeros((A_HEADS, 1), F32)
    scores = []
    for j in range(n_pages):
        f = _lane_scan(lf_refs[j][...], lane, (1, 2, 4, 8, 16, 32, 64)) + offset
        offset = f[:, LANES - 1:LANES]
        kt = k_refs[j][...].reshape(width, PAGE_SIZE).astype(BF16)
        scores.append(jnp.dot(q, kt, preferred_element_type=F32) - per_row(f))
    fnew = _lane_scan(lfn_ref[0], lane, (1, 2, 4, 8, 16, 32, 64)) + offset
    s_new = jnp.dot(q, kn_ref[0], preferred_element_type=F32) - per_row(fnew)
    q_tok = lax.broadcasted_iota(jnp.int32, (nrow, LANES), 0) // A_HEADS
    k_tok = lax.broadcasted_iota(jnp.int32, (nrow, LANES), 1)
    scores.append(jnp.where(k_tok <= q_tok, s_new, NEG_INF))

    m = jnp.max(functools.reduce(jnp.maximum, scores), axis=1, keepdims=True)
    probs = [jnp.exp(s - m) for s in scores]
    l = jnp.sum(functools.reduce(jnp.add, probs), axis=1, keepdims=True)
    acc = lax.dot_general(probs[n_pages].astype(BF16), vn_ref[0], _NT, preferred_element_type=F32)
    for j in range(n_pages):
        vt = v_refs[j][...].reshape(width, PAGE_SIZE).astype(BF16)
        acc = acc + lax.dot_general(probs[j].astype(BF16), vt, _NT, preferred_element_type=F32)
    row_head = lax.broadcasted_iota(jnp.int32, acc.shape, 0) % A_HEADS
    col_head = lax.broadcasted_iota(jnp.int32, acc.shape, 1) // A_HEAD_DIM
    out = jnp.where(row_head == col_head, acc / l, 0.0)
    o_ref[0] = jnp.sum(out.reshape(nq, A_HEADS, width), axis=1)


def _sample_attention(page_table, q, kn, vn, lfn, cache_kt, cache_vt, cache_lf):
    b, n_pages = page_table.shape
    nq = q.shape[1] // A_HEADS
    page = lambda j: pl.BlockSpec((None, A_HEADS, A_HEAD_DIM, PAGE_SIZE), lambda i, pt: (pt[i, j], 0, 0, 0))
    lfpage = lambda j: pl.BlockSpec((None, A_HEADS, PAGE_SIZE), lambda i, pt: (pt[i, j], 0, 0))
    per_seq = lambda a: pl.BlockSpec((1,) + a.shape[1:], lambda i, pt: (i, 0, 0))
    grid_spec = pltpu.PrefetchScalarGridSpec(
        num_scalar_prefetch=1,
        grid=(b,),
        in_specs=([per_seq(q), per_seq(kn), per_seq(vn), per_seq(lfn)]
                  + [page(j) for j in range(n_pages)] + [page(j) for j in range(n_pages)]
                  + [lfpage(j) for j in range(n_pages)]),
        out_specs=pl.BlockSpec((1, nq, A_WIDTH), lambda i, pt: (i, 0, 0)),
    )
    return pl.pallas_call(
        functools.partial(_sattn_kernel, n_pages=n_pages),
        out_shape=jax.ShapeDtypeStruct((b, nq, A_WIDTH), F32),
        grid_spec=grid_spec,
        compiler_params=_params("arbitrary"),
        name="paged_fox_attention",
    )(page_table, q, kn, vn, lfn, *([cache_kt] * n_pages), *([cache_vt] * n_pages), *([cache_lf] * n_pages))


def _split_weights(w_in, b_fox_f, b_ml_i, b_ml_f):
    d = w_in.shape[0]
    o = 0
    parts = {}
    for name, n in (("aq", A_WIDTH), ("ak", A_WIDTH), ("av", A_WIDTH), ("af", A_HEADS), ("bq", B_WIDTH),
                    ("bk", B_WIDTH), ("bv", B_WIDTH), ("bi", B_HEADS), ("bf", B_HEADS), ("bo", B_WIDTH),
                    ("ga", d), ("gb", d)):
        parts[name] = w_in[:, o:o + n]
        o += n
    assert o == w_in.shape[1]
    n_small = A_HEADS + 2 * B_HEADS
    wa = jnp.concatenate([parts["aq"] * (A_HEAD_DIM ** -0.5), parts["ak"], parts["av"]], axis=1).astype(BF16)
    wqk = jnp.concatenate([parts["bq"], parts["bk"]], axis=1).astype(BF16)
    wv = parts["bv"].astype(BF16)
    wg = jnp.concatenate([parts["bo"], parts["ga"], parts["gb"]], axis=1).astype(BF16)
    ws = jnp.concatenate([parts["af"], parts["bi"], parts["bf"], jnp.zeros((d, LANES - n_small), F32)],
                         axis=1).astype(BF16)
    bs = jnp.concatenate([b_fox_f, b_ml_i, b_ml_f, jnp.zeros((LANES - n_small,), F32)]).reshape(1, LANES)
    return wa, wqk, wv, wg, ws, bs


def kernel(x_prompt, x_sample, cache_k, cache_v, cache_logf, page_table, state_C, state_n, state_m, state_conv,
           c_prompt, c_sample, w_ada, b_ada, g_pre_mix, g_post_mix, g_pre_mlp, g_post_mlp, w_in, b_fox_f,
           b_ml_i, b_ml_f, conv_w, conv_b, w_proj_a, w_proj_b, w_out, w_up, w_down):
    assert w_in.shape[0] == 1, "one trunk layer"
    bsz, seq, d = x_prompt.shape
    dbsz, dseq, _ = x_sample.shape

    wa, wqk, wv, wg, ws, bs = _split_weights(w_in[0], b_fox_f[0], b_ml_i[0], b_ml_f[0])
    wpa, wpb, wo = w_proj_a[0].astype(BF16), w_proj_b[0].astype(BF16), w_out[0].astype(BF16)
    wu, wd = w_up[0].astype(BF16), w_down[0].astype(BF16)
    g1, g2, g3, g4 = g_pre_mix, g_post_mix, g_pre_mlp, g_post_mlp
    cw, cb = conv_w[0], conv_b

    n_c = bsz + dbsz
    c_all = jnp.concatenate([c_prompt, c_sample, jnp.zeros((-n_c % SUBLANES, d), F32)], axis=0)
    mod = _adaln(c_all, w_ada[0], b_ada)

    def token_path(x2, mods, tm, tiles_per_group):
        sh_m, sc_m, gt_m, sh_f, sc_f, gt_f = mods
        return _in_proj(x2, sc_m, sh_m, g1, wa, wqk, wv, wg, ws, bs, tm, tiles_per_group), (gt_m, sc_f, sh_f, gt_f)

    def finish(x2, ya, hb, og, rest, tm, tiles_per_group):
        gt_m, sc_f, sh_f, gt_f = rest
        x1, h2 = _mix_out(ya, hb, og, x2, gt_m, sc_f, sh_f, g2, g3, wpa, wpb, wo, tm, tiles_per_group)
        return _mlp(h2, x1, gt_f, g4, wu, wd, tm, tiles_per_group)

    tm = 256
    m_p = bsz * seq
    x2 = x_prompt.reshape(m_p, d)
    mods = tuple(t.reshape(bsz, 1, d) for t in jnp.split(mod[:bsz], N_MOD, axis=-1))
    (qa, k32, v32, ka, va, qkb, vb, og, sm), rest = token_path(x2, mods, tm, seq // tm)

    lf = sm[:, :A_HEADS].reshape(bsz, seq, A_HEADS)
    frow = _cumsum_lanes(jnp.transpose(lf, (0, 2, 1))).reshape(bsz, A_HEADS // 2, 2, seq)
    fcol = jnp.transpose(frow, (0, 1, 3, 2))
    r3 = lambda a: a.reshape(bsz, seq, a.shape[-1])
    ya = _attention(r3(qa), r3(ka), r3(va), fcol).reshape(m_p, A_WIDTH)

    qkc = _conv(r3(qkb), jnp.zeros((bsz, SUBLANES, 2 * B_WIDTH), F32), cw, cb, tc=512)
    gcol = sm[:, A_HEADS:A_HEADS + 2 * B_HEADS].reshape(bsz, seq, 2 * B_HEADS)
    grow = jnp.transpose(gcol, (0, 2, 1))
    hb, c_p, n_p, m_pr = _mlstm(qkc, r3(vb), gcol, grow,
                                jnp.zeros((bsz, B_HEADS, B_HEAD_DIM, B_HEAD_DIM), F32),
                                jnp.zeros((bsz, B_HEADS, 1, B_HEAD_DIM), F32),
                                jnp.zeros((bsz, B_HEADS, 1, 1), F32), t=128, nb=2)
    y_prompt = finish(x2, ya, hb.reshape(m_p, B_WIDTH), og, rest, tm, seq // tm).reshape(bsz, seq, d)

    new_k_prompt = k32.reshape(1, bsz, seq, A_HEADS, A_HEAD_DIM)
    new_v_prompt = v32.reshape(1, bsz, seq, A_HEADS, A_HEAD_DIM)
    new_logf_prompt = lf[None]
    new_conv_prompt = r3(qkb)[:, seq - (CONV_W - 1):, :][None]

    m_s = dbsz * dseq
    xs2 = x_sample.reshape(m_s, d)
    mods = tuple(t.reshape(1, m_s, d) for t in jnp.split(jnp.repeat(mod[bsz:n_c], dseq, axis=0), N_MOD, axis=-1))
    (qa, k32, v32, ka, va, qkb, vb, og, sm), rest = token_path(xs2, mods, m_s, 1)

    tok_minor = lambda a: jnp.pad(jnp.transpose(a.reshape(dbsz, dseq, -1), (0, 2, 1)),
                                  ((0, 0), (0, 0), (0, LANES - dseq)))
    lf_s = sm[:, :A_HEADS].reshape(dbsz, dseq, A_HEADS)
    q_heads = qa.reshape(dbsz, dseq, A_HEADS, 1, A_HEAD_DIM)
    q_bd = (q_heads * jnp.eye(A_HEADS, dtype=BF16)[None, None, :, :, None]).reshape(dbsz, dseq * A_HEADS, A_WIDTH)
    ya = _sample_attention(
        page_table, q_bd, tok_minor(ka), tok_minor(va), tok_minor(lf_s),
        jnp.transpose(cache_k[0], (0, 2, 3, 1)), jnp.transpose(cache_v[0], (0, 2, 3, 1)),
        jnp.transpose(cache_logf[0], (0, 2, 1)))
    ya = ya.astype(BF16).reshape(m_s, A_WIDTH)

    t_s = 16
    conv_in = jnp.pad(qkb.reshape(dbsz, dseq, 2 * B_WIDTH), ((0, 0), (0, SUBLANES - dseq), (0, 0)))
    conv_init = jnp.pad(state_conv[0], ((0, 0), (SUBLANES - (CONV_W - 1), 0), (0, 0)))
    qkc = _conv(conv_in, conv_init, cw, cb, tc=SUBLANES)
    tok_valid = (jnp.arange(t_s) < dseq)[None, :, None]
    pad_t = lambda a: jnp.pad(a, ((0, 0), (0, t_s - a.shape[1]), (0, 0)))
    qkc = jnp.where(tok_valid, pad_t(qkc), jnp.zeros((), BF16))
    vb_s = pad_t(vb.reshape(dbsz, dseq, B_WIDTH))
    gates = pad_t(sm[:, A_HEADS:A_HEADS + 2 * B_HEADS].reshape(dbsz, dseq, 2 * B_HEADS))
    neutral = jnp.concatenate([jnp.full((B_HEADS,), NEG_INF, F32), jnp.zeros((B_HEADS,), F32)])
    gcol = jnp.where(tok_valid, gates, neutral)
    grow = jnp.transpose(gcol, (0, 2, 1))
    hb, c_s, n_s, m_sm = _mlstm(qkc, vb_s, gcol, grow, state_C[0], state_n[0][:, :, None, :],
                                state_m[0][:, :, None, None], t=t_s, nb=8)
    hb = hb[:, :dseq, :].reshape(m_s, B_WIDTH)
    y_sample = finish(xs2, ya, hb, og, rest, m_s, 1).reshape(dbsz, dseq, d)

    new_k_sample = k32.reshape(1, dbsz, dseq, A_HEADS, A_HEAD_DIM)
    new_v_sample = v32.reshape(1, dbsz, dseq, A_HEADS, A_HEAD_DIM)
    new_logf_sample = lf_s[None]
    qkb3 = qkb.reshape(dbsz, dseq, 2 * B_WIDTH)
    new_conv_sample = jnp.concatenate([state_conv[0], qkb3], axis=1)[:, dseq:, :][None]

    return (y_prompt, y_sample, new_k_prompt, new_v_prompt, new_logf_prompt,
            c_p[None], n_p[:, :, 0, :][None], m_pr[:, :, 0, 0][None], new_conv_prompt,
            new_k_sample, new_v_sample, new_logf_sample,
            c_s[None], n_s[:, :, 0, :][None], m_sm[:, :, 0, 0][None], new_conv_sample)
```

```python
import functools

import jax
import jax.numpy as jnp
from jax import lax
from jax.experimental import pallas as pl
from jax.experimental.pallas import tpu as pltpu

F32 = jnp.float32
BF16 = jnp.bfloat16

A_HEADS = 8
A_HEAD_DIM = 64
A_WIDTH = A_HEADS * A_HEAD_DIM
B_HEADS = 4
B_HEAD_DIM = 128
B_WIDTH = B_HEADS * B_HEAD_DIM
CONV_W = 4
N_MOD = 6
RMS_EPS = 1e-6
NEG_INF = -1e30
PAGE_SIZE = 128
LOG2E = 1.4426950408889634

LANES = 128
SUBLANES = 8
VMEM_LIMIT_BYTES = 56 * 1024 * 1024

_NT = (((1,), (1,)), ((), ()))
_TN = (((0,), (0,)), ((), ()))


def _params(*sem):
    return pltpu.CompilerParams(dimension_semantics=sem, vmem_limit_bytes=VMEM_LIMIT_BYTES)


def _resident(shape):
    nd = len(shape)
    return pl.BlockSpec(shape, lambda *_: (0,) * nd, pipeline_mode=pl.Buffered(1))


def _rms(x, g):
    r = lax.rsqrt(jnp.mean(x * x, axis=-1, keepdims=True) + RMS_EPS)
    return (x * r) * g


def _log_sigmoid(x):
    return jnp.minimum(x, 0.0) - jnp.log1p(jnp.exp(-jnp.abs(x)))


def _mod_kernel(c_ref, w_ref, b_ref, o_ref):
    c = c_ref[...]
    a = (c * jax.nn.sigmoid(c)).astype(BF16)
    o_ref[...] = jnp.dot(a, w_ref[...].astype(BF16), preferred_element_type=F32) + b_ref[...]


def _adaln(c, w, b, tn=1024):
    r, d = c.shape
    n = w.shape[1]
    return pl.pallas_call(
        _mod_kernel,
        out_shape=jax.ShapeDtypeStruct((r, n), F32),
        grid=(n // tn,),
        in_specs=[pl.BlockSpec((r, d), lambda j: (0, 0)),
                  pl.BlockSpec((d, tn), lambda j: (0, j)),
                  pl.BlockSpec((1, tn), lambda j: (0, j))],
        out_specs=pl.BlockSpec((r, tn), lambda j: (0, j)),
        compiler_params=_params("arbitrary"),
        name="adaln",
    )(c, w, b)


def _mod_spec(arr, tm, tiles_per_group):
    _, r, d = arr.shape
    if r == 1:
        return pl.BlockSpec((1, 1, d), lambda i: (i // tiles_per_group, 0, 0))
    assert r == tm
    return pl.BlockSpec((1, tm, d), lambda i: (i, 0, 0))


def _in_kernel(x_ref, sc_ref, sh_ref, g_ref, wa_ref, wqk_ref, wv_ref, wg_ref, ws_ref, bs_ref,
               qa_ref, k32_ref, v32_ref, ka_ref, va_ref, qkb_ref, vb_ref, og_ref, sm_ref):
    h = _rms(x_ref[...], g_ref[...]) * (1.0 + sc_ref[0]) + sh_ref[0]
    hb = h.astype(BF16)
    za = jnp.dot(hb, wa_ref[...], preferred_element_type=F32)
    qa_ref[...] = za[:, :A_WIDTH].astype(BF16)
    k = za[:, A_WIDTH:2 * A_WIDTH]
    v = za[:, 2 * A_WIDTH:]
    k32_ref[...] = k
    v32_ref[...] = v
    ka_ref[...] = k.astype(BF16)
    va_ref[...] = v.astype(BF16)
    qkb_ref[...] = jnp.dot(hb, wqk_ref[...], preferred_element_type=F32)
    vb_ref[...] = jnp.dot(hb, wv_ref[...], preferred_element_type=F32).astype(BF16)
    og_ref[...] = jax.nn.sigmoid(jnp.dot(hb, wg_ref[...], preferred_element_type=F32)).astype(BF16)
    zs = jnp.dot(hb, ws_ref[...], preferred_element_type=F32) + bs_ref[...]
    lane = lax.broadcasted_iota(jnp.int32, zs.shape, 1)
    is_input_gate = (lane >= A_HEADS) & (lane < A_HEADS + B_HEADS)
    sm_ref[...] = jnp.where(is_input_gate, zs, _log_sigmoid(zs))


def _in_proj(x, sc, sh, g, wa, wqk, wv, wg, ws, bs, tm, tiles_per_group):
    m, d = x.shape
    row = lambda n: pl.BlockSpec((tm, n), lambda i: (i, 0))
    outs = [(A_WIDTH, BF16), (A_WIDTH, F32), (A_WIDTH, F32), (A_WIDTH, BF16), (A_WIDTH, BF16),
            (2 * B_WIDTH, F32), (B_WIDTH, BF16), (wg.shape[1], BF16), (LANES, F32)]
    return pl.pallas_call(
        _in_kernel,
        out_shape=[jax.ShapeDtypeStruct((m, n), dt) for n, dt in outs],
        grid=(m // tm,),
        in_specs=[row(d), _mod_spec(sc, tm, tiles_per_group), _mod_spec(sh, tm, tiles_per_group),
                  _resident(g.shape), _resident(wa.shape), _resident(wqk.shape), _resident(wv.shape),
                  _resident(wg.shape), _resident(ws.shape), _resident(bs.shape)],
        out_specs=[row(n) for n, _ in outs],
        compiler_params=_params("arbitrary"),
        name="in_proj",
    )(x, sc, sh, g, wa, wqk, wv, wg, ws, bs)


def _mix_kernel(ya_ref, hb_ref, og_ref, x_ref, gt_ref, sc_ref, sh_ref, g1_ref, g2_ref,
                wpa_ref, wpb_ref, wo_ref, x1_ref, h2_ref):
    d = x_ref.shape[1]
    og = og_ref[...]
    ya = jnp.dot(ya_ref[...], wpa_ref[...], preferred_element_type=F32)
    yb_in = (og[:, :B_WIDTH].astype(F32) * hb_ref[...].astype(F32)).astype(BF16)
    yb = jnp.dot(yb_in, wpb_ref[...], preferred_element_type=F32)
    y = og[:, B_WIDTH:B_WIDTH + d].astype(F32) * ya + og[:, B_WIDTH + d:].astype(F32) * yb
    yo = jnp.dot(y.astype(BF16), wo_ref[...], preferred_element_type=F32)
    x1 = x_ref[...] + gt_ref[0] * _rms(yo, g1_ref[...])
    x1_ref[...] = x1
    h2_ref[...] = (_rms(x1, g2_ref[...]) * (1.0 + sc_ref[0]) + sh_ref[0]).astype(BF16)


def _mix_out(ya, hb, og, x, gt, sc, sh, g1, g2, wpa, wpb, wo, tm, tiles_per_group):
    m, d = x.shape
    row = lambda n: pl.BlockSpec((tm, n), lambda i: (i, 0))
    ms = lambda a: _mod_spec(a, tm, tiles_per_group)
    return pl.pallas_call(
        _mix_kernel,
        out_shape=[jax.ShapeDtypeStruct((m, d), F32), jax.ShapeDtypeStruct((m, d), BF16)],
        grid=(m // tm,),
        in_specs=[row(ya.shape[1]), row(hb.shape[1]), row(og.shape[1]), row(d), ms(gt), ms(sc), ms(sh),
                  _resident(g1.shape), _resident(g2.shape), _resident(wpa.shape), _resident(wpb.shape),
                  _resident(wo.shape)],
        out_specs=[row(d), row(d)],
        compiler_params=_params("arbitrary"),
        name="mix_out",
    )(ya, hb, og, x, gt, sc, sh, g1, g2, wpa, wpb, wo)


def _mlp_kernel(h2_ref, x1_ref, gt_ref, g_ref, wu_ref, wd_ref, y_ref, *, fc):
    h2 = h2_ref[...]
    acc = jnp.zeros(x1_ref.shape, F32)
    for c in range(wu_ref.shape[1] // fc):
        u = jnp.dot(h2, wu_ref[:, c * fc:(c + 1) * fc], preferred_element_type=F32)
        u = jnp.square(jnp.maximum(u, 0.0)).astype(BF16)
        acc = acc + jnp.dot(u, wd_ref[c * fc:(c + 1) * fc, :], preferred_element_type=F32)
    y_ref[...] = x1_ref[...] + gt_ref[0] * _rms(acc, g_ref[...])


def _mlp(h2, x1, gt, g, wu, wd, tm, tiles_per_group, fc=1024):
    m, d = x1.shape
    row = lambda n: pl.BlockSpec((tm, n), lambda i: (i, 0))
    return pl.pallas_call(
        functools.partial(_mlp_kernel, fc=fc),
        out_shape=jax.ShapeDtypeStruct((m, d), F32),
        grid=(m // tm,),
        in_specs=[row(d), row(d), _mod_spec(gt, tm, tiles_per_group), _resident(g.shape),
                  _resident(wu.shape), _resident(wd.shape)],
        out_specs=row(d),
        compiler_params=_params("arbitrary"),
        name="mlp",
    )(h2, x1, gt, g, wu, wd)


def _lane_scan(x, lane, shifts):
    for s in shifts:
        x = x + jnp.where(lane >= s, pltpu.roll(x, s, axis=1), 0.0)
    return x


def _cumsum_kernel(x_ref, o_ref):
    rows, length = x_ref.shape[1], x_ref.shape[2]
    lane = lax.broadcasted_iota(jnp.int32, (rows, LANES), 1)
    carry = jnp.zeros((rows, 1), F32)
    for c in range(length // LANES):
        x = _lane_scan(x_ref[0, :, c * LANES:(c + 1) * LANES], lane, (1, 2, 4, 8, 16, 32, 64)) + carry
        o_ref[0, :, c * LANES:(c + 1) * LANES] = x
        carry = x[:, LANES - 1:LANES]


def _cumsum_lanes(x):
    b, r, length = x.shape
    spec = pl.BlockSpec((1, r, length), lambda i: (i, 0, 0))
    return pl.pallas_call(
        _cumsum_kernel,
        out_shape=jax.ShapeDtypeStruct(x.shape, F32),
        grid=(b,),
        in_specs=[spec],
        out_specs=spec,
        compiler_params=_params("arbitrary"),
        name="logf_cumsum",
    )(x)


def _aug_lane0(hh):
    return A_HEAD_DIM if hh == 0 else 0


def _attn_kernel(q_ref, k_ref, v_ref, fc_ref, o_ref, kaug_ref, vaug_ref, s_ref, *, blk_q, blk_k, chunk):
    qi = pl.program_id(2)
    length = k_ref.shape[1]
    lane = lax.broadcasted_iota(jnp.int32, (1, LANES), 1)
    own = (lane < A_HEAD_DIM, lane >= A_HEAD_DIM)

    @pl.when(qi == 0)
    def _():
        def build(c, _):
            rows = pl.ds(pl.multiple_of(c * chunk, chunk), chunk)
            k = k_ref[0, rows, :]
            v = v_ref[0, rows, :]
            f = fc_ref[0, 0, rows, :]
            for hh in range(2):
                a0 = _aug_lane0(hh)
                neg_f = f[:, hh:hh + 1] * (-LOG2E)
                hi = neg_f.astype(BF16).astype(F32)
                mid = (neg_f - hi).astype(BF16).astype(F32)
                lo = (neg_f - hi - mid).astype(BF16).astype(F32)
                bias = jnp.where(lane == a0, hi, jnp.where(lane == a0 + 1, mid, jnp.where(lane == a0 + 2, lo, 0.0)))
                kaug_ref[hh, rows, :] = jnp.where(own[hh], k, bias.astype(BF16))
                vaug_ref[hh, rows, :] = jnp.where(own[hh], v, jnp.where(lane == a0, 1.0, 0.0).astype(BF16))
            return 0
        lax.fori_loop(0, length // chunk, build, 0)

    q = q_ref[0]
    qh = []
    for hh in range(2):
        a0 = _aug_lane0(hh)
        ones = jnp.where((lane >= a0) & (lane < a0 + 3), 1.0, 0.0).astype(BF16)
        qh.append(jnp.where(own[hh], q, ones))
    def rows(j):
        return pl.ds(pl.multiple_of(j * blk_k, blk_k), blk_k)

    def scores(j, slot, r0=0):
        for hh in range(2):
            s_ref[slot, hh, r0:, :] = lax.dot_general(qh[hh][r0:], kaug_ref[hh, rows(j), :], _NT,
                                                     preferred_element_type=F32)

    def fold(j, slot, carry, causal, r0=0):
        probs, stats = [], []
        for hh in range(2):
            m = carry[hh][0]
            sh = s_ref[slot, hh, r0:, :]
            if causal:
                row = lax.broadcasted_iota(jnp.int32, sh.shape, 0)
                col = lax.broadcasted_iota(jnp.int32, sh.shape, 1)
                sh = jnp.where(col <= row, sh, NEG_INF)
            m_new = jnp.maximum(m, jnp.max(sh, axis=1, keepdims=True))
            probs.append(jnp.exp2(sh - m_new).astype(BF16))
            stats.append((m_new, jnp.exp2(m - m_new)))
        return tuple(
            (stats[hh][0], stats[hh][1] * carry[hh][1]
             + jnp.dot(probs[hh], vaug_ref[hh, rows(j), :], preferred_element_type=F32))
            for hh in range(2))

    assert blk_q == 2 * blk_k
    init = tuple((jnp.full((blk_q, 1), NEG_INF, F32), jnp.zeros((blk_q, LANES), F32)) for _ in range(2))

    def body(p, carry):
        scores(2 * p + 1, 1)
        carry = fold(2 * p, 0, carry, False)
        scores(2 * p + 2, 0)
        return fold(2 * p + 1, 1, carry, False)

    scores(0, 0)
    carry = lax.fori_loop(0, qi, body, init)
    scores(2 * qi + 1, 1, blk_k)
    carry = fold(2 * qi, 0, carry, True)
    lower = fold(2 * qi + 1, 1, tuple((m[blk_k:], acc[blk_k:]) for m, acc in carry), True, blk_k)
    a0, a1 = (jnp.concatenate([carry[hh][1][:blk_k], lower[hh][1]], axis=0) for hh in range(2))
    o_ref[0] = jnp.where(own[0], a0 / a0[:, A_HEAD_DIM:A_HEAD_DIM + 1], a1 / a1[:, 0:1]).astype(BF16)


def _attention(q, k, v, fcol, blk_q=1024, blk_k=512):
    b, length, width = q.shape
    pairs = width // LANES
    blk_q, blk_k = min(blk_q, length), min(blk_k, length)
    return pl.pallas_call(
        functools.partial(_attn_kernel, blk_q=blk_q, blk_k=blk_k, chunk=min(512, length)),
        out_shape=jax.ShapeDtypeStruct(q.shape, BF16),
        grid=(b, pairs, length // blk_q),
        in_specs=[pl.BlockSpec((1, blk_q, LANES), lambda i, p, t: (i, t, p)),
                  pl.BlockSpec((1, length, LANES), lambda i, p, t: (i, 0, p)),
                  pl.BlockSpec((1, length, LANES), lambda i, p, t: (i, 0, p)),
                  pl.BlockSpec((1, 1, length, 2), lambda i, p, t: (i, p, 0, 0))],
        out_specs=pl.BlockSpec((1, blk_q, LANES), lambda i, p, t: (i, t, p)),
        scratch_shapes=[pltpu.VMEM((2, length, LANES), BF16), pltpu.VMEM((2, length, LANES), BF16),
                        pltpu.VMEM((2, 2, blk_q, blk_k), F32)],
        compiler_params=_params("arbitrary", "arbitrary", "arbitrary"),
        name="fox_attention",
    )(q, k, v, fcol)


def _conv_kernel(x_ref, prev_ref, init_ref, w_ref, b_ref, o_ref, xp_ref, *, tc):
    halo = jnp.where(pl.program_id(1) == 0, init_ref[0], prev_ref[0])
    xp_ref[0:SUBLANES, :] = halo
    xp_ref[SUBLANES:SUBLANES + tc, :] = x_ref[0]
    y = b_ref[...]
    for j in range(CONV_W):
        off = SUBLANES - (CONV_W - 1) + j
        y = y + w_ref[j:j + 1, :] * xp_ref[off:off + tc, :]
    y = y * jax.nn.sigmoid(y)
    lane = lax.broadcasted_iota(jnp.int32, (1, y.shape[1]), 1)
    y = y * jnp.where(lane < B_WIDTH, 1.0, B_HEAD_DIM ** -0.5)
    o_ref[0] = y.astype(BF16)


def _conv(x, init, w, b, tc):
    bsz, length, c = x.shape
    tpb = tc // SUBLANES
    return pl.pallas_call(
        functools.partial(_conv_kernel, tc=tc),
        out_shape=jax.ShapeDtypeStruct(x.shape, BF16),
        grid=(bsz, length // tc),
        in_specs=[pl.BlockSpec((1, tc, c), lambda i, t: (i, t, 0)),
                  pl.BlockSpec((1, SUBLANES, c), lambda i, t: (i, jnp.maximum(t * tpb - 1, 0), 0)),
                  pl.BlockSpec((1, SUBLANES, c), lambda i, t: (i, 0, 0)),
                  pl.BlockSpec((CONV_W, c), lambda i, t: (0, 0)),
                  pl.BlockSpec((1, c), lambda i, t: (0, 0))],
        out_specs=pl.BlockSpec((1, tc, c), lambda i, t: (i, t, 0)),
        scratch_shapes=[pltpu.VMEM((tc + SUBLANES, c), F32)],
        compiler_params=_params("arbitrary", "arbitrary"),
        name="short_conv",
    )(x, x, init, w, b)


_BNT = (((2,), (2,)), ((0,), (0,)))
_BNN = (((2,), (1,)), ((0,), (0,)))
_BTN = (((1,), (1,)), ((0,), (0,)))


def _mlstm_heads(q, k, v, ir, fr, ic, fc, c_st, n_st, m_st):
    n, t, _ = q.shape
    row = lax.broadcasted_iota(jnp.int32, (n, t, t), 1)
    col = lax.broadcasted_iota(jnp.int32, (n, t, t), 2)
    tril = col <= row
    bcum_c = jnp.sum(jnp.where(tril, fr, 0.0), axis=2, keepdims=True)
    bcum_r = jnp.sum(jnp.where(row <= col, fc, 0.0), axis=1, keepdims=True)
    dlog = jnp.where(tril, bcum_c - bcum_r + ir, NEG_INF)
    inter = bcum_c + m_st
    mt = jnp.maximum(inter, jnp.max(dlog, axis=2, keepdims=True))
    a = jnp.exp(dlog - mt) * lax.dot_general(q, k, _BNT, preferred_element_type=F32)
    si = jnp.exp(inter - mt)
    num = (lax.dot_general(a.astype(BF16), v, _BNN, preferred_element_type=F32)
           + si * lax.dot_general(q, c_st.astype(BF16), _BNN, preferred_element_type=F32))
    den = jnp.sum(a, axis=2, keepdims=True) + si * jnp.sum(q.astype(F32) * n_st, axis=2, keepdims=True)
    h = num / jnp.maximum(jnp.abs(den), jnp.exp(-mt))
    bl = bcum_r[:, :, t - 1:t]
    m_new = jnp.maximum(bl + m_st, jnp.max(bl - bcum_r + ir, axis=2, keepdims=True))
    kw = jnp.exp(bl - bcum_c + ic - m_new) * k.astype(F32)
    decay = jnp.exp(bl + m_st - m_new)
    c_new = decay * c_st + lax.dot_general(kw.astype(BF16), v, _BTN, preferred_element_type=F32)
    n_new = decay * n_st + jnp.sum(kw, axis=1, keepdims=True)
    return h, c_new, n_new, m_new


def _mlstm_kernel(q_ref, k_ref, v_ref, gc_ref, gr_ref, c0_ref, n0_ref, m0_ref,
                  h_ref, c_ref, n_ref, m_ref):
    @pl.when(pl.program_id(1) == 0)
    def _():
        c_ref[...] = c0_ref[...]
        n_ref[...] = n0_ref[...]
        m_ref[...] = m0_ref[...]

    nb = q_ref.shape[0]
    where = [(s, hd, slice(hd * B_HEAD_DIM, (hd + 1) * B_HEAD_DIM)) for s in range(nb) for hd in range(B_HEADS)]
    heads = lambda ref: jnp.stack([ref[s, :, sl] for s, _, sl in where])
    gc = [gc_ref[s] for s in range(nb)]
    gr = [gr_ref[s] for s in range(nb)]
    h, c_new, n_new, m_new = _mlstm_heads(
        heads(q_ref), heads(k_ref), heads(v_ref),
        jnp.stack([gr[s][hd:hd + 1, :] for s, hd, _ in where]),
        jnp.stack([gr[s][B_HEADS + hd:B_HEADS + hd + 1, :] for s, hd, _ in where]),
        jnp.stack([gc[s][:, hd:hd + 1] for s, hd, _ in where]),
        jnp.stack([gc[s][:, B_HEADS + hd:B_HEADS + hd + 1] for s, hd, _ in where]),
        c_ref[...].reshape((nb * B_HEADS,) + c_ref.shape[2:]),
        n_ref[...].reshape((nb * B_HEADS,) + n_ref.shape[2:]),
        m_ref[...].reshape((nb * B_HEADS,) + m_ref.shape[2:]))
    for i, (s, _, sl) in enumerate(where):
        h_ref[s, :, sl] = h[i].astype(BF16)
    c_ref[...] = c_new.reshape(c_ref.shape)
    n_ref[...] = n_new.reshape(n_ref.shape)
    m_ref[...] = m_new.reshape(m_ref.shape)


def _mlstm(qk, v, gcol, grow, c0, n0, m0, t, nb=1):
    b, length, _ = v.shape
    st = lambda a: pl.BlockSpec((nb,) + a.shape[1:], lambda i, j: (i, 0, 0, 0))
    return pl.pallas_call(
        _mlstm_kernel,
        out_shape=[jax.ShapeDtypeStruct(v.shape, BF16), jax.ShapeDtypeStruct(c0.shape, F32),
                   jax.ShapeDtypeStruct(n0.shape, F32), jax.ShapeDtypeStruct(m0.shape, F32)],
        grid=(b // nb, length // t),
        in_specs=[pl.BlockSpec((nb, t, B_WIDTH), lambda i, j: (i, j, 0)),
                  pl.BlockSpec((nb, t, B_WIDTH), lambda i, j: (i, j, 1)),
                  pl.BlockSpec((nb, t, B_WIDTH), lambda i, j: (i, j, 0)),
                  pl.BlockSpec((nb, t, 2 * B_HEADS), lambda i, j: (i, j, 0)),
                  pl.BlockSpec((nb, 2 * B_HEADS, t), lambda i, j: (i, 0, j)),
                  st(c0), st(n0), st(m0)],
        out_specs=[pl.BlockSpec((nb, t, B_WIDTH), lambda i, j: (i, j, 0)), st(c0), st(n0), st(m0)],
        compiler_params=_params("arbitrary", "arbitrary"),
        name="mlstm",
    )(qk, qk, v, gcol, grow, c0, n0, m0)


def _sattn_kernel(pt_ref, q_ref, kn_ref, vn_ref, lfn_ref, *rest, n_pages):
    del pt_ref
    k_refs = rest[:n_pages]
    v_refs = rest[n_pages:2 * n_pages]
    lf_refs = rest[2 * n_pages:3 * n_pages]
    o_ref = rest[3 * n_pages]
    q = q_ref[0]
    nrow, width = q.shape
    nq = nrow // A_HEADS
    lane = lax.broadcasted_iota(jnp.int32, (A_HEADS, LANES), 1)
    per_row = lambda f: jnp.concatenate([f] * nq, axis=0)

    offset = jnp.zeros((A_HEADS, 1), F32)
    scores = []
    for j in range(n_pages):
        f = _lane_scan(lf_refs[j][...], lane, (1, 2, 4, 8, 16, 32, 64)) + offset
        offset = f[:, LANES - 1:LANES]
        kt = k_refs[j][...].reshape(width, PAGE_SIZE).astype(BF16)
        scores.append(jnp.dot(q, kt, preferred_element_type=F32) - per_row(f * LOG2E))
    fnew = _lane_scan(lfn_ref[0], lane, (1, 2, 4, 8, 16, 32, 64)) + offset
    s_new = jnp.dot(q, kn_ref[0], preferred_element_type=F32) - per_row(fnew * LOG2E)
    q_tok = lax.broadcasted_iota(jnp.int32, (nrow, LANES), 0) // A_HEADS
    k_tok = lax.broadcasted_iota(jnp.int32, (nrow, LANES), 1)
    scores.append(jnp.where(k_tok <= q_tok, s_new, NEG_INF))

    m = jnp.max(functools.reduce(jnp.maximum, scores), axis=1, keepdims=True)
    probs = [jnp.exp2(s - m) for s in scores]
    l = jnp.sum(functools.reduce(jnp.add, probs), axis=1, keepdims=True)
    acc = lax.dot_general(probs[n_pages].astype(BF16), vn_ref[0], _NT, preferred_element_type=F32)
    for j in range(n_pages):
        vt = v_refs[j][...].reshape(width, PAGE_SIZE).astype(BF16)
        acc = acc + lax.dot_general(probs[j].astype(BF16), vt, _NT, preferred_element_type=F32)
    row_head = lax.broadcasted_iota(jnp.int32, acc.shape, 0) % A_HEADS
    col_head = lax.broadcasted_iota(jnp.int32, acc.shape, 1) // A_HEAD_DIM
    out = jnp.where(row_head == col_head, acc / l, 0.0)
    o_ref[0] = jnp.sum(out.reshape(nq, A_HEADS, width), axis=1)


def _sample_attention(page_table, q, kn, vn, lfn, cache_kt, cache_vt, cache_lf):
    b, n_pages = page_table.shape
    nq = q.shape[1] // A_HEADS
    page = lambda j: pl.BlockSpec((None, A_HEADS, A_HEAD_DIM, PAGE_SIZE), lambda i, pt: (pt[i, j], 0, 0, 0))
    lfpage = lambda j: pl.BlockSpec((None, A_HEADS, PAGE_SIZE), lambda i, pt: (pt[i, j], 0, 0))
    per_seq = lambda a: pl.BlockSpec((1,) + a.shape[1:], lambda i, pt: (i, 0, 0))
    grid_spec = pltpu.PrefetchScalarGridSpec(
        num_scalar_prefetch=1,
        grid=(b,),
        in_specs=([per_seq(q), per_seq(kn), per_seq(vn), per_seq(lfn)]
                  + [page(j) for j in range(n_pages)] + [page(j) for j in range(n_pages)]
                  + [lfpage(j) for j in range(n_pages)]),
        out_specs=pl.BlockSpec((1, nq, A_WIDTH), lambda i, pt: (i, 0, 0)),
    )
    return pl.pallas_call(
        functools.partial(_sattn_kernel, n_pages=n_pages),
        out_shape=jax.ShapeDtypeStruct((b, nq, A_WIDTH), F32),
        grid_spec=grid_spec,
        compiler_params=_params("arbitrary"),
        name="paged_fox_attention",
    )(page_table, q, kn, vn, lfn, *([cache_kt] * n_pages), *([cache_vt] * n_pages), *([cache_lf] * n_pages))


def _split_weights(w_in, b_fox_f, b_ml_i, b_ml_f):
    d = w_in.shape[0]
    o = 0
    parts = {}
    for name, n in (("aq", A_WIDTH), ("ak", A_WIDTH), ("av", A_WIDTH), ("af", A_HEADS), ("bq", B_WIDTH),
                    ("bk", B_WIDTH), ("bv", B_WIDTH), ("bi", B_HEADS), ("bf", B_HEADS), ("bo", B_WIDTH),
                    ("ga", d), ("gb", d)):
        parts[name] = w_in[:, o:o + n]
        o += n
    assert o == w_in.shape[1]
    n_small = A_HEADS + 2 * B_HEADS
    wa = jnp.concatenate([parts["aq"] * (A_HEAD_DIM ** -0.5 * LOG2E), parts["ak"], parts["av"]],
                         axis=1).astype(BF16)
    wqk = jnp.concatenate([parts["bq"], parts["bk"]], axis=1).astype(BF16)
    wv = parts["bv"].astype(BF16)
    wg = jnp.concatenate([parts["bo"], parts["ga"], parts["gb"]], axis=1).astype(BF16)
    ws = jnp.concatenate([parts["af"], parts["bi"], parts["bf"], jnp.zeros((d, LANES - n_small), F32)],
                         axis=1).astype(BF16)
    bs = jnp.concatenate([b_fox_f, b_ml_i, b_ml_f, jnp.zeros((LANES - n_small,), F32)]).reshape(1, LANES)
    return wa, wqk, wv, wg, ws, bs


def kernel(x_prompt, x_sample, cache_k, cache_v, cache_logf, page_table, state_C, state_n, state_m, state_conv,
           c_prompt, c_sample, w_ada, b_ada, g_pre_mix, g_post_mix, g_pre_mlp, g_post_mlp, w_in, b_fox_f,
           b_ml_i, b_ml_f, conv_w, conv_b, w_proj_a, w_proj_b, w_out, w_up, w_down):
    assert w_in.shape[0] == 1, "one trunk layer"
    bsz, seq, d = x_prompt.shape
    dbsz, dseq, _ = x_sample.shape

    wa, wqk, wv, wg, ws, bs = _split_weights(w_in[0], b_fox_f[0], b_ml_i[0], b_ml_f[0])
    wpa, wpb, wo = w_proj_a[0].astype(BF16), w_proj_b[0].astype(BF16), w_out[0].astype(BF16)
    wu, wd = w_up[0].astype(BF16), w_down[0].astype(BF16)
    g1, g2, g3, g4 = g_pre_mix, g_post_mix, g_pre_mlp, g_post_mlp
    cw, cb = conv_w[0], conv_b

    n_c = bsz + dbsz
    c_all = jnp.concatenate([c_prompt, c_sample, jnp.zeros((-n_c % SUBLANES, d), F32)], axis=0)
    mod = _adaln(c_all, w_ada[0], b_ada)

    def token_path(x2, mods, tm, tiles_per_group):
        sh_m, sc_m, gt_m, sh_f, sc_f, gt_f = mods
        return _in_proj(x2, sc_m, sh_m, g1, wa, wqk, wv, wg, ws, bs, tm, tiles_per_group), (gt_m, sc_f, sh_f, gt_f)

    def finish(x2, ya, hb, og, rest, tm, tiles_per_group):
        gt_m, sc_f, sh_f, gt_f = rest
        x1, h2 = _mix_out(ya, hb, og, x2, gt_m, sc_f, sh_f, g2, g3, wpa, wpb, wo, tm, tiles_per_group)
        return _mlp(h2, x1, gt_f, g4, wu, wd, tm, tiles_per_group)

    tm = 256
    m_p = bsz * seq
    x2 = x_prompt.reshape(m_p, d)
    mods = tuple(t.reshape(bsz, 1, d) for t in jnp.split(mod[:bsz], N_MOD, axis=-1))
    (qa, k32, v32, ka, va, qkb, vb, og, sm), rest = token_path(x2, mods, tm, seq // tm)

    lf = sm[:, :A_HEADS].reshape(bsz, seq, A_HEADS)
    frow = _cumsum_lanes(jnp.transpose(lf, (0, 2, 1))).reshape(bsz, A_HEADS // 2, 2, seq)
    fcol = jnp.transpose(frow, (0, 1, 3, 2))
    r3 = lambda a: a.reshape(bsz, seq, a.shape[-1])
    ya = _attention(r3(qa), r3(ka), r3(va), fcol).reshape(m_p, A_WIDTH)

    qkc = _conv(r3(qkb), jnp.zeros((bsz, SUBLANES, 2 * B_WIDTH), F32), cw, cb, tc=512)
    gcol = sm[:, A_HEADS:A_HEADS + 2 * B_HEADS].reshape(bsz, seq, 2 * B_HEADS)
    grow = jnp.transpose(gcol, (0, 2, 1))
    hb, c_p, n_p, m_pr = _mlstm(qkc, r3(vb), gcol, grow,
                                jnp.zeros((bsz, B_HEADS, B_HEAD_DIM, B_HEAD_DIM), F32),
                                jnp.zeros((bsz, B_HEADS, 1, B_HEAD_DIM), F32),
                                jnp.zeros((bsz, B_HEADS, 1, 1), F32), t=128, nb=2)
    y_prompt = finish(x2, ya, hb.reshape(m_p, B_WIDTH), og, rest, tm, seq // tm).reshape(bsz, seq, d)

    new_k_prompt = k32.reshape(1, bsz, seq, A_HEADS, A_HEAD_DIM)
    new_v_prompt = v32.reshape(1, bsz, seq, A_HEADS, A_HEAD_DIM)
    new_logf_prompt = lf[None]
    new_conv_prompt = r3(qkb)[:, seq - (CONV_W - 1):, :][None]

    m_s = dbsz * dseq
    xs2 = x_sample.reshape(m_s, d)
    mods = tuple(t.reshape(1, m_s, d) for t in jnp.split(jnp.repeat(mod[bsz:n_c], dseq, axis=0), N_MOD, axis=-1))
    (qa, k32, v32, ka, va, qkb, vb, og, sm), rest = token_path(xs2, mods, m_s, 1)

    tok_minor = lambda a: jnp.pad(jnp.transpose(a.reshape(dbsz, dseq, -1), (0, 2, 1)),
                                  ((0, 0), (0, 0), (0, LANES - dseq)))
    lf_s = sm[:, :A_HEADS].reshape(dbsz, dseq, A_HEADS)
    q_heads = qa.reshape(dbsz, dseq, A_HEADS, 1, A_HEAD_DIM)
    q_bd = (q_heads * jnp.eye(A_HEADS, dtype=BF16)[None, None, :, :, None]).reshape(dbsz, dseq * A_HEADS, A_WIDTH)
    ya = _sample_attention(
        page_table, q_bd, tok_minor(ka), tok_minor(va), tok_minor(lf_s),
        jnp.transpose(cache_k[0], (0, 2, 3, 1)), jnp.transpose(cache_v[0], (0, 2, 3, 1)),
        jnp.transpose(cache_logf[0], (0, 2, 1)))
    ya = ya.astype(BF16).reshape(m_s, A_WIDTH)

    t_s = 16
    conv_in = jnp.pad(qkb.reshape(dbsz, dseq, 2 * B_WIDTH), ((0, 0), (0, SUBLANES - dseq), (0, 0)))
    conv_init = jnp.pad(state_conv[0], ((0, 0), (SUBLANES - (CONV_W - 1), 0), (0, 0)))
    qkc = _conv(conv_in, conv_init, cw, cb, tc=SUBLANES)
    tok_valid = (jnp.arange(t_s) < dseq)[None, :, None]
    pad_t = lambda a: jnp.pad(a, ((0, 0), (0, t_s - a.shape[1]), (0, 0)))
    qkc = jnp.where(tok_valid, pad_t(qkc), jnp.zeros((), BF16))
    vb_s = pad_t(vb.reshape(dbsz, dseq, B_WIDTH))
    gates = pad_t(sm[:, A_HEADS:A_HEADS + 2 * B_HEADS].reshape(dbsz, dseq, 2 * B_HEADS))
    neutral = jnp.concatenate([jnp.full((B_HEADS,), NEG_INF, F32), jnp.zeros((B_HEADS,), F32)])
    gcol = jnp.where(tok_valid, gates, neutral)
    grow = jnp.transpose(gcol, (0, 2, 1))
    hb, c_s, n_s, m_sm = _mlstm(qkc, vb_s, gcol, grow, state_C[0], state_n[0][:, :, None, :],
                                state_m[0][:, :, None, None], t=t_s, nb=8)
    hb = hb[:, :dseq, :].reshape(m_s, B_WIDTH)
    y_sample = finish(xs2, ya, hb, og, rest, m_s, 1).reshape(dbsz, dseq, d)

    new_k_sample = k32.reshape(1, dbsz, dseq, A_HEADS, A_HEAD_DIM)
    new_v_sample = v32.reshape(1, dbsz, dseq, A_HEADS, A_HEAD_DIM)
    new_logf_sample = lf_s[None]
    qkb3 = qkb.reshape(dbsz, dseq, 2 * B_WIDTH)
    new_conv_sample = jnp.concatenate([state_conv[0], qkb3], axis=1)[:, dseq:, :][None]

    return (y_prompt, y_sample, new_k_prompt, new_v_prompt, new_logf_prompt,
            c_p[None], n_p[:, :, 0, :][None], m_pr[:, :, 0, 0][None], new_conv_prompt,
            new_k_sample, new_v_sample, new_logf_sample,
            c_s[None], n_s[:, :, 0, :][None], m_sm[:, :, 0, 0][None], new_conv_sample)
```

```python
import functools

import jax
import jax.numpy as jnp
from jax import lax
from jax.experimental import pallas as pl
from jax.experimental.pallas import tpu as pltpu

F32 = jnp.float32
BF16 = jnp.bfloat16

A_HEADS = 8
A_HEAD_DIM = 64
A_WIDTH = A_HEADS * A_HEAD_DIM
B_HEADS = 4
B_HEAD_DIM = 128
B_WIDTH = B_HEADS * B_HEAD_DIM
CONV_W = 4
N_MOD = 6
RMS_EPS = 1e-6
NEG_INF = -1e30
PAGE_SIZE = 128
LOG2E = 1.4426950408889634

LANES = 128
SUBLANES = 8
VMEM_LIMIT_BYTES = 56 * 1024 * 1024

_NT = (((1,), (1,)), ((), ()))
_TN = (((0,), (0,)), ((), ()))


def _params(*sem):
    return pltpu.CompilerParams(dimension_semantics=sem, vmem_limit_bytes=VMEM_LIMIT_BYTES)


def _resident(shape):
    nd = len(shape)
    return pl.BlockSpec(shape, lambda *_: (0,) * nd, pipeline_mode=pl.Buffered(1))


def _rms(x, g):
    r = lax.rsqrt(jnp.mean(x * x, axis=-1, keepdims=True) + RMS_EPS)
    return (x * r) * g


def _log_sigmoid(x):
    return jnp.minimum(x, 0.0) - jnp.log1p(jnp.exp(-jnp.abs(x)))


def _mod_kernel(c_ref, w_ref, b_ref, o_ref):
    c = c_ref[...]
    a = (c * jax.nn.sigmoid(c)).astype(BF16)
    o_ref[...] = jnp.dot(a, w_ref[...].astype(BF16), preferred_element_type=F32) + b_ref[...]


def _adaln(c, w, b, tn=1024):
    r, d = c.shape
    n = w.shape[1]
    return pl.pallas_call(
        _mod_kernel,
        out_shape=jax.ShapeDtypeStruct((r, n), F32),
        grid=(n // tn,),
        in_specs=[pl.BlockSpec((r, d), lambda j: (0, 0)),
                  pl.BlockSpec((d, tn), lambda j: (0, j)),
                  pl.BlockSpec((1, tn), lambda j: (0, j))],
        out_specs=pl.BlockSpec((r, tn), lambda j: (0, j)),
        compiler_params=_params("arbitrary"),
        name="adaln",
    )(c, w, b)


def _mod_spec(arr, tm, tiles_per_group):
    _, r, d = arr.shape
    if r == 1:
        return pl.BlockSpec((1, 1, d), lambda i: (i // tiles_per_group, 0, 0))
    assert r == tm
    return pl.BlockSpec((1, tm, d), lambda i: (i, 0, 0))


N_GATES = A_HEADS + 2 * B_HEADS
GATE_ROWS = 16


def _gate_act(z, idx):
    return jnp.where((idx >= A_HEADS) & (idx < A_HEADS + B_HEADS), z, _log_sigmoid(z))


def _in_kernel(x_ref, sc_ref, sh_ref, g_ref, wq_ref, wkt_ref, wv_ref, wqk_ref, wbv_ref, wg_ref,
               ws_ref, bs_ref, wst_ref, bst_ref, *rest, token_major_k):
    if token_major_k:
        wk_ref, *rest = rest
    qa_ref, kt_ref, vt_ref, va_ref, qkb_ref, vb_ref, og_ref, sm_ref, smt_ref, *ka_ref = rest
    h = _rms(x_ref[...], g_ref[...]) * (1.0 + sc_ref[0]) + sh_ref[0]
    hb = h.astype(BF16)
    qa_ref[...] = jnp.dot(hb, wq_ref[...], preferred_element_type=F32).astype(BF16)
    kt_ref[0] = lax.dot_general(wkt_ref[...], hb, _NT, preferred_element_type=F32)
    v = jnp.dot(hb, wv_ref[...], preferred_element_type=F32)
    vt_ref[0] = v.T
    va_ref[...] = v.astype(BF16)
    if token_major_k:
        ka_ref[0][...] = jnp.dot(hb, wk_ref[...], preferred_element_type=F32).astype(BF16)
    qkb_ref[...] = jnp.dot(hb, wqk_ref[...], preferred_element_type=F32)
    vb_ref[...] = jnp.dot(hb, wbv_ref[...], preferred_element_type=F32).astype(BF16)
    og_ref[...] = jax.nn.sigmoid(jnp.dot(hb, wg_ref[...], preferred_element_type=F32)).astype(BF16)
    zs = jnp.dot(hb, ws_ref[...], preferred_element_type=F32) + bs_ref[...]
    sm_ref[...] = _gate_act(zs, lax.broadcasted_iota(jnp.int32, zs.shape, 1))
    zt = lax.dot_general(wst_ref[...], hb, _NT, preferred_element_type=F32) + bst_ref[...]
    smt_ref[0] = _gate_act(zt, lax.broadcasted_iota(jnp.int32, zt.shape, 0))


def _in_proj(x, sc, sh, g, w, tm, tiles_per_group, token_major_k):
    m, d = x.shape
    group_tokens = tm * tiles_per_group
    groups = m // group_tokens
    row = lambda n: pl.BlockSpec((tm, n), lambda i: (i, 0))
    col = lambda n: pl.BlockSpec((1, n, tm), lambda i: (i // tiles_per_group, 0, i % tiles_per_group))
    rows_out = lambda n, dt: (jax.ShapeDtypeStruct((m, n), dt), row(n))
    cols_out = lambda n: (jax.ShapeDtypeStruct((groups, n, group_tokens), F32), col(n))
    outs = [rows_out(A_WIDTH, BF16), cols_out(A_WIDTH), cols_out(A_WIDTH), rows_out(A_WIDTH, BF16),
            rows_out(2 * B_WIDTH, F32), rows_out(B_WIDTH, BF16), rows_out(w["g"].shape[1], BF16),
            rows_out(LANES, F32), cols_out(GATE_ROWS)]
    weights = [w["q"], w["kt"], w["v"], w["qk"], w["bv"], w["g"], w["s"], w["bs"], w["st"], w["bst"]]
    if token_major_k:
        weights.append(w["k"])
        outs.append(rows_out(A_WIDTH, BF16))
    return pl.pallas_call(
        functools.partial(_in_kernel, token_major_k=token_major_k),
        out_shape=[o[0] for o in outs],
        grid=(m // tm,),
        in_specs=[row(d), _mod_spec(sc, tm, tiles_per_group), _mod_spec(sh, tm, tiles_per_group),
                  _resident(g.shape)] + [_resident(a.shape) for a in weights],
        out_specs=[o[1] for o in outs],
        compiler_params=_params("arbitrary"),
        name="in_proj",
    )(x, sc, sh, g, *weights)


def _mix_kernel(ya_ref, hb_ref, og_ref, x_ref, gt_ref, sc_ref, sh_ref, g1_ref, g2_ref,
                wpa_ref, wpb_ref, wo_ref, x1_ref, h2_ref):
    d = x_ref.shape[1]
    og = og_ref[...]
    ya = jnp.dot(ya_ref[...], wpa_ref[...], preferred_element_type=F32)
    yb_in = (og[:, :B_WIDTH].astype(F32) * hb_ref[...].astype(F32)).astype(BF16)
    yb = jnp.dot(yb_in, wpb_ref[...], preferred_element_type=F32)
    y = og[:, B_WIDTH:B_WIDTH + d].astype(F32) * ya + og[:, B_WIDTH + d:].astype(F32) * yb
    yo = jnp.dot(y.astype(BF16), wo_ref[...], preferred_element_type=F32)
    x1 = x_ref[...] + gt_ref[0] * _rms(yo, g1_ref[...])
    x1_ref[...] = x1
    h2_ref[...] = (_rms(x1, g2_ref[...]) * (1.0 + sc_ref[0]) + sh_ref[0]).astype(BF16)


def _mix_out(ya, hb, og, x, gt, sc, sh, g1, g2, wpa, wpb, wo, tm, tiles_per_group):
    m, d = x.shape
    row = lambda n: pl.BlockSpec((tm, n), lambda i: (i, 0))
    ms = lambda a: _mod_spec(a, tm, tiles_per_group)
    return pl.pallas_call(
        _mix_kernel,
        out_shape=[jax.ShapeDtypeStruct((m, d), F32), jax.ShapeDtypeStruct((m, d), BF16)],
        grid=(m // tm,),
        in_specs=[row(ya.shape[1]), row(hb.shape[1]), row(og.shape[1]), row(d), ms(gt), ms(sc), ms(sh),
                  _resident(g1.shape), _resident(g2.shape), _resident(wpa.shape), _resident(wpb.shape),
                  _resident(wo.shape)],
        out_specs=[row(d), row(d)],
        compiler_params=_params("arbitrary"),
        name="mix_out",
    )(ya, hb, og, x, gt, sc, sh, g1, g2, wpa, wpb, wo)


def _mlp_kernel(h2_ref, x1_ref, gt_ref, g_ref, wu_ref, wd_ref, y_ref, *, fc):
    h2 = h2_ref[...]
    acc = jnp.zeros(x1_ref.shape, F32)
    for c in range(wu_ref.shape[1] // fc):
        u = jnp.dot(h2, wu_ref[:, c * fc:(c + 1) * fc], preferred_element_type=F32)
        u = jnp.square(jnp.maximum(u, 0.0)).astype(BF16)
        acc = acc + jnp.dot(u, wd_ref[c * fc:(c + 1) * fc, :], preferred_element_type=F32)
    y_ref[...] = x1_ref[...] + gt_ref[0] * _rms(acc, g_ref[...])


def _mlp(h2, x1, gt, g, wu, wd, tm, tiles_per_group, fc=1024):
    m, d = x1.shape
    row = lambda n: pl.BlockSpec((tm, n), lambda i: (i, 0))
    return pl.pallas_call(
        functools.partial(_mlp_kernel, fc=fc),
        out_shape=jax.ShapeDtypeStruct((m, d), F32),
        grid=(m // tm,),
        in_specs=[row(d), row(d), _mod_spec(gt, tm, tiles_per_group), _resident(g.shape),
                  _resident(wu.shape), _resident(wd.shape)],
        out_specs=row(d),
        compiler_params=_params("arbitrary"),
        name="mlp",
    )(h2, x1, gt, g, wu, wd)


def _lane_scan(x, lane, shifts):
    for s in shifts:
        x = x + jnp.where(lane >= s, pltpu.roll(x, s, axis=1), 0.0)
    return x


def _cumsum_kernel(x_ref, o_ref):
    rows, length = x_ref.shape[1], x_ref.shape[2]
    lane = lax.broadcasted_iota(jnp.int32, (rows, LANES), 1)
    carry = jnp.zeros((rows, 1), F32)
    for c in range(length // LANES):
        x = _lane_scan(x_ref[0, :, c * LANES:(c + 1) * LANES], lane, (1, 2, 4, 8, 16, 32, 64)) + carry
        o_ref[0, :, c * LANES:(c + 1) * LANES] = x
        carry = x[:, LANES - 1:LANES]


def _cumsum_lanes(x):
    b, r, length = x.shape
    spec = pl.BlockSpec((1, r, length), lambda i: (i, 0, 0))
    return pl.pallas_call(
        _cumsum_kernel,
        out_shape=jax.ShapeDtypeStruct(x.shape, F32),
        grid=(b,),
        in_specs=[spec],
        out_specs=spec,
        compiler_params=_params("arbitrary"),
        name="logf_cumsum",
    )(x)


def _aug_lane0(hh):
    return A_HEAD_DIM if hh == 0 else 0


def _attn_kernel(q_ref, kt_ref, v_ref, f_ref, o_ref, kaug_ref, vaug_ref, s_ref, *, blk_q, blk_k, chunk):
    pair = pl.program_id(1)
    qi = pl.program_id(2)
    length = v_ref.shape[1]
    lane = lax.broadcasted_iota(jnp.int32, (1, LANES), 1)
    own = (lane < A_HEAD_DIM, lane >= A_HEAD_DIM)
    feat = lax.broadcasted_iota(jnp.int32, (LANES, 1), 0)

    @pl.when(qi == 0)
    def _():
        def build(c, _):
            toks = pl.ds(pl.multiple_of(c * chunk, chunk), chunk)
            kt = kt_ref[0, :, toks].astype(BF16)
            v = v_ref[0, toks, :]
            for hh in range(2):
                a0 = _aug_lane0(hh)
                neg_f = f_ref[0, pl.ds(2 * pair + hh, 1), toks] * (-LOG2E)
                hi = neg_f.astype(BF16).astype(F32)
                mid = (neg_f - hi).astype(BF16).astype(F32)
                lo = (neg_f - hi - mid).astype(BF16).astype(F32)
                bias = jnp.where(feat == a0, hi, jnp.where(feat == a0 + 1, mid, jnp.where(feat == a0 + 2, lo, 0.0)))
                own_rows = (feat < A_HEAD_DIM) if hh == 0 else (feat >= A_HEAD_DIM)
                kaug_ref[hh, :, toks] = jnp.where(own_rows, kt, bias.astype(BF16))
                vaug_ref[hh, toks, :] = jnp.where(own[hh], v, jnp.where(lane == a0, 1.0, 0.0).astype(BF16))
            return 0
        lax.fori_loop(0, length // chunk, build, 0)

    q = q_ref[0]
    qh = []
    for hh in range(2):
        a0 = _aug_lane0(hh)
        ones = jnp.where((lane >= a0) & (lane < a0 + 3), 1.0, 0.0).astype(BF16)
        qh.append(jnp.where(own[hh], q, ones))

    def rows(j):
        return pl.ds(pl.multiple_of(j * blk_k, blk_k), blk_k)

    def scores(j, slot, r0=0):
        for hh in range(2):
            s_ref[slot, hh, r0:, :] = jnp.dot(qh[hh][r0:], kaug_ref[hh, :, rows(j)], preferred_element_type=F32)

    def fold(j, slot, carry, causal, r0=0):
        probs, stats = [], []
        for hh in range(2):
            m = carry[hh][0]
            sh = s_ref[slot, hh, r0:, :]
            if causal:
                row = lax.broadcasted_iota(jnp.int32, sh.shape, 0)
                col = lax.broadcasted_iota(jnp.int32, sh.shape, 1)
                sh = jnp.where(col <= row, sh, NEG_INF)
            m_new = jnp.maximum(m, jnp.max(sh, axis=1, keepdims=True))
            probs.append(jnp.exp2(sh - m_new).astype(BF16))
            stats.append((m_new, jnp.exp2(m - m_new)))
        return tuple(
            (stats[hh][0], stats[hh][1] * carry[hh][1]
             + jnp.dot(probs[hh], vaug_ref[hh, rows(j), :], preferred_element_type=F32))
            for hh in range(2))

    assert blk_q == 2 * blk_k
    init = tuple((jnp.full((blk_q, 1), NEG_INF, F32), jnp.zeros((blk_q, LANES), F32)) for _ in range(2))

    def body(p, carry):
        scores(2 * p + 1, 1)
        carry = fold(2 * p, 0, carry, False)
        scores(2 * p + 2, 0)
        return fold(2 * p + 1, 1, carry, False)

    scores(0, 0)
    carry = lax.fori_loop(0, qi, body, init)
    scores(2 * qi + 1, 1, blk_k)
    carry = fold(2 * qi, 0, carry, True)
    lower = fold(2 * qi + 1, 1, tuple((m[blk_k:], acc[blk_k:]) for m, acc in carry), True, blk_k)
    a0, a1 = (jnp.concatenate([carry[hh][1][:blk_k], lower[hh][1]], axis=0) for hh in range(2))
    o_ref[0] = jnp.where(own[0], a0 / a0[:, A_HEAD_DIM:A_HEAD_DIM + 1], a1 / a1[:, 0:1]).astype(BF16)


def _attention(q, kt, v, f, blk_q=1024, blk_k=512):
    b, length, width = q.shape
    pairs = width // LANES
    blk_q, blk_k = min(blk_q, length), min(blk_k, length)
    return pl.pallas_call(
        functools.partial(_attn_kernel, blk_q=blk_q, blk_k=blk_k, chunk=min(512, length)),
        out_shape=jax.ShapeDtypeStruct(q.shape, BF16),
        grid=(b, pairs, length // blk_q),
        in_specs=[pl.BlockSpec((1, blk_q, LANES), lambda i, p, t: (i, t, p)),
                  pl.BlockSpec((1, LANES, length), lambda i, p, t: (i, p, 0)),
                  pl.BlockSpec((1, length, LANES), lambda i, p, t: (i, 0, p)),
                  pl.BlockSpec((1,) + f.shape[1:], lambda i, p, t: (i, 0, 0))],
        out_specs=pl.BlockSpec((1, blk_q, LANES), lambda i, p, t: (i, t, p)),
        scratch_shapes=[pltpu.VMEM((2, LANES, length), BF16), pltpu.VMEM((2, length, LANES), BF16),
                        pltpu.VMEM((2, 2, blk_q, blk_k), F32)],
        compiler_params=_params("arbitrary", "arbitrary", "arbitrary"),
        name="fox_attention",
    )(q, kt, v, f)


def _conv_kernel(x_ref, prev_ref, init_ref, w_ref, b_ref, o_ref, xp_ref, *, tc):
    lane = lax.broadcasted_iota(jnp.int32, (1, x_ref.shape[2]), 1)
    k_scale = jnp.where(lane < B_WIDTH, 1.0, B_HEAD_DIM ** -0.5)
    for s in range(x_ref.shape[0]):
        xp_ref[s, 0:SUBLANES, :] = jnp.where(pl.program_id(1) == 0, init_ref[s], prev_ref[s])
        xp_ref[s, SUBLANES:SUBLANES + tc, :] = x_ref[s]
        y = b_ref[...]
        for j in range(CONV_W):
            off = SUBLANES - (CONV_W - 1) + j
            y = y + w_ref[j:j + 1, :] * xp_ref[s, off:off + tc, :]
        o_ref[s] = (y * jax.nn.sigmoid(y) * k_scale).astype(BF16)


def _conv(x, init, w, b, tc, nb=1):
    bsz, length, c = x.shape
    tpb = tc // SUBLANES
    return pl.pallas_call(
        functools.partial(_conv_kernel, tc=tc),
        out_shape=jax.ShapeDtypeStruct(x.shape, BF16),
        grid=(bsz // nb, length // tc),
        in_specs=[pl.BlockSpec((nb, tc, c), lambda i, t: (i, t, 0)),
                  pl.BlockSpec((nb, SUBLANES, c), lambda i, t: (i, jnp.maximum(t * tpb - 1, 0), 0)),
                  pl.BlockSpec((nb, SUBLANES, c), lambda i, t: (i, 0, 0)),
                  pl.BlockSpec((CONV_W, c), lambda i, t: (0, 0)),
                  pl.BlockSpec((1, c), lambda i, t: (0, 0))],
        out_specs=pl.BlockSpec((nb, tc, c), lambda i, t: (i, t, 0)),
        scratch_shapes=[pltpu.VMEM((nb, tc + SUBLANES, c), F32)],
        compiler_params=_params("arbitrary", "arbitrary"),
        name="short_conv",
    )(x, x, init, w, b)


_BNT = (((2,), (2,)), ((0,), (0,)))
_BNN = (((2,), (1,)), ((0,), (0,)))
_BTN = (((1,), (1,)), ((0,), (0,)))


def _mlstm_heads(q, k, v, ir, fr, ic, fc, c_st, n_st, m_st):
    n, t, _ = q.shape
    row = lax.broadcasted_iota(jnp.int32, (n, t, t), 1)
    col = lax.broadcasted_iota(jnp.int32, (n, t, t), 2)
    tril = col <= row
    bcum_c = jnp.sum(jnp.where(tril, fr, 0.0), axis=2, keepdims=True)
    bcum_r = jnp.sum(jnp.where(row <= col, fc, 0.0), axis=1, keepdims=True)
    dlog = jnp.where(tril, bcum_c - bcum_r + ir, NEG_INF)
    inter = bcum_c + m_st
    mt = jnp.maximum(inter, jnp.max(dlog, axis=2, keepdims=True))
    a = jnp.exp(dlog - mt) * lax.dot_general(q, k, _BNT, preferred_element_type=F32)
    si = jnp.exp(inter - mt)
    num = (lax.dot_general(a.astype(BF16), v, _BNN, preferred_element_type=F32)
           + si * lax.dot_general(q, c_st.astype(BF16), _BNN, preferred_element_type=F32))
    den = jnp.sum(a, axis=2, keepdims=True) + si * jnp.sum(q.astype(F32) * n_st, axis=2, keepdims=True)
    h = num / jnp.maximum(jnp.abs(den), jnp.exp(-mt))
    bl = bcum_r[:, :, t - 1:t]
    m_new = jnp.maximum(bl + m_st, jnp.max(bl - bcum_r + ir, axis=2, keepdims=True))
    kw = jnp.exp(bl - bcum_c + ic - m_new) * k.astype(F32)
    decay = jnp.exp(bl + m_st - m_new)
    c_new = decay * c_st + lax.dot_general(kw.astype(BF16), v, _BTN, preferred_element_type=F32)
    n_new = decay * n_st + jnp.sum(kw, axis=1, keepdims=True)
    return h, c_new, n_new, m_new


def _mlstm_kernel(q_ref, k_ref, v_ref, gc_ref, gr_ref, c0_ref, n0_ref, m0_ref,
                  h_ref, c_ref, n_ref, m_ref, *, g0):
    @pl.when(pl.program_id(1) == 0)
    def _():
        c_ref[...] = c0_ref[...]
        n_ref[...] = n0_ref[...]
        m_ref[...] = m0_ref[...]

    nb = q_ref.shape[0]
    where = [(s, hd, slice(hd * B_HEAD_DIM, (hd + 1) * B_HEAD_DIM)) for s in range(nb) for hd in range(B_HEADS)]
    heads = lambda ref: jnp.stack([ref[s, :, sl] for s, _, sl in where])
    gc = [gc_ref[s] for s in range(nb)]
    gr = [gr_ref[s] for s in range(nb)]
    gi, gf = g0, g0 + B_HEADS
    h, c_new, n_new, m_new = _mlstm_heads(
        heads(q_ref), heads(k_ref), heads(v_ref),
        jnp.stack([gr[s][gi + hd:gi + hd + 1, :] for s, hd, _ in where]),
        jnp.stack([gr[s][gf + hd:gf + hd + 1, :] for s, hd, _ in where]),
        jnp.stack([gc[s][:, gi + hd:gi + hd + 1] for s, hd, _ in where]),
        jnp.stack([gc[s][:, gf + hd:gf + hd + 1] for s, hd, _ in where]),
        c_ref[...].reshape((nb * B_HEADS,) + c_ref.shape[2:]),
        n_ref[...].reshape((nb * B_HEADS,) + n_ref.shape[2:]),
        m_ref[...].reshape((nb * B_HEADS,) + m_ref.shape[2:]))
    for i, (s, _, sl) in enumerate(where):
        h_ref[s, :, sl] = h[i].astype(BF16)
    c_ref[...] = c_new.reshape(c_ref.shape)
    n_ref[...] = n_new.reshape(n_ref.shape)
    m_ref[...] = m_new.reshape(m_ref.shape)


def _mlstm(qk, v, gcol, grow, c0, n0, m0, t, nb=1, g0=0):
    b, length, _ = v.shape
    st = lambda a: pl.BlockSpec((nb,) + a.shape[1:], lambda i, j: (i, 0, 0, 0))
    return pl.pallas_call(
        functools.partial(_mlstm_kernel, g0=g0),
        out_shape=[jax.ShapeDtypeStruct(v.shape, BF16), jax.ShapeDtypeStruct(c0.shape, F32),
                   jax.ShapeDtypeStruct(n0.shape, F32), jax.ShapeDtypeStruct(m0.shape, F32)],
        grid=(b // nb, length // t),
        in_specs=[pl.BlockSpec((nb, t, B_WIDTH), lambda i, j: (i, j, 0)),
                  pl.BlockSpec((nb, t, B_WIDTH), lambda i, j: (i, j, 1)),
                  pl.BlockSpec((nb, t, B_WIDTH), lambda i, j: (i, j, 0)),
                  pl.BlockSpec((nb, t, gcol.shape[2]), lambda i, j: (i, j, 0)),
                  pl.BlockSpec((nb, grow.shape[1], t), lambda i, j: (i, 0, j)),
                  st(c0), st(n0), st(m0)],
        out_specs=[pl.BlockSpec((nb, t, B_WIDTH), lambda i, j: (i, j, 0)), st(c0), st(n0), st(m0)],
        compiler_params=_params("arbitrary", "arbitrary"),
        name="mlstm",
    )(qk, qk, v, gcol, grow, c0, n0, m0)


def _sattn_kernel(pt_ref, q_ref, kn_ref, vn_ref, lfn_ref, *rest, n_pages):
    del pt_ref
    k_refs = rest[:n_pages]
    v_refs = rest[n_pages:2 * n_pages]
    lf_refs = rest[2 * n_pages:3 * n_pages]
    o_ref = rest[3 * n_pages]
    q = q_ref[0]
    nrow, width = q.shape
    nq = nrow // A_HEADS
    lane = lax.broadcasted_iota(jnp.int32, (A_HEADS, LANES), 1)
    per_row = lambda f: jnp.concatenate([f] * nq, axis=0)

    offset = jnp.zeros((A_HEADS, 1), F32)
    scores = []
    for j in range(n_pages):
        f = _lane_scan(lf_refs[j][...], lane, (1, 2, 4, 8, 16, 32, 64)) + offset
        offset = f[:, LANES - 1:LANES]
        kt = k_refs[j][...].reshape(width, PAGE_SIZE).astype(BF16)
        scores.append(jnp.dot(q, kt, preferred_element_type=F32) - per_row(f * LOG2E))
    n_new = kn_ref.shape[1]
    fnew = (_lane_scan(lfn_ref[0], lane, (1, 2, 4, 8, 16, 32, 64)) + offset)[:, :n_new]
    s_new = lax.dot_general(q, kn_ref[0], _NT, preferred_element_type=F32) - per_row(fnew * LOG2E)
    q_tok = lax.broadcasted_iota(jnp.int32, (nrow, n_new), 0) // A_HEADS
    k_tok = lax.broadcasted_iota(jnp.int32, (nrow, n_new), 1)
    s_new = jnp.where(k_tok <= q_tok, s_new, NEG_INF)

    m = jnp.maximum(jnp.max(functools.reduce(jnp.maximum, scores), axis=1, keepdims=True),
                    jnp.max(s_new, axis=1, keepdims=True))
    probs = [jnp.exp2(s - m) for s in scores]
    p_new = jnp.exp2(s_new - m)
    l = (jnp.sum(functools.reduce(jnp.add, probs), axis=1, keepdims=True)
         + jnp.sum(p_new, axis=1, keepdims=True))
    acc = jnp.dot(p_new.astype(BF16), vn_ref[0], preferred_element_type=F32)
    for j in range(n_pages):
        vt = v_refs[j][...].reshape(width, PAGE_SIZE).astype(BF16)
        acc = acc + lax.dot_general(probs[j].astype(BF16), vt, _NT, preferred_element_type=F32)
    row_head = lax.broadcasted_iota(jnp.int32, acc.shape, 0) % A_HEADS
    col_head = lax.broadcasted_iota(jnp.int32, acc.shape, 1) // A_HEAD_DIM
    out = jnp.where(row_head == col_head, acc / l, 0.0)
    o_ref[0] = jnp.sum(out.reshape(nq, A_HEADS, width), axis=1)


def _sample_attention(page_table, q, kn, vn, lfn, cache_kt, cache_vt, cache_lf):
    b, n_pages = page_table.shape
    nq = q.shape[1] // A_HEADS
    page = lambda j: pl.BlockSpec((None, A_HEADS, A_HEAD_DIM, PAGE_SIZE), lambda i, pt: (pt[i, j], 0, 0, 0))
    lfpage = lambda j: pl.BlockSpec((None, A_HEADS, PAGE_SIZE), lambda i, pt: (pt[i, j], 0, 0))
    per_seq = lambda a: pl.BlockSpec((1,) + a.shape[1:], lambda i, pt: (i, 0, 0))
    grid_spec = pltpu.PrefetchScalarGridSpec(
        num_scalar_prefetch=1,
        grid=(b,),
        in_specs=([per_seq(q), per_seq(kn), per_seq(vn), per_seq(lfn)]
                  + [page(j) for j in range(n_pages)] + [page(j) for j in range(n_pages)]
                  + [lfpage(j) for j in range(n_pages)]),
        out_specs=pl.BlockSpec((1, nq, A_WIDTH), lambda i, pt: (i, 0, 0)),
    )
    return pl.pallas_call(
        functools.partial(_sattn_kernel, n_pages=n_pages),
        out_shape=jax.ShapeDtypeStruct((b, nq, A_WIDTH), F32),
        grid_spec=grid_spec,
        compiler_params=_params("arbitrary"),
        name="paged_fox_attention",
    )(page_table, q, kn, vn, lfn, *([cache_kt] * n_pages), *([cache_vt] * n_pages), *([cache_lf] * n_pages))


def _split_weights(w_in, b_fox_f, b_ml_i, b_ml_f):
    d = w_in.shape[0]
    o = 0
    parts = {}
    for name, n in (("aq", A_WIDTH), ("ak", A_WIDTH), ("av", A_WIDTH), ("af", A_HEADS), ("bq", B_WIDTH),
                    ("bk", B_WIDTH), ("bv", B_WIDTH), ("bi", B_HEADS), ("bf", B_HEADS), ("bo", B_WIDTH),
                    ("ga", d), ("gb", d)):
        parts[name] = w_in[:, o:o + n]
        o += n
    assert o == w_in.shape[1]
    small = jnp.concatenate([parts["af"], parts["bi"], parts["bf"]], axis=1)
    bias = jnp.concatenate([b_fox_f, b_ml_i, b_ml_f])
    bf = lambda a: a.astype(BF16)
    return {
        "q": bf(parts["aq"] * (A_HEAD_DIM ** -0.5 * LOG2E)),
        "k": bf(parts["ak"]), "kt": bf(parts["ak"].T),
        "v": bf(parts["av"]),
        "qk": bf(jnp.concatenate([parts["bq"], parts["bk"]], axis=1)),
        "bv": bf(parts["bv"]),
        "g": bf(jnp.concatenate([parts["bo"], parts["ga"], parts["gb"]], axis=1)),
        "s": bf(jnp.pad(small, ((0, 0), (0, LANES - N_GATES)))),
        "bs": jnp.pad(bias, (0, LANES - N_GATES)).reshape(1, LANES),
        "st": bf(jnp.pad(small.T, ((0, GATE_ROWS - N_GATES), (0, 0)))),
        "bst": jnp.pad(bias, (0, GATE_ROWS - N_GATES)).reshape(GATE_ROWS, 1),
    }


def kernel(x_prompt, x_sample, cache_k, cache_v, cache_logf, page_table, state_C, state_n, state_m, state_conv,
           c_prompt, c_sample, w_ada, b_ada, g_pre_mix, g_post_mix, g_pre_mlp, g_post_mlp, w_in, b_fox_f,
           b_ml_i, b_ml_f, conv_w, conv_b, w_proj_a, w_proj_b, w_out, w_up, w_down):
    assert w_in.shape[0] == 1, "one trunk layer"
    bsz, seq, d = x_prompt.shape
    dbsz, dseq, _ = x_sample.shape

    w_proj = _split_weights(w_in[0], b_fox_f[0], b_ml_i[0], b_ml_f[0])
    wpa, wpb, wo = w_proj_a[0].astype(BF16), w_proj_b[0].astype(BF16), w_out[0].astype(BF16)
    wu, wd = w_up[0].astype(BF16), w_down[0].astype(BF16)
    g1, g2, g3, g4 = g_pre_mix, g_post_mix, g_pre_mlp, g_post_mlp
    cw, cb = conv_w[0], conv_b

    n_c = bsz + dbsz
    c_all = jnp.concatenate([c_prompt, c_sample, jnp.zeros((-n_c % SUBLANES, d), F32)], axis=0)
    mod = _adaln(c_all, w_ada[0], b_ada)

    def token_path(x2, mods, tm, tiles_per_group, token_major_k):
        sh_m, sc_m, gt_m, sh_f, sc_f, gt_f = mods
        outs = _in_proj(x2, sc_m, sh_m, g1, w_proj, tm, tiles_per_group, token_major_k)
        return outs, (gt_m, sc_f, sh_f, gt_f)

    def head_split(kt):
        g_, _, n_ = kt.shape
        return jnp.transpose(kt.reshape(g_, A_HEADS, A_HEAD_DIM, n_), (0, 3, 1, 2))[None]

    def finish(x2, ya, hb, og, rest, tm, tiles_per_group):
        gt_m, sc_f, sh_f, gt_f = rest
        x1, h2 = _mix_out(ya, hb, og, x2, gt_m, sc_f, sh_f, g2, g3, wpa, wpb, wo, tm, tiles_per_group)
        return _mlp(h2, x1, gt_f, g4, wu, wd, tm, tiles_per_group)

    tm = 256
    m_p = bsz * seq
    x2 = x_prompt.reshape(m_p, d)
    mods = tuple(t.reshape(bsz, 1, d) for t in jnp.split(mod[:bsz], N_MOD, axis=-1))
    (qa, kt, vt, va, qkb, vb, og, sm, smt), rest = token_path(x2, mods, tm, seq // tm, False)

    r3 = lambda a: a.reshape(bsz, seq, a.shape[-1])
    ya = _attention(r3(qa), kt, r3(va), _cumsum_lanes(smt)).reshape(m_p, A_WIDTH)

    qkc = _conv(r3(qkb), jnp.zeros((bsz, SUBLANES, 2 * B_WIDTH), F32), cw, cb, tc=512)
    hb, c_p, n_p, m_pr = _mlstm(qkc, r3(vb), r3(sm), smt,
                                jnp.zeros((bsz, B_HEADS, B_HEAD_DIM, B_HEAD_DIM), F32),
                                jnp.zeros((bsz, B_HEADS, 1, B_HEAD_DIM), F32),
                                jnp.zeros((bsz, B_HEADS, 1, 1), F32), t=128, nb=2, g0=A_HEADS)
    y_prompt = finish(x2, ya, hb.reshape(m_p, B_WIDTH), og, rest, tm, seq // tm).reshape(bsz, seq, d)

    new_k_prompt = head_split(kt)
    new_v_prompt = head_split(vt)
    new_logf_prompt = jnp.transpose(smt[:, :A_HEADS, :], (0, 2, 1))[None]
    new_conv_prompt = r3(qkb)[:, seq - (CONV_W - 1):, :][None]

    m_s = dbsz * dseq
    xs2 = x_sample.reshape(m_s, d)
    mods = tuple(t.reshape(1, m_s, d) for t in jnp.split(jnp.repeat(mod[bsz:n_c], dseq, axis=0), N_MOD, axis=-1))
    (qa, kt, vt, va, qkb, vb, og, sm, smt, ka), rest = token_path(xs2, mods, m_s, 1, True)

    n_new = 16
    new_rows = lambda a: jnp.pad(a.reshape(dbsz, dseq, -1), ((0, 0), (0, n_new - dseq), (0, 0)))
    lf_s = sm[:, :A_HEADS].reshape(dbsz, dseq, A_HEADS)
    lf_new = jnp.pad(jnp.transpose(lf_s, (0, 2, 1)), ((0, 0), (0, 0), (0, LANES - dseq)))
    q_heads = qa.reshape(dbsz, dseq, A_HEADS, 1, A_HEAD_DIM)
    q_bd = (q_heads * jnp.eye(A_HEADS, dtype=BF16)[None, None, :, :, None]).reshape(dbsz, dseq * A_HEADS, A_WIDTH)
    ya = _sample_attention(
        page_table, q_bd, new_rows(ka), new_rows(va), lf_new,
        jnp.transpose(cache_k[0], (0, 2, 3, 1)), jnp.transpose(cache_v[0], (0, 2, 3, 1)),
        jnp.transpose(cache_logf[0], (0, 2, 1)))
    ya = ya.astype(BF16).reshape(m_s, A_WIDTH)

    t_s = 16
    conv_in = jnp.pad(qkb.reshape(dbsz, dseq, 2 * B_WIDTH), ((0, 0), (0, SUBLANES - dseq), (0, 0)))
    conv_init = jnp.pad(state_conv[0], ((0, 0), (SUBLANES - (CONV_W - 1), 0), (0, 0)))
    qkc = _conv(conv_in, conv_init, cw, cb, tc=SUBLANES, nb=min(16, dbsz))
    tok_valid = (jnp.arange(t_s) < dseq)[None, :, None]
    pad_t = lambda a: jnp.pad(a, ((0, 0), (0, t_s - a.shape[1]), (0, 0)))
    qkc = jnp.where(tok_valid, pad_t(qkc), jnp.zeros((), BF16))
    vb_s = pad_t(vb.reshape(dbsz, dseq, B_WIDTH))
    gates = pad_t(sm[:, A_HEADS:A_HEADS + 2 * B_HEADS].reshape(dbsz, dseq, 2 * B_HEADS))
    neutral = jnp.concatenate([jnp.full((B_HEADS,), NEG_INF, F32), jnp.zeros((B_HEADS,), F32)])
    gcol = jnp.where(tok_valid, gates, neutral)
    grow = jnp.transpose(gcol, (0, 2, 1))
    hb, c_s, n_s, m_sm = _mlstm(qkc, vb_s, gcol, grow, state_C[0], state_n[0][:, :, None, :],
                                state_m[0][:, :, None, None], t=t_s, nb=8)
    hb = hb[:, :dseq, :].reshape(m_s, B_WIDTH)
    y_sample = finish(xs2, ya, hb, og, rest, m_s, 1).reshape(dbsz, dseq, d)

    new_k_sample = head_split(kt).reshape(1, dbsz, dseq, A_HEADS, A_HEAD_DIM)
    new_v_sample = head_split(vt).reshape(1, dbsz, dseq, A_HEADS, A_HEAD_DIM)
    new_logf_sample = lf_s[None]
    qkb3 = qkb.reshape(dbsz, dseq, 2 * B_WIDTH)
    new_conv_sample = jnp.concatenate([state_conv[0], qkb3], axis=1)[:, dseq:, :][None]

    return (y_prompt, y_sample, new_k_prompt, new_v_prompt, new_logf_prompt,
            c_p[None], n_p[:, :, 0, :][None], m_pr[:, :, 0, 0][None], new_conv_prompt,
            new_k_sample, new_v_sample, new_logf_sample,
            c_s[None], n_s[:, :, 0, :][None], m_sm[:, :, 0, 0][None], new_conv_sample)
```

```python
import functools

import jax
import jax.numpy as jnp
from jax import lax
from jax.experimental import pallas as pl
from jax.experimental.pallas import tpu as pltpu

F32 = jnp.float32
BF16 = jnp.bfloat16

A_HEADS = 8
A_HEAD_DIM = 64
A_WIDTH = A_HEADS * A_HEAD_DIM
B_HEADS = 4
B_HEAD_DIM = 128
B_WIDTH = B_HEADS * B_HEAD_DIM
CONV_W = 4
N_MOD = 6
RMS_EPS = 1e-6
NEG_INF = -1e30
PAGE_SIZE = 128
LOG2E = 1.4426950408889634

LANES = 128
SUBLANES = 8
VMEM_LIMIT_BYTES = 56 * 1024 * 1024

_NT = (((1,), (1,)), ((), ()))
_TN = (((0,), (0,)), ((), ()))


def _params(*sem):
    return pltpu.CompilerParams(dimension_semantics=sem, vmem_limit_bytes=VMEM_LIMIT_BYTES)


def _resident(shape):
    nd = len(shape)
    return pl.BlockSpec(shape, lambda *_: (0,) * nd, pipeline_mode=pl.Buffered(1))


def _rms(x, g):
    r = lax.rsqrt(jnp.mean(x * x, axis=-1, keepdims=True) + RMS_EPS)
    return (x * r) * g


def _log_sigmoid(x):
    return jnp.minimum(x, 0.0) - jnp.log1p(jnp.exp(-jnp.abs(x)))


def _mod_kernel(c_ref, w_ref, b_ref, o_ref):
    c = c_ref[...]
    a = (c * jax.nn.sigmoid(c)).astype(BF16)
    o_ref[...] = jnp.dot(a, w_ref[...].astype(BF16), preferred_element_type=F32) + b_ref[...]


def _adaln(c, w, b, tn=1024):
    r, d = c.shape
    n = w.shape[1]
    return pl.pallas_call(
        _mod_kernel,
        out_shape=jax.ShapeDtypeStruct((r, n), F32),
        grid=(n // tn,),
        in_specs=[pl.BlockSpec((r, d), lambda j: (0, 0)),
                  pl.BlockSpec((d, tn), lambda j: (0, j)),
                  pl.BlockSpec((1, tn), lambda j: (0, j))],
        out_specs=pl.BlockSpec((r, tn), lambda j: (0, j)),
        compiler_params=_params("arbitrary"),
        name="adaln",
    )(c, w, b)


def _mod_spec(arr, tm, tiles_per_group):
    g, r, d = arr.shape
    if r == 1:
        return pl.BlockSpec((1, 1, d), lambda i: (i // tiles_per_group, 0, 0))
    assert g == 1
    return pl.BlockSpec((1, tm, d), lambda i: (0, i, 0))


N_GATES = A_HEADS + 2 * B_HEADS
GATE_ROWS = 16


def _gate_act(z, idx):
    return jnp.where((idx >= A_HEADS) & (idx < A_HEADS + B_HEADS), z, _log_sigmoid(z))


def _in_kernel(x_ref, sc_ref, sh_ref, g_ref, wq_ref, wkt_ref, wv_ref, wqk_ref, wbv_ref, wg_ref,
               ws_ref, bs_ref, wst_ref, bst_ref, *rest, token_major_k):
    if token_major_k:
        wk_ref, *rest = rest
    qa_ref, kt_ref, vt_ref, va_ref, qkb_ref, vb_ref, og_ref, sm_ref, smt_ref, *rest = rest
    if token_major_k:
        ka_ref, = rest
    h = _rms(x_ref[...], g_ref[...]) * (1.0 + sc_ref[0]) + sh_ref[0]
    hb = h.astype(BF16)
    qa_ref[...] = jnp.dot(hb, wq_ref[...], preferred_element_type=F32).astype(BF16)
    kt_ref[0] = lax.dot_general(wkt_ref[...], hb, _NT, preferred_element_type=F32)
    v = jnp.dot(hb, wv_ref[...], preferred_element_type=F32)
    vt_ref[0] = v.T
    va_ref[...] = v.astype(BF16)
    if token_major_k:
        ka_ref[...] = jnp.dot(hb, wk_ref[...], preferred_element_type=F32).astype(BF16)
    qkb_ref[...] = jnp.dot(hb, wqk_ref[...], preferred_element_type=F32)
    vb_ref[...] = jnp.dot(hb, wbv_ref[...], preferred_element_type=F32).astype(BF16)
    og_ref[...] = jax.nn.sigmoid(jnp.dot(hb, wg_ref[...], preferred_element_type=F32)).astype(BF16)
    zs = jnp.dot(hb, ws_ref[...], preferred_element_type=F32) + bs_ref[...]
    sm_ref[...] = _gate_act(zs, lax.broadcasted_iota(jnp.int32, zs.shape, 1))
    zt = lax.dot_general(wst_ref[...], hb, _NT, preferred_element_type=F32) + bst_ref[...]
    smt_ref[0] = _gate_act(zt, lax.broadcasted_iota(jnp.int32, zt.shape, 0))


def _in_proj(x, sc, sh, g, w, tm, tiles_per_group, token_major_k):
    m, d = x.shape
    group_tokens = tm * tiles_per_group
    groups = m // group_tokens
    row = lambda n: pl.BlockSpec((tm, n), lambda i: (i, 0))
    col = lambda n: pl.BlockSpec((1, n, tm), lambda i: (i // tiles_per_group, 0, i % tiles_per_group))
    rows_out = lambda n, dt: (jax.ShapeDtypeStruct((m, n), dt), row(n))
    cols_out = lambda n: (jax.ShapeDtypeStruct((groups, n, group_tokens), F32), col(n))
    outs = [rows_out(A_WIDTH, BF16), cols_out(A_WIDTH), cols_out(A_WIDTH), rows_out(A_WIDTH, BF16),
            rows_out(2 * B_WIDTH, F32), rows_out(B_WIDTH, BF16), rows_out(w["g"].shape[1], BF16),
            rows_out(LANES, F32), cols_out(GATE_ROWS)]
    weights = [w["q"], w["kt"], w["v"], w["qk"], w["bv"], w["g"], w["s"], w["bs"], w["st"], w["bst"]]
    if token_major_k:
        weights.append(w["k"])
        outs.append(rows_out(A_WIDTH, BF16))
    return pl.pallas_call(
        functools.partial(_in_kernel, token_major_k=token_major_k),
        out_shape=[o[0] for o in outs],
        grid=(m // tm,),
        in_specs=[row(d), _mod_spec(sc, tm, tiles_per_group), _mod_spec(sh, tm, tiles_per_group),
                  _resident(g.shape)] + [_resident(a.shape) for a in weights],
        out_specs=[o[1] for o in outs],
        compiler_params=_params("arbitrary"),
        name="in_proj",
    )(x, sc, sh, g, *weights)


MIX_PARTS = 2


def _mix_kernel(ya_ref, hb_ref, og_ref, x_ref, gt_ref, sc_ref, sh_ref, g1_ref, g2_ref,
                wpa_ref, wpb_ref, wo_ref, x1_ref, h2_ref):
    tm, d = x_ref.shape
    parts = [slice(r, r + tm // MIX_PARTS) for r in range(0, tm, tm // MIX_PARTS)]
    mod = lambda ref, p: ref[0] if ref.shape[1] == 1 else ref[0, p, :]
    ya = [jnp.dot(ya_ref[p, :], wpa_ref[...], preferred_element_type=F32) for p in parts]
    yb = [jnp.dot((og_ref[p, :B_WIDTH].astype(F32) * hb_ref[p, :].astype(F32)).astype(BF16), wpb_ref[...],
                  preferred_element_type=F32) for p in parts]
    yo = []
    for p, a, b in zip(parts, ya, yb):
        y = og_ref[p, B_WIDTH:B_WIDTH + d].astype(F32) * a + og_ref[p, B_WIDTH + d:].astype(F32) * b
        yo.append(jnp.dot(y.astype(BF16), wo_ref[...], preferred_element_type=F32))
    for p, o in zip(parts, yo):
        x1 = x_ref[p, :] + mod(gt_ref, p) * _rms(o, g1_ref[...])
        x1_ref[p, :] = x1
        h2_ref[p, :] = (_rms(x1, g2_ref[...]) * (1.0 + mod(sc_ref, p)) + mod(sh_ref, p)).astype(BF16)


def _mix_out(ya, hb, og, x, gt, sc, sh, g1, g2, wpa, wpb, wo, tm, tiles_per_group):
    m, d = x.shape
    row = lambda n: pl.BlockSpec((tm, n), lambda i: (i, 0))
    ms = lambda a: _mod_spec(a, tm, tiles_per_group)
    return pl.pallas_call(
        _mix_kernel,
        out_shape=[jax.ShapeDtypeStruct((m, d), F32), jax.ShapeDtypeStruct((m, d), BF16)],
        grid=(m // tm,),
        in_specs=[row(ya.shape[1]), row(hb.shape[1]), row(og.shape[1]), row(d), ms(gt), ms(sc), ms(sh),
                  _resident(g1.shape), _resident(g2.shape), _resident(wpa.shape), _resident(wpb.shape),
                  _resident(wo.shape)],
        out_specs=[row(d), row(d)],
        compiler_params=_params("arbitrary"),
        name="mix_out",
    )(ya, hb, og, x, gt, sc, sh, g1, g2, wpa, wpb, wo)


def _mlp_kernel(h2_ref, x1_ref, gt_ref, g_ref, wu_ref, wd_ref, y_ref, *, fc):
    h2 = h2_ref[...]
    acc = jnp.zeros(x1_ref.shape, F32)
    for c in range(wu_ref.shape[1] // fc):
        u = jnp.dot(h2, wu_ref[:, c * fc:(c + 1) * fc], preferred_element_type=F32)
        u = jnp.square(jnp.maximum(u, 0.0)).astype(BF16)
        acc = acc + jnp.dot(u, wd_ref[c * fc:(c + 1) * fc, :], preferred_element_type=F32)
    y_ref[...] = x1_ref[...] + gt_ref[0] * _rms(acc, g_ref[...])


def _mlp(h2, x1, gt, g, wu, wd, tm, tiles_per_group, fc=1024):
    m, d = x1.shape
    row = lambda n: pl.BlockSpec((tm, n), lambda i: (i, 0))
    return pl.pallas_call(
        functools.partial(_mlp_kernel, fc=fc),
        out_shape=jax.ShapeDtypeStruct((m, d), F32),
        grid=(m // tm,),
        in_specs=[row(d), row(d), _mod_spec(gt, tm, tiles_per_group), _resident(g.shape),
                  _resident(wu.shape), _resident(wd.shape)],
        out_specs=row(d),
        compiler_params=_params("arbitrary"),
        name="mlp",
    )(h2, x1, gt, g, wu, wd)


def _lane_scan(x, lane, shifts):
    for s in shifts:
        x = x + jnp.where(lane >= s, pltpu.roll(x, s, axis=1), 0.0)
    return x


def _cumsum_kernel(x_ref, o_ref):
    rows, length = x_ref.shape[1], x_ref.shape[2]
    lane = lax.broadcasted_iota(jnp.int32, (rows, LANES), 1)
    carry = jnp.zeros((rows, 1), F32)
    for c in range(length // LANES):
        x = _lane_scan(x_ref[0, :, c * LANES:(c + 1) * LANES], lane, (1, 2, 4, 8, 16, 32, 64)) + carry
        o_ref[0, :, c * LANES:(c + 1) * LANES] = x
        carry = x[:, LANES - 1:LANES]


def _cumsum_lanes(x):
    b, r, length = x.shape
    spec = pl.BlockSpec((1, r, length), lambda i: (i, 0, 0))
    return pl.pallas_call(
        _cumsum_kernel,
        out_shape=jax.ShapeDtypeStruct(x.shape, F32),
        grid=(b,),
        in_specs=[spec],
        out_specs=spec,
        compiler_params=_params("arbitrary"),
        name="logf_cumsum",
    )(x)


def _aug_lane0(hh):
    return A_HEAD_DIM if hh == 0 else 0


def _attn_kernel(q_ref, kt_ref, v_ref, f_ref, o_ref, kaug_ref, vaug_ref, s_ref, *, blk_q, blk_k, chunk):
    pair = pl.program_id(1)
    qi = pl.program_id(2)
    length = v_ref.shape[1]
    lane = lax.broadcasted_iota(jnp.int32, (1, LANES), 1)
    own = (lane < A_HEAD_DIM, lane >= A_HEAD_DIM)
    feat = lax.broadcasted_iota(jnp.int32, (LANES, 1), 0)

    @pl.when(qi == 0)
    def _():
        def build(c, _):
            toks = pl.ds(pl.multiple_of(c * chunk, chunk), chunk)
            kt = kt_ref[0, :, toks].astype(BF16)
            v = v_ref[0, toks, :]
            for hh in range(2):
                a0 = _aug_lane0(hh)
                neg_f = f_ref[0, pl.ds(2 * pair + hh, 1), toks] * (-LOG2E)
                hi = neg_f.astype(BF16).astype(F32)
                mid = (neg_f - hi).astype(BF16).astype(F32)
                lo = (neg_f - hi - mid).astype(BF16).astype(F32)
                bias = jnp.where(feat == a0, hi, jnp.where(feat == a0 + 1, mid, jnp.where(feat == a0 + 2, lo, 0.0)))
                own_rows = (feat < A_HEAD_DIM) if hh == 0 else (feat >= A_HEAD_DIM)
                kaug_ref[hh, :, toks] = jnp.where(own_rows, kt, bias.astype(BF16))
                vaug_ref[hh, toks, :] = jnp.where(own[hh], v, jnp.where(lane == a0, 1.0, 0.0).astype(BF16))
            return 0
        lax.fori_loop(0, length // chunk, build, 0)

    q = q_ref[0]
    qh = []
    for hh in range(2):
        a0 = _aug_lane0(hh)
        ones = jnp.where((lane >= a0) & (lane < a0 + 3), 1.0, 0.0).astype(BF16)
        qh.append(jnp.where(own[hh], q, ones))

    def rows(j):
        return pl.ds(pl.multiple_of(j * blk_k, blk_k), blk_k)

    def scores(j, slot, r0=0):
        for hh in range(2):
            s_ref[slot, hh, r0:, :] = jnp.dot(qh[hh][r0:], kaug_ref[hh, :, rows(j)], preferred_element_type=F32)

    def fold(j, slot, carry, causal, r0=0):
        probs, stats = [], []
        for hh in range(2):
            m = carry[hh][0]
            sh = s_ref[slot, hh, r0:, :]
            if causal:
                row = lax.broadcasted_iota(jnp.int32, sh.shape, 0)
                col = lax.broadcasted_iota(jnp.int32, sh.shape, 1)
                sh = jnp.where(col <= row, sh, NEG_INF)
            m_new = jnp.maximum(m, jnp.max(sh, axis=1, keepdims=True))
            probs.append(jnp.exp2(sh - m_new).astype(BF16))
            stats.append((m_new, jnp.exp2(m - m_new)))
        return tuple(
            (stats[hh][0], stats[hh][1] * carry[hh][1]
             + jnp.dot(probs[hh], vaug_ref[hh, rows(j), :], preferred_element_type=F32))
            for hh in range(2))

    assert blk_q == 2 * blk_k
    init = tuple((jnp.full((blk_q, 1), NEG_INF, F32), jnp.zeros((blk_q, LANES), F32)) for _ in range(2))

    def body(p, carry):
        scores(2 * p + 1, 1)
        carry = fold(2 * p, 0, carry, False)
        scores(2 * p + 2, 0)
        return fold(2 * p + 1, 1, carry, False)

    scores(0, 0)
    carry = lax.fori_loop(0, qi, body, init)
    scores(2 * qi + 1, 1, blk_k)
    carry = fold(2 * qi, 0, carry, True)
    lower = fold(2 * qi + 1, 1, tuple((m[blk_k:], acc[blk_k:]) for m, acc in carry), True, blk_k)
    a0, a1 = (jnp.concatenate([carry[hh][1][:blk_k], lower[hh][1]], axis=0) for hh in range(2))
    o_ref[0] = jnp.where(own[0], a0 / a0[:, A_HEAD_DIM:A_HEAD_DIM + 1], a1 / a1[:, 0:1]).astype(BF16)


def _attention(q, kt, v, f, blk_q=1024, blk_k=512):
    b, length, width = q.shape
    pairs = width // LANES
    blk_q, blk_k = min(blk_q, length), min(blk_k, length)
    return pl.pallas_call(
        functools.partial(_attn_kernel, blk_q=blk_q, blk_k=blk_k, chunk=min(512, length)),
        out_shape=jax.ShapeDtypeStruct(q.shape, BF16),
        grid=(b, pairs, length // blk_q),
        in_specs=[pl.BlockSpec((1, blk_q, LANES), lambda i, p, t: (i, t, p)),
                  pl.BlockSpec((1, LANES, length), lambda i, p, t: (i, p, 0)),
                  pl.BlockSpec((1, length, LANES), lambda i, p, t: (i, 0, p)),
                  pl.BlockSpec((1,) + f.shape[1:], lambda i, p, t: (i, 0, 0))],
        out_specs=pl.BlockSpec((1, blk_q, LANES), lambda i, p, t: (i, t, p)),
        scratch_shapes=[pltpu.VMEM((2, LANES, length), BF16), pltpu.VMEM((2, length, LANES), BF16),
                        pltpu.VMEM((2, 2, blk_q, blk_k), F32)],
        compiler_params=_params("arbitrary", "arbitrary", "arbitrary"),
        name="fox_attention",
    )(q, kt, v, f)


def _conv_silu(xp_ref, w_ref, b_ref, tc):
    y = b_ref[...]
    for j in range(CONV_W):
        off = SUBLANES - (CONV_W - 1) + j
        y = y + w_ref[j:j + 1, :] * xp_ref[off:off + tc, :]
    lane = lax.broadcasted_iota(jnp.int32, (1, y.shape[1]), 1)
    return y * jax.nn.sigmoid(y) * jnp.where(lane < B_WIDTH, 1.0, B_HEAD_DIM ** -0.5)


def _conv_kernel(x_ref, prev_ref, init_ref, w_ref, b_ref, o_ref, xp_ref, *, tc):
    for s in range(x_ref.shape[0]):
        xp_ref[s, 0:SUBLANES, :] = jnp.where(pl.program_id(1) == 0, init_ref[s], prev_ref[s])
        xp_ref[s, SUBLANES:SUBLANES + tc, :] = x_ref[s]
        o_ref[s] = _conv_silu(xp_ref.at[s], w_ref, b_ref, tc).astype(BF16)


def _conv(x, init, w, b, tc, nb=1):
    bsz, length, c = x.shape
    tpb = tc // SUBLANES
    return pl.pallas_call(
        functools.partial(_conv_kernel, tc=tc),
        out_shape=jax.ShapeDtypeStruct(x.shape, BF16),
        grid=(bsz // nb, length // tc),
        in_specs=[pl.BlockSpec((nb, tc, c), lambda i, t: (i, t, 0)),
                  pl.BlockSpec((nb, SUBLANES, c), lambda i, t: (i, jnp.maximum(t * tpb - 1, 0), 0)),
                  pl.BlockSpec((nb, SUBLANES, c), lambda i, t: (i, 0, 0)),
                  pl.BlockSpec((CONV_W, c), lambda i, t: (0, 0)),
                  pl.BlockSpec((1, c), lambda i, t: (0, 0))],
        out_specs=pl.BlockSpec((nb, tc, c), lambda i, t: (i, t, 0)),
        scratch_shapes=[pltpu.VMEM((nb, tc + SUBLANES, c), F32)],
        compiler_params=_params("arbitrary", "arbitrary"),
        name="short_conv",
    )(x, x, init, w, b)


_BNT = (((2,), (2,)), ((0,), (0,)))
_BNN = (((2,), (1,)), ((0,), (0,)))
_BTN = (((1,), (1,)), ((0,), (0,)))


def _mlstm_heads(q, k, v, ir, fr, ic, fc, c_st, n_st, m_st):
    n, t, _ = q.shape
    row = lax.broadcasted_iota(jnp.int32, (n, t, t), 1)
    col = lax.broadcasted_iota(jnp.int32, (n, t, t), 2)
    tril = col <= row
    bcum_c = jnp.sum(jnp.where(tril, fr, 0.0), axis=2, keepdims=True)
    bcum_r = jnp.sum(jnp.where(row <= col, fc, 0.0), axis=1, keepdims=True)
    dlog = jnp.where(tril, bcum_c - bcum_r + ir, NEG_INF)
    inter = bcum_c + m_st
    mt = jnp.maximum(inter, jnp.max(dlog, axis=2, keepdims=True))
    a = jnp.exp(dlog - mt) * lax.dot_general(q, k, _BNT, preferred_element_type=F32)
    si = jnp.exp(inter - mt)
    num = (lax.dot_general(a.astype(BF16), v, _BNN, preferred_element_type=F32)
           + si * lax.dot_general(q, c_st.astype(BF16), _BNN, preferred_element_type=F32))
    den = jnp.sum(a, axis=2, keepdims=True) + si * jnp.sum(q.astype(F32) * n_st, axis=2, keepdims=True)
    h = num / jnp.maximum(jnp.abs(den), jnp.exp(-mt))
    bl = bcum_r[:, :, t - 1:t]
    m_new = jnp.maximum(bl + m_st, jnp.max(bl - bcum_r + ir, axis=2, keepdims=True))
    kw = jnp.exp(bl - bcum_c + ic - m_new) * k.astype(F32)
    decay = jnp.exp(bl + m_st - m_new)
    c_new = decay * c_st + lax.dot_general(kw.astype(BF16), v, _BTN, preferred_element_type=F32)
    n_new = decay * n_st + jnp.sum(kw, axis=1, keepdims=True)
    return h, c_new, n_new, m_new


def _mlstm_kernel(q_ref, k_ref, v_ref, gc_ref, gr_ref, c0_ref, n0_ref, m0_ref,
                  h_ref, c_ref, n_ref, m_ref, *, g0):
    @pl.when(pl.program_id(1) == 0)
    def _():
        c_ref[...] = c0_ref[...]
        n_ref[...] = n0_ref[...]
        m_ref[...] = m0_ref[...]

    nb = q_ref.shape[0]
    where = [(s, hd, slice(hd * B_HEAD_DIM, (hd + 1) * B_HEAD_DIM)) for s in range(nb) for hd in range(B_HEADS)]
    heads = lambda ref: jnp.stack([ref[s, :, sl] for s, _, sl in where])
    gc = [gc_ref[s] for s in range(nb)]
    gr = [gr_ref[s] for s in range(nb)]
    gi, gf = g0, g0 + B_HEADS
    h, c_new, n_new, m_new = _mlstm_heads(
        heads(q_ref), heads(k_ref), heads(v_ref),
        jnp.stack([gr[s][gi + hd:gi + hd + 1, :] for s, hd, _ in where]),
        jnp.stack([gr[s][gf + hd:gf + hd + 1, :] for s, hd, _ in where]),
        jnp.stack([gc[s][:, gi + hd:gi + hd + 1] for s, hd, _ in where]),
        jnp.stack([gc[s][:, gf + hd:gf + hd + 1] for s, hd, _ in where]),
        c_ref[...].reshape((nb * B_HEADS,) + c_ref.shape[2:]),
        n_ref[...].reshape((nb * B_HEADS,) + n_ref.shape[2:]),
        m_ref[...].reshape((nb * B_HEADS,) + m_ref.shape[2:]))
    for i, (s, _, sl) in enumerate(where):
        h_ref[s, :, sl] = h[i].astype(BF16)
    c_ref[...] = c_new.reshape(c_ref.shape)
    n_ref[...] = n_new.reshape(n_ref.shape)
    m_ref[...] = m_new.reshape(m_ref.shape)


def _mlstm(qk, v, gcol, grow, c0, n0, m0, t, nb=1, g0=0):
    b, length, _ = v.shape
    st = lambda a: pl.BlockSpec((nb,) + a.shape[1:], lambda i, j: (i, 0, 0, 0))
    return pl.pallas_call(
        functools.partial(_mlstm_kernel, g0=g0),
        out_shape=[jax.ShapeDtypeStruct(v.shape, BF16), jax.ShapeDtypeStruct(c0.shape, F32),
                   jax.ShapeDtypeStruct(n0.shape, F32), jax.ShapeDtypeStruct(m0.shape, F32)],
        grid=(b // nb, length // t),
        in_specs=[pl.BlockSpec((nb, t, B_WIDTH), lambda i, j: (i, j, 0)),
                  pl.BlockSpec((nb, t, B_WIDTH), lambda i, j: (i, j, 1)),
                  pl.BlockSpec((nb, t, B_WIDTH), lambda i, j: (i, j, 0)),
                  pl.BlockSpec((nb, t, gcol.shape[2]), lambda i, j: (i, j, 0)),
                  pl.BlockSpec((nb, grow.shape[1], t), lambda i, j: (i, 0, j)),
                  st(c0), st(n0), st(m0)],
        out_specs=[pl.BlockSpec((nb, t, B_WIDTH), lambda i, j: (i, j, 0)), st(c0), st(n0), st(m0)],
        compiler_params=_params("arbitrary", "arbitrary"),
        name="mlstm",
    )(qk, qk, v, gcol, grow, c0, n0, m0)


def _sattn_kernel(pt_ref, q_ref, kn_ref, vn_ref, lfn_ref, lf_ref, *rest, n_pages):
    k_refs = rest[:n_pages]
    v_refs = rest[n_pages:2 * n_pages]
    o_ref = rest[2 * n_pages]
    seq = pl.program_id(0)
    q = q_ref[0]
    nrow, width = q.shape
    nq = nrow // A_HEADS
    lane = lax.broadcasted_iota(jnp.int32, (A_HEADS, LANES), 1)
    per_row = lambda f: jnp.concatenate([f] * nq, axis=0)

    offset = jnp.zeros((A_HEADS, 1), F32)
    scores = []
    for j in range(n_pages):
        f = _lane_scan(lf_ref[pt_ref[seq, j]], lane, (1, 2, 4, 8, 16, 32, 64)) + offset
        offset = f[:, LANES - 1:LANES]
        kt = k_refs[j][...].reshape(width, PAGE_SIZE).astype(BF16)
        scores.append(jnp.dot(q, kt, preferred_element_type=F32) - per_row(f * LOG2E))
    n_new = kn_ref.shape[1]
    fnew = (_lane_scan(lfn_ref[0], lane, (1, 2, 4, 8, 16, 32, 64)) + offset)[:, :n_new]
    s_new = lax.dot_general(q, kn_ref[0], _NT, preferred_element_type=F32) - per_row(fnew * LOG2E)
    q_tok = lax.broadcasted_iota(jnp.int32, (nrow, n_new), 0) // A_HEADS
    k_tok = lax.broadcasted_iota(jnp.int32, (nrow, n_new), 1)
    s_new = jnp.where(k_tok <= q_tok, s_new, NEG_INF)

    m = jnp.maximum(jnp.max(functools.reduce(jnp.maximum, scores), axis=1, keepdims=True),
                    jnp.max(s_new, axis=1, keepdims=True))
    probs = [jnp.exp2(s - m) for s in scores]
    p_new = jnp.exp2(s_new - m)
    l = (jnp.sum(functools.reduce(jnp.add, probs), axis=1, keepdims=True)
         + jnp.sum(p_new, axis=1, keepdims=True))
    acc = jnp.dot(p_new.astype(BF16), vn_ref[0], preferred_element_type=F32)
    for j in range(n_pages):
        vt = v_refs[j][...].reshape(width, PAGE_SIZE).astype(BF16)
        acc = acc + lax.dot_general(probs[j].astype(BF16), vt, _NT, preferred_element_type=F32)
    row_head = lax.broadcasted_iota(jnp.int32, acc.shape, 0) % A_HEADS
    col_head = lax.broadcasted_iota(jnp.int32, acc.shape, 1) // A_HEAD_DIM
    out = jnp.where(row_head == col_head, acc / l, 0.0)
    o_ref[0] = jnp.sum(out.reshape(nq, A_HEADS, width), axis=1)


def _sample_attention(page_table, q, kn, vn, lfn, cache_kt, cache_vt, cache_lf):
    b, n_pages = page_table.shape
    nq = q.shape[1] // A_HEADS
    page = lambda j: pl.BlockSpec((None, A_HEADS, A_HEAD_DIM, PAGE_SIZE), lambda i, pt: (pt[i, j], 0, 0, 0))
    per_seq = lambda a: pl.BlockSpec((1,) + a.shape[1:], lambda i, pt: (i, 0, 0))
    grid_spec = pltpu.PrefetchScalarGridSpec(
        num_scalar_prefetch=1,
        grid=(b,),
        in_specs=([per_seq(q), per_seq(kn), per_seq(vn), per_seq(lfn), _resident(cache_lf.shape)]
                  + [page(j) for j in range(n_pages)] + [page(j) for j in range(n_pages)]),
        out_specs=pl.BlockSpec((1, nq, A_WIDTH), lambda i, pt: (i, 0, 0)),
    )
    return pl.pallas_call(
        functools.partial(_sattn_kernel, n_pages=n_pages),
        out_shape=jax.ShapeDtypeStruct((b, nq, A_WIDTH), F32),
        grid_spec=grid_spec,
        compiler_params=_params("arbitrary"),
        name="paged_fox_attention",
    )(page_table, q, kn, vn, lfn, cache_lf, *([cache_kt] * n_pages), *([cache_vt] * n_pages))


def _split_weights(w_in, b_fox_f, b_ml_i, b_ml_f):
    d = w_in.shape[0]
    o = 0
    parts = {}
    for name, n in (("aq", A_WIDTH), ("ak", A_WIDTH), ("av", A_WIDTH), ("af", A_HEADS), ("bq", B_WIDTH),
                    ("bk", B_WIDTH), ("bv", B_WIDTH), ("bi", B_HEADS), ("bf", B_HEADS), ("bo", B_WIDTH),
                    ("ga", d), ("gb", d)):
        parts[name] = w_in[:, o:o + n]
        o += n
    assert o == w_in.shape[1]
    small = jnp.concatenate([parts["af"], parts["bi"], parts["bf"]], axis=1)
    bias = jnp.concatenate([b_fox_f, b_ml_i, b_ml_f])
    bf = lambda a: a.astype(BF16)
    return {
        "q": bf(parts["aq"] * (A_HEAD_DIM ** -0.5 * LOG2E)),
        "k": bf(parts["ak"]), "kt": bf(parts["ak"].T),
        "v": bf(parts["av"]),
        "qk": bf(jnp.concatenate([parts["bq"], parts["bk"]], axis=1)),
        "bv": bf(parts["bv"]),
        "g": bf(jnp.concatenate([parts["bo"], parts["ga"], parts["gb"]], axis=1)),
        "s": bf(jnp.pad(small, ((0, 0), (0, LANES - N_GATES)))),
        "bs": jnp.pad(bias, (0, LANES - N_GATES)).reshape(1, LANES),
        "st": bf(jnp.pad(small.T, ((0, GATE_ROWS - N_GATES), (0, 0)))),
        "bst": jnp.pad(bias, (0, GATE_ROWS - N_GATES)).reshape(GATE_ROWS, 1),
    }


def kernel(x_prompt, x_sample, cache_k, cache_v, cache_logf, page_table, state_C, state_n, state_m, state_conv,
           c_prompt, c_sample, w_ada, b_ada, g_pre_mix, g_post_mix, g_pre_mlp, g_post_mlp, w_in, b_fox_f,
           b_ml_i, b_ml_f, conv_w, conv_b, w_proj_a, w_proj_b, w_out, w_up, w_down):
    assert w_in.shape[0] == 1, "one trunk layer"
    bsz, seq, d = x_prompt.shape
    dbsz, dseq, _ = x_sample.shape

    w_proj = _split_weights(w_in[0], b_fox_f[0], b_ml_i[0], b_ml_f[0])
    wpa, wpb, wo = w_proj_a[0].astype(BF16), w_proj_b[0].astype(BF16), w_out[0].astype(BF16)
    wu, wd = w_up[0].astype(BF16), w_down[0].astype(BF16)
    g1, g2, g3, g4 = g_pre_mix, g_post_mix, g_pre_mlp, g_post_mlp
    cw, cb = conv_w[0], conv_b

    n_c = bsz + dbsz
    c_all = jnp.concatenate([c_prompt, c_sample, jnp.zeros((-n_c % SUBLANES, d), F32)], axis=0)
    mod = _adaln(c_all, w_ada[0], b_ada)

    def token_path(x2, mods, tm, tiles_per_group, token_major_k):
        sh_m, sc_m, gt_m, sh_f, sc_f, gt_f = mods
        outs = _in_proj(x2, sc_m, sh_m, g1, w_proj, tm, tiles_per_group, token_major_k)
        return outs, (gt_m, sc_f, sh_f, gt_f)

    def head_split(kt):
        g_, _, n_ = kt.shape
        return jnp.transpose(kt.reshape(g_, A_HEADS, A_HEAD_DIM, n_), (0, 3, 1, 2))[None]

    def finish(x2, ya, hb, og, rest, tm, group_tokens):
        gt_m, sc_f, sh_f, gt_f = rest
        x1, h2 = _mix_out(ya, hb, og, x2, gt_m, sc_f, sh_f, g2, g3, wpa, wpb, wo, tm, group_tokens // tm)
        tm_mlp = min(tm, 256)
        return _mlp(h2, x1, gt_f, g4, wu, wd, tm_mlp, group_tokens // tm_mlp)

    tm = 256
    m_p = bsz * seq
    x2 = x_prompt.reshape(m_p, d)
    mods = tuple(t.reshape(bsz, 1, d) for t in jnp.split(mod[:bsz], N_MOD, axis=-1))
    (qa, kt, vt, va, qkb, vb, og, sm, smt), rest = token_path(x2, mods, tm, seq // tm, False)

    r3 = lambda a: a.reshape(bsz, seq, a.shape[-1])
    ya = _attention(r3(qa), kt, r3(va), _cumsum_lanes(smt)).reshape(m_p, A_WIDTH)

    qkc = _conv(r3(qkb), jnp.zeros((bsz, SUBLANES, 2 * B_WIDTH), F32), cw, cb, tc=512)
    hb, c_p, n_p, m_pr = _mlstm(qkc, r3(vb), r3(sm), smt,
                                jnp.zeros((bsz, B_HEADS, B_HEAD_DIM, B_HEAD_DIM), F32),
                                jnp.zeros((bsz, B_HEADS, 1, B_HEAD_DIM), F32),
                                jnp.zeros((bsz, B_HEADS, 1, 1), F32), t=128, nb=2, g0=A_HEADS)
    y_prompt = finish(x2, ya, hb.reshape(m_p, B_WIDTH), og, rest, 2 * tm, seq).reshape(bsz, seq, d)

    new_k_prompt = head_split(kt)
    new_v_prompt = head_split(vt)
    new_logf_prompt = jnp.transpose(smt[:, :A_HEADS, :], (0, 2, 1))[None]
    new_conv_prompt = r3(qkb)[:, seq - (CONV_W - 1):, :][None]

    m_s = dbsz * dseq
    xs2 = x_sample.reshape(m_s, d)
    mods = tuple(t.reshape(1, m_s, d) for t in jnp.split(jnp.repeat(mod[bsz:n_c], dseq, axis=0), N_MOD, axis=-1))
    (qa, kt, vt, va, qkb, vb, og, sm, smt, ka), rest = token_path(xs2, mods, m_s, 1, True)

    n_new = 16
    new_rows = lambda a: jnp.pad(a.reshape(dbsz, dseq, -1), ((0, 0), (0, n_new - dseq), (0, 0)))
    lf_s = sm[:, :A_HEADS].reshape(dbsz, dseq, A_HEADS)
    lf_new = jnp.pad(jnp.transpose(lf_s, (0, 2, 1)), ((0, 0), (0, 0), (0, LANES - dseq)))
    q_heads = qa.reshape(dbsz, dseq, A_HEADS, 1, A_HEAD_DIM)
    q_bd = (q_heads * jnp.eye(A_HEADS, dtype=BF16)[None, None, :, :, None]).reshape(dbsz, dseq * A_HEADS, A_WIDTH)
    ya = _sample_attention(
        page_table, q_bd, new_rows(ka), new_rows(va), lf_new,
        jnp.transpose(cache_k[0], (0, 2, 3, 1)), jnp.transpose(cache_v[0], (0, 2, 3, 1)),
        jnp.transpose(cache_logf[0], (0, 2, 1)))
    ya = ya.astype(BF16).reshape(m_s, A_WIDTH)

    t_s = 16
    conv_in = jnp.pad(qkb.reshape(dbsz, dseq, 2 * B_WIDTH), ((0, 0), (0, SUBLANES - dseq), (0, 0)))
    conv_init = jnp.pad(state_conv[0], ((0, 0), (SUBLANES - (CONV_W - 1), 0), (0, 0)))
    qkc = _conv(conv_in, conv_init, cw, cb, tc=SUBLANES, nb=min(16, dbsz))
    tok_valid = (jnp.arange(t_s) < dseq)[None, :, None]
    pad_t = lambda a: jnp.pad(a, ((0, 0), (0, t_s - a.shape[1]), (0, 0)))
    qkc = jnp.where(tok_valid, pad_t(qkc), jnp.zeros((), BF16))
    vb_s = pad_t(vb.reshape(dbsz, dseq, B_WIDTH))
    gates = pad_t(sm[:, A_HEADS:A_HEADS + 2 * B_HEADS].reshape(dbsz, dseq, 2 * B_HEADS))
    neutral = jnp.concatenate([jnp.full((B_HEADS,), NEG_INF, F32), jnp.zeros((B_HEADS,), F32)])
    gcol = jnp.where(tok_valid, gates, neutral)
    grow = jnp.transpose(gcol, (0, 2, 1))
    hb, c_s, n_s, m_sm = _mlstm(qkc, vb_s, gcol, grow, state_C[0], state_n[0][:, :, None, :],
                                state_m[0][:, :, None, None], t=t_s, nb=8)
    hb = hb[:, :dseq, :].reshape(m_s, B_WIDTH)
    y_sample = finish(xs2, ya, hb, og, rest, m_s, m_s).reshape(dbsz, dseq, d)

    new_k_sample = head_split(kt).reshape(1, dbsz, dseq, A_HEADS, A_HEAD_DIM)
    new_v_sample = head_split(vt).reshape(1, dbsz, dseq, A_HEADS, A_HEAD_DIM)
    new_logf_sample = lf_s[None]
    qkb3 = qkb.reshape(dbsz, dseq, 2 * B_WIDTH)
    new_conv_sample = jnp.concatenate([state_conv[0], qkb3], axis=1)[:, dseq:, :][None]

    return (y_prompt, y_sample, new_k_prompt, new_v_prompt, new_logf_prompt,
            c_p[None], n_p[:, :, 0, :][None], m_pr[:, :, 0, 0][None], new_conv_prompt,
            new_k_sample, new_v_sample, new_logf_sample,
            c_s[None], n_s[:, :, 0, :][None], m_sm[:, :, 0, 0][None], new_conv_sample)
```

```python
import functools

import jax
import jax.numpy as jnp
from jax import lax
from jax.experimental import pallas as pl
from jax.experimental.pallas import tpu as pltpu

F32 = jnp.float32
BF16 = jnp.bfloat16

A_HEADS = 8
A_HEAD_DIM = 64
A_WIDTH = A_HEADS * A_HEAD_DIM
B_HEADS = 4
B_HEAD_DIM = 128
B_WIDTH = B_HEADS * B_HEAD_DIM
CONV_W = 4
N_MOD = 6
RMS_EPS = 1e-6
NEG_INF = -1e30
PAGE_SIZE = 128
LOG2E = 1.4426950408889634

LANES = 128
SUBLANES = 8
VMEM_LIMIT_BYTES = 56 * 1024 * 1024

_NT = (((1,), (1,)), ((), ()))
_TN = (((0,), (0,)), ((), ()))


def _params(*sem):
    return pltpu.CompilerParams(dimension_semantics=sem, vmem_limit_bytes=VMEM_LIMIT_BYTES)


def _resident(shape):
    nd = len(shape)
    return pl.BlockSpec(shape, lambda *_: (0,) * nd, pipeline_mode=pl.Buffered(1))


def _rms(x, g):
    r = lax.rsqrt(jnp.mean(x * x, axis=-1, keepdims=True) + RMS_EPS)
    return (x * r) * g


def _log_sigmoid(x):
    return jnp.minimum(x, 0.0) - jnp.log1p(jnp.exp(-jnp.abs(x)))


def _mod_kernel(c_ref, w_ref, b_ref, o_ref):
    c = c_ref[...]
    a = (c * jax.nn.sigmoid(c)).astype(BF16)
    o_ref[...] = jnp.dot(a, w_ref[...].astype(BF16), preferred_element_type=F32) + b_ref[...]


def _adaln(c, w, b, tn=1024):
    r, d = c.shape
    n = w.shape[1]
    return pl.pallas_call(
        _mod_kernel,
        out_shape=jax.ShapeDtypeStruct((r, n), F32),
        grid=(n // tn,),
        in_specs=[pl.BlockSpec((r, d), lambda j: (0, 0)),
                  pl.BlockSpec((d, tn), lambda j: (0, j)),
                  pl.BlockSpec((1, tn), lambda j: (0, j))],
        out_specs=pl.BlockSpec((r, tn), lambda j: (0, j)),
        compiler_params=_params("arbitrary"),
        name="adaln",
    )(c, w, b)


def _mod_spec(arr, tm, tiles_per_group):
    g, r, d = arr.shape
    if r == 1:
        return pl.BlockSpec((1, 1, d), lambda i: (i // tiles_per_group, 0, 0))
    assert g == 1
    return pl.BlockSpec((1, tm, d), lambda i: (0, i, 0))


N_GATES = A_HEADS + 2 * B_HEADS
GATE_ROWS = 16


def _gate_act(z, idx):
    return jnp.where((idx >= A_HEADS) & (idx < A_HEADS + B_HEADS), z, _log_sigmoid(z))


def _in_kernel(x_ref, sc_ref, sh_ref, g_ref, wq_ref, wk_ref, wv_ref, wqk_ref, wbv_ref, wg_ref,
               ws_ref, bs_ref, wst_ref, bst_ref, q_ref, kt_ref, vt_ref, ka_ref, *rest, token_minor_q):
    if not token_minor_q:
        va_ref, *rest = rest
    qkb_ref, vb_ref, og_ref, sm_ref, smt_ref = rest
    h = _rms(x_ref[...], g_ref[...]) * (1.0 + sc_ref[0]) + sh_ref[0]
    hb = h.astype(BF16)
    q = jnp.dot(hb, wq_ref[...], preferred_element_type=F32)
    if token_minor_q:
        q_ref[0] = q.T.astype(BF16)
    else:
        q_ref[...] = q.astype(BF16)
    k = jnp.dot(hb, wk_ref[...], preferred_element_type=F32)
    kt_ref[0] = k.T
    ka_ref[...] = k.astype(BF16)
    v = jnp.dot(hb, wv_ref[...], preferred_element_type=F32)
    vt_ref[0] = v.T
    if not token_minor_q:
        va_ref[...] = v.astype(BF16)
    qkb_ref[...] = jnp.dot(hb, wqk_ref[...], preferred_element_type=F32)
    vb_ref[...] = jnp.dot(hb, wbv_ref[...], preferred_element_type=F32).astype(BF16)
    og_ref[...] = jax.nn.sigmoid(jnp.dot(hb, wg_ref[...], preferred_element_type=F32)).astype(BF16)
    zs = jnp.dot(hb, ws_ref[...], preferred_element_type=F32) + bs_ref[...]
    sm_ref[...] = _gate_act(zs, lax.broadcasted_iota(jnp.int32, zs.shape, 1))
    zt = lax.dot_general(wst_ref[...], hb, _NT, preferred_element_type=F32) + bst_ref[...]
    smt_ref[0] = _gate_act(zt, lax.broadcasted_iota(jnp.int32, zt.shape, 0))


def _in_proj(x, sc, sh, g, w, tm, tiles_per_group, token_minor_q):
    m, d = x.shape
    group_tokens = tm * tiles_per_group
    groups = m // group_tokens
    row = lambda n: pl.BlockSpec((tm, n), lambda i: (i, 0))
    col = lambda n: pl.BlockSpec((1, n, tm), lambda i: (i // tiles_per_group, 0, i % tiles_per_group))
    rows_out = lambda n, dt: (jax.ShapeDtypeStruct((m, n), dt), row(n))
    cols_out = lambda n, dt=F32: (jax.ShapeDtypeStruct((groups, n, group_tokens), dt), col(n))
    outs = [cols_out(A_WIDTH, BF16) if token_minor_q else rows_out(A_WIDTH, BF16),
            cols_out(A_WIDTH), cols_out(A_WIDTH), rows_out(A_WIDTH, BF16)]
    if not token_minor_q:
        outs.append(rows_out(A_WIDTH, BF16))
    outs += [rows_out(2 * B_WIDTH, F32), rows_out(B_WIDTH, BF16), rows_out(w["g"].shape[1], BF16),
             rows_out(LANES, F32), cols_out(GATE_ROWS)]
    weights = [w["q"], w["k"], w["v"], w["qk"], w["bv"], w["g"], w["s"], w["bs"], w["st"], w["bst"]]
    return pl.pallas_call(
        functools.partial(_in_kernel, token_minor_q=token_minor_q),
        out_shape=[o[0] for o in outs],
        grid=(m // tm,),
        in_specs=[row(d), _mod_spec(sc, tm, tiles_per_group), _mod_spec(sh, tm, tiles_per_group),
                  _resident(g.shape)] + [_resident(a.shape) for a in weights],
        out_specs=[o[1] for o in outs],
        compiler_params=_params("arbitrary"),
        name="in_proj",
    )(x, sc, sh, g, *weights)


MIX_PARTS = 2


def _mix_kernel(ya_ref, hb_ref, og_ref, x_ref, gt_ref, sc_ref, sh_ref, g1_ref, g2_ref,
                wpa_ref, wpb_ref, wo_ref, x1_ref, h2_ref):
    tm, d = x_ref.shape
    parts = [slice(r, r + tm // MIX_PARTS) for r in range(0, tm, tm // MIX_PARTS)]
    mod = lambda ref, p: ref[0] if ref.shape[1] == 1 else ref[0, p, :]
    ya = [jnp.dot(ya_ref[p, :], wpa_ref[...], preferred_element_type=F32) for p in parts]
    yb = [jnp.dot((og_ref[p, :B_WIDTH].astype(F32) * hb_ref[p, :].astype(F32)).astype(BF16), wpb_ref[...],
                  preferred_element_type=F32) for p in parts]
    yo = []
    for p, a, b in zip(parts, ya, yb):
        y = og_ref[p, B_WIDTH:B_WIDTH + d].astype(F32) * a + og_ref[p, B_WIDTH + d:].astype(F32) * b
        yo.append(jnp.dot(y.astype(BF16), wo_ref[...], preferred_element_type=F32))
    for p, o in zip(parts, yo):
        x1 = x_ref[p, :] + mod(gt_ref, p) * _rms(o, g1_ref[...])
        x1_ref[p, :] = x1
        h2_ref[p, :] = (_rms(x1, g2_ref[...]) * (1.0 + mod(sc_ref, p)) + mod(sh_ref, p)).astype(BF16)


def _mix_out(ya, hb, og, x, gt, sc, sh, g1, g2, wpa, wpb, wo, tm, tiles_per_group):
    m, d = x.shape
    row = lambda n: pl.BlockSpec((tm, n), lambda i: (i, 0))
    ms = lambda a: _mod_spec(a, tm, tiles_per_group)
    return pl.pallas_call(
        _mix_kernel,
        out_shape=[jax.ShapeDtypeStruct((m, d), F32), jax.ShapeDtypeStruct((m, d), BF16)],
        grid=(m // tm,),
        in_specs=[row(ya.shape[1]), row(hb.shape[1]), row(og.shape[1]), row(d), ms(gt), ms(sc), ms(sh),
                  _resident(g1.shape), _resident(g2.shape), _resident(wpa.shape), _resident(wpb.shape),
                  _resident(wo.shape)],
        out_specs=[row(d), row(d)],
        compiler_params=_params("arbitrary"),
        name="mix_out",
    )(ya, hb, og, x, gt, sc, sh, g1, g2, wpa, wpb, wo)


def _mlp_kernel(h2_ref, x1_ref, gt_ref, g_ref, wu_ref, wd_ref, y_ref, *, fc):
    h2 = h2_ref[...]
    acc = jnp.zeros(x1_ref.shape, F32)
    for c in range(wu_ref.shape[1] // fc):
        u = jnp.dot(h2, wu_ref[:, c * fc:(c + 1) * fc], preferred_element_type=F32)
        u = jnp.square(jnp.maximum(u, 0.0)).astype(BF16)
        acc = acc + jnp.dot(u, wd_ref[c * fc:(c + 1) * fc, :], preferred_element_type=F32)
    y_ref[...] = x1_ref[...] + gt_ref[0] * _rms(acc, g_ref[...])


def _mlp(h2, x1, gt, g, wu, wd, tm, tiles_per_group, fc=1024):
    m, d = x1.shape
    row = lambda n: pl.BlockSpec((tm, n), lambda i: (i, 0))
    return pl.pallas_call(
        functools.partial(_mlp_kernel, fc=fc),
        out_shape=jax.ShapeDtypeStruct((m, d), F32),
        grid=(m // tm,),
        in_specs=[row(d), row(d), _mod_spec(gt, tm, tiles_per_group), _resident(g.shape),
                  _resident(wu.shape), _resident(wd.shape)],
        out_specs=row(d),
        compiler_params=_params("arbitrary"),
        name="mlp",
    )(h2, x1, gt, g, wu, wd)


def _lane_scan(x, lane, shifts):
    for s in shifts:
        x = x + jnp.where(lane >= s, pltpu.roll(x, s, axis=1), 0.0)
    return x


def _cumsum_kernel(x_ref, o_ref):
    rows, length = x_ref.shape[1], x_ref.shape[2]
    lane = lax.broadcasted_iota(jnp.int32, (rows, LANES), 1)
    carry = jnp.zeros((rows, 1), F32)
    for c in range(length // LANES):
        x = _lane_scan(x_ref[0, :, c * LANES:(c + 1) * LANES], lane, (1, 2, 4, 8, 16, 32, 64)) + carry
        o_ref[0, :, c * LANES:(c + 1) * LANES] = x
        carry = x[:, LANES - 1:LANES]


def _cumsum_lanes(x):
    b, r, length = x.shape
    spec = pl.BlockSpec((1, r, length), lambda i: (i, 0, 0))
    return pl.pallas_call(
        _cumsum_kernel,
        out_shape=jax.ShapeDtypeStruct(x.shape, F32),
        grid=(b,),
        in_specs=[spec],
        out_specs=spec,
        compiler_params=_params("arbitrary"),
        name="logf_cumsum",
    )(x)


V_ROWS = 80


def _aug_lane0(hh):
    return A_HEAD_DIM if hh == 0 else 0


def _attn_kernel(qt_ref, k_ref, vt_ref, fc_ref, o_ref, kaug_ref, vaug_ref, s_ref, *, blk_q, blk_k, chunk):
    qi = pl.program_id(2)
    length = k_ref.shape[1]
    lane = lax.broadcasted_iota(jnp.int32, (1, LANES), 1)
    own = (lane < A_HEAD_DIM, lane >= A_HEAD_DIM)
    feat = lax.broadcasted_iota(jnp.int32, (LANES, 1), 0)

    @pl.when(qi == 0)
    def _():
        ones_row = jnp.where(lax.broadcasted_iota(jnp.int32, (V_ROWS - A_HEAD_DIM, chunk), 0) == 0, 1.0, 0.0)

        def build(c, _):
            toks = pl.ds(pl.multiple_of(c * chunk, chunk), chunk)
            k = k_ref[0, toks, :]
            f = fc_ref[0, 0, toks, :]
            vt = vt_ref[0, :, toks]
            for hh in range(2):
                a0 = _aug_lane0(hh)
                neg_f = f[:, hh:hh + 1] * (-LOG2E)
                hi = neg_f.astype(BF16).astype(F32)
                mid = (neg_f - hi).astype(BF16).astype(F32)
                lo = (neg_f - hi - mid).astype(BF16).astype(F32)
                bias = jnp.where(lane == a0, hi, jnp.where(lane == a0 + 1, mid, jnp.where(lane == a0 + 2, lo, 0.0)))
                kaug_ref[hh, toks, :] = jnp.where(own[hh], k, bias.astype(BF16))
                dims = vt[hh * A_HEAD_DIM:(hh + 1) * A_HEAD_DIM]
                vaug_ref[hh, :, toks] = jnp.concatenate([dims, ones_row], axis=0).astype(BF16)
            return 0
        lax.fori_loop(0, length // chunk, build, 0)

    qt = qt_ref[0]
    qh = []
    for hh in range(2):
        a0 = _aug_lane0(hh)
        ones = jnp.where((feat >= a0) & (feat < a0 + 3), 1.0, 0.0).astype(BF16)
        qh.append(jnp.where((feat < A_HEAD_DIM) if hh == 0 else (feat >= A_HEAD_DIM), qt, ones))

    def keys(j):
        return pl.ds(pl.multiple_of(j * blk_k, blk_k), blk_k)

    def scores(j, slot, c0=0):
        for hh in range(2):
            s_ref[slot, hh, :, c0:] = jnp.dot(kaug_ref[hh, keys(j), :], qh[hh][:, c0:], preferred_element_type=F32)

    def fold(j, slot, carry, causal, c0=0):
        probs, stats = [], []
        for hh in range(2):
            m = carry[hh][0]
            sh = s_ref[slot, hh, :, c0:]
            if causal:
                key = lax.broadcasted_iota(jnp.int32, sh.shape, 0)
                qry = lax.broadcasted_iota(jnp.int32, sh.shape, 1)
                sh = jnp.where(key <= qry, sh, NEG_INF)
            m_new = jnp.maximum(m, jnp.max(sh, axis=0, keepdims=True))
            probs.append(jnp.exp2(sh - m_new).astype(BF16))
            stats.append((m_new, jnp.exp2(m - m_new)))
        return tuple(
            (stats[hh][0], stats[hh][1] * carry[hh][1]
             + jnp.dot(vaug_ref[hh, :, keys(j)], probs[hh], preferred_element_type=F32))
            for hh in range(2))

    assert blk_q == 2 * blk_k
    init = tuple((jnp.full((1, blk_q), NEG_INF, F32), jnp.zeros((V_ROWS, blk_q), F32)) for _ in range(2))

    def body(p, carry):
        scores(2 * p + 1, 1)
        carry = fold(2 * p, 0, carry, False)
        scores(2 * p + 2, 0)
        return fold(2 * p + 1, 1, carry, False)

    scores(0, 0)
    carry = lax.fori_loop(0, qi, body, init)
    scores(2 * qi + 1, 1, blk_k)
    carry = fold(2 * qi, 0, carry, True)
    upper = fold(2 * qi + 1, 1, tuple((m[:, blk_k:], acc[:, blk_k:]) for m, acc in carry), True, blk_k)
    out = []
    for hh in range(2):
        acc = jnp.concatenate([carry[hh][1][:, :blk_k], upper[hh][1]], axis=1)
        out.append(acc[:A_HEAD_DIM] / acc[A_HEAD_DIM:A_HEAD_DIM + 1])
    o_ref[0] = jnp.concatenate(out, axis=0).T.astype(BF16)


def _attention(qt, k, vt, fcol, blk_q=1024, blk_k=512):
    b, length, width = k.shape
    pairs = width // LANES
    blk_q, blk_k = min(blk_q, length), min(blk_k, length)
    return pl.pallas_call(
        functools.partial(_attn_kernel, blk_q=blk_q, blk_k=blk_k, chunk=min(512, length)),
        out_shape=jax.ShapeDtypeStruct(k.shape, BF16),
        grid=(b, pairs, length // blk_q),
        in_specs=[pl.BlockSpec((1, LANES, blk_q), lambda i, p, t: (i, p, t)),
                  pl.BlockSpec((1, length, LANES), lambda i, p, t: (i, 0, p)),
                  pl.BlockSpec((1, LANES, length), lambda i, p, t: (i, p, 0)),
                  pl.BlockSpec((1, 1, length, 2), lambda i, p, t: (i, p, 0, 0))],
        out_specs=pl.BlockSpec((1, blk_q, LANES), lambda i, p, t: (i, t, p)),
        scratch_shapes=[pltpu.VMEM((2, length, LANES), BF16), pltpu.VMEM((2, V_ROWS, length), BF16),
                        pltpu.VMEM((2, 2, blk_k, blk_q), F32)],
        compiler_params=_params("arbitrary", "arbitrary", "arbitrary"),
        name="fox_attention",
    )(qt, k, vt, fcol)


def _conv_silu(xp_ref, w_ref, b_ref, tc):
    y = b_ref[...]
    for j in range(CONV_W):
        off = SUBLANES - (CONV_W - 1) + j
        y = y + w_ref[j:j + 1, :] * xp_ref[off:off + tc, :]
    lane = lax.broadcasted_iota(jnp.int32, (1, y.shape[1]), 1)
    return y * jax.nn.sigmoid(y) * jnp.where(lane < B_WIDTH, 1.0, B_HEAD_DIM ** -0.5)


def _conv_kernel(x_ref, prev_ref, init_ref, w_ref, b_ref, o_ref, xp_ref, *, tc):
    for s in range(x_ref.shape[0]):
        xp_ref[s, 0:SUBLANES, :] = jnp.where(pl.program_id(1) == 0, init_ref[s], prev_ref[s])
        xp_ref[s, SUBLANES:SUBLANES + tc, :] = x_ref[s]
        o_ref[s] = _conv_silu(xp_ref.at[s], w_ref, b_ref, tc).astype(BF16)


def _conv(x, init, w, b, tc, nb=1):
    bsz, length, c = x.shape
    tpb = tc // SUBLANES
    return pl.pallas_call(
        functools.partial(_conv_kernel, tc=tc),
        out_shape=jax.ShapeDtypeStruct(x.shape, BF16),
        grid=(bsz // nb, length // tc),
        in_specs=[pl.BlockSpec((nb, tc, c), lambda i, t: (i, t, 0)),
                  pl.BlockSpec((nb, SUBLANES, c), lambda i, t: (i, jnp.maximum(t * tpb - 1, 0), 0)),
                  pl.BlockSpec((nb, SUBLANES, c), lambda i, t: (i, 0, 0)),
                  pl.BlockSpec((CONV_W, c), lambda i, t: (0, 0)),
                  pl.BlockSpec((1, c), lambda i, t: (0, 0))],
        out_specs=pl.BlockSpec((nb, tc, c), lambda i, t: (i, t, 0)),
        scratch_shapes=[pltpu.VMEM((nb, tc + SUBLANES, c), F32)],
        compiler_params=_params("arbitrary", "arbitrary"),
        name="short_conv",
    )(x, x, init, w, b)


_BNT = (((2,), (2,)), ((0,), (0,)))
_BNN = (((2,), (1,)), ((0,), (0,)))
_BTN = (((1,), (1,)), ((0,), (0,)))


def _mlstm_heads(q, k, v, ir, fr, ic, fc, c_st, n_st, m_st):
    n, t, _ = q.shape
    row = lax.broadcasted_iota(jnp.int32, (n, t, t), 1)
    col = lax.broadcasted_iota(jnp.int32, (n, t, t), 2)
    tril = col <= row
    bcum_c = jnp.sum(jnp.where(tril, fr, 0.0), axis=2, keepdims=True)
    bcum_r = jnp.sum(jnp.where(row <= col, fc, 0.0), axis=1, keepdims=True)
    dlog = jnp.where(tril, bcum_c - bcum_r + ir, NEG_INF)
    inter = bcum_c + m_st
    mt = jnp.maximum(inter, jnp.max(dlog, axis=2, keepdims=True))
    a = jnp.exp(dlog - mt) * lax.dot_general(q, k, _BNT, preferred_element_type=F32)
    si = jnp.exp(inter - mt)
    num = (lax.dot_general(a.astype(BF16), v, _BNN, preferred_element_type=F32)
           + si * lax.dot_general(q, c_st.astype(BF16), _BNN, preferred_element_type=F32))
    den = jnp.sum(a, axis=2, keepdims=True) + si * jnp.sum(q.astype(F32) * n_st, axis=2, keepdims=True)
    h = num / jnp.maximum(jnp.abs(den), jnp.exp(-mt))
    bl = bcum_r[:, :, t - 1:t]
    m_new = jnp.maximum(bl + m_st, jnp.max(bl - bcum_r + ir, axis=2, keepdims=True))
    kw = jnp.exp(bl - bcum_c + ic - m_new) * k.astype(F32)
    decay = jnp.exp(bl + m_st - m_new)
    c_new = decay * c_st + lax.dot_general(kw.astype(BF16), v, _BTN, preferred_element_type=F32)
    n_new = decay * n_st + jnp.sum(kw, axis=1, keepdims=True)
    return h, c_new, n_new, m_new


def _mlstm_kernel(q_ref, k_ref, v_ref, gc_ref, gr_ref, c0_ref, n0_ref, m0_ref,
                  h_ref, c_ref, n_ref, m_ref, *, g0):
    @pl.when(pl.program_id(1) == 0)
    def _():
        c_ref[...] = c0_ref[...]
        n_ref[...] = n0_ref[...]
        m_ref[...] = m0_ref[...]

    nb = q_ref.shape[0]
    where = [(s, hd, slice(hd * B_HEAD_DIM, (hd + 1) * B_HEAD_DIM)) for s in range(nb) for hd in range(B_HEADS)]
    heads = lambda ref: jnp.stack([ref[s, :, sl] for s, _, sl in where])
    gc = [gc_ref[s] for s in range(nb)]
    gr = [gr_ref[s] for s in range(nb)]
    gi, gf = g0, g0 + B_HEADS
    h, c_new, n_new, m_new = _mlstm_heads(
        heads(q_ref), heads(k_ref), heads(v_ref),
        jnp.stack([gr[s][gi + hd:gi + hd + 1, :] for s, hd, _ in where]),
        jnp.stack([gr[s][gf + hd:gf + hd + 1, :] for s, hd, _ in where]),
        jnp.stack([gc[s][:, gi + hd:gi + hd + 1] for s, hd, _ in where]),
        jnp.stack([gc[s][:, gf + hd:gf + hd + 1] for s, hd, _ in where]),
        c_ref[...].reshape((nb * B_HEADS,) + c_ref.shape[2:]),
        n_ref[...].reshape((nb * B_HEADS,) + n_ref.shape[2:]),
        m_ref[...].reshape((nb * B_HEADS,) + m_ref.shape[2:]))
    for i, (s, _, sl) in enumerate(where):
        h_ref[s, :, sl] = h[i].astype(BF16)
    c_ref[...] = c_new.reshape(c_ref.shape)
    n_ref[...] = n_new.reshape(n_ref.shape)
    m_ref[...] = m_new.reshape(m_ref.shape)


def _mlstm(qk, v, gcol, grow, c0, n0, m0, t, nb=1, g0=0):
    b, length, _ = v.shape
    st = lambda a: pl.BlockSpec((nb,) + a.shape[1:], lambda i, j: (i, 0, 0, 0))
    return pl.pallas_call(
        functools.partial(_mlstm_kernel, g0=g0),
        out_shape=[jax.ShapeDtypeStruct(v.shape, BF16), jax.ShapeDtypeStruct(c0.shape, F32),
                   jax.ShapeDtypeStruct(n0.shape, F32), jax.ShapeDtypeStruct(m0.shape, F32)],
        grid=(b // nb, length // t),
        in_specs=[pl.BlockSpec((nb, t, B_WIDTH), lambda i, j: (i, j, 0)),
                  pl.BlockSpec((nb, t, B_WIDTH), lambda i, j: (i, j, 1)),
                  pl.BlockSpec((nb, t, B_WIDTH), lambda i, j: (i, j, 0)),
                  pl.BlockSpec((nb, t, gcol.shape[2]), lambda i, j: (i, j, 0)),
                  pl.BlockSpec((nb, grow.shape[1], t), lambda i, j: (i, 0, j)),
                  st(c0), st(n0), st(m0)],
        out_specs=[pl.BlockSpec((nb, t, B_WIDTH), lambda i, j: (i, j, 0)), st(c0), st(n0), st(m0)],
        compiler_params=_params("arbitrary", "arbitrary"),
        name="mlstm",
    )(qk, qk, v, gcol, grow, c0, n0, m0)


def _sattn_kernel(pt_ref, q_ref, kn_ref, vn_ref, lfn_ref, lf_ref, *rest, n_pages):
    k_refs = rest[:n_pages]
    v_refs = rest[n_pages:2 * n_pages]
    o_ref = rest[2 * n_pages]
    seq = pl.program_id(0)
    q = q_ref[0]
    nrow, width = q.shape
    nq = nrow // A_HEADS
    lane = lax.broadcasted_iota(jnp.int32, (A_HEADS, LANES), 1)
    per_row = lambda f: jnp.concatenate([f] * nq, axis=0)

    offset = jnp.zeros((A_HEADS, 1), F32)
    scores = []
    for j in range(n_pages):
        f = _lane_scan(lf_ref[pt_ref[seq, j]], lane, (1, 2, 4, 8, 16, 32, 64)) + offset
        offset = f[:, LANES - 1:LANES]
        kt = k_refs[j][...].reshape(width, PAGE_SIZE).astype(BF16)
        scores.append(jnp.dot(q, kt, preferred_element_type=F32) - per_row(f * LOG2E))
    n_new = kn_ref.shape[1]
    fnew = (_lane_scan(lfn_ref[0], lane, (1, 2, 4, 8, 16, 32, 64)) + offset)[:, :n_new]
    s_new = lax.dot_general(q, kn_ref[0], _NT, preferred_element_type=F32) - per_row(fnew * LOG2E)
    q_tok = lax.broadcasted_iota(jnp.int32, (nrow, n_new), 0) // A_HEADS
    k_tok = lax.broadcasted_iota(jnp.int32, (nrow, n_new), 1)
    s_new = jnp.where(k_tok <= q_tok, s_new, NEG_INF)

    m = jnp.maximum(jnp.max(functools.reduce(jnp.maximum, scores), axis=1, keepdims=True),
                    jnp.max(s_new, axis=1, keepdims=True))
    probs = [jnp.exp2(s - m) for s in scores]
    p_new = jnp.exp2(s_new - m)
    l = (jnp.sum(functools.reduce(jnp.add, probs), axis=1, keepdims=True)
         + jnp.sum(p_new, axis=1, keepdims=True))
    acc = jnp.dot(p_new.astype(BF16), vn_ref[0], preferred_element_type=F32)
    for j in range(n_pages):
        vt = v_refs[j][...].reshape(width, PAGE_SIZE).astype(BF16)
        acc = acc + lax.dot_general(probs[j].astype(BF16), vt, _NT, preferred_element_type=F32)
    row_head = lax.broadcasted_iota(jnp.int32, acc.shape, 0) % A_HEADS
    col_head = lax.broadcasted_iota(jnp.int32, acc.shape, 1) // A_HEAD_DIM
    out = jnp.where(row_head == col_head, acc / l, 0.0)
    o_ref[0] = jnp.sum(out.reshape(nq, A_HEADS, width), axis=1)


def _sample_attention(page_table, q, kn, vn, lfn, cache_kt, cache_vt, cache_lf):
    b, n_pages = page_table.shape
    nq = q.shape[1] // A_HEADS
    page = lambda j: pl.BlockSpec((None, A_HEADS, A_HEAD_DIM, PAGE_SIZE), lambda i, pt: (pt[i, j], 0, 0, 0))
    per_seq = lambda a: pl.BlockSpec((1,) + a.shape[1:], lambda i, pt: (i, 0, 0))
    grid_spec = pltpu.PrefetchScalarGridSpec(
        num_scalar_prefetch=1,
        grid=(b,),
        in_specs=([per_seq(q), per_seq(kn), per_seq(vn), per_seq(lfn), _resident(cache_lf.shape)]
                  + [page(j) for j in range(n_pages)] + [page(j) for j in range(n_pages)]),
        out_specs=pl.BlockSpec((1, nq, A_WIDTH), lambda i, pt: (i, 0, 0)),
    )
    return pl.pallas_call(
        functools.partial(_sattn_kernel, n_pages=n_pages),
        out_shape=jax.ShapeDtypeStruct((b, nq, A_WIDTH), F32),
        grid_spec=grid_spec,
        compiler_params=_params("arbitrary"),
        name="paged_fox_attention",
    )(page_table, q, kn, vn, lfn, cache_lf, *([cache_kt] * n_pages), *([cache_vt] * n_pages))


def _split_weights(w_in, b_fox_f, b_ml_i, b_ml_f):
    d = w_in.shape[0]
    o = 0
    parts = {}
    for name, n in (("aq", A_WIDTH), ("ak", A_WIDTH), ("av", A_WIDTH), ("af", A_HEADS), ("bq", B_WIDTH),
                    ("bk", B_WIDTH), ("bv", B_WIDTH), ("bi", B_HEADS), ("bf", B_HEADS), ("bo", B_WIDTH),
                    ("ga", d), ("gb", d)):
        parts[name] = w_in[:, o:o + n]
        o += n
    assert o == w_in.shape[1]
    small = jnp.concatenate([parts["af"], parts["bi"], parts["bf"]], axis=1)
    bias = jnp.concatenate([b_fox_f, b_ml_i, b_ml_f])
    bf = lambda a: a.astype(BF16)
    return {
        "q": bf(parts["aq"] * (A_HEAD_DIM ** -0.5 * LOG2E)),
        "k": bf(parts["ak"]),
        "v": bf(parts["av"]),
        "qk": bf(jnp.concatenate([parts["bq"], parts["bk"]], axis=1)),
        "bv": bf(parts["bv"]),
        "g": bf(jnp.concatenate([parts["bo"], parts["ga"], parts["gb"]], axis=1)),
        "s": bf(jnp.pad(small, ((0, 0), (0, LANES - N_GATES)))),
        "bs": jnp.pad(bias, (0, LANES - N_GATES)).reshape(1, LANES),
        "st": bf(jnp.pad(small.T, ((0, GATE_ROWS - N_GATES), (0, 0)))),
        "bst": jnp.pad(bias, (0, GATE_ROWS - N_GATES)).reshape(GATE_ROWS, 1),
    }


def kernel(x_prompt, x_sample, cache_k, cache_v, cache_logf, page_table, state_C, state_n, state_m, state_conv,
           c_prompt, c_sample, w_ada, b_ada, g_pre_mix, g_post_mix, g_pre_mlp, g_post_mlp, w_in, b_fox_f,
           b_ml_i, b_ml_f, conv_w, conv_b, w_proj_a, w_proj_b, w_out, w_up, w_down):
    assert w_in.shape[0] == 1, "one trunk layer"
    bsz, seq, d = x_prompt.shape
    dbsz, dseq, _ = x_sample.shape

    w_proj = _split_weights(w_in[0], b_fox_f[0], b_ml_i[0], b_ml_f[0])
    wpa, wpb, wo = w_proj_a[0].astype(BF16), w_proj_b[0].astype(BF16), w_out[0].astype(BF16)
    wu, wd = w_up[0].astype(BF16), w_down[0].astype(BF16)
    g1, g2, g3, g4 = g_pre_mix, g_post_mix, g_pre_mlp, g_post_mlp
    cw, cb = conv_w[0], conv_b

    n_c = bsz + dbsz
    c_all = jnp.concatenate([c_prompt, c_sample, jnp.zeros((-n_c % SUBLANES, d), F32)], axis=0)
    mod = _adaln(c_all, w_ada[0], b_ada)

    def token_path(x2, mods, tm, tiles_per_group, token_minor_q):
        sh_m, sc_m, gt_m, sh_f, sc_f, gt_f = mods
        outs = _in_proj(x2, sc_m, sh_m, g1, w_proj, tm, tiles_per_group, token_minor_q)
        return outs, (gt_m, sc_f, sh_f, gt_f)

    def head_split(kt):
        g_, _, n_ = kt.shape
        return jnp.transpose(kt.reshape(g_, A_HEADS, A_HEAD_DIM, n_), (0, 3, 1, 2))[None]

    def finish(x2, ya, hb, og, rest, tm, group_tokens):
        gt_m, sc_f, sh_f, gt_f = rest
        x1, h2 = _mix_out(ya, hb, og, x2, gt_m, sc_f, sh_f, g2, g3, wpa, wpb, wo, tm, group_tokens // tm)
        tm_mlp = min(tm, 256)
        return _mlp(h2, x1, gt_f, g4, wu, wd, tm_mlp, group_tokens // tm_mlp)

    tm = 256
    m_p = bsz * seq
    x2 = x_prompt.reshape(m_p, d)
    mods = tuple(t.reshape(bsz, 1, d) for t in jnp.split(mod[:bsz], N_MOD, axis=-1))
    (qt, kt, vt, ka, qkb, vb, og, sm, smt), rest = token_path(x2, mods, tm, seq // tm, True)

    r3 = lambda a: a.reshape(bsz, seq, a.shape[-1])
    f_heads = _cumsum_lanes(smt)[:, :A_HEADS, :].reshape(bsz, A_HEADS // 2, 2, seq)
    ya = _attention(qt, r3(ka), vt, jnp.transpose(f_heads, (0, 1, 3, 2))).reshape(m_p, A_WIDTH)

    qkc = _conv(r3(qkb), jnp.zeros((bsz, SUBLANES, 2 * B_WIDTH), F32), cw, cb, tc=512)
    hb, c_p, n_p, m_pr = _mlstm(qkc, r3(vb), r3(sm), smt,
                                jnp.zeros((bsz, B_HEADS, B_HEAD_DIM, B_HEAD_DIM), F32),
                                jnp.zeros((bsz, B_HEADS, 1, B_HEAD_DIM), F32),
                                jnp.zeros((bsz, B_HEADS, 1, 1), F32), t=128, nb=2, g0=A_HEADS)
    y_prompt = finish(x2, ya, hb.reshape(m_p, B_WIDTH), og, rest, 2 * tm, seq).reshape(bsz, seq, d)

    new_k_prompt = head_split(kt)
    new_v_prompt = head_split(vt)
    new_logf_prompt = jnp.transpose(smt[:, :A_HEADS, :], (0, 2, 1))[None]
    new_conv_prompt = r3(qkb)[:, seq - (CONV_W - 1):, :][None]

    m_s = dbsz * dseq
    xs2 = x_sample.reshape(m_s, d)
    mods = tuple(t.reshape(1, m_s, d) for t in jnp.split(jnp.repeat(mod[bsz:n_c], dseq, axis=0), N_MOD, axis=-1))
    (qa, kt, vt, ka, va, qkb, vb, og, sm, smt), rest = token_path(xs2, mods, m_s, 1, False)

    n_new = 16
    new_rows = lambda a: jnp.pad(a.reshape(dbsz, dseq, -1), ((0, 0), (0, n_new - dseq), (0, 0)))
    lf_s = sm[:, :A_HEADS].reshape(dbsz, dseq, A_HEADS)
    lf_new = jnp.pad(jnp.transpose(lf_s, (0, 2, 1)), ((0, 0), (0, 0), (0, LANES - dseq)))
    q_heads = qa.reshape(dbsz, dseq, A_HEADS, 1, A_HEAD_DIM)
    q_bd = (q_heads * jnp.eye(A_HEADS, dtype=BF16)[None, None, :, :, None]).reshape(dbsz, dseq * A_HEADS, A_WIDTH)
    ya = _sample_attention(
        page_table, q_bd, new_rows(ka), new_rows(va), lf_new,
        jnp.transpose(cache_k[0], (0, 2, 3, 1)), jnp.transpose(cache_v[0], (0, 2, 3, 1)),
        jnp.transpose(cache_logf[0], (0, 2, 1)))
    ya = ya.astype(BF16).reshape(m_s, A_WIDTH)

    t_s = 16
    conv_in = jnp.pad(qkb.reshape(dbsz, dseq, 2 * B_WIDTH), ((0, 0), (0, SUBLANES - dseq), (0, 0)))
    conv_init = jnp.pad(state_conv[0], ((0, 0), (SUBLANES - (CONV_W - 1), 0), (0, 0)))
    qkc = _conv(conv_in, conv_init, cw, cb, tc=SUBLANES, nb=min(16, dbsz))
    tok_valid = (jnp.arange(t_s) < dseq)[None, :, None]
    pad_t = lambda a: jnp.pad(a, ((0, 0), (0, t_s - a.shape[1]), (0, 0)))
    qkc = jnp.where(tok_valid, pad_t(qkc), jnp.zeros((), BF16))
    vb_s = pad_t(vb.reshape(dbsz, dseq, B_WIDTH))
    gates = pad_t(sm[:, A_HEADS:A_HEADS + 2 * B_HEADS].reshape(dbsz, dseq, 2 * B_HEADS))
    neutral = jnp.concatenate([jnp.full((B_HEADS,), NEG_INF, F32), jnp.zeros((B_HEADS,), F32)])
    gcol = jnp.where(tok_valid, gates, neutral)
    grow = jnp.transpose(gcol, (0, 2, 1))
    hb, c_s, n_s, m_sm = _mlstm(qkc, vb_s, gcol, grow, state_C[0], state_n[0][:, :, None, :],
                                state_m[0][:, :, None, None], t=t_s, nb=8)
    hb = hb[:, :dseq, :].reshape(m_s, B_WIDTH)
    y_sample = finish(xs2, ya, hb, og, rest, m_s, m_s).reshape(dbsz, dseq, d)

    new_k_sample = head_split(kt).reshape(1, dbsz, dseq, A_HEADS, A_HEAD_DIM)
    new_v_sample = head_split(vt).reshape(1, dbsz, dseq, A_HEADS, A_HEAD_DIM)
    new_logf_sample = lf_s[None]
    qkb3 = qkb.reshape(dbsz, dseq, 2 * B_WIDTH)
    new_conv_sample = jnp.concatenate([state_conv[0], qkb3], axis=1)[:, dseq:, :][None]

    return (y_prompt, y_sample, new_k_prompt, new_v_prompt, new_logf_prompt,
            c_p[None], n_p[:, :, 0, :][None], m_pr[:, :, 0, 0][None], new_conv_prompt,
            new_k_sample, new_v_sample, new_logf_sample,
            c_s[None], n_s[:, :, 0, :][None], m_sm[:, :, 0, 0][None], new_conv_sample)
```

```python
import functools

import jax
import jax.numpy as jnp
from jax import lax
from jax.experimental import pallas as pl
from jax.experimental.pallas import tpu as pltpu

F32 = jnp.float32
BF16 = jnp.bfloat16

A_HEADS = 8
A_HEAD_DIM = 64
A_WIDTH = A_HEADS * A_HEAD_DIM
B_HEADS = 4
B_HEAD_DIM = 128
B_WIDTH = B_HEADS * B_HEAD_DIM
CONV_W = 4
N_MOD = 6
RMS_EPS = 1e-6
NEG_INF = -1e30
PAGE_SIZE = 128
LOG2E = 1.4426950408889634

LANES = 128
SUBLANES = 8
VMEM_LIMIT_BYTES = 56 * 1024 * 1024

_NT = (((1,), (1,)), ((), ()))
_TN = (((0,), (0,)), ((), ()))


def _params(*sem):
    return pltpu.CompilerParams(dimension_semantics=sem, vmem_limit_bytes=VMEM_LIMIT_BYTES)


def _resident(shape):
    nd = len(shape)
    return pl.BlockSpec(shape, lambda *_: (0,) * nd, pipeline_mode=pl.Buffered(1))


def _rms(x, g):
    r = lax.rsqrt(jnp.mean(x * x, axis=-1, keepdims=True) + RMS_EPS)
    return (x * r) * g


def _log_sigmoid(x):
    return jnp.minimum(x, 0.0) - jnp.log1p(jnp.exp(-jnp.abs(x)))


def _mod_kernel(c_ref, w_ref, b_ref, o_ref):
    c = c_ref[...]
    a = (c * jax.nn.sigmoid(c)).astype(BF16)
    o_ref[...] = jnp.dot(a, w_ref[...].astype(BF16), preferred_element_type=F32) + b_ref[...]


def _adaln(c, w, b, tn=1024):
    r, d = c.shape
    n = w.shape[1]
    return pl.pallas_call(
        _mod_kernel,
        out_shape=jax.ShapeDtypeStruct((r, n), F32),
        grid=(n // tn,),
        in_specs=[pl.BlockSpec((r, d), lambda j: (0, 0)),
                  pl.BlockSpec((d, tn), lambda j: (0, j)),
                  pl.BlockSpec((1, tn), lambda j: (0, j))],
        out_specs=pl.BlockSpec((r, tn), lambda j: (0, j)),
        compiler_params=_params("arbitrary"),
        name="adaln",
    )(c, w, b)


def _mod_spec(arr, tm, tiles_per_group):
    g, r, d = arr.shape
    if r == 1:
        return pl.BlockSpec((1, 1, d), lambda i: (i // tiles_per_group, 0, 0))
    assert g == 1
    return pl.BlockSpec((1, tm, d), lambda i: (0, i, 0))


N_GATES = A_HEADS + 2 * B_HEADS
GATE_ROWS = 16


def _gate_act(z, idx):
    return jnp.where((idx >= A_HEADS) & (idx < A_HEADS + B_HEADS), z, _log_sigmoid(z))


def _in_kernel(x_ref, sc_ref, sh_ref, g_ref, wq_ref, wk_ref, wv_ref, wqk_ref, wbv_ref, wg_ref,
               ws_ref, bs_ref, wst_ref, bst_ref, q_ref, kt_ref, vt_ref, ka_ref, *rest, token_minor_q):
    if not token_minor_q:
        va_ref, *rest = rest
    qkb_ref, vb_ref, og_ref, sm_ref, smt_ref = rest
    h = _rms(x_ref[...], g_ref[...]) * (1.0 + sc_ref[0]) + sh_ref[0]
    hb = h.astype(BF16)
    q = jnp.dot(hb, wq_ref[...], preferred_element_type=F32)
    if token_minor_q:
        q_ref[0] = q.T.astype(BF16)
    else:
        q_ref[...] = q.astype(BF16)
    k = jnp.dot(hb, wk_ref[...], preferred_element_type=F32)
    kt_ref[0] = k.T
    ka_ref[...] = k.astype(BF16)
    v = jnp.dot(hb, wv_ref[...], preferred_element_type=F32)
    vt_ref[0] = v.T
    if not token_minor_q:
        va_ref[...] = v.astype(BF16)
    qkb_ref[...] = jnp.dot(hb, wqk_ref[...], preferred_element_type=F32)
    vb_ref[...] = jnp.dot(hb, wbv_ref[...], preferred_element_type=F32).astype(BF16)
    og_ref[...] = jax.nn.sigmoid(jnp.dot(hb, wg_ref[...], preferred_element_type=F32)).astype(BF16)
    zs = jnp.dot(hb, ws_ref[...], preferred_element_type=F32) + bs_ref[...]
    sm_ref[...] = _gate_act(zs, lax.broadcasted_iota(jnp.int32, zs.shape, 1))
    zt = lax.dot_general(wst_ref[...], hb, _NT, preferred_element_type=F32) + bst_ref[...]
    smt_ref[0] = _gate_act(zt, lax.broadcasted_iota(jnp.int32, zt.shape, 0))


def _in_proj(x, sc, sh, g, w, tm, tiles_per_group, token_minor_q):
    m, d = x.shape
    group_tokens = tm * tiles_per_group
    groups = m // group_tokens
    row = lambda n: pl.BlockSpec((tm, n), lambda i: (i, 0))
    col = lambda n: pl.BlockSpec((1, n, tm), lambda i: (i // tiles_per_group, 0, i % tiles_per_group))
    rows_out = lambda n, dt: (jax.ShapeDtypeStruct((m, n), dt), row(n))
    cols_out = lambda n, dt=F32: (jax.ShapeDtypeStruct((groups, n, group_tokens), dt), col(n))
    outs = [cols_out(A_WIDTH, BF16) if token_minor_q else rows_out(A_WIDTH, BF16),
            cols_out(A_WIDTH), cols_out(A_WIDTH), rows_out(A_WIDTH, BF16)]
    if not token_minor_q:
        outs.append(rows_out(A_WIDTH, BF16))
    outs += [rows_out(2 * B_WIDTH, F32), rows_out(B_WIDTH, BF16), rows_out(w["g"].shape[1], BF16),
             rows_out(LANES, F32), cols_out(GATE_ROWS)]
    weights = [w["q"], w["k"], w["v"], w["qk"], w["bv"], w["g"], w["s"], w["bs"], w["st"], w["bst"]]
    return pl.pallas_call(
        functools.partial(_in_kernel, token_minor_q=token_minor_q),
        out_shape=[o[0] for o in outs],
        grid=(m // tm,),
        in_specs=[row(d), _mod_spec(sc, tm, tiles_per_group), _mod_spec(sh, tm, tiles_per_group),
                  _resident(g.shape)] + [_resident(a.shape) for a in weights],
        out_specs=[o[1] for o in outs],
        compiler_params=_params("arbitrary"),
        name="in_proj",
    )(x, sc, sh, g, *weights)


MIX_PARTS = 2


def _mix_kernel(ya_ref, hb_ref, og_ref, x_ref, gt_ref, sc_ref, sh_ref, g1_ref, g2_ref,
                wpa_ref, wpb_ref, wo_ref, x1_ref, h2_ref):
    tm, d = x_ref.shape
    parts = [slice(r, r + tm // MIX_PARTS) for r in range(0, tm, tm // MIX_PARTS)]
    mod = lambda ref, p: ref[0] if ref.shape[1] == 1 else ref[0, p, :]
    ya = [jnp.dot(ya_ref[p, :], wpa_ref[...], preferred_element_type=F32) for p in parts]
    yb = [jnp.dot((og_ref[p, :B_WIDTH].astype(F32) * hb_ref[p, :].astype(F32)).astype(BF16), wpb_ref[...],
                  preferred_element_type=F32) for p in parts]
    yo = []
    for p, a, b in zip(parts, ya, yb):
        y = og_ref[p, B_WIDTH:B_WIDTH + d].astype(F32) * a + og_ref[p, B_WIDTH + d:].astype(F32) * b
        yo.append(jnp.dot(y.astype(BF16), wo_ref[...], preferred_element_type=F32))
    for p, o in zip(parts, yo):
        x1 = x_ref[p, :] + mod(gt_ref, p) * _rms(o, g1_ref[...])
        x1_ref[p, :] = x1
        h2_ref[p, :] = (_rms(x1, g2_ref[...]) * (1.0 + mod(sc_ref, p)) + mod(sh_ref, p)).astype(BF16)


def _mix_out(ya, hb, og, x, gt, sc, sh, g1, g2, wpa, wpb, wo, tm, tiles_per_group):
    m, d = x.shape
    row = lambda n: pl.BlockSpec((tm, n), lambda i: (i, 0))
    ms = lambda a: _mod_spec(a, tm, tiles_per_group)
    return pl.pallas_call(
        _mix_kernel,
        out_shape=[jax.ShapeDtypeStruct((m, d), F32), jax.ShapeDtypeStruct((m, d), BF16)],
        grid=(m // tm,),
        in_specs=[row(ya.shape[1]), row(hb.shape[1]), row(og.shape[1]), row(d), ms(gt), ms(sc), ms(sh),
                  _resident(g1.shape), _resident(g2.shape), _resident(wpa.shape), _resident(wpb.shape),
                  _resident(wo.shape)],
        out_specs=[row(d), row(d)],
        compiler_params=_params("arbitrary"),
        name="mix_out",
    )(ya, hb, og, x, gt, sc, sh, g1, g2, wpa, wpb, wo)


def _mlp_kernel(h2_ref, x1_ref, gt_ref, g_ref, wu_ref, wd_ref, y_ref, *, fc):
    h2 = h2_ref[...]
    acc = jnp.zeros(x1_ref.shape, F32)
    for c in range(wu_ref.shape[1] // fc):
        u = jnp.dot(h2, wu_ref[:, c * fc:(c + 1) * fc], preferred_element_type=F32)
        u = jnp.square(jnp.maximum(u, 0.0)).astype(BF16)
        acc = acc + jnp.dot(u, wd_ref[c * fc:(c + 1) * fc, :], preferred_element_type=F32)
    y_ref[...] = x1_ref[...] + gt_ref[0] * _rms(acc, g_ref[...])


def _mlp(h2, x1, gt, g, wu, wd, tm, tiles_per_group, fc=1024):
    m, d = x1.shape
    row = lambda n: pl.BlockSpec((tm, n), lambda i: (i, 0))
    return pl.pallas_call(
        functools.partial(_mlp_kernel, fc=fc),
        out_shape=jax.ShapeDtypeStruct((m, d), F32),
        grid=(m // tm,),
        in_specs=[row(d), row(d), _mod_spec(gt, tm, tiles_per_group), _resident(g.shape),
                  _resident(wu.shape), _resident(wd.shape)],
        out_specs=row(d),
        compiler_params=_params("arbitrary"),
        name="mlp",
    )(h2, x1, gt, g, wu, wd)


def _lane_scan(x, lane, shifts):
    for s in shifts:
        x = x + jnp.where(lane >= s, pltpu.roll(x, s, axis=1), 0.0)
    return x


def _cumsum_kernel(x_ref, o_ref):
    rows, length = x_ref.shape[1], x_ref.shape[2]
    lane = lax.broadcasted_iota(jnp.int32, (rows, LANES), 1)
    carry = jnp.zeros((rows, 1), F32)
    for c in range(length // LANES):
        x = _lane_scan(x_ref[0, :, c * LANES:(c + 1) * LANES], lane, (1, 2, 4, 8, 16, 32, 64)) + carry
        o_ref[0, :, c * LANES:(c + 1) * LANES] = x
        carry = x[:, LANES - 1:LANES]


def _cumsum_lanes(x):
    b, r, length = x.shape
    spec = pl.BlockSpec((1, r, length), lambda i: (i, 0, 0))
    return pl.pallas_call(
        _cumsum_kernel,
        out_shape=jax.ShapeDtypeStruct(x.shape, F32),
        grid=(b,),
        in_specs=[spec],
        out_specs=spec,
        compiler_params=_params("arbitrary"),
        name="logf_cumsum",
    )(x)


V_ROWS = 80


def _aug_lane0(hh):
    return A_HEAD_DIM if hh == 0 else 0


def _attn_kernel(qt_ref, k_ref, vt_ref, fc_ref, o_ref, kaug_ref, vaug_ref, s_ref, *, blk_q, blk_k, chunk):
    qi = pl.program_id(2)
    length = k_ref.shape[1]
    lane = lax.broadcasted_iota(jnp.int32, (1, LANES), 1)
    own = (lane < A_HEAD_DIM, lane >= A_HEAD_DIM)
    feat = lax.broadcasted_iota(jnp.int32, (LANES, 1), 0)

    @pl.when(qi == 0)
    def _():
        ones_row = jnp.where(lax.broadcasted_iota(jnp.int32, (V_ROWS - A_HEAD_DIM, chunk), 0) == 0, 1.0, 0.0)

        def build(c, _):
            toks = pl.ds(pl.multiple_of(c * chunk, chunk), chunk)
            k = k_ref[0, toks, :]
            f = fc_ref[0, 0, toks, :]
            vt = vt_ref[0, :, toks]
            for hh in range(2):
                a0 = _aug_lane0(hh)
                neg_f = f[:, hh:hh + 1] * (-LOG2E)
                hi = neg_f.astype(BF16).astype(F32)
                mid = (neg_f - hi).astype(BF16).astype(F32)
                lo = (neg_f - hi - mid).astype(BF16).astype(F32)
                bias = jnp.where(lane == a0, hi, jnp.where(lane == a0 + 1, mid, jnp.where(lane == a0 + 2, lo, 0.0)))
                kaug_ref[hh, toks, :] = jnp.where(own[hh], k, bias.astype(BF16))
                dims = vt[hh * A_HEAD_DIM:(hh + 1) * A_HEAD_DIM]
                vaug_ref[hh, :, toks] = jnp.concatenate([dims, ones_row], axis=0).astype(BF16)
            return 0
        lax.fori_loop(0, length // chunk, build, 0)

    qt = qt_ref[0]
    qh = []
    for hh in range(2):
        a0 = _aug_lane0(hh)
        ones = jnp.where((feat >= a0) & (feat < a0 + 3), 1.0, 0.0).astype(BF16)
        qh.append(jnp.where((feat < A_HEAD_DIM) if hh == 0 else (feat >= A_HEAD_DIM), qt, ones))

    def keys(j):
        return pl.ds(pl.multiple_of(j * blk_k, blk_k), blk_k)

    def scores(j, slot, c0=0):
        for hh in range(2):
            s_ref[slot, hh, :, c0:] = jnp.dot(kaug_ref[hh, keys(j), :], qh[hh][:, c0:], preferred_element_type=F32)

    def fold(j, slot, carry, causal, c0=0):
        probs, stats = [], []
        for hh in range(2):
            m = carry[hh][0]
            sh = s_ref[slot, hh, :, c0:]
            if causal:
                key = lax.broadcasted_iota(jnp.int32, sh.shape, 0)
                qry = lax.broadcasted_iota(jnp.int32, sh.shape, 1)
                sh = jnp.where(key <= qry, sh, NEG_INF)
            m_new = jnp.maximum(m, jnp.max(sh, axis=0, keepdims=True))
            probs.append(jnp.exp2(sh - m_new).astype(BF16))
            stats.append((m_new, jnp.exp2(m - m_new)))
        return tuple(
            (stats[hh][0], stats[hh][1] * carry[hh][1]
             + jnp.dot(vaug_ref[hh, :, keys(j)], probs[hh], preferred_element_type=F32))
            for hh in range(2))

    assert blk_q == 2 * blk_k
    init = tuple((jnp.full((1, blk_q), NEG_INF, F32), jnp.zeros((V_ROWS, blk_q), F32)) for _ in range(2))

    def body(p, carry):
        scores(2 * p + 1, 1)
        carry = fold(2 * p, 0, carry, False)
        scores(2 * p + 2, 0)
        return fold(2 * p + 1, 1, carry, False)

    scores(0, 0)
    carry = lax.fori_loop(0, qi, body, init)
    scores(2 * qi + 1, 1, blk_k)
    carry = fold(2 * qi, 0, carry, True)
    upper = fold(2 * qi + 1, 1, tuple((m[:, blk_k:], acc[:, blk_k:]) for m, acc in carry), True, blk_k)
    out = []
    for hh in range(2):
        acc = jnp.concatenate([carry[hh][1][:, :blk_k], upper[hh][1]], axis=1)
        out.append(acc[:A_HEAD_DIM] / acc[A_HEAD_DIM:A_HEAD_DIM + 1])
    o_ref[0] = jnp.concatenate(out, axis=0).T.astype(BF16)


def _attention(qt, k, vt, fcol, blk_q=1024, blk_k=512):
    b, length, width = k.shape
    pairs = width // LANES
    blk_q, blk_k = min(blk_q, length), min(blk_k, length)
    return pl.pallas_call(
        functools.partial(_attn_kernel, blk_q=blk_q, blk_k=blk_k, chunk=min(512, length)),
        out_shape=jax.ShapeDtypeStruct(k.shape, BF16),
        grid=(b, pairs, length // blk_q),
        in_specs=[pl.BlockSpec((1, LANES, blk_q), lambda i, p, t: (i, p, t)),
                  pl.BlockSpec((1, length, LANES), lambda i, p, t: (i, 0, p)),
                  pl.BlockSpec((1, LANES, length), lambda i, p, t: (i, p, 0)),
                  pl.BlockSpec((1, 1, length, 2), lambda i, p, t: (i, p, 0, 0))],
        out_specs=pl.BlockSpec((1, blk_q, LANES), lambda i, p, t: (i, t, p)),
        scratch_shapes=[pltpu.VMEM((2, length, LANES), BF16), pltpu.VMEM((2, V_ROWS, length), BF16),
                        pltpu.VMEM((2, 2, blk_k, blk_q), F32)],
        compiler_params=_params("arbitrary", "arbitrary", "arbitrary"),
        name="fox_attention",
    )(qt, k, vt, fcol)


def _conv_silu(xp_ref, w_ref, b_ref, tc):
    y = b_ref[...]
    for j in range(CONV_W):
        off = SUBLANES - (CONV_W - 1) + j
        y = y + w_ref[j:j + 1, :] * xp_ref[off:off + tc, :]
    lane = lax.broadcasted_iota(jnp.int32, (1, y.shape[1]), 1)
    return y * jax.nn.sigmoid(y) * jnp.where(lane < B_WIDTH, 1.0, B_HEAD_DIM ** -0.5)


def _conv_kernel(x_ref, prev_ref, init_ref, w_ref, b_ref, o_ref, xp_ref, *, tc):
    for s in range(x_ref.shape[0]):
        xp_ref[s, 0:SUBLANES, :] = jnp.where(pl.program_id(1) == 0, init_ref[s], prev_ref[s])
        xp_ref[s, SUBLANES:SUBLANES + tc, :] = x_ref[s]
        o_ref[s] = _conv_silu(xp_ref.at[s], w_ref, b_ref, tc).astype(BF16)


def _conv(x, init, w, b, tc, nb=1):
    bsz, length, c = x.shape
    tpb = tc // SUBLANES
    return pl.pallas_call(
        functools.partial(_conv_kernel, tc=tc),
        out_shape=jax.ShapeDtypeStruct(x.shape, BF16),
        grid=(bsz // nb, length // tc),
        in_specs=[pl.BlockSpec((nb, tc, c), lambda i, t: (i, t, 0)),
                  pl.BlockSpec((nb, SUBLANES, c), lambda i, t: (i, jnp.maximum(t * tpb - 1, 0), 0)),
                  pl.BlockSpec((nb, SUBLANES, c), lambda i, t: (i, 0, 0)),
                  pl.BlockSpec((CONV_W, c), lambda i, t: (0, 0)),
                  pl.BlockSpec((1, c), lambda i, t: (0, 0))],
        out_specs=pl.BlockSpec((nb, tc, c), lambda i, t: (i, t, 0)),
        scratch_shapes=[pltpu.VMEM((nb, tc + SUBLANES, c), F32)],
        compiler_params=_params("arbitrary", "arbitrary"),
        name="short_conv",
    )(x, x, init, w, b)


_BNT = (((2,), (2,)), ((0,), (0,)))
_BNN = (((2,), (1,)), ((0,), (0,)))
_BTN = (((1,), (1,)), ((0,), (0,)))


def _mlstm_heads(q, k, v, ir, fr, ic, fc, c_st, n_st, m_st):
    n, t, _ = q.shape
    row = lax.broadcasted_iota(jnp.int32, (n, t, t), 1)
    col = lax.broadcasted_iota(jnp.int32, (n, t, t), 2)
    tril = col <= row
    bcum_c = jnp.sum(jnp.where(tril, fr, 0.0), axis=2, keepdims=True)
    bcum_r = jnp.sum(jnp.where(row <= col, fc, 0.0), axis=1, keepdims=True)
    dlog = jnp.where(tril, bcum_c - bcum_r + ir, NEG_INF)
    inter = bcum_c + m_st
    mt = jnp.maximum(inter, jnp.max(dlog, axis=2, keepdims=True))
    a = jnp.exp(dlog - mt) * lax.dot_general(q, k, _BNT, preferred_element_type=F32)
    si = jnp.exp(inter - mt)
    num = (lax.dot_general(a.astype(BF16), v, _BNN, preferred_element_type=F32)
           + si * lax.dot_general(q, c_st.astype(BF16), _BNN, preferred_element_type=F32))
    den = jnp.sum(a, axis=2, keepdims=True) + si * jnp.sum(q.astype(F32) * n_st, axis=2, keepdims=True)
    h = num / jnp.maximum(jnp.abs(den), jnp.exp(-mt))
    bl = bcum_r[:, :, t - 1:t]
    m_new = jnp.maximum(bl + m_st, jnp.max(bl - bcum_r + ir, axis=2, keepdims=True))
    kw = jnp.exp(bl - bcum_c + ic - m_new) * k.astype(F32)
    decay = jnp.exp(bl + m_st - m_new)
    c_new = decay * c_st + lax.dot_general(kw.astype(BF16), v, _BTN, preferred_element_type=F32)
    n_new = decay * n_st + jnp.sum(kw, axis=1, keepdims=True)
    return h, c_new, n_new, m_new


def _mlstm_kernel(q_ref, k_ref, v_ref, gc_ref, gr_ref, c0_ref, n0_ref, m0_ref,
                  h_ref, c_ref, n_ref, m_ref, *, g0):
    @pl.when(pl.program_id(1) == 0)
    def _():
        c_ref[...] = c0_ref[...]
        n_ref[...] = n0_ref[...]
        m_ref[...] = m0_ref[...]

    nb = q_ref.shape[0]
    where = [(s, hd, slice(hd * B_HEAD_DIM, (hd + 1) * B_HEAD_DIM)) for s in range(nb) for hd in range(B_HEADS)]
    heads = lambda ref: jnp.stack([ref[s, :, sl] for s, _, sl in where])
    gc = [gc_ref[s] for s in range(nb)]
    gr = [gr_ref[s] for s in range(nb)]
    gi, gf = g0, g0 + B_HEADS
    h, c_new, n_new, m_new = _mlstm_heads(
        heads(q_ref), heads(k_ref), heads(v_ref),
        jnp.stack([gr[s][gi + hd:gi + hd + 1, :] for s, hd, _ in where]),
        jnp.stack([gr[s][gf + hd:gf + hd + 1, :] for s, hd, _ in where]),
        jnp.stack([gc[s][:, gi + hd:gi + hd + 1] for s, hd, _ in where]),
        jnp.stack([gc[s][:, gf + hd:gf + hd + 1] for s, hd, _ in where]),
        c_ref[...].reshape((nb * B_HEADS,) + c_ref.shape[2:]),
        n_ref[...].reshape((nb * B_HEADS,) + n_ref.shape[2:]),
        m_ref[...].reshape((nb * B_HEADS,) + m_ref.shape[2:]))
    for i, (s, _, sl) in enumerate(where):
        h_ref[s, :, sl] = h[i].astype(BF16)
    c_ref[...] = c_new.reshape(c_ref.shape)
    n_ref[...] = n_new.reshape(n_ref.shape)
    m_ref[...] = m_new.reshape(m_ref.shape)


def _mlstm(qk, v, gcol, grow, c0, n0, m0, t, nb=1, g0=0):
    b, length, _ = v.shape
    st = lambda a: pl.BlockSpec((nb,) + a.shape[1:], lambda i, j: (i, 0, 0, 0))
    return pl.pallas_call(
        functools.partial(_mlstm_kernel, g0=g0),
        out_shape=[jax.ShapeDtypeStruct(v.shape, BF16), jax.ShapeDtypeStruct(c0.shape, F32),
                   jax.ShapeDtypeStruct(n0.shape, F32), jax.ShapeDtypeStruct(m0.shape, F32)],
        grid=(b // nb, length // t),
        in_specs=[pl.BlockSpec((nb, t, B_WIDTH), lambda i, j: (i, j, 0)),
                  pl.BlockSpec((nb, t, B_WIDTH), lambda i, j: (i, j, 1)),
                  pl.BlockSpec((nb, t, B_WIDTH), lambda i, j: (i, j, 0)),
                  pl.BlockSpec((nb, t, gcol.shape[2]), lambda i, j: (i, j, 0)),
                  pl.BlockSpec((nb, grow.shape[1], t), lambda i, j: (i, 0, j)),
                  st(c0), st(n0), st(m0)],
        out_specs=[pl.BlockSpec((nb, t, B_WIDTH), lambda i, j: (i, j, 0)), st(c0), st(n0), st(m0)],
        compiler_params=_params("arbitrary", "arbitrary"),
        name="mlstm",
    )(qk, qk, v, gcol, grow, c0, n0, m0)


def _sattn_kernel(pt_ref, q_ref, kn_ref, vn_ref, lfn_ref, lf_ref, *rest, n_pages):
    nb, nrow, width = q_ref.shape
    pages = lambda refs, s: refs[s * n_pages:(s + 1) * n_pages]
    k_refs = [pages(rest[:nb * n_pages], s) for s in range(nb)]
    v_refs = [pages(rest[nb * n_pages:2 * nb * n_pages], s) for s in range(nb)]
    o_ref = rest[2 * nb * n_pages]
    first = pl.program_id(0) * nb
    nq = nrow // A_HEADS
    n_new = kn_ref.shape[1]
    lane = lax.broadcasted_iota(jnp.int32, (A_HEADS, LANES), 1)
    per_row = lambda f: jnp.concatenate([f] * nq, axis=0)
    scan = lambda x: _lane_scan(x, lane, (1, 2, 4, 8, 16, 32, 64))

    local = [[scan(lf_ref[pt_ref[first + s, j]]) for j in range(n_pages)] for s in range(nb)]
    new_local = [scan(lfn_ref[s]) for s in range(nb)]
    f_keys, f_new = [], []
    for s in range(nb):
        offset = jnp.zeros((A_HEADS, 1), F32)
        f_keys.append([])
        for j in range(n_pages):
            f = local[s][j] + offset
            offset = f[:, LANES - 1:LANES]
            f_keys[s].append(f)
        f_new.append((new_local[s] + offset)[:, :n_new])

    scores = [[jnp.dot(q_ref[s], k_refs[s][j][...].reshape(width, PAGE_SIZE).astype(BF16),
                       preferred_element_type=F32) - per_row(f_keys[s][j] * LOG2E)
               for j in range(n_pages)] for s in range(nb)]
    q_tok = lax.broadcasted_iota(jnp.int32, (nrow, n_new), 0) // A_HEADS
    k_tok = lax.broadcasted_iota(jnp.int32, (nrow, n_new), 1)
    s_new = [jnp.where(k_tok <= q_tok,
                       lax.dot_general(q_ref[s], kn_ref[s], _NT, preferred_element_type=F32)
                       - per_row(f_new[s] * LOG2E), NEG_INF) for s in range(nb)]

    m = [jnp.maximum(jnp.max(functools.reduce(jnp.maximum, scores[s]), axis=1, keepdims=True),
                     jnp.max(s_new[s], axis=1, keepdims=True)) for s in range(nb)]
    probs = [[jnp.exp2(x - m[s]) for x in scores[s]] for s in range(nb)]
    p_new = [jnp.exp2(s_new[s] - m[s]) for s in range(nb)]
    l = [jnp.sum(functools.reduce(jnp.add, probs[s]), axis=1, keepdims=True)
         + jnp.sum(p_new[s], axis=1, keepdims=True) for s in range(nb)]
    acc = [jnp.dot(p_new[s].astype(BF16), vn_ref[s], preferred_element_type=F32) for s in range(nb)]
    for s in range(nb):
        for j in range(n_pages):
            vt = v_refs[s][j][...].reshape(width, PAGE_SIZE).astype(BF16)
            acc[s] = acc[s] + lax.dot_general(probs[s][j].astype(BF16), vt, _NT, preferred_element_type=F32)
    row_head = lax.broadcasted_iota(jnp.int32, (nrow, width), 0) % A_HEADS
    col_head = lax.broadcasted_iota(jnp.int32, (nrow, width), 1) // A_HEAD_DIM
    for s in range(nb):
        out = jnp.where(row_head == col_head, acc[s] / l[s], 0.0)
        o_ref[s] = jnp.sum(out.reshape(nq, A_HEADS, width), axis=1)


def _sample_attention(page_table, q, kn, vn, lfn, cache_kt, cache_vt, cache_lf, nb=2):
    b, n_pages = page_table.shape
    nq = q.shape[1] // A_HEADS
    page = lambda s, j: pl.BlockSpec((None, A_HEADS, A_HEAD_DIM, PAGE_SIZE),
                                     lambda i, pt: (pt[i * nb + s, j], 0, 0, 0))
    per_seq = lambda a: pl.BlockSpec((nb,) + a.shape[1:], lambda i, pt: (i, 0, 0))
    all_pages = [page(s, j) for s in range(nb) for j in range(n_pages)]
    grid_spec = pltpu.PrefetchScalarGridSpec(
        num_scalar_prefetch=1,
        grid=(b // nb,),
        in_specs=([per_seq(q), per_seq(kn), per_seq(vn), per_seq(lfn), _resident(cache_lf.shape)]
                  + all_pages + all_pages),
        out_specs=pl.BlockSpec((nb, nq, A_WIDTH), lambda i, pt: (i, 0, 0)),
    )
    return pl.pallas_call(
        functools.partial(_sattn_kernel, n_pages=n_pages),
        out_shape=jax.ShapeDtypeStruct((b, nq, A_WIDTH), F32),
        grid_spec=grid_spec,
        compiler_params=_params("arbitrary"),
        name="paged_fox_attention",
    )(page_table, q, kn, vn, lfn, cache_lf, *([cache_kt] * (nb * n_pages)), *([cache_vt] * (nb * n_pages)))


def _split_weights(w_in, b_fox_f, b_ml_i, b_ml_f):
    d = w_in.shape[0]
    o = 0
    parts = {}
    for name, n in (("aq", A_WIDTH), ("ak", A_WIDTH), ("av", A_WIDTH), ("af", A_HEADS), ("bq", B_WIDTH),
                    ("bk", B_WIDTH), ("bv", B_WIDTH), ("bi", B_HEADS), ("bf", B_HEADS), ("bo", B_WIDTH),
                    ("ga", d), ("gb", d)):
        parts[name] = w_in[:, o:o + n]
        o += n
    assert o == w_in.shape[1]
    small = jnp.concatenate([parts["af"], parts["bi"], parts["bf"]], axis=1)
    bias = jnp.concatenate([b_fox_f, b_ml_i, b_ml_f])
    bf = lambda a: a.astype(BF16)
    return {
        "q": bf(parts["aq"] * (A_HEAD_DIM ** -0.5 * LOG2E)),
        "k": bf(parts["ak"]),
        "v": bf(parts["av"]),
        "qk": bf(jnp.concatenate([parts["bq"], parts["bk"]], axis=1)),
        "bv": bf(parts["bv"]),
        "g": bf(jnp.concatenate([parts["bo"], parts["ga"], parts["gb"]], axis=1)),
        "s": bf(jnp.pad(small, ((0, 0), (0, LANES - N_GATES)))),
        "bs": jnp.pad(bias, (0, LANES - N_GATES)).reshape(1, LANES),
        "st": bf(jnp.pad(small.T, ((0, GATE_ROWS - N_GATES), (0, 0)))),
        "bst": jnp.pad(bias, (0, GATE_ROWS - N_GATES)).reshape(GATE_ROWS, 1),
    }


def kernel(x_prompt, x_sample, cache_k, cache_v, cache_logf, page_table, state_C, state_n, state_m, state_conv,
           c_prompt, c_sample, w_ada, b_ada, g_pre_mix, g_post_mix, g_pre_mlp, g_post_mlp, w_in, b_fox_f,
           b_ml_i, b_ml_f, conv_w, conv_b, w_proj_a, w_proj_b, w_out, w_up, w_down):
    assert w_in.shape[0] == 1, "one trunk layer"
    bsz, seq, d = x_prompt.shape
    dbsz, dseq, _ = x_sample.shape

    w_proj = _split_weights(w_in[0], b_fox_f[0], b_ml_i[0], b_ml_f[0])
    wpa, wpb, wo = w_proj_a[0].astype(BF16), w_proj_b[0].astype(BF16), w_out[0].astype(BF16)
    wu, wd = w_up[0].astype(BF16), w_down[0].astype(BF16)
    g1, g2, g3, g4 = g_pre_mix, g_post_mix, g_pre_mlp, g_post_mlp
    cw, cb = conv_w[0], conv_b

    n_c = bsz + dbsz
    c_all = jnp.concatenate([c_prompt, c_sample, jnp.zeros((-n_c % SUBLANES, d), F32)], axis=0)
    mod = _adaln(c_all, w_ada[0], b_ada)

    def token_path(x2, mods, tm, tiles_per_group, token_minor_q):
        sh_m, sc_m, gt_m, sh_f, sc_f, gt_f = mods
        outs = _in_proj(x2, sc_m, sh_m, g1, w_proj, tm, tiles_per_group, token_minor_q)
        return outs, (gt_m, sc_f, sh_f, gt_f)

    def head_split(kt):
        g_, _, n_ = kt.shape
        return jnp.transpose(kt.reshape(g_, A_HEADS, A_HEAD_DIM, n_), (0, 3, 1, 2))[None]

    def finish(x2, ya, hb, og, rest, tm, group_tokens):
        gt_m, sc_f, sh_f, gt_f = rest
        x1, h2 = _mix_out(ya, hb, og, x2, gt_m, sc_f, sh_f, g2, g3, wpa, wpb, wo, tm, group_tokens // tm)
        tm_mlp = min(tm, 256)
        return _mlp(h2, x1, gt_f, g4, wu, wd, tm_mlp, group_tokens // tm_mlp)

    tm = 256
    m_p = bsz * seq
    x2 = x_prompt.reshape(m_p, d)
    mods = tuple(t.reshape(bsz, 1, d) for t in jnp.split(mod[:bsz], N_MOD, axis=-1))
    (qt, kt, vt, ka, qkb, vb, og, sm, smt), rest = token_path(x2, mods, tm, seq // tm, True)

    r3 = lambda a: a.reshape(bsz, seq, a.shape[-1])
    f_heads = _cumsum_lanes(smt)[:, :A_HEADS, :].reshape(bsz, A_HEADS // 2, 2, seq)
    ya = _attention(qt, r3(ka), vt, jnp.transpose(f_heads, (0, 1, 3, 2))).reshape(m_p, A_WIDTH)

    qkc = _conv(r3(qkb), jnp.zeros((bsz, SUBLANES, 2 * B_WIDTH), F32), cw, cb, tc=512)
    hb, c_p, n_p, m_pr = _mlstm(qkc, r3(vb), r3(sm), smt,
                                jnp.zeros((bsz, B_HEADS, B_HEAD_DIM, B_HEAD_DIM), F32),
                                jnp.zeros((bsz, B_HEADS, 1, B_HEAD_DIM), F32),
                                jnp.zeros((bsz, B_HEADS, 1, 1), F32), t=128, nb=2, g0=A_HEADS)
    y_prompt = finish(x2, ya, hb.reshape(m_p, B_WIDTH), og, rest, 2 * tm, seq).reshape(bsz, seq, d)

    new_k_prompt = head_split(kt)
    new_v_prompt = head_split(vt)
    new_logf_prompt = jnp.transpose(smt[:, :A_HEADS, :], (0, 2, 1))[None]
    new_conv_prompt = r3(qkb)[:, seq - (CONV_W - 1):, :][None]

    m_s = dbsz * dseq
    xs2 = x_sample.reshape(m_s, d)
    mods = tuple(t.reshape(1, m_s, d) for t in jnp.split(jnp.repeat(mod[bsz:n_c], dseq, axis=0), N_MOD, axis=-1))
    (qa, kt, vt, ka, va, qkb, vb, og, sm, smt), rest = token_path(xs2, mods, m_s, 1, False)

    n_new = 16
    new_rows = lambda a: jnp.pad(a.reshape(dbsz, dseq, -1), ((0, 0), (0, n_new - dseq), (0, 0)))
    lf_s = sm[:, :A_HEADS].reshape(dbsz, dseq, A_HEADS)
    lf_new = jnp.pad(jnp.transpose(lf_s, (0, 2, 1)), ((0, 0), (0, 0), (0, LANES - dseq)))
    q_heads = qa.reshape(dbsz, dseq, A_HEADS, 1, A_HEAD_DIM)
    q_bd = (q_heads * jnp.eye(A_HEADS, dtype=BF16)[None, None, :, :, None]).reshape(dbsz, dseq * A_HEADS, A_WIDTH)
    ya = _sample_attention(
        page_table, q_bd, new_rows(ka), new_rows(va), lf_new,
        jnp.transpose(cache_k[0], (0, 2, 3, 1)), jnp.transpose(cache_v[0], (0, 2, 3, 1)),
        jnp.transpose(cache_logf[0], (0, 2, 1)))
    ya = ya.astype(BF16).reshape(m_s, A_WIDTH)

    t_s = 16
    conv_in = jnp.pad(qkb.reshape(dbsz, dseq, 2 * B_WIDTH), ((0, 0), (0, SUBLANES - dseq), (0, 0)))
    conv_init = jnp.pad(state_conv[0], ((0, 0), (SUBLANES - (CONV_W - 1), 0), (0, 0)))
    qkc = _conv(conv_in, conv_init, cw, cb, tc=SUBLANES, nb=min(16, dbsz))
    tok_valid = (jnp.arange(t_s) < dseq)[None, :, None]
    pad_t = lambda a: jnp.pad(a, ((0, 0), (0, t_s - a.shape[1]), (0, 0)))
    qkc = jnp.where(tok_valid, pad_t(qkc), jnp.zeros((), BF16))
    vb_s = pad_t(vb.reshape(dbsz, dseq, B_WIDTH))
    gates = pad_t(sm[:, A_HEADS:A_HEADS + 2 * B_HEADS].reshape(dbsz, dseq, 2 * B_HEADS))
    neutral = jnp.concatenate([jnp.full((B_HEADS,), NEG_INF, F32), jnp.zeros((B_HEADS,), F32)])
    gcol = jnp.where(tok_valid, gates, neutral)
    grow = jnp.transpose(gcol, (0, 2, 1))
    hb, c_s, n_s, m_sm = _mlstm(qkc, vb_s, gcol, grow, state_C[0], state_n[0][:, :, None, :],
                                state_m[0][:, :, None, None], t=t_s, nb=8)
    hb = hb[:, :dseq, :].reshape(m_s, B_WIDTH)
    y_sample = finish(xs2, ya, hb, og, rest, m_s, m_s).reshape(dbsz, dseq, d)

    new_k_sample = head_split(kt).reshape(1, dbsz, dseq, A_HEADS, A_HEAD_DIM)
    new_v_sample = head_split(vt).reshape(1, dbsz, dseq, A_HEADS, A_HEAD_DIM)
    new_logf_sample = lf_s[None]
    qkb3 = qkb.reshape(dbsz, dseq, 2 * B_WIDTH)
    new_conv_sample = jnp.concatenate([state_conv[0], qkb3], axis=1)[:, dseq:, :][None]

    return (y_prompt, y_sample, new_k_prompt, new_v_prompt, new_logf_prompt,
            c_p[None], n_p[:, :, 0, :][None], m_pr[:, :, 0, 0][None], new_conv_prompt,
            new_k_sample, new_v_sample, new_logf_sample,
            c_s[None], n_s[:, :, 0, :][None], m_sm[:, :, 0, 0][None], new_conv_sample)
```

```python
import functools

import jax
import jax.numpy as jnp
from jax import lax
from jax.experimental import pallas as pl
from jax.experimental.pallas import tpu as pltpu

F32 = jnp.float32
BF16 = jnp.bfloat16

A_HEADS = 8
A_HEAD_DIM = 64
A_WIDTH = A_HEADS * A_HEAD_DIM
B_HEADS = 4
B_HEAD_DIM = 128
B_WIDTH = B_HEADS * B_HEAD_DIM
CONV_W = 4
N_MOD = 6
RMS_EPS = 1e-6
NEG_INF = -1e30
PAGE_SIZE = 128
LOG2E = 1.4426950408889634

LANES = 128
SUBLANES = 8
VMEM_LIMIT_BYTES = 56 * 1024 * 1024

_NT = (((1,), (1,)), ((), ()))
_TN = (((0,), (0,)), ((), ()))


def _params(*sem):
    return pltpu.CompilerParams(dimension_semantics=sem, vmem_limit_bytes=VMEM_LIMIT_BYTES)


def _resident(shape):
    nd = len(shape)
    return pl.BlockSpec(shape, lambda *_: (0,) * nd, pipeline_mode=pl.Buffered(1))


def _rms(x, g):
    r = lax.rsqrt(jnp.mean(x * x, axis=-1, keepdims=True) + RMS_EPS)
    return (x * r) * g


def _log_sigmoid(x):
    return jnp.minimum(x, 0.0) - jnp.log1p(jnp.exp(-jnp.abs(x)))


def _mod_kernel(c_ref, w_ref, b_ref, o_ref):
    c = c_ref[...]
    a = (c * jax.nn.sigmoid(c)).astype(BF16)
    o_ref[...] = jnp.dot(a, w_ref[...].astype(BF16), preferred_element_type=F32) + b_ref[...]


def _adaln(c, w, b, tn=1024):
    r, d = c.shape
    n = w.shape[1]
    return pl.pallas_call(
        _mod_kernel,
        out_shape=jax.ShapeDtypeStruct((r, n), F32),
        grid=(n // tn,),
        in_specs=[pl.BlockSpec((r, d), lambda j: (0, 0)),
                  pl.BlockSpec((d, tn), lambda j: (0, j)),
                  pl.BlockSpec((1, tn), lambda j: (0, j))],
        out_specs=pl.BlockSpec((r, tn), lambda j: (0, j)),
        compiler_params=_params("arbitrary"),
        name="adaln",
    )(c, w, b)


SHIFT_MIX, SCALE_MIX, GATE_MIX, SHIFT_MLP, SCALE_MLP, GATE_MLP = range(N_MOD)


def _mod_spec(arr, which, tm, tiles_per_group):
    g, r, width = arr.shape
    d = width // N_MOD
    if r == 1:
        return pl.BlockSpec((1, 1, d), lambda i: (i // tiles_per_group, 0, which))
    assert g == 1
    return pl.BlockSpec((1, tm, d), lambda i: (0, i, which))


TOKEN_PARTS = 2
N_GATES = A_HEADS + 2 * B_HEADS
GATE_ROWS = 16


def _gate_act(z, idx):
    return jnp.where((idx >= A_HEADS) & (idx < A_HEADS + B_HEADS), z, _log_sigmoid(z))


def _in_kernel(x_ref, sc_ref, sh_ref, g_ref, wq_ref, wk_ref, wv_ref, wqk_ref, wbv_ref, wg_ref,
               ws_ref, bs_ref, wst_ref, bst_ref, q_ref, kt_ref, vt_ref, ka_ref, *rest, token_minor_q):
    if not token_minor_q:
        va_ref, *rest = rest
    qkb_ref, vb_ref, og_ref, sm_ref, smt_ref = rest
    tm = x_ref.shape[0]
    parts = [slice(r, r + tm // TOKEN_PARTS) for r in range(0, tm, tm // TOKEN_PARTS)]
    mod = lambda ref, p: ref[0] if ref.shape[1] == 1 else ref[0, p, :]
    hb = [(_rms(x_ref[p, :], g_ref[...]) * (1.0 + mod(sc_ref, p)) + mod(sh_ref, p)).astype(BF16) for p in parts]
    proj = lambda w_ref: [jnp.dot(h, w_ref[...], preferred_element_type=F32) for h in hb]
    for p, q in zip(parts, proj(wq_ref)):
        if token_minor_q:
            q_ref[0, :, p] = q.T.astype(BF16)
        else:
            q_ref[p, :] = q.astype(BF16)
    for p, k in zip(parts, proj(wk_ref)):
        kt_ref[0, :, p] = k.T
        ka_ref[p, :] = k.astype(BF16)
    for p, v in zip(parts, proj(wv_ref)):
        vt_ref[0, :, p] = v.T
        if not token_minor_q:
            va_ref[p, :] = v.astype(BF16)
    for p, z in zip(parts, proj(wqk_ref)):
        qkb_ref[p, :] = z
    for p, z in zip(parts, proj(wbv_ref)):
        vb_ref[p, :] = z.astype(BF16)
    for p, z in zip(parts, proj(wg_ref)):
        og_ref[p, :] = jax.nn.sigmoid(z).astype(BF16)
    for p, z in zip(parts, proj(ws_ref)):
        z = z + bs_ref[...]
        sm_ref[p, :] = _gate_act(z, lax.broadcasted_iota(jnp.int32, z.shape, 1))
    for p, h in zip(parts, hb):
        z = lax.dot_general(wst_ref[...], h, _NT, preferred_element_type=F32) + bst_ref[...]
        smt_ref[0, :, p] = _gate_act(z, lax.broadcasted_iota(jnp.int32, z.shape, 0))


def _in_proj(x, mod, g, w, tm, tiles_per_group, token_minor_q):
    m, d = x.shape
    group_tokens = tm * tiles_per_group
    groups = m // group_tokens
    row = lambda n: pl.BlockSpec((tm, n), lambda i: (i, 0))
    col = lambda n: pl.BlockSpec((1, n, tm), lambda i: (i // tiles_per_group, 0, i % tiles_per_group))
    rows_out = lambda n, dt: (jax.ShapeDtypeStruct((m, n), dt), row(n))
    cols_out = lambda n, dt=F32: (jax.ShapeDtypeStruct((groups, n, group_tokens), dt), col(n))
    outs = [cols_out(A_WIDTH, BF16) if token_minor_q else rows_out(A_WIDTH, BF16),
            cols_out(A_WIDTH), cols_out(A_WIDTH), rows_out(A_WIDTH, BF16)]
    if not token_minor_q:
        outs.append(rows_out(A_WIDTH, BF16))
    outs += [rows_out(2 * B_WIDTH, F32), rows_out(B_WIDTH, BF16), rows_out(w["g"].shape[1], BF16),
             rows_out(LANES, F32), cols_out(GATE_ROWS)]
    weights = [w["q"], w["k"], w["v"], w["qk"], w["bv"], w["g"], w["s"], w["bs"], w["st"], w["bst"]]
    return pl.pallas_call(
        functools.partial(_in_kernel, token_minor_q=token_minor_q),
        out_shape=[o[0] for o in outs],
        grid=(m // tm,),
        in_specs=[row(d), _mod_spec(mod, SCALE_MIX, tm, tiles_per_group),
                  _mod_spec(mod, SHIFT_MIX, tm, tiles_per_group),
                  _resident(g.shape)] + [_resident(a.shape) for a in weights],
        out_specs=[o[1] for o in outs],
        compiler_params=_params("arbitrary"),
        name="in_proj",
    )(x, mod, mod, g, *weights)


def _mix_kernel(ya_ref, hb_ref, og_ref, x_ref, gt_ref, sc_ref, sh_ref, g1_ref, g2_ref,
                wpa_ref, wpb_ref, wo_ref, x1_ref, h2_ref):
    tm, d = x_ref.shape
    parts = [slice(r, r + tm // TOKEN_PARTS) for r in range(0, tm, tm // TOKEN_PARTS)]
    mod = lambda ref, p: ref[0] if ref.shape[1] == 1 else ref[0, p, :]
    ya = [jnp.dot(ya_ref[p, :], wpa_ref[...], preferred_element_type=F32) for p in parts]
    yb = [jnp.dot((og_ref[p, :B_WIDTH].astype(F32) * hb_ref[p, :].astype(F32)).astype(BF16), wpb_ref[...],
                  preferred_element_type=F32) for p in parts]
    yo = []
    for p, a, b in zip(parts, ya, yb):
        y = og_ref[p, B_WIDTH:B_WIDTH + d].astype(F32) * a + og_ref[p, B_WIDTH + d:].astype(F32) * b
        yo.append(jnp.dot(y.astype(BF16), wo_ref[...], preferred_element_type=F32))
    for p, o in zip(parts, yo):
        x1 = x_ref[p, :] + mod(gt_ref, p) * _rms(o, g1_ref[...])
        x1_ref[p, :] = x1
        h2_ref[p, :] = (_rms(x1, g2_ref[...]) * (1.0 + mod(sc_ref, p)) + mod(sh_ref, p)).astype(BF16)


def _mix_out(ya, hb, og, x, mod, g1, g2, wpa, wpb, wo, tm, tiles_per_group):
    m, d = x.shape
    row = lambda n: pl.BlockSpec((tm, n), lambda i: (i, 0))
    ms = lambda which: _mod_spec(mod, which, tm, tiles_per_group)
    return pl.pallas_call(
        _mix_kernel,
        out_shape=[jax.ShapeDtypeStruct((m, d), F32), jax.ShapeDtypeStruct((m, d), BF16)],
        grid=(m // tm,),
        in_specs=[row(ya.shape[1]), row(hb.shape[1]), row(og.shape[1]), row(d),
                  ms(GATE_MIX), ms(SCALE_MLP), ms(SHIFT_MLP),
                  _resident(g1.shape), _resident(g2.shape), _resident(wpa.shape), _resident(wpb.shape),
                  _resident(wo.shape)],
        out_specs=[row(d), row(d)],
        compiler_params=_params("arbitrary"),
        name="mix_out",
    )(ya, hb, og, x, mod, mod, mod, g1, g2, wpa, wpb, wo)


def _mlp_kernel(h2_ref, x1_ref, gt_ref, g_ref, wu_ref, wd_ref, y_ref, *, fc):
    tm = x1_ref.shape[0]
    parts = [slice(r, r + tm // TOKEN_PARTS) for r in range(0, tm, tm // TOKEN_PARTS)]
    mod = lambda ref, p: ref[0] if ref.shape[1] == 1 else ref[0, p, :]
    acc = [jnp.zeros((tm // TOKEN_PARTS, x1_ref.shape[1]), F32) for _ in parts]
    for c in range(wu_ref.shape[1] // fc):
        u = [jnp.dot(h2_ref[p, :], wu_ref[:, c * fc:(c + 1) * fc], preferred_element_type=F32) for p in parts]
        u = [jnp.square(jnp.maximum(x, 0.0)).astype(BF16) for x in u]
        acc = [a + jnp.dot(x, wd_ref[c * fc:(c + 1) * fc, :], preferred_element_type=F32) for a, x in zip(acc, u)]
    for p, a in zip(parts, acc):
        y_ref[p, :] = x1_ref[p, :] + mod(gt_ref, p) * _rms(a, g_ref[...])


def _mlp(h2, x1, mod, g, wu, wd, tm, tiles_per_group, fc=1024):
    m, d = x1.shape
    row = lambda n: pl.BlockSpec((tm, n), lambda i: (i, 0))
    return pl.pallas_call(
        functools.partial(_mlp_kernel, fc=fc),
        out_shape=jax.ShapeDtypeStruct((m, d), F32),
        grid=(m // tm,),
        in_specs=[row(d), row(d), _mod_spec(mod, GATE_MLP, tm, tiles_per_group), _resident(g.shape),
                  _resident(wu.shape), _resident(wd.shape)],
        out_specs=row(d),
        compiler_params=_params("arbitrary"),
        name="mlp",
    )(h2, x1, mod, g, wu, wd)


def _lane_scan(x, lane, shifts):
    for s in shifts:
        x = x + jnp.where(lane >= s, pltpu.roll(x, s, axis=1), 0.0)
    return x


def _cumsum_kernel(x_ref, o_ref):
    rows, length = x_ref.shape[1], x_ref.shape[2]
    lane = lax.broadcasted_iota(jnp.int32, (rows, LANES), 1)
    carry = jnp.zeros((rows, 1), F32)
    fill = jnp.zeros((LANES - rows, LANES), F32)
    for c in range(length // LANES):
        x = _lane_scan(x_ref[0, :, c * LANES:(c + 1) * LANES], lane, (1, 2, 4, 8, 16, 32, 64)) + carry
        carry = x[:, LANES - 1:LANES]
        by_token = jnp.concatenate([x, fill], axis=0).T
        for p in range(o_ref.shape[1]):
            o_ref[0, p, c * LANES:(c + 1) * LANES, :] = by_token[:, 2 * p:2 * p + 2]


def _cumsum_token_major(x, pairs):
    b, r, length = x.shape
    return pl.pallas_call(
        _cumsum_kernel,
        out_shape=jax.ShapeDtypeStruct((b, pairs, length, 2), F32),
        grid=(b,),
        in_specs=[pl.BlockSpec((1, r, length), lambda i: (i, 0, 0))],
        out_specs=pl.BlockSpec((1, pairs, length, 2), lambda i: (i, 0, 0, 0)),
        compiler_params=_params("arbitrary"),
        name="logf_cumsum",
    )(x)


V_ROWS = 80


def _aug_lane0(hh):
    return A_HEAD_DIM if hh == 0 else 0


def _attn_kernel(qt_ref, k_ref, vt_ref, fc_ref, o_ref, kaug_ref, vaug_ref, s_ref, *, blk_q, blk_k, chunk):
    qi = pl.program_id(2)
    length = k_ref.shape[1]
    lane = lax.broadcasted_iota(jnp.int32, (1, LANES), 1)
    own = (lane < A_HEAD_DIM, lane >= A_HEAD_DIM)
    feat = lax.broadcasted_iota(jnp.int32, (LANES, 1), 0)

    @pl.when(qi == 0)
    def _():
        ones_row = jnp.where(lax.broadcasted_iota(jnp.int32, (V_ROWS - A_HEAD_DIM, chunk), 0) == 0, 1.0, 0.0)

        def build(c, _):
            toks = pl.ds(pl.multiple_of(c * chunk, chunk), chunk)
            k = k_ref[0, toks, :]
            f = fc_ref[0, 0, toks, :]
            vt = vt_ref[0, :, toks]
            for hh in range(2):
                a0 = _aug_lane0(hh)
                neg_f = f[:, hh:hh + 1] * (-LOG2E)
                hi = neg_f.astype(BF16).astype(F32)
                mid = (neg_f - hi).astype(BF16).astype(F32)
                lo = (neg_f - hi - mid).astype(BF16).astype(F32)
                bias = jnp.where(lane == a0, hi, jnp.where(lane == a0 + 1, mid, jnp.where(lane == a0 + 2, lo, 0.0)))
                kaug_ref[hh, toks, :] = jnp.where(own[hh], k, bias.astype(BF16))
                dims = vt[hh * A_HEAD_DIM:(hh + 1) * A_HEAD_DIM]
                vaug_ref[hh, :, toks] = jnp.concatenate([dims, ones_row], axis=0).astype(BF16)
            return 0
        lax.fori_loop(0, length // chunk, build, 0)

    qt = qt_ref[0]
    qh = []
    for hh in range(2):
        a0 = _aug_lane0(hh)
        ones = jnp.where((feat >= a0) & (feat < a0 + 3), 1.0, 0.0).astype(BF16)
        qh.append(jnp.where((feat < A_HEAD_DIM) if hh == 0 else (feat >= A_HEAD_DIM), qt, ones))

    def keys(j):
        return pl.ds(pl.multiple_of(j * blk_k, blk_k), blk_k)

    def scores(j, slot, c0=0):
        for hh in range(2):
            s_ref[slot, hh, :, c0:] = jnp.dot(kaug_ref[hh, keys(j), :], qh[hh][:, c0:], preferred_element_type=F32)

    def fold(j, slot, carry, causal, c0=0):
        probs, stats = [], []
        for hh in range(2):
            m = carry[hh][0]
            sh = s_ref[slot, hh, :, c0:]
            if causal:
                key = lax.broadcasted_iota(jnp.int32, sh.shape, 0)
                qry = lax.broadcasted_iota(jnp.int32, sh.shape, 1)
                sh = jnp.where(key <= qry, sh, NEG_INF)
            m_new = jnp.maximum(m, jnp.max(sh, axis=0, keepdims=True))
            probs.append(jnp.exp2(sh - m_new).astype(BF16))
            stats.append((m_new, jnp.exp2(m - m_new)))
        return tuple(
            (stats[hh][0], stats[hh][1] * carry[hh][1]
             + jnp.dot(vaug_ref[hh, :, keys(j)], probs[hh], preferred_element_type=F32))
            for hh in range(2))

    assert blk_q == 2 * blk_k
    init = tuple((jnp.full((1, blk_q), NEG_INF, F32), jnp.zeros((V_ROWS, blk_q), F32)) for _ in range(2))

    def body(p, carry):
        scores(2 * p + 1, 1)
        carry = fold(2 * p, 0, carry, False)
        scores(2 * p + 2, 0)
        return fold(2 * p + 1, 1, carry, False)

    scores(0, 0)
    carry = lax.fori_loop(0, qi, body, init)
    scores(2 * qi + 1, 1, blk_k)
    carry = fold(2 * qi, 0, carry, True)
    upper = fold(2 * qi + 1, 1, tuple((m[:, blk_k:], acc[:, blk_k:]) for m, acc in carry), True, blk_k)
    out = []
    for hh in range(2):
        acc = jnp.concatenate([carry[hh][1][:, :blk_k], upper[hh][1]], axis=1)
        out.append(acc[:A_HEAD_DIM] / acc[A_HEAD_DIM:A_HEAD_DIM + 1])
    o_ref[0] = jnp.concatenate(out, axis=0).T.astype(BF16)


def _attention(qt, k, vt, fcol, blk_q=1024, blk_k=512):
    b, length, width = k.shape
    pairs = width // LANES
    blk_q, blk_k = min(blk_q, length), min(blk_k, length)
    return pl.pallas_call(
        functools.partial(_attn_kernel, blk_q=blk_q, blk_k=blk_k, chunk=min(512, length)),
        out_shape=jax.ShapeDtypeStruct(k.shape, BF16),
        grid=(b, pairs, length // blk_q),
        in_specs=[pl.BlockSpec((1, LANES, blk_q), lambda i, p, t: (i, p, t)),
                  pl.BlockSpec((1, length, LANES), lambda i, p, t: (i, 0, p)),
                  pl.BlockSpec((1, LANES, length), lambda i, p, t: (i, p, 0)),
                  pl.BlockSpec((1, 1, length, 2), lambda i, p, t: (i, p, 0, 0))],
        out_specs=pl.BlockSpec((1, blk_q, LANES), lambda i, p, t: (i, t, p)),
        scratch_shapes=[pltpu.VMEM((2, length, LANES), BF16), pltpu.VMEM((2, V_ROWS, length), BF16),
                        pltpu.VMEM((2, 2, blk_k, blk_q), F32)],
        compiler_params=_params("arbitrary", "arbitrary", "arbitrary"),
        name="fox_attention",
    )(qt, k, vt, fcol)


def _conv_silu(xp_ref, w_ref, b_ref, tc):
    y = b_ref[...]
    for j in range(CONV_W):
        off = SUBLANES - (CONV_W - 1) + j
        y = y + w_ref[j:j + 1, :] * xp_ref[off:off + tc, :]
    lane = lax.broadcasted_iota(jnp.int32, (1, y.shape[1]), 1)
    return y * jax.nn.sigmoid(y) * jnp.where(lane < B_WIDTH, 1.0, B_HEAD_DIM ** -0.5)


def _conv_kernel(x_ref, prev_ref, init_ref, w_ref, b_ref, o_ref, xp_ref, *, tc):
    for s in range(x_ref.shape[0]):
        xp_ref[s, 0:SUBLANES, :] = jnp.where(pl.program_id(1) == 0, init_ref[s], prev_ref[s])
        xp_ref[s, SUBLANES:SUBLANES + tc, :] = x_ref[s]
        o_ref[s] = _conv_silu(xp_ref.at[s], w_ref, b_ref, tc).astype(BF16)


def _conv(x, init, w, b, tc, nb=1):
    bsz, length, c = x.shape
    tpb = tc // SUBLANES
    return pl.pallas_call(
        functools.partial(_conv_kernel, tc=tc),
        out_shape=jax.ShapeDtypeStruct(x.shape, BF16),
        grid=(bsz // nb, length // tc),
        in_specs=[pl.BlockSpec((nb, tc, c), lambda i, t: (i, t, 0)),
                  pl.BlockSpec((nb, SUBLANES, c), lambda i, t: (i, jnp.maximum(t * tpb - 1, 0), 0)),
                  pl.BlockSpec((nb, SUBLANES, c), lambda i, t: (i, 0, 0)),
                  pl.BlockSpec((CONV_W, c), lambda i, t: (0, 0)),
                  pl.BlockSpec((1, c), lambda i, t: (0, 0))],
        out_specs=pl.BlockSpec((nb, tc, c), lambda i, t: (i, t, 0)),
        scratch_shapes=[pltpu.VMEM((nb, tc + SUBLANES, c), F32)],
        compiler_params=_params("arbitrary", "arbitrary"),
        name="short_conv",
    )(x, x, init, w, b)


_BNT = (((2,), (2,)), ((0,), (0,)))
_BNN = (((2,), (1,)), ((0,), (0,)))
_BTN = (((1,), (1,)), ((0,), (0,)))


def _mlstm_heads(q, k, v, ir, fr, ic, fc, c_st, n_st, m_st):
    n, t, _ = q.shape
    row = lax.broadcasted_iota(jnp.int32, (n, t, t), 1)
    col = lax.broadcasted_iota(jnp.int32, (n, t, t), 2)
    tril = col <= row
    bcum_c = jnp.sum(jnp.where(tril, fr, 0.0), axis=2, keepdims=True)
    bcum_r = jnp.sum(jnp.where(row <= col, fc, 0.0), axis=1, keepdims=True)
    dlog = jnp.where(tril, bcum_c - bcum_r + ir, NEG_INF)
    inter = bcum_c + m_st
    mt = jnp.maximum(inter, jnp.max(dlog, axis=2, keepdims=True))
    a = jnp.exp(dlog - mt) * lax.dot_general(q, k, _BNT, preferred_element_type=F32)
    si = jnp.exp(inter - mt)
    num = (lax.dot_general(a.astype(BF16), v, _BNN, preferred_element_type=F32)
           + si * lax.dot_general(q, c_st.astype(BF16), _BNN, preferred_element_type=F32))
    den = jnp.sum(a, axis=2, keepdims=True) + si * jnp.sum(q.astype(F32) * n_st, axis=2, keepdims=True)
    h = num / jnp.maximum(jnp.abs(den), jnp.exp(-mt))
    bl = bcum_r[:, :, t - 1:t]
    m_new = jnp.maximum(bl + m_st, jnp.max(bl - bcum_r + ir, axis=2, keepdims=True))
    kw = jnp.exp(bl - bcum_c + ic - m_new) * k.astype(F32)
    decay = jnp.exp(bl + m_st - m_new)
    c_new = decay * c_st + lax.dot_general(kw.astype(BF16), v, _BTN, preferred_element_type=F32)
    n_new = decay * n_st + jnp.sum(kw, axis=1, keepdims=True)
    return h, c_new, n_new, m_new


def _mlstm_kernel(q_ref, k_ref, v_ref, gc_ref, gr_ref, c0_ref, n0_ref, m0_ref,
                  h_ref, c_ref, n_ref, m_ref, *, g0):
    @pl.when(pl.program_id(1) == 0)
    def _():
        c_ref[...] = c0_ref[...]
        n_ref[...] = n0_ref[...]
        m_ref[...] = m0_ref[...]

    nb = q_ref.shape[0]
    where = [(s, hd, slice(hd * B_HEAD_DIM, (hd + 1) * B_HEAD_DIM)) for s in range(nb) for hd in range(B_HEADS)]
    heads = lambda ref: jnp.stack([ref[s, :, sl] for s, _, sl in where])
    gc = [gc_ref[s] for s in range(nb)]
    gr = [gr_ref[s] for s in range(nb)]
    gi, gf = g0, g0 + B_HEADS
    h, c_new, n_new, m_new = _mlstm_heads(
        heads(q_ref), heads(k_ref), heads(v_ref),
        jnp.stack([gr[s][gi + hd:gi + hd + 1, :] for s, hd, _ in where]),
        jnp.stack([gr[s][gf + hd:gf + hd + 1, :] for s, hd, _ in where]),
        jnp.stack([gc[s][:, gi + hd:gi + hd + 1] for s, hd, _ in where]),
        jnp.stack([gc[s][:, gf + hd:gf + hd + 1] for s, hd, _ in where]),
        c_ref[...].reshape((nb * B_HEADS,) + c_ref.shape[2:]),
        n_ref[...].reshape((nb * B_HEADS,) + n_ref.shape[2:]),
        m_ref[...].reshape((nb * B_HEADS,) + m_ref.shape[2:]))
    for i, (s, _, sl) in enumerate(where):
        h_ref[s, :, sl] = h[i].astype(BF16)
    c_ref[...] = c_new.reshape(c_ref.shape)
    n_ref[...] = n_new.reshape(n_ref.shape)
    m_ref[...] = m_new.reshape(m_ref.shape)


def _mlstm(qk, v, gcol, grow, c0, n0, m0, t, nb=1, g0=0):
    b, length, _ = v.shape
    st = lambda a: pl.BlockSpec((nb,) + a.shape[1:], lambda i, j: (i, 0, 0, 0))
    return pl.pallas_call(
        functools.partial(_mlstm_kernel, g0=g0),
        out_shape=[jax.ShapeDtypeStruct(v.shape, BF16), jax.ShapeDtypeStruct(c0.shape, F32),
                   jax.ShapeDtypeStruct(n0.shape, F32), jax.ShapeDtypeStruct(m0.shape, F32)],
        grid=(b // nb, length // t),
        in_specs=[pl.BlockSpec((nb, t, B_WIDTH), lambda i, j: (i, j, 0)),
                  pl.BlockSpec((nb, t, B_WIDTH), lambda i, j: (i, j, 1)),
                  pl.BlockSpec((nb, t, B_WIDTH), lambda i, j: (i, j, 0)),
                  pl.BlockSpec((nb, t, gcol.shape[2]), lambda i, j: (i, j, 0)),
                  pl.BlockSpec((nb, grow.shape[1], t), lambda i, j: (i, 0, j)),
                  st(c0), st(n0), st(m0)],
        out_specs=[pl.BlockSpec((nb, t, B_WIDTH), lambda i, j: (i, j, 0)), st(c0), st(n0), st(m0)],
        compiler_params=_params("arbitrary", "arbitrary"),
        name="mlstm",
    )(qk, qk, v, gcol, grow, c0, n0, m0)


def _sattn_kernel(pt_ref, q_ref, kn_ref, vn_ref, lfn_ref, lf_ref, *rest, n_pages):
    nb, nrow, width = q_ref.shape
    pages = lambda refs, s: refs[s * n_pages:(s + 1) * n_pages]
    k_refs = [pages(rest[:nb * n_pages], s) for s in range(nb)]
    v_refs = [pages(rest[nb * n_pages:2 * nb * n_pages], s) for s in range(nb)]
    o_ref = rest[2 * nb * n_pages]
    first = pl.program_id(0) * nb
    nq = nrow // A_HEADS
    n_new = kn_ref.shape[1]
    lane = lax.broadcasted_iota(jnp.int32, (A_HEADS, LANES), 1)
    per_row = lambda f: jnp.concatenate([f] * nq, axis=0)
    scan = lambda x: _lane_scan(x, lane, (1, 2, 4, 8, 16, 32, 64))

    local = [[scan(lf_ref[pt_ref[first + s, j]]) for j in range(n_pages)] for s in range(nb)]
    new_local = [scan(lfn_ref[s]) for s in range(nb)]
    f_keys, f_new = [], []
    for s in range(nb):
        offset = jnp.zeros((A_HEADS, 1), F32)
        f_keys.append([])
        for j in range(n_pages):
            f = local[s][j] + offset
            offset = f[:, LANES - 1:LANES]
            f_keys[s].append(f)
        f_new.append((new_local[s] + offset)[:, :n_new])

    scores = [[jnp.dot(q_ref[s], k_refs[s][j][...].reshape(width, PAGE_SIZE).astype(BF16),
                       preferred_element_type=F32) - per_row(f_keys[s][j] * LOG2E)
               for j in range(n_pages)] for s in range(nb)]
    q_tok = lax.broadcasted_iota(jnp.int32, (nrow, n_new), 0) // A_HEADS
    k_tok = lax.broadcasted_iota(jnp.int32, (nrow, n_new), 1)
    s_new = [jnp.where(k_tok <= q_tok,
                       lax.dot_general(q_ref[s], kn_ref[s], _NT, preferred_element_type=F32)
                       - per_row(f_new[s] * LOG2E), NEG_INF) for s in range(nb)]

    m = [jnp.maximum(jnp.max(functools.reduce(jnp.maximum, scores[s]), axis=1, keepdims=True),
                     jnp.max(s_new[s], axis=1, keepdims=True)) for s in range(nb)]
    probs = [[jnp.exp2(x - m[s]) for x in scores[s]] for s in range(nb)]
    p_new = [jnp.exp2(s_new[s] - m[s]) for s in range(nb)]
    l = [jnp.sum(functools.reduce(jnp.add, probs[s]), axis=1, keepdims=True)
         + jnp.sum(p_new[s], axis=1, keepdims=True) for s in range(nb)]
    acc = [jnp.dot(p_new[s].astype(BF16), vn_ref[s], preferred_element_type=F32) for s in range(nb)]
    for s in range(nb):
        for j in range(n_pages):
            vt = v_refs[s][j][...].reshape(width, PAGE_SIZE).astype(BF16)
            acc[s] = acc[s] + lax.dot_general(probs[s][j].astype(BF16), vt, _NT, preferred_element_type=F32)
    row_head = lax.broadcasted_iota(jnp.int32, (nrow, width), 0) % A_HEADS
    col_head = lax.broadcasted_iota(jnp.int32, (nrow, width), 1) // A_HEAD_DIM
    for s in range(nb):
        out = jnp.where(row_head == col_head, acc[s] / l[s], 0.0)
        o_ref[s] = jnp.sum(out.reshape(nq, A_HEADS, width), axis=1)


def _sample_attention(page_table, q, kn, vn, lfn, cache_kt, cache_vt, cache_lf, nb=2):
    b, n_pages = page_table.shape
    nq = q.shape[1] // A_HEADS
    page = lambda s, j: pl.BlockSpec((None, A_HEADS, A_HEAD_DIM, PAGE_SIZE),
                                     lambda i, pt: (pt[i * nb + s, j], 0, 0, 0))
    per_seq = lambda a: pl.BlockSpec((nb,) + a.shape[1:], lambda i, pt: (i, 0, 0))
    all_pages = [page(s, j) for s in range(nb) for j in range(n_pages)]
    grid_spec = pltpu.PrefetchScalarGridSpec(
        num_scalar_prefetch=1,
        grid=(b // nb,),
        in_specs=([per_seq(q), per_seq(kn), per_seq(vn), per_seq(lfn), _resident(cache_lf.shape)]
                  + all_pages + all_pages),
        out_specs=pl.BlockSpec((nb, nq, A_WIDTH), lambda i, pt: (i, 0, 0)),
    )
    return pl.pallas_call(
        functools.partial(_sattn_kernel, n_pages=n_pages),
        out_shape=jax.ShapeDtypeStruct((b, nq, A_WIDTH), F32),
        grid_spec=grid_spec,
        compiler_params=_params("arbitrary"),
        name="paged_fox_attention",
    )(page_table, q, kn, vn, lfn, cache_lf, *([cache_kt] * (nb * n_pages)), *([cache_vt] * (nb * n_pages)))


def _split_weights(w_in, b_fox_f, b_ml_i, b_ml_f):
    d = w_in.shape[0]
    o = 0
    parts = {}
    for name, n in (("aq", A_WIDTH), ("ak", A_WIDTH), ("av", A_WIDTH), ("af", A_HEADS), ("bq", B_WIDTH),
                    ("bk", B_WIDTH), ("bv", B_WIDTH), ("bi", B_HEADS), ("bf", B_HEADS), ("bo", B_WIDTH),
                    ("ga", d), ("gb", d)):
        parts[name] = w_in[:, o:o + n]
        o += n
    assert o == w_in.shape[1]
    small = jnp.concatenate([parts["af"], parts["bi"], parts["bf"]], axis=1)
    bias = jnp.concatenate([b_fox_f, b_ml_i, b_ml_f])
    bf = lambda a: a.astype(BF16)
    return {
        "q": bf(parts["aq"] * (A_HEAD_DIM ** -0.5 * LOG2E)),
        "k": bf(parts["ak"]),
        "v": bf(parts["av"]),
        "qk": bf(jnp.concatenate([parts["bq"], parts["bk"]], axis=1)),
        "bv": bf(parts["bv"]),
        "g": bf(jnp.concatenate([parts["bo"], parts["ga"], parts["gb"]], axis=1)),
        "s": bf(jnp.pad(small, ((0, 0), (0, LANES - N_GATES)))),
        "bs": jnp.pad(bias, (0, LANES - N_GATES)).reshape(1, LANES),
        "st": bf(jnp.pad(small.T, ((0, GATE_ROWS - N_GATES), (0, 0)))),
        "bst": jnp.pad(bias, (0, GATE_ROWS - N_GATES)).reshape(GATE_ROWS, 1),
    }


def kernel(x_prompt, x_sample, cache_k, cache_v, cache_logf, page_table, state_C, state_n, state_m, state_conv,
           c_prompt, c_sample, w_ada, b_ada, g_pre_mix, g_post_mix, g_pre_mlp, g_post_mlp, w_in, b_fox_f,
           b_ml_i, b_ml_f, conv_w, conv_b, w_proj_a, w_proj_b, w_out, w_up, w_down):
    assert w_in.shape[0] == 1, "one trunk layer"
    bsz, seq, d = x_prompt.shape
    dbsz, dseq, _ = x_sample.shape

    w_proj = _split_weights(w_in[0], b_fox_f[0], b_ml_i[0], b_ml_f[0])
    wpa, wpb, wo = w_proj_a[0].astype(BF16), w_proj_b[0].astype(BF16), w_out[0].astype(BF16)
    wu, wd = w_up[0].astype(BF16), w_down[0].astype(BF16)
    g1, g2, g3, g4 = g_pre_mix, g_post_mix, g_pre_mlp, g_post_mlp
    cw, cb = conv_w[0], conv_b

    n_c = bsz + dbsz
    c_all = jnp.concatenate([c_prompt, c_sample, jnp.zeros((-n_c % SUBLANES, d), F32)], axis=0)
    mod = _adaln(c_all, w_ada[0], b_ada)

    def token_path(x2, mods, tm, tiles_per_group, token_minor_q):
        return _in_proj(x2, mods, g1, w_proj, tm, tiles_per_group, token_minor_q), mods

    def head_split(kt):
        g_, _, n_ = kt.shape
        return jnp.transpose(kt.reshape(g_, A_HEADS, A_HEAD_DIM, n_), (0, 3, 1, 2))[None]

    def finish(x2, ya, hb, og, mods, tm, group_tokens):
        x1, h2 = _mix_out(ya, hb, og, x2, mods, g2, g3, wpa, wpb, wo, tm, group_tokens // tm)
        return _mlp(h2, x1, mods, g4, wu, wd, tm, group_tokens // tm)

    tm = min(512, seq)
    m_p = bsz * seq
    x2 = x_prompt.reshape(m_p, d)
    mods = mod[:bsz].reshape(bsz, 1, N_MOD * d)
    (qt, kt, vt, ka, qkb, vb, og, sm, smt), rest = token_path(x2, mods, tm, seq // tm, True)

    r3 = lambda a: a.reshape(bsz, seq, a.shape[-1])
    ya = _attention(qt, r3(ka), vt, _cumsum_token_major(smt, A_HEADS // 2)).reshape(m_p, A_WIDTH)

    qkc = _conv(r3(qkb), jnp.zeros((bsz, SUBLANES, 2 * B_WIDTH), F32), cw, cb, tc=512)
    hb, c_p, n_p, m_pr = _mlstm(qkc, r3(vb), r3(sm), smt,
                                jnp.zeros((bsz, B_HEADS, B_HEAD_DIM, B_HEAD_DIM), F32),
                                jnp.zeros((bsz, B_HEADS, 1, B_HEAD_DIM), F32),
                                jnp.zeros((bsz, B_HEADS, 1, 1), F32), t=128, nb=2, g0=A_HEADS)
    y_prompt = finish(x2, ya, hb.reshape(m_p, B_WIDTH), og, rest, tm, seq).reshape(bsz, seq, d)

    new_k_prompt = head_split(kt)
    new_v_prompt = head_split(vt)
    new_logf_prompt = jnp.transpose(smt[:, :A_HEADS, :], (0, 2, 1))[None]
    new_conv_prompt = r3(qkb)[:, seq - (CONV_W - 1):, :][None]

    m_s = dbsz * dseq
    xs2 = x_sample.reshape(m_s, d)
    mods = jnp.repeat(mod[bsz:n_c], dseq, axis=0).reshape(1, m_s, N_MOD * d)
    (qa, kt, vt, ka, va, qkb, vb, og, sm, smt), rest = token_path(xs2, mods, m_s, 1, False)

    n_new = 16
    new_rows = lambda a: jnp.pad(a.reshape(dbsz, dseq, -1), ((0, 0), (0, n_new - dseq), (0, 0)))
    lf_s = sm[:, :A_HEADS].reshape(dbsz, dseq, A_HEADS)
    lf_new = jnp.pad(jnp.transpose(lf_s, (0, 2, 1)), ((0, 0), (0, 0), (0, LANES - dseq)))
    q_heads = qa.reshape(dbsz, dseq, A_HEADS, 1, A_HEAD_DIM)
    q_bd = (q_heads * jnp.eye(A_HEADS, dtype=BF16)[None, None, :, :, None]).reshape(dbsz, dseq * A_HEADS, A_WIDTH)
    ya = _sample_attention(
        page_table, q_bd, new_rows(ka), new_rows(va), lf_new,
        jnp.transpose(cache_k[0], (0, 2, 3, 1)), jnp.transpose(cache_v[0], (0, 2, 3, 1)),
        jnp.transpose(cache_logf[0], (0, 2, 1)))
    ya = ya.astype(BF16).reshape(m_s, A_WIDTH)

    t_s = 16
    conv_in = jnp.pad(qkb.reshape(dbsz, dseq, 2 * B_WIDTH), ((0, 0), (0, SUBLANES - dseq), (0, 0)))
    conv_init = jnp.pad(state_conv[0], ((0, 0), (SUBLANES - (CONV_W - 1), 0), (0, 0)))
    qkc = _conv(conv_in, conv_init, cw, cb, tc=SUBLANES, nb=min(16, dbsz))
    tok_valid = (jnp.arange(t_s) < dseq)[None, :, None]
    pad_t = lambda a: jnp.pad(a, ((0, 0), (0, t_s - a.shape[1]), (0, 0)))
    qkc = jnp.where(tok_valid, pad_t(qkc), jnp.zeros((), BF16))
    vb_s = pad_t(vb.reshape(dbsz, dseq, B_WIDTH))
    gates = pad_t(sm[:, A_HEADS:A_HEADS + 2 * B_HEADS].reshape(dbsz, dseq, 2 * B_HEADS))
    neutral = jnp.concatenate([jnp.full((B_HEADS,), NEG_INF, F32), jnp.zeros((B_HEADS,), F32)])
    gcol = jnp.where(tok_valid, gates, neutral)
    grow = jnp.transpose(gcol, (0, 2, 1))
    hb, c_s, n_s, m_sm = _mlstm(qkc, vb_s, gcol, grow, state_C[0], state_n[0][:, :, None, :],
                                state_m[0][:, :, None, None], t=t_s, nb=8)
    hb = hb[:, :dseq, :].reshape(m_s, B_WIDTH)
    y_sample = finish(xs2, ya, hb, og, rest, m_s, m_s).reshape(dbsz, dseq, d)

    new_k_sample = head_split(kt).reshape(1, dbsz, dseq, A_HEADS, A_HEAD_DIM)
    new_v_sample = head_split(vt).reshape(1, dbsz, dseq, A_HEADS, A_HEAD_DIM)
    new_logf_sample = lf_s[None]
    qkb3 = qkb.reshape(dbsz, dseq, 2 * B_WIDTH)
    new_conv_sample = jnp.concatenate([state_conv[0], qkb3], axis=1)[:, dseq:, :][None]

    return (y_prompt, y_sample, new_k_prompt, new_v_prompt, new_logf_prompt,
            c_p[None], n_p[:, :, 0, :][None], m_pr[:, :, 0, 0][None], new_conv_prompt,
            new_k_sample, new_v_sample, new_logf_sample,
            c_s[None], n_s[:, :, 0, :][None], m_sm[:, :, 0, 0][None], new_conv_sample)
```

```python
import functools

import jax
import jax.numpy as jnp
from jax import lax
from jax.experimental import pallas as pl
from jax.experimental.pallas import tpu as pltpu

F32 = jnp.float32
BF16 = jnp.bfloat16

A_HEADS = 8
A_HEAD_DIM = 64
A_WIDTH = A_HEADS * A_HEAD_DIM
B_HEADS = 4
B_HEAD_DIM = 128
B_WIDTH = B_HEADS * B_HEAD_DIM
CONV_W = 4
N_MOD = 6
RMS_EPS = 1e-6
NEG_INF = -1e30
PAGE_SIZE = 128
LOG2E = 1.4426950408889634

LANES = 128
SUBLANES = 8
VMEM_LIMIT_BYTES = 56 * 1024 * 1024

_NT = (((1,), (1,)), ((), ()))
_TN = (((0,), (0,)), ((), ()))


def _params(*sem):
    return pltpu.CompilerParams(dimension_semantics=sem, vmem_limit_bytes=VMEM_LIMIT_BYTES)


def _resident(shape):
    nd = len(shape)
    return pl.BlockSpec(shape, lambda *_: (0,) * nd, pipeline_mode=pl.Buffered(1))


def _rms(x, g):
    r = lax.rsqrt(jnp.mean(x * x, axis=-1, keepdims=True) + RMS_EPS)
    return (x * r) * g


def _log_sigmoid(x):
    return jnp.minimum(x, 0.0) - jnp.log1p(jnp.exp(-jnp.abs(x)))


def _mod_kernel(c_ref, w_ref, b_ref, o_ref):
    c = c_ref[...]
    a = (c * jax.nn.sigmoid(c)).astype(BF16)
    o_ref[...] = jnp.dot(a, w_ref[...].astype(BF16), preferred_element_type=F32) + b_ref[...]


def _adaln(c, w, b, tn=1024):
    r, d = c.shape
    n = w.shape[1]
    return pl.pallas_call(
        _mod_kernel,
        out_shape=jax.ShapeDtypeStruct((r, n), F32),
        grid=(n // tn,),
        in_specs=[pl.BlockSpec((r, d), lambda j: (0, 0)),
                  pl.BlockSpec((d, tn), lambda j: (0, j)),
                  pl.BlockSpec((1, tn), lambda j: (0, j))],
        out_specs=pl.BlockSpec((r, tn), lambda j: (0, j)),
        compiler_params=_params("arbitrary"),
        name="adaln",
    )(c, w, b)


SHIFT_MIX, SCALE_MIX, GATE_MIX, SHIFT_MLP, SCALE_MLP, GATE_MLP = range(N_MOD)


def _mod_spec(arr, which, tm, tiles_per_group):
    g, r, width = arr.shape
    d = width // N_MOD
    if r == 1:
        return pl.BlockSpec((1, 1, d), lambda i: (i // tiles_per_group, 0, which))
    assert g == 1
    return pl.BlockSpec((1, tm, d), lambda i: (0, i, which))


TOKEN_PARTS = 2
N_GATES = A_HEADS + 2 * B_HEADS
GATE_ROWS = 16


def _gate_act(z, idx):
    return jnp.where((idx >= A_HEADS) & (idx < A_HEADS + B_HEADS), z, _log_sigmoid(z))


def _in_kernel(x_ref, sc_ref, sh_ref, g_ref, wq_ref, wk_ref, wv_ref, wqk_ref, wbv_ref, wg_ref,
               ws_ref, bs_ref, wst_ref, bst_ref, q_ref, kt_ref, vt_ref, ka_ref, *rest, token_minor_q):
    if not token_minor_q:
        va_ref, *rest = rest
    qkb_ref, vb_ref, og_ref, sm_ref, smt_ref = rest
    tm = x_ref.shape[0]
    parts = [slice(r, r + tm // TOKEN_PARTS) for r in range(0, tm, tm // TOKEN_PARTS)]
    mod = lambda ref, p: ref[0] if ref.shape[1] == 1 else ref[0, p, :]
    hb = [(_rms(x_ref[p, :], g_ref[...]) * (1.0 + mod(sc_ref, p)) + mod(sh_ref, p)).astype(BF16) for p in parts]
    proj = lambda w_ref: [jnp.dot(h, w_ref[...], preferred_element_type=F32) for h in hb]
    for p, q in zip(parts, proj(wq_ref)):
        if token_minor_q:
            q_ref[0, :, p] = q.T.astype(BF16)
        else:
            q_ref[p, :] = q.astype(BF16)
    for p, k in zip(parts, proj(wk_ref)):
        kt_ref[0, :, p] = k.T
        ka_ref[p, :] = k.astype(BF16)
    for p, v in zip(parts, proj(wv_ref)):
        vt_ref[0, :, p] = v.T
        if not token_minor_q:
            va_ref[p, :] = v.astype(BF16)
    for p, z in zip(parts, proj(wqk_ref)):
        qkb_ref[p, :] = z
    for p, z in zip(parts, proj(wbv_ref)):
        vb_ref[p, :] = z.astype(BF16)
    for p, z in zip(parts, proj(wg_ref)):
        og_ref[p, :] = jax.nn.sigmoid(z).astype(BF16)
    for p, z in zip(parts, proj(ws_ref)):
        z = z + bs_ref[...]
        sm_ref[p, :] = _gate_act(z, lax.broadcasted_iota(jnp.int32, z.shape, 1))
    for p, h in zip(parts, hb):
        z = lax.dot_general(wst_ref[...], h, _NT, preferred_element_type=F32) + bst_ref[...]
        smt_ref[0, :, p] = _gate_act(z, lax.broadcasted_iota(jnp.int32, z.shape, 0))


def _in_proj(x, mod, g, w, tm, tiles_per_group, token_minor_q):
    m, d = x.shape
    group_tokens = tm * tiles_per_group
    groups = m // group_tokens
    row = lambda n: pl.BlockSpec((tm, n), lambda i: (i, 0))
    col = lambda n: pl.BlockSpec((1, n, tm), lambda i: (i // tiles_per_group, 0, i % tiles_per_group))
    rows_out = lambda n, dt: (jax.ShapeDtypeStruct((m, n), dt), row(n))
    cols_out = lambda n, dt=F32: (jax.ShapeDtypeStruct((groups, n, group_tokens), dt), col(n))
    outs = [cols_out(A_WIDTH, BF16) if token_minor_q else rows_out(A_WIDTH, BF16),
            cols_out(A_WIDTH), cols_out(A_WIDTH), rows_out(A_WIDTH, BF16)]
    if not token_minor_q:
        outs.append(rows_out(A_WIDTH, BF16))
    outs += [rows_out(2 * B_WIDTH, F32), rows_out(B_WIDTH, BF16), rows_out(w["g"].shape[1], BF16),
             rows_out(LANES, F32), cols_out(GATE_ROWS)]
    weights = [w["q"], w["k"], w["v"], w["qk"], w["bv"], w["g"], w["s"], w["bs"], w["st"], w["bst"]]
    return pl.pallas_call(
        functools.partial(_in_kernel, token_minor_q=token_minor_q),
        out_shape=[o[0] for o in outs],
        grid=(m // tm,),
        in_specs=[row(d), _mod_spec(mod, SCALE_MIX, tm, tiles_per_group),
                  _mod_spec(mod, SHIFT_MIX, tm, tiles_per_group),
                  _resident(g.shape)] + [_resident(a.shape) for a in weights],
        out_specs=[o[1] for o in outs],
        compiler_params=_params("arbitrary"),
        name="in_proj",
    )(x, mod, mod, g, *weights)


def _mix_kernel(ya_ref, hb_ref, og_ref, x_ref, gt_ref, sc_ref, sh_ref, g1_ref, g2_ref,
                wpa_ref, wpb_ref, wo_ref, x1_ref, h2_ref):
    tm, d = x_ref.shape
    parts = [slice(r, r + tm // TOKEN_PARTS) for r in range(0, tm, tm // TOKEN_PARTS)]
    mod = lambda ref, p: ref[0] if ref.shape[1] == 1 else ref[0, p, :]
    ya = [jnp.dot(ya_ref[p, :], wpa_ref[...], preferred_element_type=F32) for p in parts]
    yb = [jnp.dot((og_ref[p, :B_WIDTH].astype(F32) * hb_ref[p, :].astype(F32)).astype(BF16), wpb_ref[...],
                  preferred_element_type=F32) for p in parts]
    yo = []
    for p, a, b in zip(parts, ya, yb):
        y = og_ref[p, B_WIDTH:B_WIDTH + d].astype(F32) * a + og_ref[p, B_WIDTH + d:].astype(F32) * b
        yo.append(jnp.dot(y.astype(BF16), wo_ref[...], preferred_element_type=F32))
    for p, o in zip(parts, yo):
        x1 = x_ref[p, :] + mod(gt_ref, p) * _rms(o, g1_ref[...])
        x1_ref[p, :] = x1
        h2_ref[p, :] = (_rms(x1, g2_ref[...]) * (1.0 + mod(sc_ref, p)) + mod(sh_ref, p)).astype(BF16)


def _mix_out(ya, hb, og, x, mod, g1, g2, wpa, wpb, wo, tm, tiles_per_group):
    m, d = x.shape
    row = lambda n: pl.BlockSpec((tm, n), lambda i: (i, 0))
    ms = lambda which: _mod_spec(mod, which, tm, tiles_per_group)
    return pl.pallas_call(
        _mix_kernel,
        out_shape=[jax.ShapeDtypeStruct((m, d), F32), jax.ShapeDtypeStruct((m, d), BF16)],
        grid=(m // tm,),
        in_specs=[row(ya.shape[1]), row(hb.shape[1]), row(og.shape[1]), row(d),
                  ms(GATE_MIX), ms(SCALE_MLP), ms(SHIFT_MLP),
                  _resident(g1.shape), _resident(g2.shape), _resident(wpa.shape), _resident(wpb.shape),
                  _resident(wo.shape)],
        out_specs=[row(d), row(d)],
        compiler_params=_params("arbitrary"),
        name="mix_out",
    )(ya, hb, og, x, mod, mod, mod, g1, g2, wpa, wpb, wo)


def _mlp_kernel(h2_ref, x1_ref, gt_ref, g_ref, wu_ref, wd_ref, y_ref, *, fc):
    tm = x1_ref.shape[0]
    parts = [slice(r, r + tm // TOKEN_PARTS) for r in range(0, tm, tm // TOKEN_PARTS)]
    mod = lambda ref, p: ref[0] if ref.shape[1] == 1 else ref[0, p, :]
    acc = [jnp.zeros((tm // TOKEN_PARTS, x1_ref.shape[1]), F32) for _ in parts]
    for c in range(wu_ref.shape[1] // fc):
        u = [jnp.dot(h2_ref[p, :], wu_ref[:, c * fc:(c + 1) * fc], preferred_element_type=F32) for p in parts]
        u = [jnp.square(jnp.maximum(x, 0.0)).astype(BF16) for x in u]
        acc = [a + jnp.dot(x, wd_ref[c * fc:(c + 1) * fc, :], preferred_element_type=F32) for a, x in zip(acc, u)]
    for p, a in zip(parts, acc):
        y_ref[p, :] = x1_ref[p, :] + mod(gt_ref, p) * _rms(a, g_ref[...])


def _mlp(h2, x1, mod, g, wu, wd, tm, tiles_per_group, fc=1024):
    m, d = x1.shape
    row = lambda n: pl.BlockSpec((tm, n), lambda i: (i, 0))
    return pl.pallas_call(
        functools.partial(_mlp_kernel, fc=fc),
        out_shape=jax.ShapeDtypeStruct((m, d), F32),
        grid=(m // tm,),
        in_specs=[row(d), row(d), _mod_spec(mod, GATE_MLP, tm, tiles_per_group), _resident(g.shape),
                  _resident(wu.shape), _resident(wd.shape)],
        out_specs=row(d),
        compiler_params=_params("arbitrary"),
        name="mlp",
    )(h2, x1, mod, g, wu, wd)


def _lane_scan(x, lane, shifts):
    for s in shifts:
        x = x + jnp.where(lane >= s, pltpu.roll(x, s, axis=1), 0.0)
    return x


def _cumsum_kernel(x_ref, o_ref):
    rows, length = x_ref.shape[1], x_ref.shape[2]
    lane = lax.broadcasted_iota(jnp.int32, (rows, LANES), 1)
    carry = jnp.zeros((rows, 1), F32)
    for c in range(length // LANES):
        x = _lane_scan(x_ref[0, :, c * LANES:(c + 1) * LANES], lane, (1, 2, 4, 8, 16, 32, 64)) + carry
        o_ref[0, :, c * LANES:(c + 1) * LANES] = x
        carry = x[:, LANES - 1:LANES]


def _cumsum_lanes(x):
    b, r, length = x.shape
    spec = pl.BlockSpec((1, r, length), lambda i: (i, 0, 0))
    return pl.pallas_call(
        _cumsum_kernel,
        out_shape=jax.ShapeDtypeStruct(x.shape, F32),
        grid=(b,),
        in_specs=[spec],
        out_specs=spec,
        compiler_params=_params("arbitrary"),
        name="logf_cumsum",
    )(x)


V_ROWS = 80


def _aug_lane0(hh):
    return A_HEAD_DIM if hh == 0 else 0


def _attn_kernel(qt_ref, k_ref, vt_ref, f_ref, o_ref, kaug_ref, vaug_ref, s_ref, *, blk_q, blk_k, chunk):
    pair = pl.program_id(1)
    qi = pl.program_id(2)
    length = k_ref.shape[1]
    lane = lax.broadcasted_iota(jnp.int32, (1, LANES), 1)
    own = (lane < A_HEAD_DIM, lane >= A_HEAD_DIM)
    feat = lax.broadcasted_iota(jnp.int32, (LANES, 1), 0)

    @pl.when(qi == 0)
    def _():
        ones_row = jnp.where(lax.broadcasted_iota(jnp.int32, (V_ROWS - A_HEAD_DIM, chunk), 0) == 0, 1.0, 0.0)

        def build(c, _):
            toks = pl.ds(pl.multiple_of(c * chunk, chunk), chunk)
            k = k_ref[0, toks, :]
            vt = vt_ref[0, :, toks]
            for hh in range(2):
                a0 = _aug_lane0(hh)
                f_row = f_ref[0, pl.ds(2 * pair + hh, 1), toks] * (-LOG2E)
                neg_f = jnp.concatenate(
                    [jnp.broadcast_to(f_row[:, t0:t0 + LANES], (LANES, LANES)).T for t0 in range(0, chunk, LANES)],
                    axis=0)
                hi = neg_f.astype(BF16).astype(F32)
                mid = (neg_f - hi).astype(BF16).astype(F32)
                lo = (neg_f - hi - mid).astype(BF16).astype(F32)
                bias = jnp.where(lane == a0, hi, jnp.where(lane == a0 + 1, mid, jnp.where(lane == a0 + 2, lo, 0.0)))
                kaug_ref[hh, toks, :] = jnp.where(own[hh], k, bias.astype(BF16))
                dims = vt[hh * A_HEAD_DIM:(hh + 1) * A_HEAD_DIM]
                vaug_ref[hh, :, toks] = jnp.concatenate([dims, ones_row], axis=0).astype(BF16)
            return 0
        lax.fori_loop(0, length // chunk, build, 0)

    qt = qt_ref[0]
    qh = []
    for hh in range(2):
        a0 = _aug_lane0(hh)
        ones = jnp.where((feat >= a0) & (feat < a0 + 3), 1.0, 0.0).astype(BF16)
        qh.append(jnp.where((feat < A_HEAD_DIM) if hh == 0 else (feat >= A_HEAD_DIM), qt, ones))

    def keys(j):
        return pl.ds(pl.multiple_of(j * blk_k, blk_k), blk_k)

    def scores(j, slot, c0=0):
        for hh in range(2):
            s_ref[slot, hh, :, c0:] = jnp.dot(kaug_ref[hh, keys(j), :], qh[hh][:, c0:], preferred_element_type=F32)

    def fold(j, slot, carry, causal, c0=0):
        probs, stats = [], []
        for hh in range(2):
            m = carry[hh][0]
            sh = s_ref[slot, hh, :, c0:]
            if causal:
                key = lax.broadcasted_iota(jnp.int32, sh.shape, 0)
                qry = lax.broadcasted_iota(jnp.int32, sh.shape, 1)
                sh = jnp.where(key <= qry, sh, NEG_INF)
            m_new = jnp.maximum(m, jnp.max(sh, axis=0, keepdims=True))
            probs.append(jnp.exp2(sh - m_new).astype(BF16))
            stats.append((m_new, jnp.exp2(m - m_new)))
        return tuple(
            (stats[hh][0], stats[hh][1] * carry[hh][1]
             + jnp.dot(vaug_ref[hh, :, keys(j)], probs[hh], preferred_element_type=F32))
            for hh in range(2))

    assert blk_q == 2 * blk_k
    init = tuple((jnp.full((1, blk_q), NEG_INF, F32), jnp.zeros((V_ROWS, blk_q), F32)) for _ in range(2))

    def body(p, carry):
        scores(2 * p + 1, 1)
        carry = fold(2 * p, 0, carry, False)
        scores(2 * p + 2, 0)
        return fold(2 * p + 1, 1, carry, False)

    scores(0, 0)
    carry = lax.fori_loop(0, qi, body, init)
    scores(2 * qi + 1, 1, blk_k)
    carry = fold(2 * qi, 0, carry, True)
    upper = fold(2 * qi + 1, 1, tuple((m[:, blk_k:], acc[:, blk_k:]) for m, acc in carry), True, blk_k)
    out = []
    for hh in range(2):
        acc = jnp.concatenate([carry[hh][1][:, :blk_k], upper[hh][1]], axis=1)
        out.append(acc[:A_HEAD_DIM] / acc[A_HEAD_DIM:A_HEAD_DIM + 1])
    o_ref[0] = jnp.concatenate(out, axis=0).T.astype(BF16)


def _attention(qt, k, vt, f, blk_q=1024, blk_k=512):
    b, length, width = k.shape
    pairs = width // LANES
    blk_q, blk_k = min(blk_q, length), min(blk_k, length)
    return pl.pallas_call(
        functools.partial(_attn_kernel, blk_q=blk_q, blk_k=blk_k, chunk=min(512, length)),
        out_shape=jax.ShapeDtypeStruct(k.shape, BF16),
        grid=(b, pairs, length // blk_q),
        in_specs=[pl.BlockSpec((1, LANES, blk_q), lambda i, p, t: (i, p, t)),
                  pl.BlockSpec((1, length, LANES), lambda i, p, t: (i, 0, p)),
                  pl.BlockSpec((1, LANES, length), lambda i, p, t: (i, p, 0)),
                  pl.BlockSpec((1,) + f.shape[1:], lambda i, p, t: (i, 0, 0))],
        out_specs=pl.BlockSpec((1, blk_q, LANES), lambda i, p, t: (i, t, p)),
        scratch_shapes=[pltpu.VMEM((2, length, LANES), BF16), pltpu.VMEM((2, V_ROWS, length), BF16),
                        pltpu.VMEM((2, 2, blk_k, blk_q), F32)],
        compiler_params=_params("arbitrary", "arbitrary", "arbitrary"),
        name="fox_attention",
    )(qt, k, vt, f)


def _conv_silu(xp_ref, w_ref, b_ref, tc):
    y = b_ref[...]
    for j in range(CONV_W):
        off = SUBLANES - (CONV_W - 1) + j
        y = y + w_ref[j:j + 1, :] * xp_ref[off:off + tc, :]
    lane = lax.broadcasted_iota(jnp.int32, (1, y.shape[1]), 1)
    return y * jax.nn.sigmoid(y) * jnp.where(lane < B_WIDTH, 1.0, B_HEAD_DIM ** -0.5)


def _conv_kernel(x_ref, prev_ref, init_ref, w_ref, b_ref, o_ref, xp_ref, *, tc):
    for s in range(x_ref.shape[0]):
        xp_ref[s, 0:SUBLANES, :] = jnp.where(pl.program_id(1) == 0, init_ref[s], prev_ref[s])
        xp_ref[s, SUBLANES:SUBLANES + tc, :] = x_ref[s]
        o_ref[s] = _conv_silu(xp_ref.at[s], w_ref, b_ref, tc).astype(BF16)


def _conv(x, init, w, b, tc, nb=1):
    bsz, length, c = x.shape
    tpb = tc // SUBLANES
    return pl.pallas_call(
        functools.partial(_conv_kernel, tc=tc),
        out_shape=jax.ShapeDtypeStruct(x.shape, BF16),
        grid=(bsz // nb, length // tc),
        in_specs=[pl.BlockSpec((nb, tc, c), lambda i, t: (i, t, 0)),
                  pl.BlockSpec((nb, SUBLANES, c), lambda i, t: (i, jnp.maximum(t * tpb - 1, 0), 0)),
                  pl.BlockSpec((nb, SUBLANES, c), lambda i, t: (i, 0, 0)),
                  pl.BlockSpec((CONV_W, c), lambda i, t: (0, 0)),
                  pl.BlockSpec((1, c), lambda i, t: (0, 0))],
        out_specs=pl.BlockSpec((nb, tc, c), lambda i, t: (i, t, 0)),
        scratch_shapes=[pltpu.VMEM((nb, tc + SUBLANES, c), F32)],
        compiler_params=_params("arbitrary", "arbitrary"),
        name="short_conv",
    )(x, x, init, w, b)


_BNT = (((2,), (2,)), ((0,), (0,)))
_BNN = (((2,), (1,)), ((0,), (0,)))
_BTN = (((1,), (1,)), ((0,), (0,)))


def _mlstm_heads(q, k, v, ir, fr, ic, fc, c_st, n_st, m_st):
    n, t, _ = q.shape
    row = lax.broadcasted_iota(jnp.int32, (n, t, t), 1)
    col = lax.broadcasted_iota(jnp.int32, (n, t, t), 2)
    tril = col <= row
    bcum_c = jnp.sum(jnp.where(tril, fr, 0.0), axis=2, keepdims=True)
    bcum_r = jnp.sum(jnp.where(row <= col, fc, 0.0), axis=1, keepdims=True)
    dlog = jnp.where(tril, bcum_c - bcum_r + ir, NEG_INF)
    inter = bcum_c + m_st
    mt = jnp.maximum(inter, jnp.max(dlog, axis=2, keepdims=True))
    a = jnp.exp(dlog - mt) * lax.dot_general(q, k, _BNT, preferred_element_type=F32)
    si = jnp.exp(inter - mt)
    num = (lax.dot_general(a.astype(BF16), v, _BNN, preferred_element_type=F32)
           + si * lax.dot_general(q, c_st.astype(BF16), _BNN, preferred_element_type=F32))
    den = jnp.sum(a, axis=2, keepdims=True) + si * jnp.sum(q.astype(F32) * n_st, axis=2, keepdims=True)
    h = num / jnp.maximum(jnp.abs(den), jnp.exp(-mt))
    bl = bcum_r[:, :, t - 1:t]
    m_new = jnp.maximum(bl + m_st, jnp.max(bl - bcum_r + ir, axis=2, keepdims=True))
    kw = jnp.exp(bl - bcum_c + ic - m_new) * k.astype(F32)
    decay = jnp.exp(bl + m_st - m_new)
    c_new = decay * c_st + lax.dot_general(kw.astype(BF16), v, _BTN, preferred_element_type=F32)
    n_new = decay * n_st + jnp.sum(kw, axis=1, keepdims=True)
    return h, c_new, n_new, m_new


def _mlstm_kernel(q_ref, k_ref, v_ref, gc_ref, gr_ref, c0_ref, n0_ref, m0_ref,
                  h_ref, c_ref, n_ref, m_ref, *, g0):
    @pl.when(pl.program_id(1) == 0)
    def _():
        c_ref[...] = c0_ref[...]
        n_ref[...] = n0_ref[...]
        m_ref[...] = m0_ref[...]

    nb = q_ref.shape[0]
    where = [(s, hd, slice(hd * B_HEAD_DIM, (hd + 1) * B_HEAD_DIM)) for s in range(nb) for hd in range(B_HEADS)]
    heads = lambda ref: jnp.stack([ref[s, :, sl] for s, _, sl in where])
    gc = [gc_ref[s] for s in range(nb)]
    gr = [gr_ref[s] for s in range(nb)]
    gi, gf = g0, g0 + B_HEADS
    h, c_new, n_new, m_new = _mlstm_heads(
        heads(q_ref), heads(k_ref), heads(v_ref),
        jnp.stack([gr[s][gi + hd:gi + hd + 1, :] for s, hd, _ in where]),
        jnp.stack([gr[s][gf + hd:gf + hd + 1, :] for s, hd, _ in where]),
        jnp.stack([gc[s][:, gi + hd:gi + hd + 1] for s, hd, _ in where]),
        jnp.stack([gc[s][:, gf + hd:gf + hd + 1] for s, hd, _ in where]),
        c_ref[...].reshape((nb * B_HEADS,) + c_ref.shape[2:]),
        n_ref[...].reshape((nb * B_HEADS,) + n_ref.shape[2:]),
        m_ref[...].reshape((nb * B_HEADS,) + m_ref.shape[2:]))
    for i, (s, _, sl) in enumerate(where):
        h_ref[s, :, sl] = h[i].astype(BF16)
    c_ref[...] = c_new.reshape(c_ref.shape)
    n_ref[...] = n_new.reshape(n_ref.shape)
    m_ref[...] = m_new.reshape(m_ref.shape)


def _mlstm(qk, v, gcol, grow, c0, n0, m0, t, nb=1, g0=0):
    b, length, _ = v.shape
    st = lambda a: pl.BlockSpec((nb,) + a.shape[1:], lambda i, j: (i, 0, 0, 0))
    return pl.pallas_call(
        functools.partial(_mlstm_kernel, g0=g0),
        out_shape=[jax.ShapeDtypeStruct(v.shape, BF16), jax.ShapeDtypeStruct(c0.shape, F32),
                   jax.ShapeDtypeStruct(n0.shape, F32), jax.ShapeDtypeStruct(m0.shape, F32)],
        grid=(b // nb, length // t),
        in_specs=[pl.BlockSpec((nb, t, B_WIDTH), lambda i, j: (i, j, 0)),
                  pl.BlockSpec((nb, t, B_WIDTH), lambda i, j: (i, j, 1)),
                  pl.BlockSpec((nb, t, B_WIDTH), lambda i, j: (i, j, 0)),
                  pl.BlockSpec((nb, t, gcol.shape[2]), lambda i, j: (i, j, 0)),
                  pl.BlockSpec((nb, grow.shape[1], t), lambda i, j: (i, 0, j)),
                  st(c0), st(n0), st(m0)],
        out_specs=[pl.BlockSpec((nb, t, B_WIDTH), lambda i, j: (i, j, 0)), st(c0), st(n0), st(m0)],
        compiler_params=_params("arbitrary", "arbitrary"),
        name="mlstm",
    )(qk, qk, v, gcol, grow, c0, n0, m0)


def _sattn_kernel(pt_ref, q_ref, kn_ref, vn_ref, lfn_ref, lf_ref, *rest, n_pages):
    nb, nrow, width = q_ref.shape
    pages = lambda refs, s: refs[s * n_pages:(s + 1) * n_pages]
    k_refs = [pages(rest[:nb * n_pages], s) for s in range(nb)]
    v_refs = [pages(rest[nb * n_pages:2 * nb * n_pages], s) for s in range(nb)]
    o_ref = rest[2 * nb * n_pages]
    first = pl.program_id(0) * nb
    nq = nrow // A_HEADS
    n_new = kn_ref.shape[1]
    lane = lax.broadcasted_iota(jnp.int32, (A_HEADS, LANES), 1)
    per_row = lambda f: jnp.concatenate([f] * nq, axis=0)
    scan = lambda x: _lane_scan(x, lane, (1, 2, 4, 8, 16, 32, 64))

    local = [[scan(lf_ref[pt_ref[first + s, j]]) for j in range(n_pages)] for s in range(nb)]
    new_local = [scan(lfn_ref[s]) for s in range(nb)]
    f_keys, f_new = [], []
    for s in range(nb):
        offset = jnp.zeros((A_HEADS, 1), F32)
        f_keys.append([])
        for j in range(n_pages):
            f = local[s][j] + offset
            offset = f[:, LANES - 1:LANES]
            f_keys[s].append(f)
        f_new.append((new_local[s] + offset)[:, :n_new])

    scores = [[jnp.dot(q_ref[s], k_refs[s][j][...].reshape(width, PAGE_SIZE).astype(BF16),
                       preferred_element_type=F32) - per_row(f_keys[s][j] * LOG2E)
               for j in range(n_pages)] for s in range(nb)]
    q_tok = lax.broadcasted_iota(jnp.int32, (nrow, n_new), 0) // A_HEADS
    k_tok = lax.broadcasted_iota(jnp.int32, (nrow, n_new), 1)
    s_new = [jnp.where(k_tok <= q_tok,
                       lax.dot_general(q_ref[s], kn_ref[s], _NT, preferred_element_type=F32)
                       - per_row(f_new[s] * LOG2E), NEG_INF) for s in range(nb)]

    m = [jnp.maximum(jnp.max(functools.reduce(jnp.maximum, scores[s]), axis=1, keepdims=True),
                     jnp.max(s_new[s], axis=1, keepdims=True)) for s in range(nb)]
    probs = [[jnp.exp2(x - m[s]) for x in scores[s]] for s in range(nb)]
    p_new = [jnp.exp2(s_new[s] - m[s]) for s in range(nb)]
    l = [jnp.sum(functools.reduce(jnp.add, probs[s]), axis=1, keepdims=True)
         + jnp.sum(p_new[s], axis=1, keepdims=True) for s in range(nb)]
    acc = [jnp.dot(p_new[s].astype(BF16), vn_ref[s], preferred_element_type=F32) for s in range(nb)]
    for s in range(nb):
        for j in range(n_pages):
            vt = v_refs[s][j][...].reshape(width, PAGE_SIZE).astype(BF16)
            acc[s] = acc[s] + lax.dot_general(probs[s][j].astype(BF16), vt, _NT, preferred_element_type=F32)
    row_head = lax.broadcasted_iota(jnp.int32, (nrow, width), 0) % A_HEADS
    col_head = lax.broadcasted_iota(jnp.int32, (nrow, width), 1) // A_HEAD_DIM
    for s in range(nb):
        out = jnp.where(row_head == col_head, acc[s] / l[s], 0.0)
        o_ref[s] = jnp.sum(out.reshape(nq, A_HEADS, width), axis=1)


def _sample_attention(page_table, q, kn, vn, lfn, cache_kt, cache_vt, cache_lf, nb=2):
    b, n_pages = page_table.shape
    nq = q.shape[1] // A_HEADS
    page = lambda s, j: pl.BlockSpec((None, A_HEADS, A_HEAD_DIM, PAGE_SIZE),
                                     lambda i, pt: (pt[i * nb + s, j], 0, 0, 0))
    per_seq = lambda a: pl.BlockSpec((nb,) + a.shape[1:], lambda i, pt: (i, 0, 0))
    all_pages = [page(s, j) for s in range(nb) for j in range(n_pages)]
    grid_spec = pltpu.PrefetchScalarGridSpec(
        num_scalar_prefetch=1,
        grid=(b // nb,),
        in_specs=([per_seq(q), per_seq(kn), per_seq(vn), per_seq(lfn), _resident(cache_lf.shape)]
                  + all_pages + all_pages),
        out_specs=pl.BlockSpec((nb, nq, A_WIDTH), lambda i, pt: (i, 0, 0)),
    )
    return pl.pallas_call(
        functools.partial(_sattn_kernel, n_pages=n_pages),
        out_shape=jax.ShapeDtypeStruct((b, nq, A_WIDTH), F32),
        grid_spec=grid_spec,
        compiler_params=_params("arbitrary"),
        name="paged_fox_attention",
    )(page_table, q, kn, vn, lfn, cache_lf, *([cache_kt] * (nb * n_pages)), *([cache_vt] * (nb * n_pages)))


def _split_weights(w_in, b_fox_f, b_ml_i, b_ml_f):
    d = w_in.shape[0]
    o = 0
    parts = {}
    for name, n in (("aq", A_WIDTH), ("ak", A_WIDTH), ("av", A_WIDTH), ("af", A_HEADS), ("bq", B_WIDTH),
                    ("bk", B_WIDTH), ("bv", B_WIDTH), ("bi", B_HEADS), ("bf", B_HEADS), ("bo", B_WIDTH),
                    ("ga", d), ("gb", d)):
        parts[name] = w_in[:, o:o + n]
        o += n
    assert o == w_in.shape[1]
    small = jnp.concatenate([parts["af"], parts["bi"], parts["bf"]], axis=1)
    bias = jnp.concatenate([b_fox_f, b_ml_i, b_ml_f])
    bf = lambda a: a.astype(BF16)
    return {
        "q": bf(parts["aq"] * (A_HEAD_DIM ** -0.5 * LOG2E)),
        "k": bf(parts["ak"]),
        "v": bf(parts["av"]),
        "qk": bf(jnp.concatenate([parts["bq"], parts["bk"]], axis=1)),
        "bv": bf(parts["bv"]),
        "g": bf(jnp.concatenate([parts["bo"], parts["ga"], parts["gb"]], axis=1)),
        "s": bf(jnp.pad(small, ((0, 0), (0, LANES - N_GATES)))),
        "bs": jnp.pad(bias, (0, LANES - N_GATES)).reshape(1, LANES),
        "st": bf(jnp.pad(small.T, ((0, GATE_ROWS - N_GATES), (0, 0)))),
        "bst": jnp.pad(bias, (0, GATE_ROWS - N_GATES)).reshape(GATE_ROWS, 1),
    }


def kernel(x_prompt, x_sample, cache_k, cache_v, cache_logf, page_table, state_C, state_n, state_m, state_conv,
           c_prompt, c_sample, w_ada, b_ada, g_pre_mix, g_post_mix, g_pre_mlp, g_post_mlp, w_in, b_fox_f,
           b_ml_i, b_ml_f, conv_w, conv_b, w_proj_a, w_proj_b, w_out, w_up, w_down):
    assert w_in.shape[0] == 1, "one trunk layer"
    bsz, seq, d = x_prompt.shape
    dbsz, dseq, _ = x_sample.shape

    w_proj = _split_weights(w_in[0], b_fox_f[0], b_ml_i[0], b_ml_f[0])
    wpa, wpb, wo = w_proj_a[0].astype(BF16), w_proj_b[0].astype(BF16), w_out[0].astype(BF16)
    wu, wd = w_up[0].astype(BF16), w_down[0].astype(BF16)
    g1, g2, g3, g4 = g_pre_mix, g_post_mix, g_pre_mlp, g_post_mlp
    cw, cb = conv_w[0], conv_b

    n_c = bsz + dbsz
    c_all = jnp.concatenate([c_prompt, c_sample, jnp.zeros((-n_c % SUBLANES, d), F32)], axis=0)
    mod = _adaln(c_all, w_ada[0], b_ada)

    def token_path(x2, mods, tm, tiles_per_group, token_minor_q):
        return _in_proj(x2, mods, g1, w_proj, tm, tiles_per_group, token_minor_q), mods

    def head_split(kt):
        g_, _, n_ = kt.shape
        return jnp.transpose(kt.reshape(g_, A_HEADS, A_HEAD_DIM, n_), (0, 3, 1, 2))[None]

    def finish(x2, ya, hb, og, mods, tm, group_tokens):
        x1, h2 = _mix_out(ya, hb, og, x2, mods, g2, g3, wpa, wpb, wo, tm, group_tokens // tm)
        return _mlp(h2, x1, mods, g4, wu, wd, tm, group_tokens // tm)

    tm = min(512, seq)
    m_p = bsz * seq
    x2 = x_prompt.reshape(m_p, d)
    mods = mod[:bsz].reshape(bsz, 1, N_MOD * d)
    (qt, kt, vt, ka, qkb, vb, og, sm, smt), rest = token_path(x2, mods, tm, seq // tm, True)

    r3 = lambda a: a.reshape(bsz, seq, a.shape[-1])
    ya = _attention(qt, r3(ka), vt, _cumsum_lanes(smt)).reshape(m_p, A_WIDTH)

    qkc = _conv(r3(qkb), jnp.zeros((bsz, SUBLANES, 2 * B_WIDTH), F32), cw, cb, tc=512)
    hb, c_p, n_p, m_pr = _mlstm(qkc, r3(vb), r3(sm), smt,
                                jnp.zeros((bsz, B_HEADS, B_HEAD_DIM, B_HEAD_DIM), F32),
                                jnp.zeros((bsz, B_HEADS, 1, B_HEAD_DIM), F32),
                                jnp.zeros((bsz, B_HEADS, 1, 1), F32), t=128, nb=2, g0=A_HEADS)
    y_prompt = finish(x2, ya, hb.reshape(m_p, B_WIDTH), og, rest, tm, seq).reshape(bsz, seq, d)

    new_k_prompt = head_split(kt)
    new_v_prompt = head_split(vt)
    new_logf_prompt = jnp.transpose(smt[:, :A_HEADS, :], (0, 2, 1))[None]
    new_conv_prompt = r3(qkb)[:, seq - (CONV_W - 1):, :][None]

    m_s = dbsz * dseq
    xs2 = x_sample.reshape(m_s, d)
    mods = jnp.repeat(mod[bsz:n_c], dseq, axis=0).reshape(1, m_s, N_MOD * d)
    (qa, kt, vt, ka, va, qkb, vb, og, sm, smt), rest = token_path(xs2, mods, m_s, 1, False)

    n_new = 16
    new_rows = lambda a: jnp.pad(a.reshape(dbsz, dseq, -1), ((0, 0), (0, n_new - dseq), (0, 0)))
    lf_s = sm[:, :A_HEADS].reshape(dbsz, dseq, A_HEADS)
    lf_new = jnp.pad(jnp.transpose(lf_s, (0, 2, 1)), ((0, 0), (0, 0), (0, LANES - dseq)))
    q_heads = qa.reshape(dbsz, dseq, A_HEADS, 1, A_HEAD_DIM)
    q_bd = (q_heads * jnp.eye(A_HEADS, dtype=BF16)[None, None, :, :, None]).reshape(dbsz, dseq * A_HEADS, A_WIDTH)
    ya = _sample_attention(
        page_table, q_bd, new_rows(ka), new_rows(va), lf_new,
        jnp.transpose(cache_k[0], (0, 2, 3, 1)), jnp.transpose(cache_v[0], (0, 2, 3, 1)),
        jnp.transpose(cache_logf[0], (0, 2, 1)))
    ya = ya.astype(BF16).reshape(m_s, A_WIDTH)

    t_s = 16
    conv_in = jnp.pad(qkb.reshape(dbsz, dseq, 2 * B_WIDTH), ((0, 0), (0, SUBLANES - dseq), (0, 0)))
    conv_init = jnp.pad(state_conv[0], ((0, 0), (SUBLANES - (CONV_W - 1), 0), (0, 0)))
    qkc = _conv(conv_in, conv_init, cw, cb, tc=SUBLANES, nb=min(16, dbsz))
    tok_valid = (jnp.arange(t_s) < dseq)[None, :, None]
    pad_t = lambda a: jnp.pad(a, ((0, 0), (0, t_s - a.shape[1]), (0, 0)))
    qkc = jnp.where(tok_valid, pad_t(qkc), jnp.zeros((), BF16))
    vb_s = pad_t(vb.reshape(dbsz, dseq, B_WIDTH))
    gates = pad_t(sm[:, A_HEADS:A_HEADS + 2 * B_HEADS].reshape(dbsz, dseq, 2 * B_HEADS))
    neutral = jnp.concatenate([jnp.full((B_HEADS,), NEG_INF, F32), jnp.zeros((B_HEADS,), F32)])
    gcol = jnp.where(tok_valid, gates, neutral)
    grow = jnp.transpose(gcol, (0, 2, 1))
    hb, c_s, n_s, m_sm = _mlstm(qkc, vb_s, gcol, grow, state_C[0], state_n[0][:, :, None, :],
                                state_m[0][:, :, None, None], t=t_s, nb=8)
    hb = hb[:, :dseq, :].reshape(m_s, B_WIDTH)
    y_sample = finish(xs2, ya, hb, og, rest, m_s, m_s).reshape(dbsz, dseq, d)

    new_k_sample = head_split(kt).reshape(1, dbsz, dseq, A_HEADS, A_HEAD_DIM)
    new_v_sample = head_split(vt).reshape(1, dbsz, dseq, A_HEADS, A_HEAD_DIM)
    new_logf_sample = lf_s[None]
    qkb3 = qkb.reshape(dbsz, dseq, 2 * B_WIDTH)
    new_conv_sample = jnp.concatenate([state_conv[0], qkb3], axis=1)[:, dseq:, :][None]

    return (y_prompt, y_sample, new_k_prompt, new_v_prompt, new_logf_prompt,
            c_p[None], n_p[:, :, 0, :][None], m_pr[:, :, 0, 0][None], new_conv_prompt,
            new_k_sample, new_v_sample, new_logf_sample,
            c_s[None], n_s[:, :, 0, :][None], m_sm[:, :, 0, 0][None], new_conv_sample)
```

```python
import functools

import jax
import jax.numpy as jnp
from jax import lax
from jax.experimental import pallas as pl
from jax.experimental.pallas import tpu as pltpu

F32 = jnp.float32
BF16 = jnp.bfloat16

A_HEADS = 8
A_HEAD_DIM = 64
A_WIDTH = A_HEADS * A_HEAD_DIM
B_HEADS = 4
B_HEAD_DIM = 128
B_WIDTH = B_HEADS * B_HEAD_DIM
CONV_W = 4
N_MOD = 6
RMS_EPS = 1e-6
NEG_INF = -1e30
PAGE_SIZE = 128
LOG2E = 1.4426950408889634

LANES = 128
SUBLANES = 8
VMEM_LIMIT_BYTES = 56 * 1024 * 1024

_NT = (((1,), (1,)), ((), ()))
_TN = (((0,), (0,)), ((), ()))


def _params(*sem):
    return pltpu.CompilerParams(dimension_semantics=sem, vmem_limit_bytes=VMEM_LIMIT_BYTES)


def _resident(shape):
    nd = len(shape)
    return pl.BlockSpec(shape, lambda *_: (0,) * nd, pipeline_mode=pl.Buffered(1))


def _rms(x, g):
    r = lax.rsqrt(jnp.mean(x * x, axis=-1, keepdims=True) + RMS_EPS)
    return (x * r) * g


def _log_sigmoid(x):
    return jnp.minimum(x, 0.0) - jnp.log1p(jnp.exp(-jnp.abs(x)))


def _mod_kernel(c_ref, w_ref, b_ref, o_ref):
    c = c_ref[...]
    a = (c * jax.nn.sigmoid(c)).astype(BF16)
    o_ref[...] = jnp.dot(a, w_ref[...].astype(BF16), preferred_element_type=F32) + b_ref[...]


def _adaln(c, w, b, tn=1024):
    r, d = c.shape
    n = w.shape[1]
    return pl.pallas_call(
        _mod_kernel,
        out_shape=jax.ShapeDtypeStruct((r, n), F32),
        grid=(n // tn,),
        in_specs=[pl.BlockSpec((r, d), lambda j: (0, 0)),
                  pl.BlockSpec((d, tn), lambda j: (0, j)),
                  pl.BlockSpec((1, tn), lambda j: (0, j))],
        out_specs=pl.BlockSpec((r, tn), lambda j: (0, j)),
        compiler_params=_params("arbitrary"),
        name="adaln",
    )(c, w, b)


SHIFT_MIX, SCALE_MIX, GATE_MIX, SHIFT_MLP, SCALE_MLP, GATE_MLP = range(N_MOD)


def _mod_spec(arr, which, tm, tiles_per_group):
    g, r, width = arr.shape
    d = width // N_MOD
    if r == 1:
        return pl.BlockSpec((1, 1, d), lambda i: (i // tiles_per_group, 0, which))
    assert g == 1
    return pl.BlockSpec((1, tm, d), lambda i: (0, i, which))


TOKEN_PARTS = 2
N_GATES = A_HEADS + 2 * B_HEADS
GATE_ROWS = 16


def _gate_act(z, idx):
    return jnp.where((idx >= A_HEADS) & (idx < A_HEADS + B_HEADS), z, _log_sigmoid(z))


def _in_kernel(x_ref, sc_ref, sh_ref, g_ref, wq_ref, wk_ref, wv_ref, wqk_ref, wbv_ref, wg_ref,
               ws_ref, bs_ref, wst_ref, bst_ref, q_ref, kt_ref, vt_ref, ka_ref, *rest, token_minor_q):
    if not token_minor_q:
        va_ref, *rest = rest
    qkb_ref, vb_ref, og_ref, sm_ref, smt_ref = rest
    tm = x_ref.shape[0]
    parts = [slice(r, r + tm // TOKEN_PARTS) for r in range(0, tm, tm // TOKEN_PARTS)]
    mod = lambda ref, p: ref[0] if ref.shape[1] == 1 else ref[0, p, :]
    hb = [(_rms(x_ref[p, :], g_ref[...]) * (1.0 + mod(sc_ref, p)) + mod(sh_ref, p)).astype(BF16) for p in parts]
    proj = lambda w_ref: [jnp.dot(h, w_ref[...], preferred_element_type=F32) for h in hb]
    for p, q in zip(parts, proj(wq_ref)):
        if token_minor_q:
            q_ref[0, :, p] = q.T.astype(BF16)
        else:
            q_ref[p, :] = q.astype(BF16)
    for p, k in zip(parts, proj(wk_ref)):
        kt_ref[0, :, p] = k.T
        ka_ref[p, :] = k.astype(BF16)
    for p, v in zip(parts, proj(wv_ref)):
        vt_ref[0, :, p] = v.T
        if not token_minor_q:
            va_ref[p, :] = v.astype(BF16)
    for p, z in zip(parts, proj(wqk_ref)):
        qkb_ref[p, :] = z
    for p, z in zip(parts, proj(wbv_ref)):
        vb_ref[p, :] = z.astype(BF16)
    for p, z in zip(parts, proj(wg_ref)):
        og_ref[p, :] = jax.nn.sigmoid(z).astype(BF16)
    for p, z in zip(parts, proj(ws_ref)):
        z = z + bs_ref[...]
        sm_ref[p, :] = _gate_act(z, lax.broadcasted_iota(jnp.int32, z.shape, 1))
    for p, h in zip(parts, hb):
        z = lax.dot_general(wst_ref[...], h, _NT, preferred_element_type=F32) + bst_ref[...]
        smt_ref[0, :, p] = _gate_act(z, lax.broadcasted_iota(jnp.int32, z.shape, 0))


def _in_proj(x, mod, g, w, tm, tiles_per_group, token_minor_q):
    m, d = x.shape
    group_tokens = tm * tiles_per_group
    groups = m // group_tokens
    row = lambda n: pl.BlockSpec((tm, n), lambda i: (i, 0))
    col = lambda n: pl.BlockSpec((1, n, tm), lambda i: (i // tiles_per_group, 0, i % tiles_per_group))
    rows_out = lambda n, dt: (jax.ShapeDtypeStruct((m, n), dt), row(n))
    cols_out = lambda n, dt=F32: (jax.ShapeDtypeStruct((groups, n, group_tokens), dt), col(n))
    outs = [cols_out(A_WIDTH, BF16) if token_minor_q else rows_out(A_WIDTH, BF16),
            cols_out(A_WIDTH), cols_out(A_WIDTH), rows_out(A_WIDTH, BF16)]
    if not token_minor_q:
        outs.append(rows_out(A_WIDTH, BF16))
    outs += [rows_out(2 * B_WIDTH, F32), rows_out(B_WIDTH, BF16), rows_out(w["g"].shape[1], BF16),
             rows_out(LANES, F32), cols_out(GATE_ROWS)]
    weights = [w["q"], w["k"], w["v"], w["qk"], w["bv"], w["g"], w["s"], w["bs"], w["st"], w["bst"]]
    return pl.pallas_call(
        functools.partial(_in_kernel, token_minor_q=token_minor_q),
        out_shape=[o[0] for o in outs],
        grid=(m // tm,),
        in_specs=[row(d), _mod_spec(mod, SCALE_MIX, tm, tiles_per_group),
                  _mod_spec(mod, SHIFT_MIX, tm, tiles_per_group),
                  _resident(g.shape)] + [_resident(a.shape) for a in weights],
        out_specs=[o[1] for o in outs],
        compiler_params=_params("arbitrary"),
        name="in_proj",
    )(x, mod, mod, g, *weights)


def _mix_kernel(ya_ref, hb_ref, og_ref, x_ref, gt_ref, sc_ref, sh_ref, g1_ref, g2_ref,
                wpa_ref, wpb_ref, wo_ref, x1_ref, h2_ref):
    tm, d = x_ref.shape
    parts = [slice(r, r + tm // TOKEN_PARTS) for r in range(0, tm, tm // TOKEN_PARTS)]
    mod = lambda ref, p: ref[0] if ref.shape[1] == 1 else ref[0, p, :]
    ya = [jnp.dot(ya_ref[p, :], wpa_ref[...], preferred_element_type=F32) for p in parts]
    yb = [jnp.dot((og_ref[p, :B_WIDTH].astype(F32) * hb_ref[p, :].astype(F32)).astype(BF16), wpb_ref[...],
                  preferred_element_type=F32) for p in parts]
    yo = []
    for p, a, b in zip(parts, ya, yb):
        y = og_ref[p, B_WIDTH:B_WIDTH + d].astype(F32) * a + og_ref[p, B_WIDTH + d:].astype(F32) * b
        yo.append(jnp.dot(y.astype(BF16), wo_ref[...], preferred_element_type=F32))
    for p, o in zip(parts, yo):
        x1 = x_ref[p, :] + mod(gt_ref, p) * _rms(o, g1_ref[...])
        x1_ref[p, :] = x1
        h2_ref[p, :] = (_rms(x1, g2_ref[...]) * (1.0 + mod(sc_ref, p)) + mod(sh_ref, p)).astype(BF16)


def _mix_out(ya, hb, og, x, mod, g1, g2, wpa, wpb, wo, tm, tiles_per_group):
    m, d = x.shape
    row = lambda n: pl.BlockSpec((tm, n), lambda i: (i, 0))
    ms = lambda which: _mod_spec(mod, which, tm, tiles_per_group)
    return pl.pallas_call(
        _mix_kernel,
        out_shape=[jax.ShapeDtypeStruct((m, d), F32), jax.ShapeDtypeStruct((m, d), BF16)],
        grid=(m // tm,),
        in_specs=[row(ya.shape[1]), row(hb.shape[1]), row(og.shape[1]), row(d),
                  ms(GATE_MIX), ms(SCALE_MLP), ms(SHIFT_MLP),
                  _resident(g1.shape), _resident(g2.shape), _resident(wpa.shape), _resident(wpb.shape),
                  _resident(wo.shape)],
        out_specs=[row(d), row(d)],
        compiler_params=_params("arbitrary"),
        name="mix_out",
    )(ya, hb, og, x, mod, mod, mod, g1, g2, wpa, wpb, wo)


def _mlp_kernel(h2_ref, x1_ref, gt_ref, g_ref, wu_ref, wd_ref, y_ref, *, fc):
    tm = x1_ref.shape[0]
    parts = [slice(r, r + tm // TOKEN_PARTS) for r in range(0, tm, tm // TOKEN_PARTS)]
    mod = lambda ref, p: ref[0] if ref.shape[1] == 1 else ref[0, p, :]
    acc = [jnp.zeros((tm // TOKEN_PARTS, x1_ref.shape[1]), F32) for _ in parts]
    for c in range(wu_ref.shape[1] // fc):
        u = [jnp.dot(h2_ref[p, :], wu_ref[:, c * fc:(c + 1) * fc], preferred_element_type=F32) for p in parts]
        u = [jnp.square(jnp.maximum(x, 0.0)).astype(BF16) for x in u]
        acc = [a + jnp.dot(x, wd_ref[c * fc:(c + 1) * fc, :], preferred_element_type=F32) for a, x in zip(acc, u)]
    for p, a in zip(parts, acc):
        y_ref[p, :] = x1_ref[p, :] + mod(gt_ref, p) * _rms(a, g_ref[...])


def _mlp(h2, x1, mod, g, wu, wd, tm, tiles_per_group, fc=1024):
    m, d = x1.shape
    row = lambda n: pl.BlockSpec((tm, n), lambda i: (i, 0))
    return pl.pallas_call(
        functools.partial(_mlp_kernel, fc=fc),
        out_shape=jax.ShapeDtypeStruct((m, d), F32),
        grid=(m // tm,),
        in_specs=[row(d), row(d), _mod_spec(mod, GATE_MLP, tm, tiles_per_group), _resident(g.shape),
                  _resident(wu.shape), _resident(wd.shape)],
        out_specs=row(d),
        compiler_params=_params("arbitrary"),
        name="mlp",
    )(h2, x1, mod, g, wu, wd)


def _lane_scan(x, lane, shifts):
    for s in shifts:
        x = x + jnp.where(lane >= s, pltpu.roll(x, s, axis=1), 0.0)
    return x


def _cumsum_kernel(x_ref, o_ref):
    rows, length = x_ref.shape[1], x_ref.shape[2]
    lane = lax.broadcasted_iota(jnp.int32, (rows, LANES), 1)
    carry = jnp.zeros((rows, 1), F32)
    for c in range(length // LANES):
        x = _lane_scan(x_ref[0, :, c * LANES:(c + 1) * LANES], lane, (1, 2, 4, 8, 16, 32, 64)) + carry
        o_ref[0, :, c * LANES:(c + 1) * LANES] = x
        carry = x[:, LANES - 1:LANES]


def _cumsum_lanes(x):
    b, r, length = x.shape
    spec = pl.BlockSpec((1, r, length), lambda i: (i, 0, 0))
    return pl.pallas_call(
        _cumsum_kernel,
        out_shape=jax.ShapeDtypeStruct(x.shape, F32),
        grid=(b,),
        in_specs=[spec],
        out_specs=spec,
        compiler_params=_params("arbitrary"),
        name="logf_cumsum",
    )(x)


V_ROWS = 80


def _aug_lane0(hh):
    return A_HEAD_DIM if hh == 0 else 0


def _attn_kernel(qt_ref, k_ref, vt_ref, f_ref, o_ref, kaug_ref, vaug_ref, s_ref, *, blk_q, blk_k, chunk):
    pair = pl.program_id(1)
    qi = pl.program_id(2)
    length = k_ref.shape[1]
    lane = lax.broadcasted_iota(jnp.int32, (1, LANES), 1)
    own = (lane < A_HEAD_DIM, lane >= A_HEAD_DIM)
    feat = lax.broadcasted_iota(jnp.int32, (LANES, 1), 0)

    @pl.when(qi == 0)
    def _():
        ones_row = jnp.where(lax.broadcasted_iota(jnp.int32, (V_ROWS - A_HEAD_DIM, chunk), 0) == 0, 1.0, 0.0)

        def build(c, _):
            toks = pl.ds(pl.multiple_of(c * chunk, chunk), chunk)
            k = k_ref[0, toks, :]
            vt = vt_ref[0, :, toks]
            for hh in range(2):
                a0 = _aug_lane0(hh)
                f_row = f_ref[0, pl.ds(2 * pair + hh, 1), toks] * (-LOG2E)
                neg_f = jnp.concatenate(
                    [jnp.broadcast_to(f_row[:, t0:t0 + LANES], (LANES, LANES)).T for t0 in range(0, chunk, LANES)],
                    axis=0)
                hi = neg_f.astype(BF16).astype(F32)
                mid = (neg_f - hi).astype(BF16).astype(F32)
                lo = (neg_f - hi - mid).astype(BF16).astype(F32)
                bias = jnp.where(lane == a0, hi, jnp.where(lane == a0 + 1, mid, jnp.where(lane == a0 + 2, lo, 0.0)))
                kaug_ref[hh, toks, :] = jnp.where(own[hh], k, bias.astype(BF16))
                dims = vt[hh * A_HEAD_DIM:(hh + 1) * A_HEAD_DIM]
                vaug_ref[hh, :, toks] = jnp.concatenate([dims, ones_row], axis=0).astype(BF16)
            return 0
        lax.fori_loop(0, length // chunk, build, 0)

    qt = qt_ref[0]
    qh = []
    for hh in range(2):
        a0 = _aug_lane0(hh)
        ones = jnp.where((feat >= a0) & (feat < a0 + 3), 1.0, 0.0).astype(BF16)
        qh.append(jnp.where((feat < A_HEAD_DIM) if hh == 0 else (feat >= A_HEAD_DIM), qt, ones))

    def keys(j):
        return pl.ds(pl.multiple_of(j * blk_k, blk_k), blk_k)

    def scores(j, slot, c0=0):
        for hh in range(2):
            s_ref[slot, hh, :, c0:] = jnp.dot(kaug_ref[hh, keys(j), :], qh[hh][:, c0:], preferred_element_type=F32)

    def fold(j, slot, carry, causal, c0=0):
        probs, stats = [], []
        for hh in range(2):
            m = carry[hh][0]
            sh = s_ref[slot, hh, :, c0:]
            if causal:
                key = lax.broadcasted_iota(jnp.int32, sh.shape, 0)
                qry = lax.broadcasted_iota(jnp.int32, sh.shape, 1)
                sh = jnp.where(key <= qry, sh, NEG_INF)
            m_new = jnp.maximum(m, jnp.max(sh, axis=0, keepdims=True))
            probs.append(jnp.exp2(sh - m_new).astype(BF16))
            stats.append((m_new, jnp.exp2(m - m_new)))
        return tuple(
            (stats[hh][0], stats[hh][1] * carry[hh][1]
             + jnp.dot(vaug_ref[hh, :, keys(j)], probs[hh], preferred_element_type=F32))
            for hh in range(2))

    assert blk_q == 2 * blk_k
    init = tuple((jnp.full((1, blk_q), NEG_INF, F32), jnp.zeros((V_ROWS, blk_q), F32)) for _ in range(2))

    def body(p, carry):
        scores(2 * p + 1, 1)
        carry = fold(2 * p, 0, carry, False)
        scores(2 * p + 2, 0)
        return fold(2 * p + 1, 1, carry, False)

    scores(0, 0)
    carry = lax.fori_loop(0, qi, body, init)
    scores(2 * qi + 1, 1, blk_k)
    carry = fold(2 * qi, 0, carry, True)
    upper = fold(2 * qi + 1, 1, tuple((m[:, blk_k:], acc[:, blk_k:]) for m, acc in carry), True, blk_k)
    out = []
    for hh in range(2):
        acc = jnp.concatenate([carry[hh][1][:, :blk_k], upper[hh][1]], axis=1)
        out.append(acc[:A_HEAD_DIM] / acc[A_HEAD_DIM:A_HEAD_DIM + 1])
    o_ref[0] = jnp.concatenate(out, axis=0).T.astype(BF16)


def _attention(qt, k, vt, f, blk_q=1024, blk_k=512):
    b, length, width = k.shape
    pairs = width // LANES
    blk_q, blk_k = min(blk_q, length), min(blk_k, length)
    return pl.pallas_call(
        functools.partial(_attn_kernel, blk_q=blk_q, blk_k=blk_k, chunk=min(512, length)),
        out_shape=jax.ShapeDtypeStruct(k.shape, BF16),
        grid=(b, pairs, length // blk_q),
        in_specs=[pl.BlockSpec((1, LANES, blk_q), lambda i, p, t: (i, p, t)),
                  pl.BlockSpec((1, length, LANES), lambda i, p, t: (i, 0, p)),
                  pl.BlockSpec((1, LANES, length), lambda i, p, t: (i, p, 0)),
                  pl.BlockSpec((1,) + f.shape[1:], lambda i, p, t: (i, 0, 0))],
        out_specs=pl.BlockSpec((1, blk_q, LANES), lambda i, p, t: (i, t, p)),
        scratch_shapes=[pltpu.VMEM((2, length, LANES), BF16), pltpu.VMEM((2, V_ROWS, length), BF16),
                        pltpu.VMEM((2, 2, blk_k, blk_q), F32)],
        compiler_params=_params("arbitrary", "arbitrary", "arbitrary"),
        name="fox_attention",
    )(qt, k, vt, f)


def _conv_silu(xp_ref, w_ref, b_ref, tc):
    y = b_ref[...]
    for j in range(CONV_W):
        off = SUBLANES - (CONV_W - 1) + j
        y = y + w_ref[j:j + 1, :] * xp_ref[off:off + tc, :]
    lane = lax.broadcasted_iota(jnp.int32, (1, y.shape[1]), 1)
    return y * jax.nn.sigmoid(y) * jnp.where(lane < B_WIDTH, 1.0, B_HEAD_DIM ** -0.5)


def _conv_kernel(x_ref, prev_ref, init_ref, w_ref, b_ref, o_ref, xp_ref, *, tc):
    for s in range(x_ref.shape[0]):
        xp_ref[s, 0:SUBLANES, :] = jnp.where(pl.program_id(1) == 0, init_ref[s], prev_ref[s])
        xp_ref[s, SUBLANES:SUBLANES + tc, :] = x_ref[s]
        o_ref[s] = _conv_silu(xp_ref.at[s], w_ref, b_ref, tc).astype(BF16)


def _conv(x, init, w, b, tc, nb=1):
    bsz, length, c = x.shape
    tpb = tc // SUBLANES
    return pl.pallas_call(
        functools.partial(_conv_kernel, tc=tc),
        out_shape=jax.ShapeDtypeStruct(x.shape, BF16),
        grid=(bsz // nb, length // tc),
        in_specs=[pl.BlockSpec((nb, tc, c), lambda i, t: (i, t, 0)),
                  pl.BlockSpec((nb, SUBLANES, c), lambda i, t: (i, jnp.maximum(t * tpb - 1, 0), 0)),
                  pl.BlockSpec((nb, SUBLANES, c), lambda i, t: (i, 0, 0)),
                  pl.BlockSpec((CONV_W, c), lambda i, t: (0, 0)),
                  pl.BlockSpec((1, c), lambda i, t: (0, 0))],
        out_specs=pl.BlockSpec((nb, tc, c), lambda i, t: (i, t, 0)),
        scratch_shapes=[pltpu.VMEM((nb, tc + SUBLANES, c), F32)],
        compiler_params=_params("arbitrary", "arbitrary"),
        name="short_conv",
    )(x, x, init, w, b)


_BNT = (((2,), (2,)), ((0,), (0,)))
_BNN = (((2,), (1,)), ((0,), (0,)))
_BTN = (((1,), (1,)), ((0,), (0,)))


def _mlstm_heads(q, k, v, ir, fr, ic, fc, c_st, n_st, m_st):
    n, t, d = q.shape
    row = lax.broadcasted_iota(jnp.int32, (n, t, t), 1)
    col = lax.broadcasted_iota(jnp.int32, (n, t, t), 2)
    tril = col <= row
    spread = lambda x, w: jnp.broadcast_to(x, (n, t, w))
    both = lambda x: (spread(x, t),) * 2 if t == d else (spread(x, t), spread(x, d))
    bcum_c = jnp.sum(jnp.where(tril, fr, 0.0), axis=2, keepdims=True)
    bcum_r = jnp.sum(jnp.where(row <= col, fc, 0.0), axis=1, keepdims=True)
    bcum_t, bcum_d = both(bcum_c)
    dlog = jnp.where(tril, bcum_t - bcum_r + ir, NEG_INF)
    mt_t, mt_d = both(jnp.maximum(bcum_c + m_st, jnp.max(dlog, axis=2, keepdims=True)))
    a = jnp.exp(dlog - mt_t) * lax.dot_general(q, k, _BNT, preferred_element_type=F32)
    si = jnp.exp(bcum_d + m_st - mt_d)
    num = (lax.dot_general(a.astype(BF16), v, _BNN, preferred_element_type=F32)
           + si * lax.dot_general(q, c_st.astype(BF16), _BNN, preferred_element_type=F32))
    den = jnp.sum(a, axis=2, keepdims=True) + si * jnp.sum(q.astype(F32) * n_st, axis=2, keepdims=True)
    h = num / jnp.maximum(jnp.abs(den), jnp.exp(-mt_d))
    bl = bcum_r[:, :, t - 1:t]
    m_new = jnp.maximum(bl + m_st, jnp.max(bl - bcum_r + ir, axis=2, keepdims=True))
    kw = jnp.exp(bl - bcum_d + spread(ic, d) - m_new) * k.astype(F32)
    decay = jnp.exp(bl + m_st - m_new)
    c_new = decay * c_st + lax.dot_general(kw.astype(BF16), v, _BTN, preferred_element_type=F32)
    n_new = decay * n_st + jnp.sum(kw, axis=1, keepdims=True)
    return h, c_new, n_new, m_new


def _mlstm_kernel(q_ref, k_ref, v_ref, gc_ref, gr_ref, c0_ref, n0_ref, m0_ref,
                  h_ref, c_ref, n_ref, m_ref, *, g0):
    @pl.when(pl.program_id(1) == 0)
    def _():
        c_ref[...] = c0_ref[...]
        n_ref[...] = n0_ref[...]
        m_ref[...] = m0_ref[...]

    nb = q_ref.shape[0]
    where = [(s, hd, slice(hd * B_HEAD_DIM, (hd + 1) * B_HEAD_DIM)) for s in range(nb) for hd in range(B_HEADS)]
    heads = lambda ref: jnp.stack([ref[s, :, sl] for s, _, sl in where])
    gc = [gc_ref[s] for s in range(nb)]
    gr = [gr_ref[s] for s in range(nb)]
    gi, gf = g0, g0 + B_HEADS
    h, c_new, n_new, m_new = _mlstm_heads(
        heads(q_ref), heads(k_ref), heads(v_ref),
        jnp.stack([gr[s][gi + hd:gi + hd + 1, :] for s, hd, _ in where]),
        jnp.stack([gr[s][gf + hd:gf + hd + 1, :] for s, hd, _ in where]),
        jnp.stack([gc[s][:, gi + hd:gi + hd + 1] for s, hd, _ in where]),
        jnp.stack([gc[s][:, gf + hd:gf + hd + 1] for s, hd, _ in where]),
        c_ref[...].reshape((nb * B_HEADS,) + c_ref.shape[2:]),
        n_ref[...].reshape((nb * B_HEADS,) + n_ref.shape[2:]),
        m_ref[...].reshape((nb * B_HEADS,) + m_ref.shape[2:]))
    for i, (s, _, sl) in enumerate(where):
        h_ref[s, :, sl] = h[i].astype(BF16)
    c_ref[...] = c_new.reshape(c_ref.shape)
    n_ref[...] = n_new.reshape(n_ref.shape)
    m_ref[...] = m_new.reshape(m_ref.shape)


def _mlstm(qk, v, gcol, grow, c0, n0, m0, t, nb=1, g0=0):
    b, length, _ = v.shape
    st = lambda a: pl.BlockSpec((nb,) + a.shape[1:], lambda i, j: (i, 0, 0, 0))
    return pl.pallas_call(
        functools.partial(_mlstm_kernel, g0=g0),
        out_shape=[jax.ShapeDtypeStruct(v.shape, BF16), jax.ShapeDtypeStruct(c0.shape, F32),
                   jax.ShapeDtypeStruct(n0.shape, F32), jax.ShapeDtypeStruct(m0.shape, F32)],
        grid=(b // nb, length // t),
        in_specs=[pl.BlockSpec((nb, t, B_WIDTH), lambda i, j: (i, j, 0)),
                  pl.BlockSpec((nb, t, B_WIDTH), lambda i, j: (i, j, 1)),
                  pl.BlockSpec((nb, t, B_WIDTH), lambda i, j: (i, j, 0)),
                  pl.BlockSpec((nb, t, gcol.shape[2]), lambda i, j: (i, j, 0)),
                  pl.BlockSpec((nb, grow.shape[1], t), lambda i, j: (i, 0, j)),
                  st(c0), st(n0), st(m0)],
        out_specs=[pl.BlockSpec((nb, t, B_WIDTH), lambda i, j: (i, j, 0)), st(c0), st(n0), st(m0)],
        compiler_params=_params("arbitrary", "arbitrary"),
        name="mlstm",
    )(qk, qk, v, gcol, grow, c0, n0, m0)


def _sattn_kernel(pt_ref, q_ref, kn_ref, vn_ref, lfn_ref, lf_ref, *rest, n_pages):
    nb, nrow, width = q_ref.shape
    pages = lambda refs, s: refs[s * n_pages:(s + 1) * n_pages]
    k_refs = [pages(rest[:nb * n_pages], s) for s in range(nb)]
    v_refs = [pages(rest[nb * n_pages:2 * nb * n_pages], s) for s in range(nb)]
    o_ref = rest[2 * nb * n_pages]
    first = pl.program_id(0) * nb
    nq = nrow // A_HEADS
    n_new = kn_ref.shape[1]
    lane = lax.broadcasted_iota(jnp.int32, (A_HEADS, LANES), 1)
    per_row = lambda f: jnp.concatenate([f] * nq, axis=0)
    scan = lambda x: _lane_scan(x, lane, (1, 2, 4, 8, 16, 32, 64))

    local = [[scan(lf_ref[pt_ref[first + s, j]]) for j in range(n_pages)] for s in range(nb)]
    new_local = [scan(lfn_ref[s]) for s in range(nb)]
    f_keys, f_new = [], []
    for s in range(nb):
        offset = jnp.zeros((A_HEADS, 1), F32)
        f_keys.append([])
        for j in range(n_pages):
            f = local[s][j] + offset
            offset = f[:, LANES - 1:LANES]
            f_keys[s].append(f)
        f_new.append((new_local[s] + offset)[:, :n_new])

    scores = [[jnp.dot(q_ref[s], k_refs[s][j][...].reshape(width, PAGE_SIZE).astype(BF16),
                       preferred_element_type=F32) - per_row(f_keys[s][j] * LOG2E)
               for j in range(n_pages)] for s in range(nb)]
    q_tok = lax.broadcasted_iota(jnp.int32, (nrow, n_new), 0) // A_HEADS
    k_tok = lax.broadcasted_iota(jnp.int32, (nrow, n_new), 1)
    s_new = [jnp.where(k_tok <= q_tok,
                       lax.dot_general(q_ref[s], kn_ref[s], _NT, preferred_element_type=F32)
                       - per_row(f_new[s] * LOG2E), NEG_INF) for s in range(nb)]

    m = [jnp.maximum(jnp.max(functools.reduce(jnp.maximum, scores[s]), axis=1, keepdims=True),
                     jnp.max(s_new[s], axis=1, keepdims=True)) for s in range(nb)]
    probs = [[jnp.exp2(x - m[s]) for x in scores[s]] for s in range(nb)]
    p_new = [jnp.exp2(s_new[s] - m[s]) for s in range(nb)]
    l = [jnp.sum(functools.reduce(jnp.add, probs[s]), axis=1, keepdims=True)
         + jnp.sum(p_new[s], axis=1, keepdims=True) for s in range(nb)]
    acc = [jnp.dot(p_new[s].astype(BF16), vn_ref[s], preferred_element_type=F32) for s in range(nb)]
    for s in range(nb):
        for j in range(n_pages):
            vt = v_refs[s][j][...].reshape(width, PAGE_SIZE).astype(BF16)
            acc[s] = acc[s] + lax.dot_general(probs[s][j].astype(BF16), vt, _NT, preferred_element_type=F32)
    row_head = lax.broadcasted_iota(jnp.int32, (nrow, width), 0) % A_HEADS
    col_head = lax.broadcasted_iota(jnp.int32, (nrow, width), 1) // A_HEAD_DIM
    for s in range(nb):
        out = jnp.where(row_head == col_head, acc[s] / l[s], 0.0)
        o_ref[s] = jnp.sum(out.reshape(nq, A_HEADS, width), axis=1)


def _sample_attention(page_table, q, kn, vn, lfn, cache_kt, cache_vt, cache_lf, nb=2):
    b, n_pages = page_table.shape
    nq = q.shape[1] // A_HEADS
    page = lambda s, j: pl.BlockSpec((None, A_HEADS, A_HEAD_DIM, PAGE_SIZE),
                                     lambda i, pt: (pt[i * nb + s, j], 0, 0, 0))
    per_seq = lambda a: pl.BlockSpec((nb,) + a.shape[1:], lambda i, pt: (i, 0, 0))
    all_pages = [page(s, j) for s in range(nb) for j in range(n_pages)]
    grid_spec = pltpu.PrefetchScalarGridSpec(
        num_scalar_prefetch=1,
        grid=(b // nb,),
        in_specs=([per_seq(q), per_seq(kn), per_seq(vn), per_seq(lfn), _resident(cache_lf.shape)]
                  + all_pages + all_pages),
        out_specs=pl.BlockSpec((nb, nq, A_WIDTH), lambda i, pt: (i, 0, 0)),
    )
    return pl.pallas_call(
        functools.partial(_sattn_kernel, n_pages=n_pages),
        out_shape=jax.ShapeDtypeStruct((b, nq, A_WIDTH), F32),
        grid_spec=grid_spec,
        compiler_params=_params("arbitrary"),
        name="paged_fox_attention",
    )(page_table, q, kn, vn, lfn, cache_lf, *([cache_kt] * (nb * n_pages)), *([cache_vt] * (nb * n_pages)))


def _split_weights(w_in, b_fox_f, b_ml_i, b_ml_f):
    d = w_in.shape[0]
    o = 0
    parts = {}
    for name, n in (("aq", A_WIDTH), ("ak", A_WIDTH), ("av", A_WIDTH), ("af", A_HEADS), ("bq", B_WIDTH),
                    ("bk", B_WIDTH), ("bv", B_WIDTH), ("bi", B_HEADS), ("bf", B_HEADS), ("bo", B_WIDTH),
                    ("ga", d), ("gb", d)):
        parts[name] = w_in[:, o:o + n]
        o += n
    assert o == w_in.shape[1]
    small = jnp.concatenate([parts["af"], parts["bi"], parts["bf"]], axis=1)
    bias = jnp.concatenate([b_fox_f, b_ml_i, b_ml_f])
    bf = lambda a: a.astype(BF16)
    return {
        "q": bf(parts["aq"] * (A_HEAD_DIM ** -0.5 * LOG2E)),
        "k": bf(parts["ak"]),
        "v": bf(parts["av"]),
        "qk": bf(jnp.concatenate([parts["bq"], parts["bk"]], axis=1)),
        "bv": bf(parts["bv"]),
        "g": bf(jnp.concatenate([parts["bo"], parts["ga"], parts["gb"]], axis=1)),
        "s": bf(jnp.pad(small, ((0, 0), (0, LANES - N_GATES)))),
        "bs": jnp.pad(bias, (0, LANES - N_GATES)).reshape(1, LANES),
        "st": bf(jnp.pad(small.T, ((0, GATE_ROWS - N_GATES), (0, 0)))),
        "bst": jnp.pad(bias, (0, GATE_ROWS - N_GATES)).reshape(GATE_ROWS, 1),
    }


def kernel(x_prompt, x_sample, cache_k, cache_v, cache_logf, page_table, state_C, state_n, state_m, state_conv,
           c_prompt, c_sample, w_ada, b_ada, g_pre_mix, g_post_mix, g_pre_mlp, g_post_mlp, w_in, b_fox_f,
           b_ml_i, b_ml_f, conv_w, conv_b, w_proj_a, w_proj_b, w_out, w_up, w_down):
    assert w_in.shape[0] == 1, "one trunk layer"
    bsz, seq, d = x_prompt.shape
    dbsz, dseq, _ = x_sample.shape

    w_proj = _split_weights(w_in[0], b_fox_f[0], b_ml_i[0], b_ml_f[0])
    wpa, wpb, wo = w_proj_a[0].astype(BF16), w_proj_b[0].astype(BF16), w_out[0].astype(BF16)
    wu, wd = w_up[0].astype(BF16), w_down[0].astype(BF16)
    g1, g2, g3, g4 = g_pre_mix, g_post_mix, g_pre_mlp, g_post_mlp
    cw, cb = conv_w[0], conv_b

    n_c = bsz + dbsz
    c_all = jnp.concatenate([c_prompt, c_sample, jnp.zeros((-n_c % SUBLANES, d), F32)], axis=0)
    mod = _adaln(c_all, w_ada[0], b_ada)

    def token_path(x2, mods, tm, tiles_per_group, token_minor_q):
        return _in_proj(x2, mods, g1, w_proj, tm, tiles_per_group, token_minor_q), mods

    def head_split(kt):
        g_, _, n_ = kt.shape
        return jnp.transpose(kt.reshape(g_, A_HEADS, A_HEAD_DIM, n_), (0, 3, 1, 2))[None]

    def finish(x2, ya, hb, og, mods, tm, group_tokens):
        x1, h2 = _mix_out(ya, hb, og, x2, mods, g2, g3, wpa, wpb, wo, tm, group_tokens // tm)
        return _mlp(h2, x1, mods, g4, wu, wd, tm, group_tokens // tm)

    tm = min(512, seq)
    m_p = bsz * seq
    x2 = x_prompt.reshape(m_p, d)
    mods = mod[:bsz].reshape(bsz, 1, N_MOD * d)
    (qt, kt, vt, ka, qkb, vb, og, sm, smt), rest = token_path(x2, mods, tm, seq // tm, True)

    r3 = lambda a: a.reshape(bsz, seq, a.shape[-1])
    ya = _attention(qt, r3(ka), vt, _cumsum_lanes(smt)).reshape(m_p, A_WIDTH)

    qkc = _conv(r3(qkb), jnp.zeros((bsz, SUBLANES, 2 * B_WIDTH), F32), cw, cb, tc=512)
    hb, c_p, n_p, m_pr = _mlstm(qkc, r3(vb), r3(sm), smt,
                                jnp.zeros((bsz, B_HEADS, B_HEAD_DIM, B_HEAD_DIM), F32),
                                jnp.zeros((bsz, B_HEADS, 1, B_HEAD_DIM), F32),
                                jnp.zeros((bsz, B_HEADS, 1, 1), F32), t=128, nb=4, g0=A_HEADS)
    y_prompt = finish(x2, ya, hb.reshape(m_p, B_WIDTH), og, rest, tm, seq).reshape(bsz, seq, d)

    new_k_prompt = head_split(kt)
    new_v_prompt = head_split(vt)
    new_logf_prompt = jnp.transpose(smt[:, :A_HEADS, :], (0, 2, 1))[None]
    new_conv_prompt = r3(qkb)[:, seq - (CONV_W - 1):, :][None]

    m_s = dbsz * dseq
    xs2 = x_sample.reshape(m_s, d)
    mods = jnp.repeat(mod[bsz:n_c], dseq, axis=0).reshape(1, m_s, N_MOD * d)
    (qa, kt, vt, ka, va, qkb, vb, og, sm, smt), rest = token_path(xs2, mods, m_s, 1, False)

    n_new = 16
    new_rows = lambda a: jnp.pad(a.reshape(dbsz, dseq, -1), ((0, 0), (0, n_new - dseq), (0, 0)))
    lf_s = sm[:, :A_HEADS].reshape(dbsz, dseq, A_HEADS)
    lf_new = jnp.pad(jnp.transpose(lf_s, (0, 2, 1)), ((0, 0), (0, 0), (0, LANES - dseq)))
    q_heads = qa.reshape(dbsz, dseq, A_HEADS, 1, A_HEAD_DIM)
    q_bd = (q_heads * jnp.eye(A_HEADS, dtype=BF16)[None, None, :, :, None]).reshape(dbsz, dseq * A_HEADS, A_WIDTH)
    ya = _sample_attention(
        page_table, q_bd, new_rows(ka), new_rows(va), lf_new,
        jnp.transpose(cache_k[0], (0, 2, 3, 1)), jnp.transpose(cache_v[0], (0, 2, 3, 1)),
        jnp.transpose(cache_logf[0], (0, 2, 1)))
    ya = ya.astype(BF16).reshape(m_s, A_WIDTH)

    t_s = 16
    conv_in = jnp.pad(qkb.reshape(dbsz, dseq, 2 * B_WIDTH), ((0, 0), (0, SUBLANES - dseq), (0, 0)))
    conv_init = jnp.pad(state_conv[0], ((0, 0), (SUBLANES - (CONV_W - 1), 0), (0, 0)))
    qkc = _conv(conv_in, conv_init, cw, cb, tc=SUBLANES, nb=min(16, dbsz))
    tok_valid = (jnp.arange(t_s) < dseq)[None, :, None]
    pad_t = lambda a: jnp.pad(a, ((0, 0), (0, t_s - a.shape[1]), (0, 0)))
    qkc = jnp.where(tok_valid, pad_t(qkc), jnp.zeros((), BF16))
    vb_s = pad_t(vb.reshape(dbsz, dseq, B_WIDTH))
    gates = pad_t(sm[:, A_HEADS:A_HEADS + 2 * B_HEADS].reshape(dbsz, dseq, 2 * B_HEADS))
    neutral = jnp.concatenate([jnp.full((B_HEADS,), NEG_INF, F32), jnp.zeros((B_HEADS,), F32)])
    gcol = jnp.where(tok_valid, gates, neutral)
    grow = jnp.transpose(gcol, (0, 2, 1))
    hb, c_s, n_s, m_sm = _mlstm(qkc, vb_s, gcol, grow, state_C[0], state_n[0][:, :, None, :],
                                state_m[0][:, :, None, None], t=t_s, nb=8)
    hb = hb[:, :dseq, :].reshape(m_s, B_WIDTH)
    y_sample = finish(xs2, ya, hb, og, rest, m_s, m_s).reshape(dbsz, dseq, d)

    new_k_sample = head_split(kt).reshape(1, dbsz, dseq, A_HEADS, A_HEAD_DIM)
    new_v_sample = head_split(vt).reshape(1, dbsz, dseq, A_HEADS, A_HEAD_DIM)
    new_logf_sample = lf_s[None]
    qkb3 = qkb.reshape(dbsz, dseq, 2 * B_WIDTH)
    new_conv_sample = jnp.concatenate([state_conv[0], qkb3], axis=1)[:, dseq:, :][None]

    return (y_prompt, y_sample, new_k_prompt, new_v_prompt, new_logf_prompt,
            c_p[None], n_p[:, :, 0, :][None], m_pr[:, :, 0, 0][None], new_conv_prompt,
            new_k_sample, new_v_sample, new_logf_sample,
            c_s[None], n_s[:, :, 0, :][None], m_sm[:, :, 0, 0][None], new_conv_sample)
```

```python
import functools

import jax
import jax.numpy as jnp
from jax import lax
from jax.experimental import pallas as pl
from jax.experimental.pallas import tpu as pltpu

F32 = jnp.float32
BF16 = jnp.bfloat16

A_HEADS = 8
A_HEAD_DIM = 64
A_WIDTH = A_HEADS * A_HEAD_DIM
B_HEADS = 4
B_HEAD_DIM = 128
B_WIDTH = B_HEADS * B_HEAD_DIM
CONV_W = 4
N_MOD = 6
RMS_EPS = 1e-6
NEG_INF = -1e30
PAGE_SIZE = 128
LOG2E = 1.4426950408889634

LANES = 128
SUBLANES = 8
VMEM_LIMIT_BYTES = 56 * 1024 * 1024

_NT = (((1,), (1,)), ((), ()))
_TN = (((0,), (0,)), ((), ()))


def _params(*sem):
    return pltpu.CompilerParams(dimension_semantics=sem, vmem_limit_bytes=VMEM_LIMIT_BYTES)


def _resident(shape):
    nd = len(shape)
    return pl.BlockSpec(shape, lambda *_: (0,) * nd, pipeline_mode=pl.Buffered(1))


def _rms(x, g):
    r = lax.rsqrt(jnp.mean(x * x, axis=-1, keepdims=True) + RMS_EPS)
    return (x * r) * g


def _log_sigmoid(x):
    return jnp.minimum(x, 0.0) - jnp.log1p(jnp.exp(-jnp.abs(x)))


def _mod_kernel(c_ref, w_ref, b_ref, o_ref):
    c = c_ref[...]
    a = (c * jax.nn.sigmoid(c)).astype(BF16)
    o_ref[...] = jnp.dot(a, w_ref[...].astype(BF16), preferred_element_type=F32) + b_ref[...]


def _adaln(c, w, b, tn=1024):
    r, d = c.shape
    n = w.shape[1]
    return pl.pallas_call(
        _mod_kernel,
        out_shape=jax.ShapeDtypeStruct((r, n), F32),
        grid=(n // tn,),
        in_specs=[pl.BlockSpec((r, d), lambda j: (0, 0)),
                  pl.BlockSpec((d, tn), lambda j: (0, j)),
                  pl.BlockSpec((1, tn), lambda j: (0, j))],
        out_specs=pl.BlockSpec((r, tn), lambda j: (0, j)),
        compiler_params=_params("arbitrary"),
        name="adaln",
    )(c, w, b)


SHIFT_MIX, SCALE_MIX, GATE_MIX, SHIFT_MLP, SCALE_MLP, GATE_MLP = range(N_MOD)


def _mod_spec(arr, which, tm, tiles_per_group, m):
    g, r, width = arr.shape
    d = width // N_MOD
    if r == 1:
        return pl.BlockSpec((1, 1, d), lambda i: (i // tiles_per_group, 0, which))
    assert g == 1
    return pl.BlockSpec((1, tm * r // m, d), lambda i: (0, i, which))


def _mod_rows(ref, part, tm):
    if ref.shape[1] == 1:
        return ref[0]
    rep = tm // ref.shape[1]
    return jnp.repeat(ref[0, part.start // rep:part.stop // rep, :], rep, axis=0)


TOKEN_PARTS = 2
N_GATES = A_HEADS + 2 * B_HEADS
GATE_ROWS = 16


def _gate_act(z, idx):
    return jnp.where((idx >= A_HEADS) & (idx < A_HEADS + B_HEADS), z, _log_sigmoid(z))


def _in_kernel(x_ref, sc_ref, sh_ref, g_ref, wq_ref, wk_ref, wv_ref, wqk_ref, wbv_ref, wg_ref,
               ws_ref, bs_ref, wst_ref, bst_ref, q_ref, kt_ref, vt_ref, ka_ref, *rest, token_minor_q):
    if not token_minor_q:
        va_ref, *rest = rest
    qkb_ref, vb_ref, og_ref, sm_ref, smt_ref = rest
    tm = x_ref.shape[0]
    parts = [slice(r, r + tm // TOKEN_PARTS) for r in range(0, tm, tm // TOKEN_PARTS)]
    mod = lambda ref, p: _mod_rows(ref, p, tm)
    hb = [(_rms(x_ref[p, :], g_ref[...]) * (1.0 + mod(sc_ref, p)) + mod(sh_ref, p)).astype(BF16) for p in parts]
    proj = lambda w_ref: [jnp.dot(h, w_ref[...], preferred_element_type=F32) for h in hb]
    for p, q in zip(parts, proj(wq_ref)):
        if token_minor_q:
            q_ref[0, :, p] = q.T.astype(BF16)
        else:
            q_ref[p, :] = q.astype(BF16)
    for p, k in zip(parts, proj(wk_ref)):
        kt_ref[0, :, p] = k.T
        ka_ref[p, :] = k.astype(BF16)
    for p, v in zip(parts, proj(wv_ref)):
        vt_ref[0, :, p] = v.T
        if not token_minor_q:
            va_ref[p, :] = v.astype(BF16)
    for p, z in zip(parts, proj(wqk_ref)):
        qkb_ref[p, :] = z
    for p, z in zip(parts, proj(wbv_ref)):
        vb_ref[p, :] = z.astype(BF16)
    for p, z in zip(parts, proj(wg_ref)):
        og_ref[p, :] = jax.nn.sigmoid(z).astype(BF16)
    for p, z in zip(parts, proj(ws_ref)):
        z = z + bs_ref[...]
        sm_ref[p, :] = _gate_act(z, lax.broadcasted_iota(jnp.int32, z.shape, 1))
    for p, h in zip(parts, hb):
        z = lax.dot_general(wst_ref[...], h, _NT, preferred_element_type=F32) + bst_ref[...]
        smt_ref[0, :, p] = _gate_act(z, lax.broadcasted_iota(jnp.int32, z.shape, 0))


def _in_proj(x, mod, g, w, tm, tiles_per_group, token_minor_q):
    m, d = x.shape
    group_tokens = tm * tiles_per_group
    groups = m // group_tokens
    row = lambda n: pl.BlockSpec((tm, n), lambda i: (i, 0))
    col = lambda n: pl.BlockSpec((1, n, tm), lambda i: (i // tiles_per_group, 0, i % tiles_per_group))
    rows_out = lambda n, dt: (jax.ShapeDtypeStruct((m, n), dt), row(n))
    cols_out = lambda n, dt=F32: (jax.ShapeDtypeStruct((groups, n, group_tokens), dt), col(n))
    outs = [cols_out(A_WIDTH, BF16) if token_minor_q else rows_out(A_WIDTH, BF16),
            cols_out(A_WIDTH), cols_out(A_WIDTH), rows_out(A_WIDTH, BF16)]
    if not token_minor_q:
        outs.append(rows_out(A_WIDTH, BF16))
    outs += [rows_out(2 * B_WIDTH, F32), rows_out(B_WIDTH, BF16), rows_out(w["g"].shape[1], BF16),
             rows_out(LANES, F32), cols_out(GATE_ROWS)]
    weights = [w["q"], w["k"], w["v"], w["qk"], w["bv"], w["g"], w["s"], w["bs"], w["st"], w["bst"]]
    return pl.pallas_call(
        functools.partial(_in_kernel, token_minor_q=token_minor_q),
        out_shape=[o[0] for o in outs],
        grid=(m // tm,),
        in_specs=[row(d), _mod_spec(mod, SCALE_MIX, tm, tiles_per_group, m),
                  _mod_spec(mod, SHIFT_MIX, tm, tiles_per_group, m),
                  _resident(g.shape)] + [_resident(a.shape) for a in weights],
        out_specs=[o[1] for o in outs],
        compiler_params=_params("arbitrary"),
        name="in_proj",
    )(x, mod, mod, g, *weights)


def _mix_kernel(ya_ref, hb_ref, og_ref, x_ref, gt_ref, sc_ref, sh_ref, g1_ref, g2_ref,
                wpa_ref, wpb_ref, wo_ref, x1_ref, h2_ref):
    tm, d = x_ref.shape
    parts = [slice(r, r + tm // TOKEN_PARTS) for r in range(0, tm, tm // TOKEN_PARTS)]
    mod = lambda ref, p: _mod_rows(ref, p, tm)
    ya = [jnp.dot(ya_ref[p, :], wpa_ref[...], preferred_element_type=F32) for p in parts]
    yb = [jnp.dot((og_ref[p, :B_WIDTH].astype(F32) * hb_ref[p, :].astype(F32)).astype(BF16), wpb_ref[...],
                  preferred_element_type=F32) for p in parts]
    yo = []
    for p, a, b in zip(parts, ya, yb):
        y = og_ref[p, B_WIDTH:B_WIDTH + d].astype(F32) * a + og_ref[p, B_WIDTH + d:].astype(F32) * b
        yo.append(jnp.dot(y.astype(BF16), wo_ref[...], preferred_element_type=F32))
    for p, o in zip(parts, yo):
        x1 = x_ref[p, :] + mod(gt_ref, p) * _rms(o, g1_ref[...])
        x1_ref[p, :] = x1
        h2_ref[p, :] = (_rms(x1, g2_ref[...]) * (1.0 + mod(sc_ref, p)) + mod(sh_ref, p)).astype(BF16)


def _mix_out(ya, hb, og, x, mod, g1, g2, wpa, wpb, wo, tm, tiles_per_group):
    m, d = x.shape
    row = lambda n: pl.BlockSpec((tm, n), lambda i: (i, 0))
    ms = lambda which: _mod_spec(mod, which, tm, tiles_per_group, m)
    return pl.pallas_call(
        _mix_kernel,
        out_shape=[jax.ShapeDtypeStruct((m, d), F32), jax.ShapeDtypeStruct((m, d), BF16)],
        grid=(m // tm,),
        in_specs=[row(ya.shape[1]), row(hb.shape[1]), row(og.shape[1]), row(d),
                  ms(GATE_MIX), ms(SCALE_MLP), ms(SHIFT_MLP),
                  _resident(g1.shape), _resident(g2.shape), _resident(wpa.shape), _resident(wpb.shape),
                  _resident(wo.shape)],
        out_specs=[row(d), row(d)],
        compiler_params=_params("arbitrary"),
        name="mix_out",
    )(ya, hb, og, x, mod, mod, mod, g1, g2, wpa, wpb, wo)


def _mlp_kernel(h2_ref, x1_ref, gt_ref, g_ref, wu_ref, wd_ref, y_ref, *, fc):
    tm = x1_ref.shape[0]
    parts = [slice(r, r + tm // TOKEN_PARTS) for r in range(0, tm, tm // TOKEN_PARTS)]
    mod = lambda ref, p: _mod_rows(ref, p, tm)
    acc = [jnp.zeros((tm // TOKEN_PARTS, x1_ref.shape[1]), F32) for _ in parts]
    for c in range(wu_ref.shape[1] // fc):
        u = [jnp.dot(h2_ref[p, :], wu_ref[:, c * fc:(c + 1) * fc], preferred_element_type=F32) for p in parts]
        u = [jnp.square(jnp.maximum(x, 0.0)).astype(BF16) for x in u]
        acc = [a + jnp.dot(x, wd_ref[c * fc:(c + 1) * fc, :], preferred_element_type=F32) for a, x in zip(acc, u)]
    for p, a in zip(parts, acc):
        y_ref[p, :] = x1_ref[p, :] + mod(gt_ref, p) * _rms(a, g_ref[...])


def _mlp(h2, x1, mod, g, wu, wd, tm, tiles_per_group, fc=1024):
    m, d = x1.shape
    row = lambda n: pl.BlockSpec((tm, n), lambda i: (i, 0))
    return pl.pallas_call(
        functools.partial(_mlp_kernel, fc=fc),
        out_shape=jax.ShapeDtypeStruct((m, d), F32),
        grid=(m // tm,),
        in_specs=[row(d), row(d), _mod_spec(mod, GATE_MLP, tm, tiles_per_group, m), _resident(g.shape),
                  _resident(wu.shape), _resident(wd.shape)],
        out_specs=row(d),
        compiler_params=_params("arbitrary"),
        name="mlp",
    )(h2, x1, mod, g, wu, wd)


def _lane_scan(x, lane, shifts):
    for s in shifts:
        x = x + jnp.where(lane >= s, pltpu.roll(x, s, axis=1), 0.0)
    return x


def _cumsum_kernel(x_ref, o_ref):
    rows, length = x_ref.shape[1], x_ref.shape[2]
    lane = lax.broadcasted_iota(jnp.int32, (rows, LANES), 1)
    carry = jnp.zeros((rows, 1), F32)
    for c in range(length // LANES):
        x = _lane_scan(x_ref[0, :, c * LANES:(c + 1) * LANES], lane, (1, 2, 4, 8, 16, 32, 64)) + carry
        o_ref[0, :, c * LANES:(c + 1) * LANES] = x
        carry = x[:, LANES - 1:LANES]


def _cumsum_lanes(x):
    b, r, length = x.shape
    spec = pl.BlockSpec((1, r, length), lambda i: (i, 0, 0))
    return pl.pallas_call(
        _cumsum_kernel,
        out_shape=jax.ShapeDtypeStruct(x.shape, F32),
        grid=(b,),
        in_specs=[spec],
        out_specs=spec,
        compiler_params=_params("arbitrary"),
        name="logf_cumsum",
    )(x)


V_ROWS = 80


def _aug_lane0(hh):
    return A_HEAD_DIM if hh == 0 else 0


def _attn_kernel(qt_ref, k_ref, vt_ref, f_ref, o_ref, kaug_ref, vaug_ref, s_ref, *, blk_q, blk_k, chunk):
    pair = pl.program_id(1)
    qi = pl.program_id(2)
    length = k_ref.shape[1]
    lane = lax.broadcasted_iota(jnp.int32, (1, LANES), 1)
    own = (lane < A_HEAD_DIM, lane >= A_HEAD_DIM)
    feat = lax.broadcasted_iota(jnp.int32, (LANES, 1), 0)

    @pl.when(qi == 0)
    def _():
        ones_row = jnp.where(lax.broadcasted_iota(jnp.int32, (V_ROWS - A_HEAD_DIM, chunk), 0) == 0, 1.0, 0.0)

        def build(c, _):
            toks = pl.ds(pl.multiple_of(c * chunk, chunk), chunk)
            k = k_ref[0, toks, :]
            vt = vt_ref[0, :, toks]
            for hh in range(2):
                a0 = _aug_lane0(hh)
                f_row = f_ref[0, pl.ds(2 * pair + hh, 1), toks] * (-LOG2E)
                neg_f = jnp.concatenate(
                    [jnp.broadcast_to(f_row[:, t0:t0 + LANES], (LANES, LANES)).T for t0 in range(0, chunk, LANES)],
                    axis=0)
                hi = neg_f.astype(BF16).astype(F32)
                mid = (neg_f - hi).astype(BF16).astype(F32)
                lo = (neg_f - hi - mid).astype(BF16).astype(F32)
                bias = jnp.where(lane == a0, hi, jnp.where(lane == a0 + 1, mid, jnp.where(lane == a0 + 2, lo, 0.0)))
                kaug_ref[hh, toks, :] = jnp.where(own[hh], k, bias.astype(BF16))
                dims = vt[hh * A_HEAD_DIM:(hh + 1) * A_HEAD_DIM]
                vaug_ref[hh, :, toks] = jnp.concatenate([dims, ones_row], axis=0).astype(BF16)
            return 0
        lax.fori_loop(0, length // chunk, build, 0)

    qt = qt_ref[0]
    qh = []
    for hh in range(2):
        a0 = _aug_lane0(hh)
        ones = jnp.where((feat >= a0) & (feat < a0 + 3), 1.0, 0.0).astype(BF16)
        qh.append(jnp.where((feat < A_HEAD_DIM) if hh == 0 else (feat >= A_HEAD_DIM), qt, ones))

    def keys(j):
        return pl.ds(pl.multiple_of(j * blk_k, blk_k), blk_k)

    def scores(j, slot, c0=0):
        for hh in range(2):
            s_ref[slot, hh, :, c0:] = jnp.dot(kaug_ref[hh, keys(j), :], qh[hh][:, c0:], preferred_element_type=F32)

    def fold(j, slot, carry, causal, c0=0):
        probs, stats = [], []
        for hh in range(2):
            m = carry[hh][0]
            sh = s_ref[slot, hh, :, c0:]
            if causal:
                key = lax.broadcasted_iota(jnp.int32, sh.shape, 0)
                qry = lax.broadcasted_iota(jnp.int32, sh.shape, 1)
                sh = jnp.where(key <= qry, sh, NEG_INF)
            m_new = jnp.maximum(m, jnp.max(sh, axis=0, keepdims=True))
            probs.append(jnp.exp2(sh - m_new).astype(BF16))
            stats.append((m_new, jnp.exp2(m - m_new)))
        return tuple(
            (stats[hh][0], stats[hh][1] * carry[hh][1]
             + jnp.dot(vaug_ref[hh, :, keys(j)], probs[hh], preferred_element_type=F32))
            for hh in range(2))

    assert blk_q == 2 * blk_k
    init = tuple((jnp.full((1, blk_q), NEG_INF, F32), jnp.zeros((V_ROWS, blk_q), F32)) for _ in range(2))

    def body(p, carry):
        scores(2 * p + 1, 1)
        carry = fold(2 * p, 0, carry, False)
        scores(2 * p + 2, 0)
        return fold(2 * p + 1, 1, carry, False)

    scores(0, 0)
    carry = lax.fori_loop(0, qi, body, init)
    scores(2 * qi + 1, 1, blk_k)
    carry = fold(2 * qi, 0, carry, True)
    upper = fold(2 * qi + 1, 1, tuple((m[:, blk_k:], acc[:, blk_k:]) for m, acc in carry), True, blk_k)
    out = []
    for hh in range(2):
        acc = jnp.concatenate([carry[hh][1][:, :blk_k], upper[hh][1]], axis=1)
        out.append(acc[:A_HEAD_DIM] / acc[A_HEAD_DIM:A_HEAD_DIM + 1])
    o_ref[0] = jnp.concatenate(out, axis=0).T.astype(BF16)


def _attention(qt, k, vt, f, blk_q=1024, blk_k=512):
    b, length, width = k.shape
    pairs = width // LANES
    blk_q, blk_k = min(blk_q, length), min(blk_k, length)
    return pl.pallas_call(
        functools.partial(_attn_kernel, blk_q=blk_q, blk_k=blk_k, chunk=min(512, length)),
        out_shape=jax.ShapeDtypeStruct(k.shape, BF16),
        grid=(b, pairs, length // blk_q),
        in_specs=[pl.BlockSpec((1, LANES, blk_q), lambda i, p, t: (i, p, t)),
                  pl.BlockSpec((1, length, LANES), lambda i, p, t: (i, 0, p)),
                  pl.BlockSpec((1, LANES, length), lambda i, p, t: (i, p, 0)),
                  pl.BlockSpec((1,) + f.shape[1:], lambda i, p, t: (i, 0, 0))],
        out_specs=pl.BlockSpec((1, blk_q, LANES), lambda i, p, t: (i, t, p)),
        scratch_shapes=[pltpu.VMEM((2, length, LANES), BF16), pltpu.VMEM((2, V_ROWS, length), BF16),
                        pltpu.VMEM((2, 2, blk_k, blk_q), F32)],
        compiler_params=_params("arbitrary", "arbitrary", "arbitrary"),
        name="fox_attention",
    )(qt, k, vt, f)


def _conv_silu(xp_ref, w_ref, b_ref, tc):
    y = b_ref[...]
    for j in range(CONV_W):
        off = SUBLANES - (CONV_W - 1) + j
        y = y + w_ref[j:j + 1, :] * xp_ref[off:off + tc, :]
    lane = lax.broadcasted_iota(jnp.int32, (1, y.shape[1]), 1)
    return y * jax.nn.sigmoid(y) * jnp.where(lane < B_WIDTH, 1.0, B_HEAD_DIM ** -0.5)


def _conv_kernel(x_ref, prev_ref, init_ref, w_ref, b_ref, o_ref, xp_ref, *, tc):
    for s in range(x_ref.shape[0]):
        xp_ref[s, 0:SUBLANES, :] = jnp.where(pl.program_id(1) == 0, init_ref[s], prev_ref[s])
        xp_ref[s, SUBLANES:SUBLANES + tc, :] = x_ref[s]
        o_ref[s] = _conv_silu(xp_ref.at[s], w_ref, b_ref, tc).astype(BF16)


def _conv(x, init, w, b, tc, nb=1):
    bsz, length, c = x.shape
    tpb = tc // SUBLANES
    return pl.pallas_call(
        functools.partial(_conv_kernel, tc=tc),
        out_shape=jax.ShapeDtypeStruct(x.shape, BF16),
        grid=(bsz // nb, length // tc),
        in_specs=[pl.BlockSpec((nb, tc, c), lambda i, t: (i, t, 0)),
                  pl.BlockSpec((nb, SUBLANES, c), lambda i, t: (i, jnp.maximum(t * tpb - 1, 0), 0)),
                  pl.BlockSpec((nb, SUBLANES, c), lambda i, t: (i, 0, 0)),
                  pl.BlockSpec((CONV_W, c), lambda i, t: (0, 0)),
                  pl.BlockSpec((1, c), lambda i, t: (0, 0))],
        out_specs=pl.BlockSpec((nb, tc, c), lambda i, t: (i, t, 0)),
        scratch_shapes=[pltpu.VMEM((nb, tc + SUBLANES, c), F32)],
        compiler_params=_params("arbitrary", "arbitrary"),
        name="short_conv",
    )(x, x, init, w, b)


_BNT = (((2,), (2,)), ((0,), (0,)))
_BNN = (((2,), (1,)), ((0,), (0,)))
_BTN = (((1,), (1,)), ((0,), (0,)))


def _mlstm_heads(q, k, v, ir, fr, ic, fc, c_st, n_st, m_st):
    n, t, d = q.shape
    row = lax.broadcasted_iota(jnp.int32, (n, t, t), 1)
    col = lax.broadcasted_iota(jnp.int32, (n, t, t), 2)
    tril = col <= row
    spread = lambda x, w: jnp.broadcast_to(x, (n, t, w))
    both = lambda x: (spread(x, t),) * 2 if t == d else (spread(x, t), spread(x, d))
    bcum_c = jnp.sum(jnp.where(tril, fr, 0.0), axis=2, keepdims=True)
    bcum_r = jnp.sum(jnp.where(row <= col, fc, 0.0), axis=1, keepdims=True)
    bcum_t, bcum_d = both(bcum_c)
    dlog = jnp.where(tril, bcum_t - bcum_r + ir, NEG_INF)
    mt_t, mt_d = both(jnp.maximum(bcum_c + m_st, jnp.max(dlog, axis=2, keepdims=True)))
    a = jnp.exp(dlog - mt_t) * lax.dot_general(q, k, _BNT, preferred_element_type=F32)
    si = jnp.exp(bcum_d + m_st - mt_d)
    num = (lax.dot_general(a.astype(BF16), v, _BNN, preferred_element_type=F32)
           + si * lax.dot_general(q, c_st.astype(BF16), _BNN, preferred_element_type=F32))
    den = jnp.sum(a, axis=2, keepdims=True) + si * jnp.sum(q.astype(F32) * n_st, axis=2, keepdims=True)
    h = num / jnp.maximum(jnp.abs(den), jnp.exp(-mt_d))
    bl = bcum_r[:, :, t - 1:t]
    m_new = jnp.maximum(bl + m_st, jnp.max(bl - bcum_r + ir, axis=2, keepdims=True))
    kw = jnp.exp(bl - bcum_d + spread(ic, d) - m_new) * k.astype(F32)
    decay = jnp.exp(bl + m_st - m_new)
    c_new = decay * c_st + lax.dot_general(kw.astype(BF16), v, _BTN, preferred_element_type=F32)
    n_new = decay * n_st + jnp.sum(kw, axis=1, keepdims=True)
    return h, c_new, n_new, m_new


def _mlstm_kernel(q_ref, k_ref, v_ref, gc_ref, gr_ref, c0_ref, n0_ref, m0_ref,
                  h_ref, c_ref, n_ref, m_ref, *, g0):
    @pl.when(pl.program_id(1) == 0)
    def _():
        c_ref[...] = c0_ref[...]
        n_ref[...] = n0_ref[...]
        m_ref[...] = m0_ref[...]

    nb = q_ref.shape[0]
    where = [(s, hd, slice(hd * B_HEAD_DIM, (hd + 1) * B_HEAD_DIM)) for s in range(nb) for hd in range(B_HEADS)]
    heads = lambda ref: jnp.stack([ref[s, :, sl] for s, _, sl in where])
    gc = [gc_ref[s] for s in range(nb)]
    gr = [gr_ref[s] for s in range(nb)]
    gi, gf = g0, g0 + B_HEADS
    h, c_new, n_new, m_new = _mlstm_heads(
        heads(q_ref), heads(k_ref), heads(v_ref),
        jnp.stack([gr[s][gi + hd:gi + hd + 1, :] for s, hd, _ in where]),
        jnp.stack([gr[s][gf + hd:gf + hd + 1, :] for s, hd, _ in where]),
        jnp.stack([gc[s][:, gi + hd:gi + hd + 1] for s, hd, _ in where]),
        jnp.stack([gc[s][:, gf + hd:gf + hd + 1] for s, hd, _ in where]),
        c_ref[...].reshape((nb * B_HEADS,) + c_ref.shape[2:]),
        n_ref[...].reshape((nb * B_HEADS,) + n_ref.shape[2:]),
        m_ref[...].reshape((nb * B_HEADS,) + m_ref.shape[2:]))
    for i, (s, _, sl) in enumerate(where):
        h_ref[s, :, sl] = h[i].astype(BF16)
    c_ref[...] = c_new.reshape(c_ref.shape)
    n_ref[...] = n_new.reshape(n_ref.shape)
    m_ref[...] = m_new.reshape(m_ref.shape)


def _mlstm(qk, v, gcol, grow, c0, n0, m0, t, nb=1, g0=0):
    b, length, _ = v.shape
    st = lambda a: pl.BlockSpec((nb,) + a.shape[1:], lambda i, j: (i, 0, 0, 0))
    return pl.pallas_call(
        functools.partial(_mlstm_kernel, g0=g0),
        out_shape=[jax.ShapeDtypeStruct(v.shape, BF16), jax.ShapeDtypeStruct(c0.shape, F32),
                   jax.ShapeDtypeStruct(n0.shape, F32), jax.ShapeDtypeStruct(m0.shape, F32)],
        grid=(b // nb, length // t),
        in_specs=[pl.BlockSpec((nb, t, B_WIDTH), lambda i, j: (i, j, 0)),
                  pl.BlockSpec((nb, t, B_WIDTH), lambda i, j: (i, j, 1)),
                  pl.BlockSpec((nb, t, B_WIDTH), lambda i, j: (i, j, 0)),
                  pl.BlockSpec((nb, t, gcol.shape[2]), lambda i, j: (i, j, 0)),
                  pl.BlockSpec((nb, grow.shape[1], t), lambda i, j: (i, 0, j)),
                  st(c0), st(n0), st(m0)],
        out_specs=[pl.BlockSpec((nb, t, B_WIDTH), lambda i, j: (i, j, 0)), st(c0), st(n0), st(m0)],
        compiler_params=_params("arbitrary", "arbitrary"),
        name="mlstm",
    )(qk, qk, v, gcol, grow, c0, n0, m0)


def _sattn_kernel(pt_ref, q_ref, kn_ref, vn_ref, lfn_ref, lf_ref, *rest, n_pages):
    nb, nrow, width = q_ref.shape
    pages = lambda refs, s: refs[s * n_pages:(s + 1) * n_pages]
    k_refs = [pages(rest[:nb * n_pages], s) for s in range(nb)]
    v_refs = [pages(rest[nb * n_pages:2 * nb * n_pages], s) for s in range(nb)]
    o_ref = rest[2 * nb * n_pages]
    first = pl.program_id(0) * nb
    nq = nrow // A_HEADS
    n_new = kn_ref.shape[1]
    lane = lax.broadcasted_iota(jnp.int32, (A_HEADS, LANES), 1)
    per_row = lambda f: jnp.concatenate([f] * nq, axis=0)
    scan = lambda x: _lane_scan(x, lane, (1, 2, 4, 8, 16, 32, 64))

    local = [[scan(lf_ref[pt_ref[first + s, j]]) for j in range(n_pages)] for s in range(nb)]
    new_local = [scan(lfn_ref[s]) for s in range(nb)]
    f_keys, f_new = [], []
    for s in range(nb):
        offset = jnp.zeros((A_HEADS, 1), F32)
        f_keys.append([])
        for j in range(n_pages):
            f = local[s][j] + offset
            offset = f[:, LANES - 1:LANES]
            f_keys[s].append(f)
        f_new.append((new_local[s] + offset)[:, :n_new])

    scores = [[jnp.dot(q_ref[s], k_refs[s][j][...].reshape(width, PAGE_SIZE).astype(BF16),
                       preferred_element_type=F32) - per_row(f_keys[s][j] * LOG2E)
               for j in range(n_pages)] for s in range(nb)]
    q_tok = lax.broadcasted_iota(jnp.int32, (nrow, n_new), 0) // A_HEADS
    k_tok = lax.broadcasted_iota(jnp.int32, (nrow, n_new), 1)
    s_new = [jnp.where(k_tok <= q_tok,
                       lax.dot_general(q_ref[s], kn_ref[s], _NT, preferred_element_type=F32)
                       - per_row(f_new[s] * LOG2E), NEG_INF) for s in range(nb)]

    m = [jnp.maximum(jnp.max(functools.reduce(jnp.maximum, scores[s]), axis=1, keepdims=True),
                     jnp.max(s_new[s], axis=1, keepdims=True)) for s in range(nb)]
    probs = [[jnp.exp2(x - m[s]) for x in scores[s]] for s in range(nb)]
    p_new = [jnp.exp2(s_new[s] - m[s]) for s in range(nb)]
    l = [jnp.sum(functools.reduce(jnp.add, probs[s]), axis=1, keepdims=True)
         + jnp.sum(p_new[s], axis=1, keepdims=True) for s in range(nb)]
    acc = [jnp.dot(p_new[s].astype(BF16), vn_ref[s], preferred_element_type=F32) for s in range(nb)]
    for s in range(nb):
        for j in range(n_pages):
            vt = v_refs[s][j][...].reshape(width, PAGE_SIZE).astype(BF16)
            acc[s] = acc[s] + lax.dot_general(probs[s][j].astype(BF16), vt, _NT, preferred_element_type=F32)
    row_head = lax.broadcasted_iota(jnp.int32, (nrow, width), 0) % A_HEADS
    col_head = lax.broadcasted_iota(jnp.int32, (nrow, width), 1) // A_HEAD_DIM
    for s in range(nb):
        out = jnp.where(row_head == col_head, acc[s] / l[s], 0.0)
        o_ref[s] = jnp.sum(out.reshape(nq, A_HEADS, width), axis=1)


def _sample_attention(page_table, q, kn, vn, lfn, cache_kt, cache_vt, cache_lf, nb=2):
    b, n_pages = page_table.shape
    nq = q.shape[1] // A_HEADS
    page = lambda s, j: pl.BlockSpec((None, A_HEADS, A_HEAD_DIM, PAGE_SIZE),
                                     lambda i, pt: (pt[i * nb + s, j], 0, 0, 0))
    per_seq = lambda a: pl.BlockSpec((nb,) + a.shape[1:], lambda i, pt: (i, 0, 0))
    all_pages = [page(s, j) for s in range(nb) for j in range(n_pages)]
    grid_spec = pltpu.PrefetchScalarGridSpec(
        num_scalar_prefetch=1,
        grid=(b // nb,),
        in_specs=([per_seq(q), per_seq(kn), per_seq(vn), per_seq(lfn), _resident(cache_lf.shape)]
                  + all_pages + all_pages),
        out_specs=pl.BlockSpec((nb, nq, A_WIDTH), lambda i, pt: (i, 0, 0)),
    )
    return pl.pallas_call(
        functools.partial(_sattn_kernel, n_pages=n_pages),
        out_shape=jax.ShapeDtypeStruct((b, nq, A_WIDTH), F32),
        grid_spec=grid_spec,
        compiler_params=_params("arbitrary"),
        name="paged_fox_attention",
    )(page_table, q, kn, vn, lfn, cache_lf, *([cache_kt] * (nb * n_pages)), *([cache_vt] * (nb * n_pages)))


def _split_weights(w_in, b_fox_f, b_ml_i, b_ml_f):
    d = w_in.shape[0]
    o = 0
    parts = {}
    for name, n in (("aq", A_WIDTH), ("ak", A_WIDTH), ("av", A_WIDTH), ("af", A_HEADS), ("bq", B_WIDTH),
                    ("bk", B_WIDTH), ("bv", B_WIDTH), ("bi", B_HEADS), ("bf", B_HEADS), ("bo", B_WIDTH),
                    ("ga", d), ("gb", d)):
        parts[name] = w_in[:, o:o + n]
        o += n
    assert o == w_in.shape[1]
    small = jnp.concatenate([parts["af"], parts["bi"], parts["bf"]], axis=1)
    bias = jnp.concatenate([b_fox_f, b_ml_i, b_ml_f])
    bf = lambda a: a.astype(BF16)
    return {
        "q": bf(parts["aq"] * (A_HEAD_DIM ** -0.5 * LOG2E)),
        "k": bf(parts["ak"]),
        "v": bf(parts["av"]),
        "qk": bf(jnp.concatenate([parts["bq"], parts["bk"]], axis=1)),
        "bv": bf(parts["bv"]),
        "g": bf(jnp.concatenate([parts["bo"], parts["ga"], parts["gb"]], axis=1)),
        "s": bf(jnp.pad(small, ((0, 0), (0, LANES - N_GATES)))),
        "bs": jnp.pad(bias, (0, LANES - N_GATES)).reshape(1, LANES),
        "st": bf(jnp.pad(small.T, ((0, GATE_ROWS - N_GATES), (0, 0)))),
        "bst": jnp.pad(bias, (0, GATE_ROWS - N_GATES)).reshape(GATE_ROWS, 1),
    }


def kernel(x_prompt, x_sample, cache_k, cache_v, cache_logf, page_table, state_C, state_n, state_m, state_conv,
           c_prompt, c_sample, w_ada, b_ada, g_pre_mix, g_post_mix, g_pre_mlp, g_post_mlp, w_in, b_fox_f,
           b_ml_i, b_ml_f, conv_w, conv_b, w_proj_a, w_proj_b, w_out, w_up, w_down):
    assert w_in.shape[0] == 1, "one trunk layer"
    bsz, seq, d = x_prompt.shape
    dbsz, dseq, _ = x_sample.shape

    w_proj = _split_weights(w_in[0], b_fox_f[0], b_ml_i[0], b_ml_f[0])
    wpa, wpb, wo = w_proj_a[0].astype(BF16), w_proj_b[0].astype(BF16), w_out[0].astype(BF16)
    wu, wd = w_up[0].astype(BF16), w_down[0].astype(BF16)
    g1, g2, g3, g4 = g_pre_mix, g_post_mix, g_pre_mlp, g_post_mlp
    cw, cb = conv_w[0], conv_b

    n_c = bsz + dbsz
    c_all = jnp.concatenate([c_prompt, c_sample, jnp.zeros((-n_c % SUBLANES, d), F32)], axis=0)
    mod = _adaln(c_all, w_ada[0], b_ada)

    def token_path(x2, mods, tm, tiles_per_group, token_minor_q):
        return _in_proj(x2, mods, g1, w_proj, tm, tiles_per_group, token_minor_q), mods

    def head_split(kt):
        g_, _, n_ = kt.shape
        return jnp.transpose(kt.reshape(g_, A_HEADS, A_HEAD_DIM, n_), (0, 3, 1, 2))[None]

    def finish(x2, ya, hb, og, mods, tm, group_tokens):
        x1, h2 = _mix_out(ya, hb, og, x2, mods, g2, g3, wpa, wpb, wo, tm, group_tokens // tm)
        return _mlp(h2, x1, mods, g4, wu, wd, tm, group_tokens // tm)

    tm = min(512, seq)
    m_p = bsz * seq
    x2 = x_prompt.reshape(m_p, d)
    mods = mod[:bsz].reshape(bsz, 1, N_MOD * d)
    (qt, kt, vt, ka, qkb, vb, og, sm, smt), rest = token_path(x2, mods, tm, seq // tm, True)

    r3 = lambda a: a.reshape(bsz, seq, a.shape[-1])
    ya = _attention(qt, r3(ka), vt, _cumsum_lanes(smt)).reshape(m_p, A_WIDTH)

    qkc = _conv(r3(qkb), jnp.zeros((bsz, SUBLANES, 2 * B_WIDTH), F32), cw, cb, tc=512)
    hb, c_p, n_p, m_pr = _mlstm(qkc, r3(vb), r3(sm), smt,
                                jnp.zeros((bsz, B_HEADS, B_HEAD_DIM, B_HEAD_DIM), F32),
                                jnp.zeros((bsz, B_HEADS, 1, B_HEAD_DIM), F32),
                                jnp.zeros((bsz, B_HEADS, 1, 1), F32), t=128, nb=4, g0=A_HEADS)
    y_prompt = finish(x2, ya, hb.reshape(m_p, B_WIDTH), og, rest, tm, seq).reshape(bsz, seq, d)

    new_k_prompt = head_split(kt)
    new_v_prompt = head_split(vt)
    new_logf_prompt = jnp.transpose(smt[:, :A_HEADS, :], (0, 2, 1))[None]
    new_conv_prompt = r3(qkb)[:, seq - (CONV_W - 1):, :][None]

    m_s = dbsz * dseq
    xs2 = x_sample.reshape(m_s, d)
    mods = mod[bsz:n_c].reshape(1, dbsz, N_MOD * d)
    (qa, kt, vt, ka, va, qkb, vb, og, sm, smt), rest = token_path(xs2, mods, m_s, 1, False)

    n_new = 16
    new_rows = lambda a: jnp.pad(a.reshape(dbsz, dseq, -1), ((0, 0), (0, n_new - dseq), (0, 0)))
    lf_s = sm[:, :A_HEADS].reshape(dbsz, dseq, A_HEADS)
    lf_new = jnp.pad(jnp.transpose(lf_s, (0, 2, 1)), ((0, 0), (0, 0), (0, LANES - dseq)))
    q_heads = qa.reshape(dbsz, dseq, A_HEADS, 1, A_HEAD_DIM)
    q_bd = (q_heads * jnp.eye(A_HEADS, dtype=BF16)[None, None, :, :, None]).reshape(dbsz, dseq * A_HEADS, A_WIDTH)
    ya = _sample_attention(
        page_table, q_bd, new_rows(ka), new_rows(va), lf_new,
        jnp.transpose(cache_k[0], (0, 2, 3, 1)), jnp.transpose(cache_v[0], (0, 2, 3, 1)),
        jnp.transpose(cache_logf[0], (0, 2, 1)))
    ya = ya.astype(BF16).reshape(m_s, A_WIDTH)

    t_s = 16
    conv_in = jnp.pad(qkb.reshape(dbsz, dseq, 2 * B_WIDTH), ((0, 0), (0, SUBLANES - dseq), (0, 0)))
    conv_init = jnp.pad(state_conv[0], ((0, 0), (SUBLANES - (CONV_W - 1), 0), (0, 0)))
    qkc = _conv(conv_in, conv_init, cw, cb, tc=SUBLANES, nb=min(16, dbsz))
    tok_valid = (jnp.arange(t_s) < dseq)[None, :, None]
    pad_t = lambda a: jnp.pad(a, ((0, 0), (0, t_s - a.shape[1]), (0, 0)))
    qkc = jnp.where(tok_valid, pad_t(qkc), jnp.zeros((), BF16))
    vb_s = pad_t(vb.reshape(dbsz, dseq, B_WIDTH))
    gates = pad_t(sm[:, A_HEADS:A_HEADS + 2 * B_HEADS].reshape(dbsz, dseq, 2 * B_HEADS))
    neutral = jnp.concatenate([jnp.full((B_HEADS,), NEG_INF, F32), jnp.zeros((B_HEADS,), F32)])
    gcol = jnp.where(tok_valid, gates, neutral)
    grow = jnp.transpose(gcol, (0, 2, 1))
    hb, c_s, n_s, m_sm = _mlstm(qkc, vb_s, gcol, grow, state_C[0], state_n[0][:, :, None, :],
                                state_m[0][:, :, None, None], t=t_s, nb=8)
    hb = hb[:, :dseq, :].reshape(m_s, B_WIDTH)
    y_sample = finish(xs2, ya, hb, og, rest, m_s, m_s).reshape(dbsz, dseq, d)

    new_k_sample = head_split(kt).reshape(1, dbsz, dseq, A_HEADS, A_HEAD_DIM)
    new_v_sample = head_split(vt).reshape(1, dbsz, dseq, A_HEADS, A_HEAD_DIM)
    new_logf_sample = lf_s[None]
    qkb3 = qkb.reshape(dbsz, dseq, 2 * B_WIDTH)
    new_conv_sample = jnp.concatenate([state_conv[0], qkb3], axis=1)[:, dseq:, :][None]

    return (y_prompt, y_sample, new_k_prompt, new_v_prompt, new_logf_prompt,
            c_p[None], n_p[:, :, 0, :][None], m_pr[:, :, 0, 0][None], new_conv_prompt,
            new_k_sample, new_v_sample, new_logf_sample,
            c_s[None], n_s[:, :, 0, :][None], m_sm[:, :, 0, 0][None], new_conv_sample)
```

```python
import functools

import jax
import jax.numpy as jnp
from jax import lax
from jax.experimental import pallas as pl
from jax.experimental.pallas import tpu as pltpu

F32 = jnp.float32
BF16 = jnp.bfloat16

A_HEADS = 8
A_HEAD_DIM = 64
A_WIDTH = A_HEADS * A_HEAD_DIM
B_HEADS = 4
B_HEAD_DIM = 128
B_WIDTH = B_HEADS * B_HEAD_DIM
CONV_W = 4
N_MOD = 6
RMS_EPS = 1e-6
NEG_INF = -1e30
PAGE_SIZE = 128
LOG2E = 1.4426950408889634

LANES = 128
SUBLANES = 8
VMEM_LIMIT_BYTES = 56 * 1024 * 1024

_NT = (((1,), (1,)), ((), ()))
_TN = (((0,), (0,)), ((), ()))


def _params(*sem):
    return pltpu.CompilerParams(dimension_semantics=sem, vmem_limit_bytes=VMEM_LIMIT_BYTES)


def _resident(shape):
    nd = len(shape)
    return pl.BlockSpec(shape, lambda *_: (0,) * nd, pipeline_mode=pl.Buffered(1))


def _rms(x, g):
    r = lax.rsqrt(jnp.mean(x * x, axis=-1, keepdims=True) + RMS_EPS)
    return (x * r) * g


def _log_sigmoid(x):
    return jnp.minimum(x, 0.0) - jnp.log1p(jnp.exp(-jnp.abs(x)))


def _mod_kernel(c_ref, w_ref, b_ref, o_ref):
    c = c_ref[...]
    a = (c * jax.nn.sigmoid(c)).astype(BF16)
    o_ref[...] = jnp.dot(a, w_ref[...].astype(BF16), preferred_element_type=F32) + b_ref[...]


def _adaln(c, w, b, tn=1024):
    r, d = c.shape
    n = w.shape[1]
    return pl.pallas_call(
        _mod_kernel,
        out_shape=jax.ShapeDtypeStruct((r, n), F32),
        grid=(n // tn,),
        in_specs=[pl.BlockSpec((r, d), lambda j: (0, 0)),
                  pl.BlockSpec((d, tn), lambda j: (0, j)),
                  pl.BlockSpec((1, tn), lambda j: (0, j))],
        out_specs=pl.BlockSpec((r, tn), lambda j: (0, j)),
        compiler_params=_params("arbitrary"),
        name="adaln",
    )(c, w, b)


SHIFT_MIX, SCALE_MIX, GATE_MIX, SHIFT_MLP, SCALE_MLP, GATE_MLP = range(N_MOD)


def _mod_spec(arr, which, tm, tiles_per_group, m):
    g, r, width = arr.shape
    d = width // N_MOD
    if r == 1:
        return pl.BlockSpec((1, 1, d), lambda i: (i // tiles_per_group, 0, which))
    assert g == 1
    return pl.BlockSpec((1, tm * r // m, d), lambda i: (0, i, which))


def _mod_rows(ref, part, tm):
    if ref.shape[1] == 1:
        return ref[0]
    rep = tm // ref.shape[1]
    return jnp.repeat(ref[0, part.start // rep:part.stop // rep, :], rep, axis=0)


TOKEN_PARTS = 2
N_GATES = A_HEADS + 2 * B_HEADS
GATE_ROWS = 16


def _gate_act(z, idx):
    return jnp.where((idx >= A_HEADS) & (idx < A_HEADS + B_HEADS), z, _log_sigmoid(z))


def _in_kernel(x_ref, sc_ref, sh_ref, g_ref, wq_ref, wk_ref, wv_ref, wqk_ref, wbv_ref, wg_ref,
               ws_ref, bs_ref, wst_ref, bst_ref, q_ref, kt_ref, vt_ref, ka_ref, *rest, token_minor_q):
    if not token_minor_q:
        va_ref, *rest = rest
    qkb_ref, vb_ref, og_ref, sm_ref, smt_ref = rest
    tm = x_ref.shape[0]
    parts = [slice(r, r + tm // TOKEN_PARTS) for r in range(0, tm, tm // TOKEN_PARTS)]
    mod = lambda ref, p: _mod_rows(ref, p, tm)
    hb = [(_rms(x_ref[p, :], g_ref[...]) * (1.0 + mod(sc_ref, p)) + mod(sh_ref, p)).astype(BF16) for p in parts]
    proj = lambda w_ref: [jnp.dot(h, w_ref[...], preferred_element_type=F32) for h in hb]
    for p, q in zip(parts, proj(wq_ref)):
        if token_minor_q:
            q_ref[0, :, p] = q.T.astype(BF16)
        else:
            q_ref[p, :] = q.astype(BF16)
    for p, k in zip(parts, proj(wk_ref)):
        kt_ref[0, :, p] = k.T
        ka_ref[p, :] = k.astype(BF16)
    for p, v in zip(parts, proj(wv_ref)):
        vt_ref[0, :, p] = v.T
        if not token_minor_q:
            va_ref[p, :] = v.astype(BF16)
    for p, z in zip(parts, proj(wqk_ref)):
        qkb_ref[p, :] = z
    for p, z in zip(parts, proj(wbv_ref)):
        vb_ref[p, :] = z.astype(BF16)
    for p, z in zip(parts, proj(wg_ref)):
        og_ref[p, :] = jax.nn.sigmoid(z).astype(BF16)
    for p, z in zip(parts, proj(ws_ref)):
        z = z + bs_ref[...]
        sm_ref[p, :] = _gate_act(z, lax.broadcasted_iota(jnp.int32, z.shape, 1))
    for p, h in zip(parts, hb):
        z = lax.dot_general(wst_ref[...], h, _NT, preferred_element_type=F32) + bst_ref[...]
        smt_ref[0, :, p] = _gate_act(z, lax.broadcasted_iota(jnp.int32, z.shape, 0))


def _in_proj(x, mod, g, w, tm, tiles_per_group, token_minor_q):
    m, d = x.shape
    group_tokens = tm * tiles_per_group
    groups = m // group_tokens
    row = lambda n: pl.BlockSpec((tm, n), lambda i: (i, 0))
    col = lambda n: pl.BlockSpec((1, n, tm), lambda i: (i // tiles_per_group, 0, i % tiles_per_group))
    rows_out = lambda n, dt: (jax.ShapeDtypeStruct((m, n), dt), row(n))
    cols_out = lambda n, dt=F32: (jax.ShapeDtypeStruct((groups, n, group_tokens), dt), col(n))
    outs = [cols_out(A_WIDTH, BF16) if token_minor_q else rows_out(A_WIDTH, BF16),
            cols_out(A_WIDTH), cols_out(A_WIDTH), rows_out(A_WIDTH, BF16)]
    if not token_minor_q:
        outs.append(rows_out(A_WIDTH, BF16))
    outs += [rows_out(2 * B_WIDTH, F32), rows_out(B_WIDTH, BF16), rows_out(w["g"].shape[1], BF16),
             rows_out(LANES, F32), cols_out(GATE_ROWS)]
    weights = [w["q"], w["k"], w["v"], w["qk"], w["bv"], w["g"], w["s"], w["bs"], w["st"], w["bst"]]
    return pl.pallas_call(
        functools.partial(_in_kernel, token_minor_q=token_minor_q),
        out_shape=[o[0] for o in outs],
        grid=(m // tm,),
        in_specs=[row(d), _mod_spec(mod, SCALE_MIX, tm, tiles_per_group, m),
                  _mod_spec(mod, SHIFT_MIX, tm, tiles_per_group, m),
                  _resident(g.shape)] + [_resident(a.shape) for a in weights],
        out_specs=[o[1] for o in outs],
        compiler_params=_params("arbitrary"),
        name="in_proj",
    )(x, mod, mod, g, *weights)


MIX_CHUNK = 256


def _mix_kernel(ya_ref, hb_ref, og_ref, x_ref, gt_ref, sc_ref, sh_ref, g1_ref, g2_ref,
                wpa_ref, wpb_ref, wo_ref, x1_ref, h2_ref):
    tm, d = x_ref.shape
    parts = [slice(r, r + tm // TOKEN_PARTS) for r in range(0, tm, tm // TOKEN_PARTS)]
    mod = lambda ref, p: _mod_rows(ref, p, tm)
    merged = []
    for p in parts:
        yb_in = (og_ref[p, :B_WIDTH].astype(F32) * hb_ref[p, :].astype(F32)).astype(BF16)
        cols = []
        for c in range(0, d, MIX_CHUNK):
            a = jnp.dot(ya_ref[p, :], wpa_ref[:, c:c + MIX_CHUNK], preferred_element_type=F32)
            b = jnp.dot(yb_in, wpb_ref[:, c:c + MIX_CHUNK], preferred_element_type=F32)
            ga = og_ref[p, B_WIDTH + c:B_WIDTH + c + MIX_CHUNK].astype(F32)
            gb = og_ref[p, B_WIDTH + d + c:B_WIDTH + d + c + MIX_CHUNK].astype(F32)
            cols.append((ga * a + gb * b).astype(BF16))
        merged.append(jnp.concatenate(cols, axis=1))
    yo = [jnp.dot(y, wo_ref[...], preferred_element_type=F32) for y in merged]
    for p, o in zip(parts, yo):
        x1 = x_ref[p, :] + mod(gt_ref, p) * _rms(o, g1_ref[...])
        x1_ref[p, :] = x1
        h2_ref[p, :] = (_rms(x1, g2_ref[...]) * (1.0 + mod(sc_ref, p)) + mod(sh_ref, p)).astype(BF16)


def _mix_out(ya, hb, og, x, mod, g1, g2, wpa, wpb, wo, tm, tiles_per_group):
    m, d = x.shape
    row = lambda n: pl.BlockSpec((tm, n), lambda i: (i, 0))
    ms = lambda which: _mod_spec(mod, which, tm, tiles_per_group, m)
    return pl.pallas_call(
        _mix_kernel,
        out_shape=[jax.ShapeDtypeStruct((m, d), F32), jax.ShapeDtypeStruct((m, d), BF16)],
        grid=(m // tm,),
        in_specs=[row(ya.shape[1]), row(hb.shape[1]), row(og.shape[1]), row(d),
                  ms(GATE_MIX), ms(SCALE_MLP), ms(SHIFT_MLP),
                  _resident(g1.shape), _resident(g2.shape), _resident(wpa.shape), _resident(wpb.shape),
                  _resident(wo.shape)],
        out_specs=[row(d), row(d)],
        compiler_params=_params("arbitrary"),
        name="mix_out",
    )(ya, hb, og, x, mod, mod, mod, g1, g2, wpa, wpb, wo)


def _mlp_kernel(h2_ref, x1_ref, gt_ref, g_ref, wu_ref, wd_ref, y_ref, *, fc):
    tm = x1_ref.shape[0]
    parts = [slice(r, r + tm // TOKEN_PARTS) for r in range(0, tm, tm // TOKEN_PARTS)]
    mod = lambda ref, p: _mod_rows(ref, p, tm)
    acc = [jnp.zeros((tm // TOKEN_PARTS, x1_ref.shape[1]), F32) for _ in parts]
    for c in range(wu_ref.shape[1] // fc):
        u = [jnp.dot(h2_ref[p, :], wu_ref[:, c * fc:(c + 1) * fc], preferred_element_type=F32) for p in parts]
        u = [jnp.square(jnp.maximum(x, 0.0)).astype(BF16) for x in u]
        acc = [a + jnp.dot(x, wd_ref[c * fc:(c + 1) * fc, :], preferred_element_type=F32) for a, x in zip(acc, u)]
    for p, a in zip(parts, acc):
        y_ref[p, :] = x1_ref[p, :] + mod(gt_ref, p) * _rms(a, g_ref[...])


def _mlp(h2, x1, mod, g, wu, wd, tm, tiles_per_group, fc=1024):
    m, d = x1.shape
    row = lambda n: pl.BlockSpec((tm, n), lambda i: (i, 0))
    return pl.pallas_call(
        functools.partial(_mlp_kernel, fc=fc),
        out_shape=jax.ShapeDtypeStruct((m, d), F32),
        grid=(m // tm,),
        in_specs=[row(d), row(d), _mod_spec(mod, GATE_MLP, tm, tiles_per_group, m), _resident(g.shape),
                  _resident(wu.shape), _resident(wd.shape)],
        out_specs=row(d),
        compiler_params=_params("arbitrary"),
        name="mlp",
    )(h2, x1, mod, g, wu, wd)


def _lane_scan(x, lane, shifts):
    for s in shifts:
        x = x + jnp.where(lane >= s, pltpu.roll(x, s, axis=1), 0.0)
    return x


def _cumsum_kernel(x_ref, o_ref):
    rows, length = x_ref.shape[1], x_ref.shape[2]
    lane = lax.broadcasted_iota(jnp.int32, (rows, LANES), 1)
    carry = jnp.zeros((rows, 1), F32)
    for c in range(length // LANES):
        x = _lane_scan(x_ref[0, :, c * LANES:(c + 1) * LANES], lane, (1, 2, 4, 8, 16, 32, 64)) + carry
        o_ref[0, :, c * LANES:(c + 1) * LANES] = x
        carry = x[:, LANES - 1:LANES]


def _cumsum_lanes(x):
    b, r, length = x.shape
    spec = pl.BlockSpec((1, r, length), lambda i: (i, 0, 0))
    return pl.pallas_call(
        _cumsum_kernel,
        out_shape=jax.ShapeDtypeStruct(x.shape, F32),
        grid=(b,),
        in_specs=[spec],
        out_specs=spec,
        compiler_params=_params("arbitrary"),
        name="logf_cumsum",
    )(x)


V_ROWS = 80


def _aug_lane0(hh):
    return A_HEAD_DIM if hh == 0 else 0


def _attn_kernel(qt_ref, k_ref, vt_ref, f_ref, o_ref, kaug_ref, vaug_ref, s_ref, *, blk_q, blk_k, chunk):
    pair = pl.program_id(1)
    qi = pl.program_id(2)
    length = k_ref.shape[1]
    lane = lax.broadcasted_iota(jnp.int32, (1, LANES), 1)
    own = (lane < A_HEAD_DIM, lane >= A_HEAD_DIM)
    feat = lax.broadcasted_iota(jnp.int32, (LANES, 1), 0)

    @pl.when(qi == 0)
    def _():
        ones_row = jnp.where(lax.broadcasted_iota(jnp.int32, (V_ROWS - A_HEAD_DIM, chunk), 0) == 0, 1.0, 0.0)

        def build(c, _):
            toks = pl.ds(pl.multiple_of(c * chunk, chunk), chunk)
            k = k_ref[0, toks, :]
            vt = vt_ref[0, :, toks]
            for hh in range(2):
                a0 = _aug_lane0(hh)
                f_row = f_ref[0, pl.ds(2 * pair + hh, 1), toks] * (-LOG2E)
                neg_f = jnp.concatenate(
                    [jnp.broadcast_to(f_row[:, t0:t0 + LANES], (LANES, LANES)).T for t0 in range(0, chunk, LANES)],
                    axis=0)
                hi = neg_f.astype(BF16).astype(F32)
                mid = (neg_f - hi).astype(BF16).astype(F32)
                lo = (neg_f - hi - mid).astype(BF16).astype(F32)
                bias = jnp.where(lane == a0, hi, jnp.where(lane == a0 + 1, mid, jnp.where(lane == a0 + 2, lo, 0.0)))
                kaug_ref[hh, toks, :] = jnp.where(own[hh], k, bias.astype(BF16))
                dims = vt[hh * A_HEAD_DIM:(hh + 1) * A_HEAD_DIM]
                vaug_ref[hh, :, toks] = jnp.concatenate([dims, ones_row], axis=0).astype(BF16)
            return 0
        lax.fori_loop(0, length // chunk, build, 0)

    qt = qt_ref[0]
    qh = []
    for hh in range(2):
        a0 = _aug_lane0(hh)
        ones = jnp.where((feat >= a0) & (feat < a0 + 3), 1.0, 0.0).astype(BF16)
        qh.append(jnp.where((feat < A_HEAD_DIM) if hh == 0 else (feat >= A_HEAD_DIM), qt, ones))

    def keys(j):
        return pl.ds(pl.multiple_of(j * blk_k, blk_k), blk_k)

    def scores(j, slot, c0=0):
        for hh in range(2):
            s_ref[slot, hh, :, c0:] = jnp.dot(kaug_ref[hh, keys(j), :], qh[hh][:, c0:], preferred_element_type=F32)

    def fold(j, slot, carry, causal, c0=0):
        probs, stats = [], []
        for hh in range(2):
            m = carry[hh][0]
            sh = s_ref[slot, hh, :, c0:]
            if causal:
                key = lax.broadcasted_iota(jnp.int32, sh.shape, 0)
                qry = lax.broadcasted_iota(jnp.int32, sh.shape, 1)
                sh = jnp.where(key <= qry, sh, NEG_INF)
            m_new = jnp.maximum(m, jnp.max(sh, axis=0, keepdims=True))
            probs.append(jnp.exp2(sh - m_new).astype(BF16))
            stats.append((m_new, jnp.exp2(m - m_new)))
        return tuple(
            (stats[hh][0], stats[hh][1] * carry[hh][1]
             + jnp.dot(vaug_ref[hh, :, keys(j)], probs[hh], preferred_element_type=F32))
            for hh in range(2))

    assert blk_q == 2 * blk_k
    init = tuple((jnp.full((1, blk_q), NEG_INF, F32), jnp.zeros((V_ROWS, blk_q), F32)) for _ in range(2))

    def body(p, carry):
        scores(2 * p + 1, 1)
        carry = fold(2 * p, 0, carry, False)
        scores(2 * p + 2, 0)
        return fold(2 * p + 1, 1, carry, False)

    scores(0, 0)
    carry = lax.fori_loop(0, qi, body, init)
    scores(2 * qi + 1, 1, blk_k)
    carry = fold(2 * qi, 0, carry, True)
    upper = fold(2 * qi + 1, 1, tuple((m[:, blk_k:], acc[:, blk_k:]) for m, acc in carry), True, blk_k)
    out = []
    for hh in range(2):
        acc = jnp.concatenate([carry[hh][1][:, :blk_k], upper[hh][1]], axis=1)
        out.append(acc[:A_HEAD_DIM] / acc[A_HEAD_DIM:A_HEAD_DIM + 1])
    o_ref[0] = jnp.concatenate(out, axis=0).T.astype(BF16)


def _attention(qt, k, vt, f, blk_q=1024, blk_k=512):
    b, length, width = k.shape
    pairs = width // LANES
    blk_q, blk_k = min(blk_q, length), min(blk_k, length)
    return pl.pallas_call(
        functools.partial(_attn_kernel, blk_q=blk_q, blk_k=blk_k, chunk=min(512, length)),
        out_shape=jax.ShapeDtypeStruct(k.shape, BF16),
        grid=(b, pairs, length // blk_q),
        in_specs=[pl.BlockSpec((1, LANES, blk_q), lambda i, p, t: (i, p, t)),
                  pl.BlockSpec((1, length, LANES), lambda i, p, t: (i, 0, p)),
                  pl.BlockSpec((1, LANES, length), lambda i, p, t: (i, p, 0)),
                  pl.BlockSpec((1,) + f.shape[1:], lambda i, p, t: (i, 0, 0))],
        out_specs=pl.BlockSpec((1, blk_q, LANES), lambda i, p, t: (i, t, p)),
        scratch_shapes=[pltpu.VMEM((2, length, LANES), BF16), pltpu.VMEM((2, V_ROWS, length), BF16),
                        pltpu.VMEM((2, 2, blk_k, blk_q), F32)],
        compiler_params=_params("arbitrary", "arbitrary", "arbitrary"),
        name="fox_attention",
    )(qt, k, vt, f)


def _conv_silu(xp_ref, w_ref, b_ref, tc):
    y = b_ref[...]
    for j in range(CONV_W):
        off = SUBLANES - (CONV_W - 1) + j
        y = y + w_ref[j:j + 1, :] * xp_ref[off:off + tc, :]
    lane = lax.broadcasted_iota(jnp.int32, (1, y.shape[1]), 1)
    return y * jax.nn.sigmoid(y) * jnp.where(lane < B_WIDTH, 1.0, B_HEAD_DIM ** -0.5)


def _conv_kernel(x_ref, prev_ref, init_ref, w_ref, b_ref, o_ref, xp_ref, *, tc):
    for s in range(x_ref.shape[0]):
        xp_ref[s, 0:SUBLANES, :] = jnp.where(pl.program_id(1) == 0, init_ref[s], prev_ref[s])
        xp_ref[s, SUBLANES:SUBLANES + tc, :] = x_ref[s]
        o_ref[s] = _conv_silu(xp_ref.at[s], w_ref, b_ref, tc).astype(BF16)


def _conv(x, init, w, b, tc, nb=1):
    bsz, length, c = x.shape
    tpb = tc // SUBLANES
    return pl.pallas_call(
        functools.partial(_conv_kernel, tc=tc),
        out_shape=jax.ShapeDtypeStruct(x.shape, BF16),
        grid=(bsz // nb, length // tc),
        in_specs=[pl.BlockSpec((nb, tc, c), lambda i, t: (i, t, 0)),
                  pl.BlockSpec((nb, SUBLANES, c), lambda i, t: (i, jnp.maximum(t * tpb - 1, 0), 0)),
                  pl.BlockSpec((nb, SUBLANES, c), lambda i, t: (i, 0, 0)),
                  pl.BlockSpec((CONV_W, c), lambda i, t: (0, 0)),
                  pl.BlockSpec((1, c), lambda i, t: (0, 0))],
        out_specs=pl.BlockSpec((nb, tc, c), lambda i, t: (i, t, 0)),
        scratch_shapes=[pltpu.VMEM((nb, tc + SUBLANES, c), F32)],
        compiler_params=_params("arbitrary", "arbitrary"),
        name="short_conv",
    )(x, x, init, w, b)


_BNT = (((2,), (2,)), ((0,), (0,)))
_BNN = (((2,), (1,)), ((0,), (0,)))
_BTN = (((1,), (1,)), ((0,), (0,)))


def _mlstm_heads(q, k, v, ir, fr, ic, fc, c_st, n_st, m_st):
    n, t, d = q.shape
    row = lax.broadcasted_iota(jnp.int32, (n, t, t), 1)
    col = lax.broadcasted_iota(jnp.int32, (n, t, t), 2)
    tril = col <= row
    spread = lambda x, w: jnp.broadcast_to(x, (n, t, w))
    both = lambda x: (spread(x, t),) * 2 if t == d else (spread(x, t), spread(x, d))
    bcum_c = jnp.sum(jnp.where(tril, fr, 0.0), axis=2, keepdims=True)
    bcum_r = jnp.sum(jnp.where(row <= col, fc, 0.0), axis=1, keepdims=True)
    bcum_t, bcum_d = both(bcum_c)
    dlog = jnp.where(tril, bcum_t - bcum_r + ir, NEG_INF)
    mt_t, mt_d = both(jnp.maximum(bcum_c + m_st, jnp.max(dlog, axis=2, keepdims=True)))
    a = jnp.exp(dlog - mt_t) * lax.dot_general(q, k, _BNT, preferred_element_type=F32)
    si = jnp.exp(bcum_d + m_st - mt_d)
    num = (lax.dot_general(a.astype(BF16), v, _BNN, preferred_element_type=F32)
           + si * lax.dot_general(q, c_st.astype(BF16), _BNN, preferred_element_type=F32))
    den = jnp.sum(a, axis=2, keepdims=True) + si * jnp.sum(q.astype(F32) * n_st, axis=2, keepdims=True)
    h = num / jnp.maximum(jnp.abs(den), jnp.exp(-mt_d))
    bl = bcum_r[:, :, t - 1:t]
    m_new = jnp.maximum(bl + m_st, jnp.max(bl - bcum_r + ir, axis=2, keepdims=True))
    kw = jnp.exp(bl - bcum_d + spread(ic, d) - m_new) * k.astype(F32)
    decay = jnp.exp(bl + m_st - m_new)
    c_new = decay * c_st + lax.dot_general(kw.astype(BF16), v, _BTN, preferred_element_type=F32)
    n_new = decay * n_st + jnp.sum(kw, axis=1, keepdims=True)
    return h, c_new, n_new, m_new


def _mlstm_kernel(q_ref, k_ref, v_ref, gc_ref, gr_ref, c0_ref, n0_ref, m0_ref,
                  h_ref, c_ref, n_ref, m_ref, *, g0):
    @pl.when(pl.program_id(1) == 0)
    def _():
        c_ref[...] = c0_ref[...]
        n_ref[...] = n0_ref[...]
        m_ref[...] = m0_ref[...]

    nb = q_ref.shape[0]
    where = [(s, hd, slice(hd * B_HEAD_DIM, (hd + 1) * B_HEAD_DIM)) for s in range(nb) for hd in range(B_HEADS)]
    heads = lambda ref: jnp.stack([ref[s, :, sl] for s, _, sl in where])
    gc = [gc_ref[s] for s in range(nb)]
    gr = [gr_ref[s] for s in range(nb)]
    gi, gf = g0, g0 + B_HEADS
    h, c_new, n_new, m_new = _mlstm_heads(
        heads(q_ref), heads(k_ref), heads(v_ref),
        jnp.stack([gr[s][gi + hd:gi + hd + 1, :] for s, hd, _ in where]),
        jnp.stack([gr[s][gf + hd:gf + hd + 1, :] for s, hd, _ in where]),
        jnp.stack([gc[s][:, gi + hd:gi + hd + 1] for s, hd, _ in where]),
        jnp.stack([gc[s][:, gf + hd:gf + hd + 1] for s, hd, _ in where]),
        c_ref[...].reshape((nb * B_HEADS,) + c_ref.shape[2:]),
        n_ref[...].reshape((nb * B_HEADS,) + n_ref.shape[2:]),
        m_ref[...].reshape((nb * B_HEADS,) + m_ref.shape[2:]))
    for i, (s, _, sl) in enumerate(where):
        h_ref[s, :, sl] = h[i].astype(BF16)
    c_ref[...] = c_new.reshape(c_ref.shape)
    n_ref[...] = n_new.reshape(n_ref.shape)
    m_ref[...] = m_new.reshape(m_ref.shape)


def _mlstm(qk, v, gcol, grow, c0, n0, m0, t, nb=1, g0=0):
    b, length, _ = v.shape
    st = lambda a: pl.BlockSpec((nb,) + a.shape[1:], lambda i, j: (i, 0, 0, 0))
    return pl.pallas_call(
        functools.partial(_mlstm_kernel, g0=g0),
        out_shape=[jax.ShapeDtypeStruct(v.shape, BF16), jax.ShapeDtypeStruct(c0.shape, F32),
                   jax.ShapeDtypeStruct(n0.shape, F32), jax.ShapeDtypeStruct(m0.shape, F32)],
        grid=(b // nb, length // t),
        in_specs=[pl.BlockSpec((nb, t, B_WIDTH), lambda i, j: (i, j, 0)),
                  pl.BlockSpec((nb, t, B_WIDTH), lambda i, j: (i, j, 1)),
                  pl.BlockSpec((nb, t, B_WIDTH), lambda i, j: (i, j, 0)),
                  pl.BlockSpec((nb, t, gcol.shape[2]), lambda i, j: (i, j, 0)),
                  pl.BlockSpec((nb, grow.shape[1], t), lambda i, j: (i, 0, j)),
                  st(c0), st(n0), st(m0)],
        out_specs=[pl.BlockSpec((nb, t, B_WIDTH), lambda i, j: (i, j, 0)), st(c0), st(n0), st(m0)],
        compiler_params=_params("arbitrary", "arbitrary"),
        name="mlstm",
    )(qk, qk, v, gcol, grow, c0, n0, m0)


def _page_copies(pt_ref, step, slot, ck_ref, cv_ref, kbuf_ref, vbuf_ref, sem_ref, nb, n_pages):
    copies = []
    for s in range(nb):
        for j in range(n_pages):
            page = pt_ref[step * nb + s, j]
            at = s * n_pages + j
            copies.append(pltpu.make_async_copy(ck_ref.at[page], kbuf_ref.at[slot, at], sem_ref.at[slot, 0]))
            copies.append(pltpu.make_async_copy(cv_ref.at[page], vbuf_ref.at[slot, at], sem_ref.at[slot, 1]))
    return copies


def _sattn_kernel(pt_ref, q_ref, kn_ref, vn_ref, lfn_ref, lf_ref, ck_ref, cv_ref, o_ref,
                  kbuf_ref, vbuf_ref, sem_ref, *, n_pages):
    nb, nrow, width = q_ref.shape
    step, n_steps = pl.program_id(0), pl.num_programs(0)
    slot = step % 2
    fetch = functools.partial(_page_copies, pt_ref, ck_ref=ck_ref, cv_ref=cv_ref, kbuf_ref=kbuf_ref,
                              vbuf_ref=vbuf_ref, sem_ref=sem_ref, nb=nb, n_pages=n_pages)

    @pl.when(step == 0)
    def _():
        for c in fetch(0, 0):
            c.start()

    @pl.when(step + 1 < n_steps)
    def _():
        for c in fetch(step + 1, 1 - slot):
            c.start()

    for c in fetch(step, slot):
        c.wait()
    k_refs = [[kbuf_ref.at[slot, s * n_pages + j] for j in range(n_pages)] for s in range(nb)]
    v_refs = [[vbuf_ref.at[slot, s * n_pages + j] for j in range(n_pages)] for s in range(nb)]
    first = step * nb
    nq = nrow // A_HEADS
    n_new = kn_ref.shape[1]
    lane = lax.broadcasted_iota(jnp.int32, (A_HEADS, LANES), 1)
    per_row = lambda f: jnp.concatenate([f] * nq, axis=0)
    scan = lambda x: _lane_scan(x, lane, (1, 2, 4, 8, 16, 32, 64))

    local = [[scan(lf_ref[pt_ref[first + s, j]]) for j in range(n_pages)] for s in range(nb)]
    new_local = [scan(lfn_ref[s]) for s in range(nb)]
    f_keys, f_new = [], []
    for s in range(nb):
        offset = jnp.zeros((A_HEADS, 1), F32)
        f_keys.append([])
        for j in range(n_pages):
            f = local[s][j] + offset
            offset = f[:, LANES - 1:LANES]
            f_keys[s].append(f)
        f_new.append((new_local[s] + offset)[:, :n_new])

    scores = [[jnp.dot(q_ref[s], k_refs[s][j][...].reshape(width, PAGE_SIZE).astype(BF16),
                       preferred_element_type=F32) - per_row(f_keys[s][j] * LOG2E)
               for j in range(n_pages)] for s in range(nb)]
    q_tok = lax.broadcasted_iota(jnp.int32, (nrow, n_new), 0) // A_HEADS
    k_tok = lax.broadcasted_iota(jnp.int32, (nrow, n_new), 1)
    s_new = [jnp.where(k_tok <= q_tok,
                       lax.dot_general(q_ref[s], kn_ref[s], _NT, preferred_element_type=F32)
                       - per_row(f_new[s] * LOG2E), NEG_INF) for s in range(nb)]

    m = [jnp.maximum(jnp.max(functools.reduce(jnp.maximum, scores[s]), axis=1, keepdims=True),
                     jnp.max(s_new[s], axis=1, keepdims=True)) for s in range(nb)]
    probs = [[jnp.exp2(x - m[s]) for x in scores[s]] for s in range(nb)]
    p_new = [jnp.exp2(s_new[s] - m[s]) for s in range(nb)]
    l = [jnp.sum(functools.reduce(jnp.add, probs[s]), axis=1, keepdims=True)
         + jnp.sum(p_new[s], axis=1, keepdims=True) for s in range(nb)]
    acc = [jnp.dot(p_new[s].astype(BF16), vn_ref[s], preferred_element_type=F32) for s in range(nb)]
    for s in range(nb):
        for j in range(n_pages):
            vt = v_refs[s][j][...].reshape(width, PAGE_SIZE).astype(BF16)
            acc[s] = acc[s] + lax.dot_general(probs[s][j].astype(BF16), vt, _NT, preferred_element_type=F32)
    row_head = lax.broadcasted_iota(jnp.int32, (nrow, width), 0) % A_HEADS
    col_head = lax.broadcasted_iota(jnp.int32, (nrow, width), 1) // A_HEAD_DIM
    for s in range(nb):
        out = jnp.where(row_head == col_head, acc[s] / l[s], 0.0)
        o_ref[s] = jnp.sum(out.reshape(nq, A_HEADS, width), axis=1)


def _sample_attention(page_table, q, kn, vn, lfn, cache_kt, cache_vt, cache_lf, nb=2):
    b, n_pages = page_table.shape
    nq = q.shape[1] // A_HEADS
    per_seq = lambda a: pl.BlockSpec((nb,) + a.shape[1:], lambda i, pt: (i, 0, 0))
    in_hbm = pl.BlockSpec(memory_space=pl.ANY)
    page_buffers = pltpu.VMEM((2, nb * n_pages) + cache_kt.shape[1:], F32)
    grid_spec = pltpu.PrefetchScalarGridSpec(
        num_scalar_prefetch=1,
        grid=(b // nb,),
        in_specs=[per_seq(q), per_seq(kn), per_seq(vn), per_seq(lfn), _resident(cache_lf.shape), in_hbm, in_hbm],
        out_specs=pl.BlockSpec((nb, nq, A_WIDTH), lambda i, pt: (i, 0, 0)),
        scratch_shapes=[page_buffers, page_buffers, pltpu.SemaphoreType.DMA((2, 2))],
    )
    return pl.pallas_call(
        functools.partial(_sattn_kernel, n_pages=n_pages),
        out_shape=jax.ShapeDtypeStruct((b, nq, A_WIDTH), F32),
        grid_spec=grid_spec,
        compiler_params=_params("arbitrary"),
        name="paged_fox_attention",
    )(page_table, q, kn, vn, lfn, cache_lf, cache_kt, cache_vt)


def _split_weights(w_in, b_fox_f, b_ml_i, b_ml_f):
    d = w_in.shape[0]
    o = 0
    parts = {}
    for name, n in (("aq", A_WIDTH), ("ak", A_WIDTH), ("av", A_WIDTH), ("af", A_HEADS), ("bq", B_WIDTH),
                    ("bk", B_WIDTH), ("bv", B_WIDTH), ("bi", B_HEADS), ("bf", B_HEADS), ("bo", B_WIDTH),
                    ("ga", d), ("gb", d)):
        parts[name] = w_in[:, o:o + n]
        o += n
    assert o == w_in.shape[1]
    small = jnp.concatenate([parts["af"], parts["bi"], parts["bf"]], axis=1)
    bias = jnp.concatenate([b_fox_f, b_ml_i, b_ml_f])
    bf = lambda a: a.astype(BF16)
    return {
        "q": bf(parts["aq"] * (A_HEAD_DIM ** -0.5 * LOG2E)),
        "k": bf(parts["ak"]),
        "v": bf(parts["av"]),
        "qk": bf(jnp.concatenate([parts["bq"], parts["bk"]], axis=1)),
        "bv": bf(parts["bv"]),
        "g": bf(jnp.concatenate([parts["bo"], parts["ga"], parts["gb"]], axis=1)),
        "s": bf(jnp.pad(small, ((0, 0), (0, LANES - N_GATES)))),
        "bs": jnp.pad(bias, (0, LANES - N_GATES)).reshape(1, LANES),
        "st": bf(jnp.pad(small.T, ((0, GATE_ROWS - N_GATES), (0, 0)))),
        "bst": jnp.pad(bias, (0, GATE_ROWS - N_GATES)).reshape(GATE_ROWS, 1),
    }


def kernel(x_prompt, x_sample, cache_k, cache_v, cache_logf, page_table, state_C, state_n, state_m, state_conv,
           c_prompt, c_sample, w_ada, b_ada, g_pre_mix, g_post_mix, g_pre_mlp, g_post_mlp, w_in, b_fox_f,
           b_ml_i, b_ml_f, conv_w, conv_b, w_proj_a, w_proj_b, w_out, w_up, w_down):
    assert w_in.shape[0] == 1, "one trunk layer"
    bsz, seq, d = x_prompt.shape
    dbsz, dseq, _ = x_sample.shape

    w_proj = _split_weights(w_in[0], b_fox_f[0], b_ml_i[0], b_ml_f[0])
    wpa, wpb, wo = w_proj_a[0].astype(BF16), w_proj_b[0].astype(BF16), w_out[0].astype(BF16)
    wu, wd = w_up[0].astype(BF16), w_down[0].astype(BF16)
    g1, g2, g3, g4 = g_pre_mix, g_post_mix, g_pre_mlp, g_post_mlp
    cw, cb = conv_w[0], conv_b

    n_c = bsz + dbsz
    c_all = jnp.concatenate([c_prompt, c_sample, jnp.zeros((-n_c % SUBLANES, d), F32)], axis=0)
    mod = _adaln(c_all, w_ada[0], b_ada)

    def token_path(x2, mods, tm, tiles_per_group, token_minor_q):
        return _in_proj(x2, mods, g1, w_proj, tm, tiles_per_group, token_minor_q), mods

    def head_split(kt):
        g_, _, n_ = kt.shape
        return jnp.transpose(kt.reshape(g_, A_HEADS, A_HEAD_DIM, n_), (0, 3, 1, 2))[None]

    def finish(x2, ya, hb, og, mods, tm, group_tokens):
        x1, h2 = _mix_out(ya, hb, og, x2, mods, g2, g3, wpa, wpb, wo, tm, group_tokens // tm)
        return _mlp(h2, x1, mods, g4, wu, wd, tm, group_tokens // tm)

    tm = min(512, seq)
    m_p = bsz * seq
    x2 = x_prompt.reshape(m_p, d)
    mods = mod[:bsz].reshape(bsz, 1, N_MOD * d)
    (qt, kt, vt, ka, qkb, vb, og, sm, smt), rest = token_path(x2, mods, tm, seq // tm, True)

    r3 = lambda a: a.reshape(bsz, seq, a.shape[-1])
    ya = _attention(qt, r3(ka), vt, _cumsum_lanes(smt)).reshape(m_p, A_WIDTH)

    qkc = _conv(r3(qkb), jnp.zeros((bsz, SUBLANES, 2 * B_WIDTH), F32), cw, cb, tc=512)
    hb, c_p, n_p, m_pr = _mlstm(qkc, r3(vb), r3(sm), smt,
                                jnp.zeros((bsz, B_HEADS, B_HEAD_DIM, B_HEAD_DIM), F32),
                                jnp.zeros((bsz, B_HEADS, 1, B_HEAD_DIM), F32),
                                jnp.zeros((bsz, B_HEADS, 1, 1), F32), t=128, nb=4, g0=A_HEADS)
    y_prompt = finish(x2, ya, hb.reshape(m_p, B_WIDTH), og, rest, tm, seq).reshape(bsz, seq, d)

    new_k_prompt = head_split(kt)
    new_v_prompt = head_split(vt)
    new_logf_prompt = jnp.transpose(smt[:, :A_HEADS, :], (0, 2, 1))[None]
    new_conv_prompt = r3(qkb)[:, seq - (CONV_W - 1):, :][None]

    m_s = dbsz * dseq
    xs2 = x_sample.reshape(m_s, d)
    mods = mod[bsz:n_c].reshape(1, dbsz, N_MOD * d)
    (qa, kt, vt, ka, va, qkb, vb, og, sm, smt), rest = token_path(xs2, mods, m_s, 1, False)

    n_new = 16
    new_rows = lambda a: jnp.pad(a.reshape(dbsz, dseq, -1), ((0, 0), (0, n_new - dseq), (0, 0)))
    lf_s = sm[:, :A_HEADS].reshape(dbsz, dseq, A_HEADS)
    lf_new = jnp.pad(jnp.transpose(lf_s, (0, 2, 1)), ((0, 0), (0, 0), (0, LANES - dseq)))
    q_heads = qa.reshape(dbsz, dseq, A_HEADS, 1, A_HEAD_DIM)
    q_bd = (q_heads * jnp.eye(A_HEADS, dtype=BF16)[None, None, :, :, None]).reshape(dbsz, dseq * A_HEADS, A_WIDTH)
    ya = _sample_attention(
        page_table, q_bd, new_rows(ka), new_rows(va), lf_new,
        jnp.transpose(cache_k[0], (0, 2, 3, 1)), jnp.transpose(cache_v[0], (0, 2, 3, 1)),
        jnp.transpose(cache_logf[0], (0, 2, 1)))
    ya = ya.astype(BF16).reshape(m_s, A_WIDTH)

    t_s = 16
    conv_in = jnp.pad(qkb.reshape(dbsz, dseq, 2 * B_WIDTH), ((0, 0), (0, SUBLANES - dseq), (0, 0)))
    conv_init = jnp.pad(state_conv[0], ((0, 0), (SUBLANES - (CONV_W - 1), 0), (0, 0)))
    qkc = _conv(conv_in, conv_init, cw, cb, tc=SUBLANES, nb=min(16, dbsz))
    tok_valid = (jnp.arange(t_s) < dseq)[None, :, None]
    pad_t = lambda a: jnp.pad(a, ((0, 0), (0, t_s - a.shape[1]), (0, 0)))
    qkc = jnp.where(tok_valid, pad_t(qkc), jnp.zeros((), BF16))
    vb_s = pad_t(vb.reshape(dbsz, dseq, B_WIDTH))
    gates = pad_t(sm[:, A_HEADS:A_HEADS + 2 * B_HEADS].reshape(dbsz, dseq, 2 * B_HEADS))
    neutral = jnp.concatenate([jnp.full((B_HEADS,), NEG_INF, F32), jnp.zeros((B_HEADS,), F32)])
    gcol = jnp.where(tok_valid, gates, neutral)
    grow = jnp.transpose(gcol, (0, 2, 1))
    hb, c_s, n_s, m_sm = _mlstm(qkc, vb_s, gcol, grow, state_C[0], state_n[0][:, :, None, :],
                                state_m[0][:, :, None, None], t=t_s, nb=min(16, dbsz))
    hb = hb[:, :dseq, :].reshape(m_s, B_WIDTH)
    y_sample = finish(xs2, ya, hb, og, rest, m_s, m_s).reshape(dbsz, dseq, d)

    new_k_sample = head_split(kt).reshape(1, dbsz, dseq, A_HEADS, A_HEAD_DIM)
    new_v_sample = head_split(vt).reshape(1, dbsz, dseq, A_HEADS, A_HEAD_DIM)
    new_logf_sample = lf_s[None]
    qkb3 = qkb.reshape(dbsz, dseq, 2 * B_WIDTH)
    new_conv_sample = jnp.concatenate([state_conv[0], qkb3], axis=1)[:, dseq:, :][None]

    return (y_prompt, y_sample, new_k_prompt, new_v_prompt, new_logf_prompt,
            c_p[None], n_p[:, :, 0, :][None], m_pr[:, :, 0, 0][None], new_conv_prompt,
            new_k_sample, new_v_sample, new_logf_sample,
            c_s[None], n_s[:, :, 0, :][None], m_sm[:, :, 0, 0][None], new_conv_sample)
```

```python
import functools

import jax
import jax.numpy as jnp
from jax import lax
from jax.experimental import pallas as pl
from jax.experimental.pallas import tpu as pltpu

F32 = jnp.float32
BF16 = jnp.bfloat16

A_HEADS = 8
A_HEAD_DIM = 64
A_WIDTH = A_HEADS * A_HEAD_DIM
B_HEADS = 4
B_HEAD_DIM = 128
B_WIDTH = B_HEADS * B_HEAD_DIM
CONV_W = 4
N_MOD = 6
RMS_EPS = 1e-6
NEG_INF = -1e30
PAGE_SIZE = 128
LOG2E = 1.4426950408889634

LANES = 128
SUBLANES = 8
VMEM_LIMIT_BYTES = 56 * 1024 * 1024

TOKEN_TILE = 512
ADALN_COLS = 1024
MLP_CHUNK = 1024
ATTN_Q_BLOCK, ATTN_K_BLOCK = 1024, 512
ATTN_BUILD_CHUNK = 512
CONV_TILE = 512
MLSTM_CHUNK, MLSTM_SEQS = 128, 4
SAMPLE_PAD = 16
SAMPLE_SEQS = 16
PAGED_SEQS = 2

_NT = (((1,), (1,)), ((), ()))
_TN = (((0,), (0,)), ((), ()))


def _params(*sem):
    return pltpu.CompilerParams(dimension_semantics=sem, vmem_limit_bytes=VMEM_LIMIT_BYTES)


def _resident(shape):
    nd = len(shape)
    return pl.BlockSpec(shape, lambda *_: (0,) * nd, pipeline_mode=pl.Buffered(1))


def _rms(x, g):
    r = lax.rsqrt(jnp.mean(x * x, axis=-1, keepdims=True) + RMS_EPS)
    return (x * r) * g


def _log_sigmoid(x):
    return jnp.minimum(x, 0.0) - jnp.log1p(jnp.exp(-jnp.abs(x)))


def _mod_kernel(c_ref, w_ref, b_ref, o_ref):
    c = c_ref[...]
    a = (c * jax.nn.sigmoid(c)).astype(BF16)
    o_ref[...] = jnp.dot(a, w_ref[...].astype(BF16), preferred_element_type=F32) + b_ref[...]


def _adaln(c, w, b, tn=ADALN_COLS):
    r, d = c.shape
    n = w.shape[1]
    return pl.pallas_call(
        _mod_kernel,
        out_shape=jax.ShapeDtypeStruct((r, n), F32),
        grid=(n // tn,),
        in_specs=[pl.BlockSpec((r, d), lambda j: (0, 0)),
                  pl.BlockSpec((d, tn), lambda j: (0, j)),
                  pl.BlockSpec((1, tn), lambda j: (0, j))],
        out_specs=pl.BlockSpec((r, tn), lambda j: (0, j)),
        compiler_params=_params("arbitrary"),
        name="adaln",
    )(c, w, b)


SHIFT_MIX, SCALE_MIX, GATE_MIX, SHIFT_MLP, SCALE_MLP, GATE_MLP = range(N_MOD)


def _mod_spec(arr, which, tm, tiles_per_group, m):
    g, r, width = arr.shape
    d = width // N_MOD
    if r == 1:
        return pl.BlockSpec((1, 1, d), lambda i: (i // tiles_per_group, 0, which))
    assert g == 1
    return pl.BlockSpec((1, tm * r // m, d), lambda i: (0, i, which))


def _mod_rows(ref, part, tm):
    if ref.shape[1] == 1:
        return ref[0]
    rep = tm // ref.shape[1]
    return jnp.repeat(ref[0, part.start // rep:part.stop // rep, :], rep, axis=0)


TOKEN_PARTS = 2
N_GATES = A_HEADS + 2 * B_HEADS
GATE_ROWS = 16


def _gate_act(z, idx):
    return jnp.where((idx >= A_HEADS) & (idx < A_HEADS + B_HEADS), z, _log_sigmoid(z))


def _in_kernel(x_ref, sc_ref, sh_ref, g_ref, wq_ref, wk_ref, wv_ref, wqk_ref, wbv_ref, wg_ref,
               ws_ref, bs_ref, wst_ref, bst_ref, q_ref, kt_ref, vt_ref, ka_ref, *rest, token_minor_q):
    if not token_minor_q:
        va_ref, *rest = rest
    qkb_ref, vb_ref, og_ref, sm_ref, smt_ref = rest
    tm = x_ref.shape[0]
    parts = [slice(r, r + tm // TOKEN_PARTS) for r in range(0, tm, tm // TOKEN_PARTS)]
    mod = lambda ref, p: _mod_rows(ref, p, tm)
    hb = [(_rms(x_ref[p, :], g_ref[...]) * (1.0 + mod(sc_ref, p)) + mod(sh_ref, p)).astype(BF16) for p in parts]
    proj = lambda w_ref: [jnp.dot(h, w_ref[...], preferred_element_type=F32) for h in hb]
    for p, q in zip(parts, proj(wq_ref)):
        if token_minor_q:
            q_ref[0, :, p] = q.T.astype(BF16)
        else:
            q_ref[p, :] = q.astype(BF16)
    for p, k in zip(parts, proj(wk_ref)):
        kt_ref[0, :, p] = k.T
        ka_ref[p, :] = k.astype(BF16)
    for p, v in zip(parts, proj(wv_ref)):
        vt_ref[0, :, p] = v.T
        if not token_minor_q:
            va_ref[p, :] = v.astype(BF16)
    for p, z in zip(parts, proj(wqk_ref)):
        qkb_ref[p, :] = z
    for p, z in zip(parts, proj(wbv_ref)):
        vb_ref[p, :] = z.astype(BF16)
    for p, z in zip(parts, proj(wg_ref)):
        og_ref[p, :] = jax.nn.sigmoid(z).astype(BF16)
    for p, z in zip(parts, proj(ws_ref)):
        z = z + bs_ref[...]
        sm_ref[p, :] = _gate_act(z, lax.broadcasted_iota(jnp.int32, z.shape, 1))
    for p, h in zip(parts, hb):
        z = lax.dot_general(wst_ref[...], h, _NT, preferred_element_type=F32) + bst_ref[...]
        smt_ref[0, :, p] = _gate_act(z, lax.broadcasted_iota(jnp.int32, z.shape, 0))


def _in_proj(x, mod, g, w, tm, tiles_per_group, token_minor_q):
    m, d = x.shape
    group_tokens = tm * tiles_per_group
    groups = m // group_tokens
    row = lambda n: pl.BlockSpec((tm, n), lambda i: (i, 0))
    col = lambda n: pl.BlockSpec((1, n, tm), lambda i: (i // tiles_per_group, 0, i % tiles_per_group))
    rows_out = lambda n, dt: (jax.ShapeDtypeStruct((m, n), dt), row(n))
    cols_out = lambda n, dt=F32: (jax.ShapeDtypeStruct((groups, n, group_tokens), dt), col(n))
    outs = [cols_out(A_WIDTH, BF16) if token_minor_q else rows_out(A_WIDTH, BF16),
            cols_out(A_WIDTH), cols_out(A_WIDTH), rows_out(A_WIDTH, BF16)]
    if not token_minor_q:
        outs.append(rows_out(A_WIDTH, BF16))
    outs += [rows_out(2 * B_WIDTH, F32), rows_out(B_WIDTH, BF16), rows_out(w["g"].shape[1], BF16),
             rows_out(LANES, F32), cols_out(GATE_ROWS)]
    weights = [w["q"], w["k"], w["v"], w["qk"], w["bv"], w["g"], w["s"], w["bs"], w["st"], w["bst"]]
    return pl.pallas_call(
        functools.partial(_in_kernel, token_minor_q=token_minor_q),
        out_shape=[o[0] for o in outs],
        grid=(m // tm,),
        in_specs=[row(d), _mod_spec(mod, SCALE_MIX, tm, tiles_per_group, m),
                  _mod_spec(mod, SHIFT_MIX, tm, tiles_per_group, m),
                  _resident(g.shape)] + [_resident(a.shape) for a in weights],
        out_specs=[o[1] for o in outs],
        compiler_params=_params("arbitrary"),
        name="in_proj",
    )(x, mod, mod, g, *weights)


MIX_CHUNK = 256


def _mix_kernel(ya_ref, hb_ref, og_ref, x_ref, gt_ref, sc_ref, sh_ref, g1_ref, g2_ref,
                wpa_ref, wpb_ref, wo_ref, x1_ref, h2_ref):
    tm, d = x_ref.shape
    parts = [slice(r, r + tm // TOKEN_PARTS) for r in range(0, tm, tm // TOKEN_PARTS)]
    mod = lambda ref, p: _mod_rows(ref, p, tm)
    merged = []
    for p in parts:
        yb_in = (og_ref[p, :B_WIDTH].astype(F32) * hb_ref[p, :].astype(F32)).astype(BF16)
        cols = []
        for c in range(0, d, MIX_CHUNK):
            a = jnp.dot(ya_ref[p, :], wpa_ref[:, c:c + MIX_CHUNK], preferred_element_type=F32)
            b = jnp.dot(yb_in, wpb_ref[:, c:c + MIX_CHUNK], preferred_element_type=F32)
            ga = og_ref[p, B_WIDTH + c:B_WIDTH + c + MIX_CHUNK].astype(F32)
            gb = og_ref[p, B_WIDTH + d + c:B_WIDTH + d + c + MIX_CHUNK].astype(F32)
            cols.append((ga * a + gb * b).astype(BF16))
        merged.append(jnp.concatenate(cols, axis=1))
    yo = [jnp.dot(y, wo_ref[...], preferred_element_type=F32) for y in merged]
    for p, o in zip(parts, yo):
        x1 = x_ref[p, :] + mod(gt_ref, p) * _rms(o, g1_ref[...])
        x1_ref[p, :] = x1
        h2_ref[p, :] = (_rms(x1, g2_ref[...]) * (1.0 + mod(sc_ref, p)) + mod(sh_ref, p)).astype(BF16)


def _mix_out(ya, hb, og, x, mod, g1, g2, wpa, wpb, wo, tm, tiles_per_group):
    m, d = x.shape
    row = lambda n: pl.BlockSpec((tm, n), lambda i: (i, 0))
    ms = lambda which: _mod_spec(mod, which, tm, tiles_per_group, m)
    return pl.pallas_call(
        _mix_kernel,
        out_shape=[jax.ShapeDtypeStruct((m, d), F32), jax.ShapeDtypeStruct((m, d), BF16)],
        grid=(m // tm,),
        in_specs=[row(ya.shape[1]), row(hb.shape[1]), row(og.shape[1]), row(d),
                  ms(GATE_MIX), ms(SCALE_MLP), ms(SHIFT_MLP),
                  _resident(g1.shape), _resident(g2.shape), _resident(wpa.shape), _resident(wpb.shape),
                  _resident(wo.shape)],
        out_specs=[row(d), row(d)],
        compiler_params=_params("arbitrary"),
        name="mix_out",
    )(ya, hb, og, x, mod, mod, mod, g1, g2, wpa, wpb, wo)


def _mlp_kernel(h2_ref, x1_ref, gt_ref, g_ref, wu_ref, wd_ref, y_ref, *, fc):
    tm = x1_ref.shape[0]
    parts = [slice(r, r + tm // TOKEN_PARTS) for r in range(0, tm, tm // TOKEN_PARTS)]
    mod = lambda ref, p: _mod_rows(ref, p, tm)
    acc = [jnp.zeros((tm // TOKEN_PARTS, x1_ref.shape[1]), F32) for _ in parts]
    for c in range(wu_ref.shape[1] // fc):
        u = [jnp.dot(h2_ref[p, :], wu_ref[:, c * fc:(c + 1) * fc], preferred_element_type=F32) for p in parts]
        u = [jnp.square(jnp.maximum(x, 0.0)).astype(BF16) for x in u]
        acc = [a + jnp.dot(x, wd_ref[c * fc:(c + 1) * fc, :], preferred_element_type=F32) for a, x in zip(acc, u)]
    for p, a in zip(parts, acc):
        y_ref[p, :] = x1_ref[p, :] + mod(gt_ref, p) * _rms(a, g_ref[...])


def _mlp(h2, x1, mod, g, wu, wd, tm, tiles_per_group, fc=MLP_CHUNK):
    m, d = x1.shape
    row = lambda n: pl.BlockSpec((tm, n), lambda i: (i, 0))
    return pl.pallas_call(
        functools.partial(_mlp_kernel, fc=fc),
        out_shape=jax.ShapeDtypeStruct((m, d), F32),
        grid=(m // tm,),
        in_specs=[row(d), row(d), _mod_spec(mod, GATE_MLP, tm, tiles_per_group, m), _resident(g.shape),
                  _resident(wu.shape), _resident(wd.shape)],
        out_specs=row(d),
        compiler_params=_params("arbitrary"),
        name="mlp",
    )(h2, x1, mod, g, wu, wd)


def _lane_scan(x, lane, shifts):
    for s in shifts:
        x = x + jnp.where(lane >= s, pltpu.roll(x, s, axis=1), 0.0)
    return x


def _cumsum_kernel(x_ref, o_ref):
    rows, length = x_ref.shape[1], x_ref.shape[2]
    lane = lax.broadcasted_iota(jnp.int32, (rows, LANES), 1)
    carry = jnp.zeros((rows, 1), F32)
    for c in range(length // LANES):
        x = _lane_scan(x_ref[0, :, c * LANES:(c + 1) * LANES], lane, (1, 2, 4, 8, 16, 32, 64)) + carry
        o_ref[0, :, c * LANES:(c + 1) * LANES] = x
        carry = x[:, LANES - 1:LANES]


def _cumsum_lanes(x):
    b, r, length = x.shape
    spec = pl.BlockSpec((1, r, length), lambda i: (i, 0, 0))
    return pl.pallas_call(
        _cumsum_kernel,
        out_shape=jax.ShapeDtypeStruct(x.shape, F32),
        grid=(b,),
        in_specs=[spec],
        out_specs=spec,
        compiler_params=_params("arbitrary"),
        name="logf_cumsum",
    )(x)


V_ROWS = 80


def _aug_lane0(hh):
    return A_HEAD_DIM if hh == 0 else 0


def _attn_kernel(qt_ref, k_ref, vt_ref, f_ref, o_ref, kaug_ref, vaug_ref, s_ref, *, blk_q, blk_k, chunk):
    pair = pl.program_id(1)
    qi = pl.program_id(2)
    length = k_ref.shape[1]
    lane = lax.broadcasted_iota(jnp.int32, (1, LANES), 1)
    own = (lane < A_HEAD_DIM, lane >= A_HEAD_DIM)
    feat = lax.broadcasted_iota(jnp.int32, (LANES, 1), 0)

    @pl.when(qi == 0)
    def _():
        ones_row = jnp.where(lax.broadcasted_iota(jnp.int32, (V_ROWS - A_HEAD_DIM, chunk), 0) == 0, 1.0, 0.0)

        def build(c, _):
            toks = pl.ds(pl.multiple_of(c * chunk, chunk), chunk)
            k = k_ref[0, toks, :]
            vt = vt_ref[0, :, toks]
            for hh in range(2):
                a0 = _aug_lane0(hh)
                neg_f = f_ref[0, pl.ds(2 * pair + hh, 1), toks] * (-LOG2E)
                hi = neg_f.astype(BF16).astype(F32)
                mid = (neg_f - hi).astype(BF16).astype(F32)
                lo = (neg_f - hi - mid).astype(BF16).astype(F32)
                bias = jnp.concatenate(
                    [jnp.where(feat == a0, hi[:, t], jnp.where(feat == a0 + 1, mid[:, t],
                                                               jnp.where(feat == a0 + 2, lo[:, t], 0.0))).T
                     for t in (slice(t0, t0 + LANES) for t0 in range(0, chunk, LANES))], axis=0)
                kaug_ref[hh, toks, :] = jnp.where(own[hh], k, bias.astype(BF16))
                dims = vt[hh * A_HEAD_DIM:(hh + 1) * A_HEAD_DIM]
                vaug_ref[hh, :, toks] = jnp.concatenate([dims, ones_row], axis=0).astype(BF16)
            return 0
        lax.fori_loop(0, length // chunk, build, 0)

    qt = qt_ref[0]
    qh = []
    for hh in range(2):
        a0 = _aug_lane0(hh)
        ones = jnp.where((feat >= a0) & (feat < a0 + 3), 1.0, 0.0).astype(BF16)
        qh.append(jnp.where((feat < A_HEAD_DIM) if hh == 0 else (feat >= A_HEAD_DIM), qt, ones))

    def keys(j):
        return pl.ds(pl.multiple_of(j * blk_k, blk_k), blk_k)

    def scores(j, slot, c0=0):
        for hh in range(2):
            s_ref[slot, hh, :, c0:] = jnp.dot(kaug_ref[hh, keys(j), :], qh[hh][:, c0:], preferred_element_type=F32)

    def fold(j, slot, carry, causal, c0=0):
        probs, stats = [], []
        for hh in range(2):
            m = carry[hh][0]
            sh = s_ref[slot, hh, :, c0:]
            if causal:
                key = lax.broadcasted_iota(jnp.int32, sh.shape, 0)
                qry = lax.broadcasted_iota(jnp.int32, sh.shape, 1)
                sh = jnp.where(key <= qry, sh, NEG_INF)
            m_new = jnp.maximum(m, jnp.max(sh, axis=0, keepdims=True))
            probs.append(jnp.exp2(sh - m_new).astype(BF16))
            stats.append((m_new, jnp.exp2(m - m_new)))
        return tuple(
            (stats[hh][0], stats[hh][1] * carry[hh][1]
             + jnp.dot(vaug_ref[hh, :, keys(j)], probs[hh], preferred_element_type=F32))
            for hh in range(2))

    assert blk_q == 2 * blk_k
    init = tuple((jnp.full((1, blk_q), NEG_INF, F32), jnp.zeros((V_ROWS, blk_q), F32)) for _ in range(2))

    def body(p, carry):
        scores(2 * p + 1, 1)
        carry = fold(2 * p, 0, carry, False)
        scores(2 * p + 2, 0)
        return fold(2 * p + 1, 1, carry, False)

    scores(0, 0)
    carry = lax.fori_loop(0, qi, body, init)
    scores(2 * qi + 1, 1, blk_k)
    carry = fold(2 * qi, 0, carry, True)
    upper = fold(2 * qi + 1, 1, tuple((m[:, blk_k:], acc[:, blk_k:]) for m, acc in carry), True, blk_k)
    out = []
    for hh in range(2):
        acc = jnp.concatenate([carry[hh][1][:, :blk_k], upper[hh][1]], axis=1)
        out.append(acc[:A_HEAD_DIM] / acc[A_HEAD_DIM:A_HEAD_DIM + 1])
    o_ref[0] = jnp.concatenate(out, axis=0).T.astype(BF16)


def _attention(qt, k, vt, f, blk_q=ATTN_Q_BLOCK, blk_k=ATTN_K_BLOCK):
    b, length, width = k.shape
    pairs = width // LANES
    blk_q, blk_k = min(blk_q, length), min(blk_k, length)
    return pl.pallas_call(
        functools.partial(_attn_kernel, blk_q=blk_q, blk_k=blk_k, chunk=min(ATTN_BUILD_CHUNK, length)),
        out_shape=jax.ShapeDtypeStruct(k.shape, BF16),
        grid=(b, pairs, length // blk_q),
        in_specs=[pl.BlockSpec((1, LANES, blk_q), lambda i, p, t: (i, p, t)),
                  pl.BlockSpec((1, length, LANES), lambda i, p, t: (i, 0, p)),
                  pl.BlockSpec((1, LANES, length), lambda i, p, t: (i, p, 0)),
                  pl.BlockSpec((1,) + f.shape[1:], lambda i, p, t: (i, 0, 0))],
        out_specs=pl.BlockSpec((1, blk_q, LANES), lambda i, p, t: (i, t, p)),
        scratch_shapes=[pltpu.VMEM((2, length, LANES), BF16), pltpu.VMEM((2, V_ROWS, length), BF16),
                        pltpu.VMEM((2, 2, blk_k, blk_q), F32)],
        compiler_params=_params("arbitrary", "arbitrary", "arbitrary"),
        name="fox_attention",
    )(qt, k, vt, f)


def _conv_silu(xp_ref, w_ref, b_ref, tc):
    y = b_ref[...]
    for j in range(CONV_W):
        off = SUBLANES - (CONV_W - 1) + j
        y = y + w_ref[j:j + 1, :] * xp_ref[off:off + tc, :]
    lane = lax.broadcasted_iota(jnp.int32, (1, y.shape[1]), 1)
    return y * jax.nn.sigmoid(y) * jnp.where(lane < B_WIDTH, 1.0, B_HEAD_DIM ** -0.5)


def _conv_kernel(x_ref, prev_ref, init_ref, w_ref, b_ref, o_ref, xp_ref, *, tc):
    for s in range(x_ref.shape[0]):
        xp_ref[s, 0:SUBLANES, :] = jnp.where(pl.program_id(1) == 0, init_ref[s], prev_ref[s])
        xp_ref[s, SUBLANES:SUBLANES + tc, :] = x_ref[s]
        o_ref[s] = _conv_silu(xp_ref.at[s], w_ref, b_ref, tc).astype(BF16)


def _conv(x, init, w, b, tc, nb=1):
    bsz, length, c = x.shape
    tpb = tc // SUBLANES
    return pl.pallas_call(
        functools.partial(_conv_kernel, tc=tc),
        out_shape=jax.ShapeDtypeStruct(x.shape, BF16),
        grid=(bsz // nb, length // tc),
        in_specs=[pl.BlockSpec((nb, tc, c), lambda i, t: (i, t, 0)),
                  pl.BlockSpec((nb, SUBLANES, c), lambda i, t: (i, jnp.maximum(t * tpb - 1, 0), 0)),
                  pl.BlockSpec((nb, SUBLANES, c), lambda i, t: (i, 0, 0)),
                  pl.BlockSpec((CONV_W, c), lambda i, t: (0, 0)),
                  pl.BlockSpec((1, c), lambda i, t: (0, 0))],
        out_specs=pl.BlockSpec((nb, tc, c), lambda i, t: (i, t, 0)),
        scratch_shapes=[pltpu.VMEM((nb, tc + SUBLANES, c), F32)],
        compiler_params=_params("arbitrary", "arbitrary"),
        name="short_conv",
    )(x, x, init, w, b)


_BNT = (((2,), (2,)), ((0,), (0,)))
_BNN = (((2,), (1,)), ((0,), (0,)))
_BTN = (((1,), (1,)), ((0,), (0,)))


def _mlstm_heads(q, k, v, ir, fr, ic, fc, c_st, n_st, m_st):
    n, t, d = q.shape
    row = lax.broadcasted_iota(jnp.int32, (n, t, t), 1)
    col = lax.broadcasted_iota(jnp.int32, (n, t, t), 2)
    tril = col <= row
    spread = lambda x, w: jnp.broadcast_to(x, (n, t, w))
    both = lambda x: (spread(x, t),) * 2 if t == d else (spread(x, t), spread(x, d))
    bcum_c = jnp.sum(jnp.where(tril, fr, 0.0), axis=2, keepdims=True)
    bcum_r = jnp.sum(jnp.where(row <= col, fc, 0.0), axis=1, keepdims=True)
    bcum_t, bcum_d = both(bcum_c)
    dlog = jnp.where(tril, bcum_t - bcum_r + ir, NEG_INF)
    mt_t, mt_d = both(jnp.maximum(bcum_c + m_st, jnp.max(dlog, axis=2, keepdims=True)))
    a = jnp.exp(dlog - mt_t) * lax.dot_general(q, k, _BNT, preferred_element_type=F32)
    si = jnp.exp(bcum_d + m_st - mt_d)
    num = (lax.dot_general(a.astype(BF16), v, _BNN, preferred_element_type=F32)
           + si * lax.dot_general(q, c_st.astype(BF16), _BNN, preferred_element_type=F32))
    den = jnp.sum(a, axis=2, keepdims=True) + si * jnp.sum(q.astype(F32) * n_st, axis=2, keepdims=True)
    h = num / jnp.maximum(jnp.abs(den), jnp.exp(-mt_d))
    bl = bcum_r[:, :, t - 1:t]
    m_new = jnp.maximum(bl + m_st, jnp.max(bl - bcum_r + ir, axis=2, keepdims=True))
    kw = jnp.exp(bl - bcum_d + spread(ic, d) - m_new) * k.astype(F32)
    decay = jnp.exp(bl + m_st - m_new)
    c_new = decay * c_st + lax.dot_general(kw.astype(BF16), v, _BTN, preferred_element_type=F32)
    n_new = decay * n_st + jnp.sum(kw, axis=1, keepdims=True)
    return h, c_new, n_new, m_new


def _mlstm_kernel(q_ref, k_ref, v_ref, gc_ref, gr_ref, c0_ref, n0_ref, m0_ref,
                  h_ref, c_ref, n_ref, m_ref, *, g0):
    @pl.when(pl.program_id(1) == 0)
    def _():
        c_ref[...] = c0_ref[...]
        n_ref[...] = n0_ref[...]
        m_ref[...] = m0_ref[...]

    nb = q_ref.shape[0]
    where = [(s, hd, slice(hd * B_HEAD_DIM, (hd + 1) * B_HEAD_DIM)) for s in range(nb) for hd in range(B_HEADS)]
    heads = lambda ref: jnp.stack([ref[s, :, sl] for s, _, sl in where])
    gc = [gc_ref[s] for s in range(nb)]
    gr = [gr_ref[s] for s in range(nb)]
    gi, gf = g0, g0 + B_HEADS
    h, c_new, n_new, m_new = _mlstm_heads(
        heads(q_ref), heads(k_ref), heads(v_ref),
        jnp.stack([gr[s][gi + hd:gi + hd + 1, :] for s, hd, _ in where]),
        jnp.stack([gr[s][gf + hd:gf + hd + 1, :] for s, hd, _ in where]),
        jnp.stack([gc[s][:, gi + hd:gi + hd + 1] for s, hd, _ in where]),
        jnp.stack([gc[s][:, gf + hd:gf + hd + 1] for s, hd, _ in where]),
        c_ref[...].reshape((nb * B_HEADS,) + c_ref.shape[2:]),
        n_ref[...].reshape((nb * B_HEADS,) + n_ref.shape[2:]),
        m_ref[...].reshape((nb * B_HEADS,) + m_ref.shape[2:]))
    for i, (s, _, sl) in enumerate(where):
        h_ref[s, :, sl] = h[i].astype(BF16)
    c_ref[...] = c_new.reshape(c_ref.shape)
    n_ref[...] = n_new.reshape(n_ref.shape)
    m_ref[...] = m_new.reshape(m_ref.shape)


def _mlstm(qk, v, gcol, grow, c0, n0, m0, t, nb=1, g0=0):
    b, length, _ = v.shape
    st = lambda a: pl.BlockSpec((nb,) + a.shape[1:], lambda i, j: (i, 0, 0, 0))
    return pl.pallas_call(
        functools.partial(_mlstm_kernel, g0=g0),
        out_shape=[jax.ShapeDtypeStruct(v.shape, BF16), jax.ShapeDtypeStruct(c0.shape, F32),
                   jax.ShapeDtypeStruct(n0.shape, F32), jax.ShapeDtypeStruct(m0.shape, F32)],
        grid=(b // nb, length // t),
        in_specs=[pl.BlockSpec((nb, t, B_WIDTH), lambda i, j: (i, j, 0)),
                  pl.BlockSpec((nb, t, B_WIDTH), lambda i, j: (i, j, 1)),
                  pl.BlockSpec((nb, t, B_WIDTH), lambda i, j: (i, j, 0)),
                  pl.BlockSpec((nb, t, gcol.shape[2]), lambda i, j: (i, j, 0)),
                  pl.BlockSpec((nb, grow.shape[1], t), lambda i, j: (i, 0, j)),
                  st(c0), st(n0), st(m0)],
        out_specs=[pl.BlockSpec((nb, t, B_WIDTH), lambda i, j: (i, j, 0)), st(c0), st(n0), st(m0)],
        compiler_params=_params("arbitrary", "arbitrary"),
        name="mlstm",
    )(qk, qk, v, gcol, grow, c0, n0, m0)


def _page_copies(pt_ref, step, slot, ck_ref, cv_ref, kbuf_ref, vbuf_ref, sem_ref, nb, n_pages):
    copies = []
    for s in range(nb):
        for j in range(n_pages):
            page = pt_ref[step * nb + s, j]
            at = s * n_pages + j
            copies.append(pltpu.make_async_copy(ck_ref.at[page], kbuf_ref.at[slot, at], sem_ref.at[slot, 0]))
            copies.append(pltpu.make_async_copy(cv_ref.at[page], vbuf_ref.at[slot, at], sem_ref.at[slot, 1]))
    return copies


def _sattn_kernel(pt_ref, q_ref, kn_ref, vn_ref, lfn_ref, lf_ref, ck_ref, cv_ref, o_ref,
                  kbuf_ref, vbuf_ref, sem_ref, *, n_pages):
    nb, nrow, width = q_ref.shape
    step, n_steps = pl.program_id(0), pl.num_programs(0)
    slot = step % 2
    fetch = functools.partial(_page_copies, pt_ref, ck_ref=ck_ref, cv_ref=cv_ref, kbuf_ref=kbuf_ref,
                              vbuf_ref=vbuf_ref, sem_ref=sem_ref, nb=nb, n_pages=n_pages)

    @pl.when(step == 0)
    def _():
        for c in fetch(0, 0):
            c.start()

    @pl.when(step + 1 < n_steps)
    def _():
        for c in fetch(step + 1, 1 - slot):
            c.start()

    for c in fetch(step, slot):
        c.wait()
    k_refs = [[kbuf_ref.at[slot, s * n_pages + j] for j in range(n_pages)] for s in range(nb)]
    v_refs = [[vbuf_ref.at[slot, s * n_pages + j] for j in range(n_pages)] for s in range(nb)]
    first = step * nb
    nq = nrow // A_HEADS
    n_new = kn_ref.shape[1]
    lane = lax.broadcasted_iota(jnp.int32, (A_HEADS, LANES), 1)
    per_row = lambda f: jnp.concatenate([f] * nq, axis=0)
    scan = lambda x: _lane_scan(x, lane, (1, 2, 4, 8, 16, 32, 64))

    local = [[scan(lf_ref[pt_ref[first + s, j]]) for j in range(n_pages)] for s in range(nb)]
    new_local = [scan(lfn_ref[s]) for s in range(nb)]
    f_keys, f_new = [], []
    for s in range(nb):
        offset = jnp.zeros((A_HEADS, 1), F32)
        f_keys.append([])
        for j in range(n_pages):
            f = local[s][j] + offset
            offset = f[:, LANES - 1:LANES]
            f_keys[s].append(f)
        f_new.append((new_local[s] + offset)[:, :n_new])

    scores = [[jnp.dot(q_ref[s], k_refs[s][j][...].reshape(width, PAGE_SIZE).astype(BF16),
                       preferred_element_type=F32) - per_row(f_keys[s][j] * LOG2E)
               for j in range(n_pages)] for s in range(nb)]
    q_tok = lax.broadcasted_iota(jnp.int32, (nrow, n_new), 0) // A_HEADS
    k_tok = lax.broadcasted_iota(jnp.int32, (nrow, n_new), 1)
    s_new = [jnp.where(k_tok <= q_tok,
                       lax.dot_general(q_ref[s], kn_ref[s], _NT, preferred_element_type=F32)
                       - per_row(f_new[s] * LOG2E), NEG_INF) for s in range(nb)]

    m = [jnp.maximum(jnp.max(functools.reduce(jnp.maximum, scores[s]), axis=1, keepdims=True),
                     jnp.max(s_new[s], axis=1, keepdims=True)) for s in range(nb)]
    probs = [[jnp.exp2(x - m[s]) for x in scores[s]] for s in range(nb)]
    p_new = [jnp.exp2(s_new[s] - m[s]) for s in range(nb)]
    l = [jnp.sum(functools.reduce(jnp.add, probs[s]), axis=1, keepdims=True)
         + jnp.sum(p_new[s], axis=1, keepdims=True) for s in range(nb)]
    acc = [jnp.dot(p_new[s].astype(BF16), vn_ref[s], preferred_element_type=F32) for s in range(nb)]
    for s in range(nb):
        for j in range(n_pages):
            vt = v_refs[s][j][...].reshape(width, PAGE_SIZE).astype(BF16)
            acc[s] = acc[s] + lax.dot_general(probs[s][j].astype(BF16), vt, _NT, preferred_element_type=F32)
    row_head = lax.broadcasted_iota(jnp.int32, (nrow, width), 0) % A_HEADS
    col_head = lax.broadcasted_iota(jnp.int32, (nrow, width), 1) // A_HEAD_DIM
    for s in range(nb):
        out = jnp.where(row_head == col_head, acc[s] / l[s], 0.0)
        o_ref[s] = jnp.sum(out.reshape(nq, A_HEADS, width), axis=1)


def _sample_attention(page_table, q, kn, vn, lfn, cache_kt, cache_vt, cache_lf, nb=PAGED_SEQS):
    b, n_pages = page_table.shape
    nq = q.shape[1] // A_HEADS
    per_seq = lambda a: pl.BlockSpec((nb,) + a.shape[1:], lambda i, pt: (i, 0, 0))
    in_hbm = pl.BlockSpec(memory_space=pl.ANY)
    page_buffers = pltpu.VMEM((2, nb * n_pages) + cache_kt.shape[1:], F32)
    grid_spec = pltpu.PrefetchScalarGridSpec(
        num_scalar_prefetch=1,
        grid=(b // nb,),
        in_specs=[per_seq(q), per_seq(kn), per_seq(vn), per_seq(lfn), _resident(cache_lf.shape), in_hbm, in_hbm],
        out_specs=pl.BlockSpec((nb, nq, A_WIDTH), lambda i, pt: (i, 0, 0)),
        scratch_shapes=[page_buffers, page_buffers, pltpu.SemaphoreType.DMA((2, 2))],
    )
    return pl.pallas_call(
        functools.partial(_sattn_kernel, n_pages=n_pages),
        out_shape=jax.ShapeDtypeStruct((b, nq, A_WIDTH), F32),
        grid_spec=grid_spec,
        compiler_params=_params("arbitrary"),
        name="paged_fox_attention",
    )(page_table, q, kn, vn, lfn, cache_lf, cache_kt, cache_vt)


def _split_weights(w_in, b_fox_f, b_ml_i, b_ml_f):
    d = w_in.shape[0]
    o = 0
    parts = {}
    for name, n in (("aq", A_WIDTH), ("ak", A_WIDTH), ("av", A_WIDTH), ("af", A_HEADS), ("bq", B_WIDTH),
                    ("bk", B_WIDTH), ("bv", B_WIDTH), ("bi", B_HEADS), ("bf", B_HEADS), ("bo", B_WIDTH),
                    ("ga", d), ("gb", d)):
        parts[name] = w_in[:, o:o + n]
        o += n
    assert o == w_in.shape[1]
    small = jnp.concatenate([parts["af"], parts["bi"], parts["bf"]], axis=1)
    bias = jnp.concatenate([b_fox_f, b_ml_i, b_ml_f])
    bf = lambda a: a.astype(BF16)
    return {
        "q": bf(parts["aq"] * (A_HEAD_DIM ** -0.5 * LOG2E)),
        "k": bf(parts["ak"]),
        "v": bf(parts["av"]),
        "qk": bf(jnp.concatenate([parts["bq"], parts["bk"]], axis=1)),
        "bv": bf(parts["bv"]),
        "g": bf(jnp.concatenate([parts["bo"], parts["ga"], parts["gb"]], axis=1)),
        "s": bf(jnp.pad(small, ((0, 0), (0, LANES - N_GATES)))),
        "bs": jnp.pad(bias, (0, LANES - N_GATES)).reshape(1, LANES),
        "st": bf(jnp.pad(small.T, ((0, GATE_ROWS - N_GATES), (0, 0)))),
        "bst": jnp.pad(bias, (0, GATE_ROWS - N_GATES)).reshape(GATE_ROWS, 1),
    }


def kernel(x_prompt, x_sample, cache_k, cache_v, cache_logf, page_table, state_C, state_n, state_m, state_conv,
           c_prompt, c_sample, w_ada, b_ada, g_pre_mix, g_post_mix, g_pre_mlp, g_post_mlp, w_in, b_fox_f,
           b_ml_i, b_ml_f, conv_w, conv_b, w_proj_a, w_proj_b, w_out, w_up, w_down):
    assert w_in.shape[0] == 1, "one trunk layer"
    bsz, seq, d = x_prompt.shape
    dbsz, dseq, _ = x_sample.shape

    w_proj = _split_weights(w_in[0], b_fox_f[0], b_ml_i[0], b_ml_f[0])
    wpa, wpb, wo = w_proj_a[0].astype(BF16), w_proj_b[0].astype(BF16), w_out[0].astype(BF16)
    wu, wd = w_up[0].astype(BF16), w_down[0].astype(BF16)
    g1, g2, g3, g4 = g_pre_mix, g_post_mix, g_pre_mlp, g_post_mlp
    cw, cb = conv_w[0], conv_b

    n_c = bsz + dbsz
    c_all = jnp.concatenate([c_prompt, c_sample, jnp.zeros((-n_c % SUBLANES, d), F32)], axis=0)
    mod = _adaln(c_all, w_ada[0], b_ada)

    def token_path(x2, mods, tm, tiles_per_group, token_minor_q):
        return _in_proj(x2, mods, g1, w_proj, tm, tiles_per_group, token_minor_q), mods

    def head_split(kt):
        g_, _, n_ = kt.shape
        return jnp.transpose(kt.reshape(g_, A_HEADS, A_HEAD_DIM, n_), (0, 3, 1, 2))[None]

    def finish(x2, ya, hb, og, mods, tm, group_tokens):
        x1, h2 = _mix_out(ya, hb, og, x2, mods, g2, g3, wpa, wpb, wo, tm, group_tokens // tm)
        return _mlp(h2, x1, mods, g4, wu, wd, tm, group_tokens // tm)

    tm = min(TOKEN_TILE, seq)
    m_p = bsz * seq
    x2 = x_prompt.reshape(m_p, d)
    mods = mod[:bsz].reshape(bsz, 1, N_MOD * d)
    (qt, kt, vt, ka, qkb, vb, og, sm, smt), rest = token_path(x2, mods, tm, seq // tm, True)

    r3 = lambda a: a.reshape(bsz, seq, a.shape[-1])
    ya = _attention(qt, r3(ka), vt, _cumsum_lanes(smt)).reshape(m_p, A_WIDTH)

    qkc = _conv(r3(qkb), jnp.zeros((bsz, SUBLANES, 2 * B_WIDTH), F32), cw, cb, tc=min(CONV_TILE, seq))
    hb, c_p, n_p, m_pr = _mlstm(qkc, r3(vb), r3(sm), smt,
                                jnp.zeros((bsz, B_HEADS, B_HEAD_DIM, B_HEAD_DIM), F32),
                                jnp.zeros((bsz, B_HEADS, 1, B_HEAD_DIM), F32),
                                jnp.zeros((bsz, B_HEADS, 1, 1), F32), t=min(MLSTM_CHUNK, seq), nb=min(MLSTM_SEQS, bsz), g0=A_HEADS)
    y_prompt = finish(x2, ya, hb.reshape(m_p, B_WIDTH), og, rest, tm, seq).reshape(bsz, seq, d)

    new_k_prompt = head_split(kt)
    new_v_prompt = head_split(vt)
    new_logf_prompt = jnp.transpose(smt[:, :A_HEADS, :], (0, 2, 1))[None]
    new_conv_prompt = r3(qkb)[:, seq - (CONV_W - 1):, :][None]

    m_s = dbsz * dseq
    xs2 = x_sample.reshape(m_s, d)
    mods = mod[bsz:n_c].reshape(1, dbsz, N_MOD * d)
    (qa, kt, vt, ka, va, qkb, vb, og, sm, smt), rest = token_path(xs2, mods, m_s, 1, False)

    n_new = SAMPLE_PAD
    new_rows = lambda a: jnp.pad(a.reshape(dbsz, dseq, -1), ((0, 0), (0, n_new - dseq), (0, 0)))
    lf_s = sm[:, :A_HEADS].reshape(dbsz, dseq, A_HEADS)
    lf_new = jnp.pad(jnp.transpose(lf_s, (0, 2, 1)), ((0, 0), (0, 0), (0, LANES - dseq)))
    q_heads = qa.reshape(dbsz, dseq, A_HEADS, 1, A_HEAD_DIM)
    q_bd = (q_heads * jnp.eye(A_HEADS, dtype=BF16)[None, None, :, :, None]).reshape(dbsz, dseq * A_HEADS, A_WIDTH)
    ya = _sample_attention(
        page_table, q_bd, new_rows(ka), new_rows(va), lf_new,
        jnp.transpose(cache_k[0], (0, 2, 3, 1)), jnp.transpose(cache_v[0], (0, 2, 3, 1)),
        jnp.transpose(cache_logf[0], (0, 2, 1)))
    ya = ya.astype(BF16).reshape(m_s, A_WIDTH)

    t_s = SAMPLE_PAD
    conv_in = jnp.pad(qkb.reshape(dbsz, dseq, 2 * B_WIDTH), ((0, 0), (0, SUBLANES - dseq), (0, 0)))
    conv_init = jnp.pad(state_conv[0], ((0, 0), (SUBLANES - (CONV_W - 1), 0), (0, 0)))
    qkc = _conv(conv_in, conv_init, cw, cb, tc=SUBLANES, nb=min(SAMPLE_SEQS, dbsz))
    tok_valid = (jnp.arange(t_s) < dseq)[None, :, None]
    pad_t = lambda a: jnp.pad(a, ((0, 0), (0, t_s - a.shape[1]), (0, 0)))
    qkc = jnp.where(tok_valid, pad_t(qkc), jnp.zeros((), BF16))
    vb_s = pad_t(vb.reshape(dbsz, dseq, B_WIDTH))
    gates = pad_t(sm[:, A_HEADS:A_HEADS + 2 * B_HEADS].reshape(dbsz, dseq, 2 * B_HEADS))
    neutral = jnp.concatenate([jnp.full((B_HEADS,), NEG_INF, F32), jnp.zeros((B_HEADS,), F32)])
    gcol = jnp.where(tok_valid, gates, neutral)
    grow = jnp.transpose(gcol, (0, 2, 1))
    hb, c_s, n_s, m_sm = _mlstm(qkc, vb_s, gcol, grow, state_C[0], state_n[0][:, :, None, :],
                                state_m[0][:, :, None, None], t=t_s, nb=min(SAMPLE_SEQS, dbsz))
    hb = hb[:, :dseq, :].reshape(m_s, B_WIDTH)
    y_sample = finish(xs2, ya, hb, og, rest, m_s, m_s).reshape(dbsz, dseq, d)

    new_k_sample = head_split(kt).reshape(1, dbsz, dseq, A_HEADS, A_HEAD_DIM)
    new_v_sample = head_split(vt).reshape(1, dbsz, dseq, A_HEADS, A_HEAD_DIM)
    new_logf_sample = lf_s[None]
    qkb3 = qkb.reshape(dbsz, dseq, 2 * B_WIDTH)
    new_conv_sample = jnp.concatenate([state_conv[0], qkb3], axis=1)[:, dseq:, :][None]

    return (y_prompt, y_sample, new_k_prompt, new_v_prompt, new_logf_prompt,
            c_p[None], n_p[:, :, 0, :][None], m_pr[:, :, 0, 0][None], new_conv_prompt,
            new_k_sample, new_v_sample, new_logf_sample,
            c_s[None], n_s[:, :, 0, :][None], m_sm[:, :, 0, 0][None], new_conv_sample)
```

```python
import functools

import jax
import jax.numpy as jnp
from jax import lax
from jax.experimental import pallas as pl
from jax.experimental.pallas import tpu as pltpu

F32 = jnp.float32
BF16 = jnp.bfloat16

A_HEADS = 8
A_HEAD_DIM = 64
A_WIDTH = A_HEADS * A_HEAD_DIM
B_HEADS = 4
B_HEAD_DIM = 128
B_WIDTH = B_HEADS * B_HEAD_DIM
CONV_W = 4
N_MOD = 6
RMS_EPS = 1e-6
NEG_INF = -1e30
PAGE_SIZE = 128
LOG2E = 1.4426950408889634

LANES = 128
SUBLANES = 8
VMEM_LIMIT_BYTES = 56 * 1024 * 1024

TOKEN_TILE = 512
ADALN_COLS = 1024
MLP_CHUNK = 1024
ATTN_Q_BLOCK, ATTN_K_BLOCK = 1024, 512
ATTN_BUILD_CHUNK = 512
CONV_TILE = 512
MLSTM_CHUNK, MLSTM_SEQS = 128, 4
SAMPLE_PAD = 16
SAMPLE_SEQS = 16
PAGED_SEQS = 2

_NT = (((1,), (1,)), ((), ()))
_TN = (((0,), (0,)), ((), ()))


def _params(*sem):
    return pltpu.CompilerParams(dimension_semantics=sem, vmem_limit_bytes=VMEM_LIMIT_BYTES)


def _resident(shape):
    nd = len(shape)
    return pl.BlockSpec(shape, lambda *_: (0,) * nd, pipeline_mode=pl.Buffered(1))


def _rms(x, g):
    r = lax.rsqrt(jnp.mean(x * x, axis=-1, keepdims=True) + RMS_EPS)
    return (x * r) * g


def _log_sigmoid(x):
    return jnp.minimum(x, 0.0) - jnp.log1p(jnp.exp(-jnp.abs(x)))


def _mod_kernel(c_ref, w_ref, b_ref, o_ref):
    c = c_ref[...]
    a = (c * jax.nn.sigmoid(c)).astype(BF16)
    o_ref[...] = jnp.dot(a, w_ref[...].astype(BF16), preferred_element_type=F32) + b_ref[...]


def _adaln(c, w, b, tn=ADALN_COLS):
    r, d = c.shape
    n = w.shape[1]
    return pl.pallas_call(
        _mod_kernel,
        out_shape=jax.ShapeDtypeStruct((r, n), F32),
        grid=(n // tn,),
        in_specs=[pl.BlockSpec((r, d), lambda j: (0, 0)),
                  pl.BlockSpec((d, tn), lambda j: (0, j)),
                  pl.BlockSpec((1, tn), lambda j: (0, j))],
        out_specs=pl.BlockSpec((r, tn), lambda j: (0, j)),
        compiler_params=_params("arbitrary"),
        name="adaln",
    )(c, w, b)


SHIFT_MIX, SCALE_MIX, GATE_MIX, SHIFT_MLP, SCALE_MLP, GATE_MLP = range(N_MOD)


def _mod_spec(arr, which, tm, tiles_per_group, m):
    g, r, width = arr.shape
    d = width // N_MOD
    if r == 1:
        return pl.BlockSpec((1, 1, d), lambda i: (i // tiles_per_group, 0, which))
    assert g == 1
    return pl.BlockSpec((1, tm * r // m, d), lambda i: (0, i, which))


def _mod_rows(ref, part, tm):
    if ref.shape[1] == 1:
        return ref[0]
    rep = tm // ref.shape[1]
    return jnp.repeat(ref[0, part.start // rep:part.stop // rep, :], rep, axis=0)


TOKEN_PARTS = 2
N_GATES = A_HEADS + 2 * B_HEADS
GATE_ROWS = 16


def _gate_act(z, idx):
    return jnp.where((idx >= A_HEADS) & (idx < A_HEADS + B_HEADS), z, _log_sigmoid(z))


def _in_kernel(x_ref, sc_ref, sh_ref, g_ref, wq_ref, wk_ref, wv_ref, wqk_ref, wbv_ref, wg_ref,
               ws_ref, bs_ref, wst_ref, bst_ref, q_ref, kt_ref, vt_ref, ka_ref, *rest, token_minor_q):
    if not token_minor_q:
        va_ref, *rest = rest
    qkb_ref, vb_ref, og_ref, sm_ref, smt_ref = rest
    tm = x_ref.shape[0]
    parts = [slice(r, r + tm // TOKEN_PARTS) for r in range(0, tm, tm // TOKEN_PARTS)]
    mod = lambda ref, p: _mod_rows(ref, p, tm)
    hb = [(_rms(x_ref[p, :], g_ref[...]) * (1.0 + mod(sc_ref, p)) + mod(sh_ref, p)).astype(BF16) for p in parts]
    proj = lambda w_ref: [jnp.dot(h, w_ref[...], preferred_element_type=F32) for h in hb]
    for p, q in zip(parts, proj(wq_ref)):
        if token_minor_q:
            q_ref[0, :, p] = q.T.astype(BF16)
        else:
            q_ref[p, :] = q.astype(BF16)
    for p, k in zip(parts, proj(wk_ref)):
        kt_ref[0, :, p] = k.T
        ka_ref[p, :] = k.astype(BF16)
    for p, v in zip(parts, proj(wv_ref)):
        vt_ref[0, :, p] = v.T
        if not token_minor_q:
            va_ref[p, :] = v.astype(BF16)
    for p, z in zip(parts, proj(wqk_ref)):
        qkb_ref[p, :] = z
    for p, z in zip(parts, proj(wbv_ref)):
        vb_ref[p, :] = z.astype(BF16)
    for p, z in zip(parts, proj(wg_ref)):
        og_ref[p, :] = jax.nn.sigmoid(z).astype(BF16)
    for p, z in zip(parts, proj(ws_ref)):
        z = z + bs_ref[...]
        sm_ref[p, :] = _gate_act(z, lax.broadcasted_iota(jnp.int32, z.shape, 1))
    for p, h in zip(parts, hb):
        z = lax.dot_general(wst_ref[...], h, _NT, preferred_element_type=F32) + bst_ref[...]
        smt_ref[0, :, p] = _gate_act(z, lax.broadcasted_iota(jnp.int32, z.shape, 0))


def _in_proj(x, mod, g, w, tm, tiles_per_group, token_minor_q):
    m, d = x.shape
    group_tokens = tm * tiles_per_group
    groups = m // group_tokens
    row = lambda n: pl.BlockSpec((tm, n), lambda i: (i, 0))
    col = lambda n: pl.BlockSpec((1, n, tm), lambda i: (i // tiles_per_group, 0, i % tiles_per_group))
    rows_out = lambda n, dt: (jax.ShapeDtypeStruct((m, n), dt), row(n))
    cols_out = lambda n, dt=F32: (jax.ShapeDtypeStruct((groups, n, group_tokens), dt), col(n))
    outs = [cols_out(A_WIDTH, BF16) if token_minor_q else rows_out(A_WIDTH, BF16),
            cols_out(A_WIDTH), cols_out(A_WIDTH), rows_out(A_WIDTH, BF16)]
    if not token_minor_q:
        outs.append(rows_out(A_WIDTH, BF16))
    outs += [rows_out(2 * B_WIDTH, F32), rows_out(B_WIDTH, BF16), rows_out(w["g"].shape[1], BF16),
             rows_out(LANES, F32), cols_out(GATE_ROWS)]
    weights = [w["q"], w["k"], w["v"], w["qk"], w["bv"], w["g"], w["s"], w["bs"], w["st"], w["bst"]]
    return pl.pallas_call(
        functools.partial(_in_kernel, token_minor_q=token_minor_q),
        out_shape=[o[0] for o in outs],
        grid=(m // tm,),
        in_specs=[row(d), _mod_spec(mod, SCALE_MIX, tm, tiles_per_group, m),
                  _mod_spec(mod, SHIFT_MIX, tm, tiles_per_group, m),
                  _resident(g.shape)] + [_resident(a.shape) for a in weights],
        out_specs=[o[1] for o in outs],
        compiler_params=_params("arbitrary"),
        name="in_proj",
    )(x, mod, mod, g, *weights)


MIX_CHUNK = 256


def _mix_kernel(ya_ref, hb_ref, og_ref, x_ref, gt_ref, sc_ref, sh_ref, g1_ref, g2_ref,
                wpa_ref, wpb_ref, wo_ref, x1_ref, h2_ref):
    tm, d = x_ref.shape
    parts = [slice(r, r + tm // TOKEN_PARTS) for r in range(0, tm, tm // TOKEN_PARTS)]
    mod = lambda ref, p: _mod_rows(ref, p, tm)
    merged = []
    for p in parts:
        yb_in = (og_ref[p, :B_WIDTH].astype(F32) * hb_ref[p, :].astype(F32)).astype(BF16)
        cols = []
        for c in range(0, d, MIX_CHUNK):
            a = jnp.dot(ya_ref[p, :], wpa_ref[:, c:c + MIX_CHUNK], preferred_element_type=F32)
            b = jnp.dot(yb_in, wpb_ref[:, c:c + MIX_CHUNK], preferred_element_type=F32)
            ga = og_ref[p, B_WIDTH + c:B_WIDTH + c + MIX_CHUNK].astype(F32)
            gb = og_ref[p, B_WIDTH + d + c:B_WIDTH + d + c + MIX_CHUNK].astype(F32)
            cols.append((ga * a + gb * b).astype(BF16))
        merged.append(jnp.concatenate(cols, axis=1))
    yo = [jnp.dot(y, wo_ref[...], preferred_element_type=F32) for y in merged]
    for p, o in zip(parts, yo):
        x1 = x_ref[p, :] + mod(gt_ref, p) * _rms(o, g1_ref[...])
        x1_ref[p, :] = x1
        h2_ref[p, :] = (_rms(x1, g2_ref[...]) * (1.0 + mod(sc_ref, p)) + mod(sh_ref, p)).astype(BF16)


def _mix_out(ya, hb, og, x, mod, g1, g2, wpa, wpb, wo, tm, tiles_per_group):
    m, d = x.shape
    row = lambda n: pl.BlockSpec((tm, n), lambda i: (i, 0))
    ms = lambda which: _mod_spec(mod, which, tm, tiles_per_group, m)
    return pl.pallas_call(
        _mix_kernel,
        out_shape=[jax.ShapeDtypeStruct((m, d), F32), jax.ShapeDtypeStruct((m, d), BF16)],
        grid=(m // tm,),
        in_specs=[row(ya.shape[1]), row(hb.shape[1]), row(og.shape[1]), row(d),
                  ms(GATE_MIX), ms(SCALE_MLP), ms(SHIFT_MLP),
                  _resident(g1.shape), _resident(g2.shape), _resident(wpa.shape), _resident(wpb.shape),
                  _resident(wo.shape)],
        out_specs=[row(d), row(d)],
        compiler_params=_params("arbitrary"),
        name="mix_out",
    )(ya, hb, og, x, mod, mod, mod, g1, g2, wpa, wpb, wo)


def _mlp_kernel(h2_ref, x1_ref, gt_ref, g_ref, wu_ref, wd_ref, y_ref, *, fc):
    tm = x1_ref.shape[0]
    parts = [slice(r, r + tm // TOKEN_PARTS) for r in range(0, tm, tm // TOKEN_PARTS)]
    mod = lambda ref, p: _mod_rows(ref, p, tm)
    acc = [jnp.zeros((tm // TOKEN_PARTS, x1_ref.shape[1]), F32) for _ in parts]
    for c in range(wu_ref.shape[1] // fc):
        u = [jnp.dot(h2_ref[p, :], wu_ref[:, c * fc:(c + 1) * fc], preferred_element_type=F32) for p in parts]
        u = [jnp.square(jnp.maximum(x, 0.0)).astype(BF16) for x in u]
        acc = [a + jnp.dot(x, wd_ref[c * fc:(c + 1) * fc, :], preferred_element_type=F32) for a, x in zip(acc, u)]
    for p, a in zip(parts, acc):
        y_ref[p, :] = x1_ref[p, :] + mod(gt_ref, p) * _rms(a, g_ref[...])


def _mlp(h2, x1, mod, g, wu, wd, tm, tiles_per_group, fc=MLP_CHUNK):
    m, d = x1.shape
    row = lambda n: pl.BlockSpec((tm, n), lambda i: (i, 0))
    return pl.pallas_call(
        functools.partial(_mlp_kernel, fc=fc),
        out_shape=jax.ShapeDtypeStruct((m, d), F32),
        grid=(m // tm,),
        in_specs=[row(d), row(d), _mod_spec(mod, GATE_MLP, tm, tiles_per_group, m), _resident(g.shape),
                  _resident(wu.shape), _resident(wd.shape)],
        out_specs=row(d),
        compiler_params=_params("arbitrary"),
        name="mlp",
    )(h2, x1, mod, g, wu, wd)


def _lane_scan(x, lane, shifts):
    for s in shifts:
        x = x + jnp.where(lane >= s, pltpu.roll(x, s, axis=1), 0.0)
    return x


def _cumsum_kernel(x_ref, o_ref):
    rows, length = x_ref.shape[1], x_ref.shape[2]
    lane = lax.broadcasted_iota(jnp.int32, (rows, LANES), 1)
    carry = jnp.zeros((rows, 1), F32)
    for c in range(length // LANES):
        x = _lane_scan(x_ref[0, :, c * LANES:(c + 1) * LANES], lane, (1, 2, 4, 8, 16, 32, 64)) + carry
        o_ref[0, :, c * LANES:(c + 1) * LANES] = x
        carry = x[:, LANES - 1:LANES]


def _cumsum_lanes(x):
    b, r, length = x.shape
    spec = pl.BlockSpec((1, r, length), lambda i: (i, 0, 0))
    return pl.pallas_call(
        _cumsum_kernel,
        out_shape=jax.ShapeDtypeStruct(x.shape, F32),
        grid=(b,),
        in_specs=[spec],
        out_specs=spec,
        compiler_params=_params("arbitrary"),
        name="logf_cumsum",
    )(x)


V_ROWS = 80


def _aug_lane0(hh):
    return A_HEAD_DIM if hh == 0 else 0


def _attn_kernel(qt_ref, k_ref, vt_ref, f_ref, o_ref, kaug_ref, vaug_ref, s_ref, *, blk_q, blk_k, chunk):
    pair = pl.program_id(1)
    qi = pl.program_id(2)
    length = k_ref.shape[1]
    lane = lax.broadcasted_iota(jnp.int32, (1, LANES), 1)
    own = (lane < A_HEAD_DIM, lane >= A_HEAD_DIM)
    feat = lax.broadcasted_iota(jnp.int32, (LANES, 1), 0)

    @pl.when(qi == 0)
    def _():
        ones_row = jnp.where(lax.broadcasted_iota(jnp.int32, (V_ROWS - A_HEAD_DIM, chunk), 0) == 0, 1.0, 0.0)

        def build(c, _):
            toks = pl.ds(pl.multiple_of(c * chunk, chunk), chunk)
            k = k_ref[0, toks, :]
            vt = vt_ref[0, :, toks]
            for hh in range(2):
                a0 = _aug_lane0(hh)
                neg_f = f_ref[0, pl.ds(2 * pair + hh, 1), toks] * (-LOG2E)
                hi = neg_f.astype(BF16).astype(F32)
                mid = (neg_f - hi).astype(BF16).astype(F32)
                lo = (neg_f - hi - mid).astype(BF16).astype(F32)
                bias = jnp.concatenate(
                    [jnp.where(feat == a0, hi[:, t], jnp.where(feat == a0 + 1, mid[:, t],
                                                               jnp.where(feat == a0 + 2, lo[:, t], 0.0))).T
                     for t in (slice(t0, t0 + LANES) for t0 in range(0, chunk, LANES))], axis=0)
                kaug_ref[hh, toks, :] = jnp.where(own[hh], k, bias.astype(BF16))
                dims = vt[hh * A_HEAD_DIM:(hh + 1) * A_HEAD_DIM]
                vaug_ref[hh, :, toks] = jnp.concatenate([dims, ones_row], axis=0).astype(BF16)
            return 0
        lax.fori_loop(0, length // chunk, build, 0)

    qt = qt_ref[0]
    qh = []
    for hh in range(2):
        a0 = _aug_lane0(hh)
        ones = jnp.where((feat >= a0) & (feat < a0 + 3), 1.0, 0.0).astype(BF16)
        qh.append(jnp.where((feat < A_HEAD_DIM) if hh == 0 else (feat >= A_HEAD_DIM), qt, ones))

    def keys(j):
        return pl.ds(pl.multiple_of(j * blk_k, blk_k), blk_k)

    def scores(j, slot, c0=0):
        for hh in range(2):
            s_ref[slot, hh, :, c0:] = jnp.dot(kaug_ref[hh, keys(j), :], qh[hh][:, c0:], preferred_element_type=F32)

    def fold(j, slot, carry, causal, c0=0):
        probs, stats = [], []
        for hh in range(2):
            m = carry[hh][0]
            sh = s_ref[slot, hh, :, c0:]
            if causal:
                key = lax.broadcasted_iota(jnp.int32, sh.shape, 0)
                qry = lax.broadcasted_iota(jnp.int32, sh.shape, 1)
                sh = jnp.where(key <= qry, sh, NEG_INF)
            m_new = jnp.maximum(m, jnp.max(sh, axis=0, keepdims=True))
            probs.append(jnp.exp2(sh - m_new).astype(BF16))
            stats.append((m_new, jnp.exp2(m - m_new)))
        return tuple(
            (stats[hh][0], stats[hh][1] * carry[hh][1]
             + jnp.dot(vaug_ref[hh, :, keys(j)], probs[hh], preferred_element_type=F32))
            for hh in range(2))

    assert blk_q == 2 * blk_k
    init = tuple((jnp.full((1, blk_q), NEG_INF, F32), jnp.zeros((V_ROWS, blk_q), F32)) for _ in range(2))

    def body(p, carry):
        scores(2 * p + 1, 1)
        carry = fold(2 * p, 0, carry, False)
        scores(2 * p + 2, 0)
        return fold(2 * p + 1, 1, carry, False)

    scores(0, 0)
    carry = lax.fori_loop(0, qi, body, init)
    scores(2 * qi + 1, 1, blk_k)
    carry = fold(2 * qi, 0, carry, True)
    upper = fold(2 * qi + 1, 1, tuple((m[:, blk_k:], acc[:, blk_k:]) for m, acc in carry), True, blk_k)
    out = []
    for hh in range(2):
        acc = jnp.concatenate([carry[hh][1][:, :blk_k], upper[hh][1]], axis=1)
        out.append(acc[:A_HEAD_DIM] / acc[A_HEAD_DIM:A_HEAD_DIM + 1])
    o_ref[0] = jnp.concatenate(out, axis=0).T.astype(BF16)


def _attention(qt, k, vt, f, blk_q=ATTN_Q_BLOCK, blk_k=ATTN_K_BLOCK):
    b, length, width = k.shape
    pairs = width // LANES
    blk_q, blk_k = min(blk_q, length), min(blk_k, length)
    return pl.pallas_call(
        functools.partial(_attn_kernel, blk_q=blk_q, blk_k=blk_k, chunk=min(ATTN_BUILD_CHUNK, length)),
        out_shape=jax.ShapeDtypeStruct(k.shape, BF16),
        grid=(b, pairs, length // blk_q),
        in_specs=[pl.BlockSpec((1, LANES, blk_q), lambda i, p, t: (i, p, t)),
                  pl.BlockSpec((1, length, LANES), lambda i, p, t: (i, 0, p)),
                  pl.BlockSpec((1, LANES, length), lambda i, p, t: (i, p, 0)),
                  pl.BlockSpec((1,) + f.shape[1:], lambda i, p, t: (i, 0, 0))],
        out_specs=pl.BlockSpec((1, blk_q, LANES), lambda i, p, t: (i, t, p)),
        scratch_shapes=[pltpu.VMEM((2, length, LANES), BF16), pltpu.VMEM((2, V_ROWS, length), BF16),
                        pltpu.VMEM((2, 2, blk_k, blk_q), F32)],
        compiler_params=_params("arbitrary", "arbitrary", "arbitrary"),
        name="fox_attention",
    )(qt, k, vt, f)


def _conv_silu(xp_ref, w_ref, b_ref, tc):
    y = b_ref[...]
    for j in range(CONV_W):
        off = SUBLANES - (CONV_W - 1) + j
        y = y + w_ref[j:j + 1, :] * xp_ref[off:off + tc, :]
    lane = lax.broadcasted_iota(jnp.int32, (1, y.shape[1]), 1)
    return y * jax.nn.sigmoid(y) * jnp.where(lane < B_WIDTH, 1.0, B_HEAD_DIM ** -0.5)


def _conv_kernel(x_ref, prev_ref, init_ref, w_ref, b_ref, o_ref, xp_ref, *, tc):
    for s in range(x_ref.shape[0]):
        xp_ref[s, 0:SUBLANES, :] = jnp.where(pl.program_id(1) == 0, init_ref[s], prev_ref[s])
        xp_ref[s, SUBLANES:SUBLANES + tc, :] = x_ref[s]
        o_ref[s] = _conv_silu(xp_ref.at[s], w_ref, b_ref, tc).astype(BF16)


def _conv(x, init, w, b, tc, nb=1):
    bsz, length, c = x.shape
    tpb = tc // SUBLANES
    return pl.pallas_call(
        functools.partial(_conv_kernel, tc=tc),
        out_shape=jax.ShapeDtypeStruct(x.shape, BF16),
        grid=(bsz // nb, length // tc),
        in_specs=[pl.BlockSpec((nb, tc, c), lambda i, t: (i, t, 0)),
                  pl.BlockSpec((nb, SUBLANES, c), lambda i, t: (i, jnp.maximum(t * tpb - 1, 0), 0)),
                  pl.BlockSpec((nb, SUBLANES, c), lambda i, t: (i, 0, 0)),
                  pl.BlockSpec((CONV_W, c), lambda i, t: (0, 0)),
                  pl.BlockSpec((1, c), lambda i, t: (0, 0))],
        out_specs=pl.BlockSpec((nb, tc, c), lambda i, t: (i, t, 0)),
        scratch_shapes=[pltpu.VMEM((nb, tc + SUBLANES, c), F32)],
        compiler_params=_params("arbitrary", "arbitrary"),
        name="short_conv",
    )(x, x, init, w, b)


_BNT = (((2,), (2,)), ((0,), (0,)))
_BNN = (((2,), (1,)), ((0,), (0,)))
_BTN = (((1,), (1,)), ((0,), (0,)))


def _mlstm_heads(q, k, v, ir, fr, ic, fc, c_st, n_st, m_st):
    n, t, d = q.shape
    row = lax.broadcasted_iota(jnp.int32, (n, t, t), 1)
    col = lax.broadcasted_iota(jnp.int32, (n, t, t), 2)
    tril = col <= row
    spread = lambda x, w: jnp.broadcast_to(x, (n, t, w))
    both = lambda x: (spread(x, t),) * 2 if t == d else (spread(x, t), spread(x, d))
    bcum_c = jnp.sum(jnp.where(tril, fr, 0.0), axis=2, keepdims=True)
    bcum_r = jnp.sum(jnp.where(row <= col, fc, 0.0), axis=1, keepdims=True)
    bcum_t, bcum_d = both(bcum_c)
    dlog = jnp.where(tril, bcum_t - bcum_r + ir, NEG_INF)
    mt_t, mt_d = both(jnp.maximum(bcum_c + m_st, jnp.max(dlog, axis=2, keepdims=True)))
    a = jnp.exp(dlog - mt_t) * lax.dot_general(q, k, _BNT, preferred_element_type=F32)
    si = jnp.exp(bcum_d + m_st - mt_d)
    num = (lax.dot_general(a.astype(BF16), v, _BNN, preferred_element_type=F32)
           + si * lax.dot_general(q, c_st.astype(BF16), _BNN, preferred_element_type=F32))
    den = jnp.sum(a, axis=2, keepdims=True) + si * jnp.sum(q.astype(F32) * n_st, axis=2, keepdims=True)
    h = num / jnp.maximum(jnp.abs(den), jnp.exp(-mt_d))
    bl = bcum_r[:, :, t - 1:t]
    m_new = jnp.maximum(bl + m_st, jnp.max(bl - bcum_r + ir, axis=2, keepdims=True))
    kw = jnp.exp(bl - bcum_d + spread(ic, d) - m_new) * k.astype(F32)
    decay = jnp.exp(bl + m_st - m_new)
    c_new = decay * c_st + lax.dot_general(kw.astype(BF16), v, _BTN, preferred_element_type=F32)
    n_new = decay * n_st + jnp.sum(kw, axis=1, keepdims=True)
    return h, c_new, n_new, m_new


def _mlstm_kernel(q_ref, k_ref, v_ref, gc_ref, gr_ref, c0_ref, n0_ref, m0_ref,
                  h_ref, c_ref, n_ref, m_ref, *, g0):
    @pl.when(pl.program_id(1) == 0)
    def _():
        c_ref[...] = c0_ref[...]
        n_ref[...] = n0_ref[...]
        m_ref[...] = m0_ref[...]

    nb = q_ref.shape[0]
    where = [(s, hd, slice(hd * B_HEAD_DIM, (hd + 1) * B_HEAD_DIM)) for s in range(nb) for hd in range(B_HEADS)]
    heads = lambda ref: jnp.stack([ref[s, :, sl] for s, _, sl in where])
    gc = [gc_ref[s] for s in range(nb)]
    gr = [gr_ref[s] for s in range(nb)]
    gi, gf = g0, g0 + B_HEADS
    h, c_new, n_new, m_new = _mlstm_heads(
        heads(q_ref), heads(k_ref), heads(v_ref),
        jnp.stack([gr[s][gi + hd:gi + hd + 1, :] for s, hd, _ in where]),
        jnp.stack([gr[s][gf + hd:gf + hd + 1, :] for s, hd, _ in where]),
        jnp.stack([gc[s][:, gi + hd:gi + hd + 1] for s, hd, _ in where]),
        jnp.stack([gc[s][:, gf + hd:gf + hd + 1] for s, hd, _ in where]),
        c_ref[...].reshape((nb * B_HEADS,) + c_ref.shape[2:]),
        n_ref[...].reshape((nb * B_HEADS,) + n_ref.shape[2:]),
        m_ref[...].reshape((nb * B_HEADS,) + m_ref.shape[2:]))
    for i, (s, _, sl) in enumerate(where):
        h_ref[s, :, sl] = h[i].astype(BF16)
    c_ref[...] = c_new.reshape(c_ref.shape)
    n_ref[...] = n_new.reshape(n_ref.shape)
    m_ref[...] = m_new.reshape(m_ref.shape)


def _mlstm(qk, v, gcol, grow, c0, n0, m0, t, nb=1, g0=0):
    b, length, _ = v.shape
    st = lambda a: pl.BlockSpec((nb,) + a.shape[1:], lambda i, j: (i, 0, 0, 0))
    return pl.pallas_call(
        functools.partial(_mlstm_kernel, g0=g0),
        out_shape=[jax.ShapeDtypeStruct(v.shape, BF16), jax.ShapeDtypeStruct(c0.shape, F32),
                   jax.ShapeDtypeStruct(n0.shape, F32), jax.ShapeDtypeStruct(m0.shape, F32)],
        grid=(b // nb, length // t),
        in_specs=[pl.BlockSpec((nb, t, B_WIDTH), lambda i, j: (i, j, 0)),
                  pl.BlockSpec((nb, t, B_WIDTH), lambda i, j: (i, j, 1)),
                  pl.BlockSpec((nb, t, B_WIDTH), lambda i, j: (i, j, 0)),
                  pl.BlockSpec((nb, t, gcol.shape[2]), lambda i, j: (i, j, 0)),
                  pl.BlockSpec((nb, grow.shape[1], t), lambda i, j: (i, 0, j)),
                  st(c0), st(n0), st(m0)],
        out_specs=[pl.BlockSpec((nb, t, B_WIDTH), lambda i, j: (i, j, 0)), st(c0), st(n0), st(m0)],
        compiler_params=_params("arbitrary", "arbitrary"),
        name="mlstm",
    )(qk, qk, v, gcol, grow, c0, n0, m0)


def _page_copies(pt_ref, step, slot, ck_ref, cv_ref, kbuf_ref, vbuf_ref, sem_ref, nb, n_pages):
    copies = []
    for s in range(nb):
        for j in range(n_pages):
            page = pt_ref[step * nb + s, j]
            at = s * n_pages + j
            copies.append(pltpu.make_async_copy(ck_ref.at[page], kbuf_ref.at[slot, at], sem_ref.at[slot, 0]))
            copies.append(pltpu.make_async_copy(cv_ref.at[page], vbuf_ref.at[slot, at], sem_ref.at[slot, 1]))
    return copies


def _sattn_kernel(pt_ref, q_ref, kn_ref, vn_ref, lfn_ref, lf_ref, ck_ref, cv_ref, o_ref,
                  kbuf_ref, vbuf_ref, sem_ref, *, n_pages):
    nb, nrow, width = q_ref.shape
    step, n_steps = pl.program_id(0), pl.num_programs(0)
    slot = step % 2
    fetch = functools.partial(_page_copies, pt_ref, ck_ref=ck_ref, cv_ref=cv_ref, kbuf_ref=kbuf_ref,
                              vbuf_ref=vbuf_ref, sem_ref=sem_ref, nb=nb, n_pages=n_pages)

    @pl.when(step == 0)
    def _():
        for c in fetch(0, 0):
            c.start()

    @pl.when(step + 1 < n_steps)
    def _():
        for c in fetch(step + 1, 1 - slot):
            c.start()

    for c in fetch(step, slot):
        c.wait()
    k_refs = [[kbuf_ref.at[slot, s * n_pages + j] for j in range(n_pages)] for s in range(nb)]
    v_refs = [[vbuf_ref.at[slot, s * n_pages + j] for j in range(n_pages)] for s in range(nb)]
    first = step * nb
    nq = nrow // A_HEADS
    n_new = kn_ref.shape[1]
    lane = lax.broadcasted_iota(jnp.int32, (A_HEADS, LANES), 1)
    per_row = lambda f: jnp.concatenate([f] * nq, axis=0)
    scan = lambda x: _lane_scan(x, lane, (1, 2, 4, 8, 16, 32, 64))

    local = [[scan(lf_ref[pt_ref[first + s, j]]) for j in range(n_pages)] for s in range(nb)]
    new_local = [scan(lfn_ref[s]) for s in range(nb)]
    f_keys, f_new = [], []
    for s in range(nb):
        offset = jnp.zeros((A_HEADS, 1), F32)
        f_keys.append([])
        for j in range(n_pages):
            f = local[s][j] + offset
            offset = f[:, LANES - 1:LANES]
            f_keys[s].append(f)
        f_new.append((new_local[s] + offset)[:, :n_new])

    scores = [[jnp.dot(q_ref[s], k_refs[s][j][...].reshape(width, PAGE_SIZE).astype(BF16),
                       preferred_element_type=F32) - per_row(f_keys[s][j] * LOG2E)
               for j in range(n_pages)] for s in range(nb)]
    q_tok = lax.broadcasted_iota(jnp.int32, (nrow, n_new), 0) // A_HEADS
    k_tok = lax.broadcasted_iota(jnp.int32, (nrow, n_new), 1)
    s_new = [jnp.where(k_tok <= q_tok,
                       lax.dot_general(q_ref[s], kn_ref[s], _NT, preferred_element_type=F32)
                       - per_row(f_new[s] * LOG2E), NEG_INF) for s in range(nb)]

    m = [jnp.maximum(jnp.max(functools.reduce(jnp.maximum, scores[s]), axis=1, keepdims=True),
                     jnp.max(s_new[s], axis=1, keepdims=True)) for s in range(nb)]
    probs = [[jnp.exp2(x - m[s]) for x in scores[s]] for s in range(nb)]
    p_new = [jnp.exp2(s_new[s] - m[s]) for s in range(nb)]
    l = [jnp.sum(functools.reduce(jnp.add, probs[s]), axis=1, keepdims=True)
         + jnp.sum(p_new[s], axis=1, keepdims=True) for s in range(nb)]
    acc = [jnp.dot(p_new[s].astype(BF16), vn_ref[s], preferred_element_type=F32) for s in range(nb)]
    for s in range(nb):
        for j in range(n_pages):
            vt = v_refs[s][j][...].reshape(width, PAGE_SIZE).astype(BF16)
            acc[s] = acc[s] + lax.dot_general(probs[s][j].astype(BF16), vt, _NT, preferred_element_type=F32)
    row_head = lax.broadcasted_iota(jnp.int32, (nrow, width), 0) % A_HEADS
    col_head = lax.broadcasted_iota(jnp.int32, (nrow, width), 1) // A_HEAD_DIM
    for s in range(nb):
        out = jnp.where(row_head == col_head, acc[s] / l[s], 0.0)
        o_ref[s] = jnp.sum(out.reshape(nq, A_HEADS, width), axis=1)


def _sample_attention(page_table, q, kn, vn, lfn, cache_kt, cache_vt, cache_lf, nb=PAGED_SEQS):
    b, n_pages = page_table.shape
    nq = q.shape[1] // A_HEADS
    per_seq = lambda a: pl.BlockSpec((nb,) + a.shape[1:], lambda i, pt: (i, 0, 0))
    in_hbm = pl.BlockSpec(memory_space=pl.ANY)
    page_buffers = pltpu.VMEM((2, nb * n_pages) + cache_kt.shape[1:], F32)
    grid_spec = pltpu.PrefetchScalarGridSpec(
        num_scalar_prefetch=1,
        grid=(b // nb,),
        in_specs=[per_seq(q), per_seq(kn), per_seq(vn), per_seq(lfn), _resident(cache_lf.shape), in_hbm, in_hbm],
        out_specs=pl.BlockSpec((nb, nq, A_WIDTH), lambda i, pt: (i, 0, 0)),
        scratch_shapes=[page_buffers, page_buffers, pltpu.SemaphoreType.DMA((2, 2))],
    )
    return pl.pallas_call(
        functools.partial(_sattn_kernel, n_pages=n_pages),
        out_shape=jax.ShapeDtypeStruct((b, nq, A_WIDTH), F32),
        grid_spec=grid_spec,
        compiler_params=_params("arbitrary"),
        name="paged_fox_attention",
    )(page_table, q, kn, vn, lfn, cache_lf, cache_kt, cache_vt)


def _split_weights(w_in, b_fox_f, b_ml_i, b_ml_f):
    d = w_in.shape[0]
    o = 0
    parts = {}
    for name, n in (("aq", A_WIDTH), ("ak", A_WIDTH), ("av", A_WIDTH), ("af", A_HEADS), ("bq", B_WIDTH),
                    ("bk", B_WIDTH), ("bv", B_WIDTH), ("bi", B_HEADS), ("bf", B_HEADS), ("bo", B_WIDTH),
                    ("ga", d), ("gb", d)):
        parts[name] = w_in[:, o:o + n]
        o += n
    assert o == w_in.shape[1]
    small = jnp.concatenate([parts["af"], parts["bi"], parts["bf"]], axis=1)
    bias = jnp.concatenate([b_fox_f, b_ml_i, b_ml_f])
    bf = lambda a: a.astype(BF16)
    return {
        "q": bf(parts["aq"] * (A_HEAD_DIM ** -0.5 * LOG2E)),
        "k": bf(parts["ak"]),
        "v": bf(parts["av"]),
        "qk": bf(jnp.concatenate([parts["bq"], parts["bk"]], axis=1)),
        "bv": bf(parts["bv"]),
        "g": bf(jnp.concatenate([parts["bo"], parts["ga"], parts["gb"]], axis=1)),
        "s": bf(jnp.pad(small, ((0, 0), (0, LANES - N_GATES)))),
        "bs": jnp.pad(bias, (0, LANES - N_GATES)).reshape(1, LANES),
        "st": bf(jnp.pad(small.T, ((0, GATE_ROWS - N_GATES), (0, 0)))),
        "bst": jnp.pad(bias, (0, GATE_ROWS - N_GATES)).reshape(GATE_ROWS, 1),
    }


def kernel(x_prompt, x_sample, cache_k, cache_v, cache_logf, page_table, state_C, state_n, state_m, state_conv,
           c_prompt, c_sample, w_ada, b_ada, g_pre_mix, g_post_mix, g_pre_mlp, g_post_mlp, w_in, b_fox_f,
           b_ml_i, b_ml_f, conv_w, conv_b, w_proj_a, w_proj_b, w_out, w_up, w_down):
    assert w_in.shape[0] == 1, "one trunk layer"
    bsz, seq, d = x_prompt.shape
    dbsz, dseq, _ = x_sample.shape

    w_proj = _split_weights(w_in[0], b_fox_f[0], b_ml_i[0], b_ml_f[0])
    wpa, wpb, wo = w_proj_a[0].astype(BF16), w_proj_b[0].astype(BF16), w_out[0].astype(BF16)
    wu, wd = w_up[0].astype(BF16), w_down[0].astype(BF16)
    g1, g2, g3, g4 = g_pre_mix, g_post_mix, g_pre_mlp, g_post_mlp
    cw, cb = conv_w[0], conv_b

    n_c = bsz + dbsz
    c_all = jnp.concatenate([c_prompt, c_sample, jnp.zeros((-n_c % SUBLANES, d), F32)], axis=0)
    mod = _adaln(c_all, w_ada[0], b_ada)

    def token_path(x2, mods, tm, tiles_per_group, token_minor_q):
        return _in_proj(x2, mods, g1, w_proj, tm, tiles_per_group, token_minor_q), mods

    def head_split(kt):
        g_, _, n_ = kt.shape
        return jnp.transpose(kt.reshape(g_, A_HEADS, A_HEAD_DIM, n_), (0, 3, 1, 2))[None]

    def finish(x2, ya, hb, og, mods, tm, group_tokens):
        x1, h2 = _mix_out(ya, hb, og, x2, mods, g2, g3, wpa, wpb, wo, tm, group_tokens // tm)
        return _mlp(h2, x1, mods, g4, wu, wd, tm, group_tokens // tm)

    tm = min(TOKEN_TILE, seq)
    m_p = bsz * seq
    x2 = x_prompt.reshape(m_p, d)
    mods = mod[:bsz].reshape(bsz, 1, N_MOD * d)
    (qt, kt, vt, ka, qkb, vb, og, sm, smt), rest = token_path(x2, mods, tm, seq // tm, True)

    r3 = lambda a: a.reshape(bsz, seq, a.shape[-1])
    ya = _attention(qt, r3(ka), vt, _cumsum_lanes(smt)).reshape(m_p, A_WIDTH)

    qkc = _conv(r3(qkb), jnp.zeros((bsz, SUBLANES, 2 * B_WIDTH), F32), cw, cb, tc=min(CONV_TILE, seq))
    hb, c_p, n_p, m_pr = _mlstm(qkc, r3(vb), r3(sm), smt,
                                jnp.zeros((bsz, B_HEADS, B_HEAD_DIM, B_HEAD_DIM), F32),
                                jnp.zeros((bsz, B_HEADS, 1, B_HEAD_DIM), F32),
                                jnp.zeros((bsz, B_HEADS, 1, 1), F32), t=min(MLSTM_CHUNK, seq), nb=min(MLSTM_SEQS, bsz), g0=A_HEADS)
    y_prompt = finish(x2, ya, hb.reshape(m_p, B_WIDTH), og, rest, min(2 * tm, seq), seq).reshape(bsz, seq, d)

    new_k_prompt = head_split(kt)
    new_v_prompt = head_split(vt)
    new_logf_prompt = jnp.transpose(smt[:, :A_HEADS, :], (0, 2, 1))[None]
    new_conv_prompt = r3(qkb)[:, seq - (CONV_W - 1):, :][None]

    m_s = dbsz * dseq
    xs2 = x_sample.reshape(m_s, d)
    mods = mod[bsz:n_c].reshape(1, dbsz, N_MOD * d)
    (qa, kt, vt, ka, va, qkb, vb, og, sm, smt), rest = token_path(xs2, mods, m_s, 1, False)

    n_new = SAMPLE_PAD
    new_rows = lambda a: jnp.pad(a.reshape(dbsz, dseq, -1), ((0, 0), (0, n_new - dseq), (0, 0)))
    lf_s = sm[:, :A_HEADS].reshape(dbsz, dseq, A_HEADS)
    lf_new = jnp.pad(jnp.transpose(lf_s, (0, 2, 1)), ((0, 0), (0, 0), (0, LANES - dseq)))
    q_heads = qa.reshape(dbsz, dseq, A_HEADS, 1, A_HEAD_DIM)
    q_bd = (q_heads * jnp.eye(A_HEADS, dtype=BF16)[None, None, :, :, None]).reshape(dbsz, dseq * A_HEADS, A_WIDTH)
    ya = _sample_attention(
        page_table, q_bd, new_rows(ka), new_rows(va), lf_new,
        jnp.transpose(cache_k[0], (0, 2, 3, 1)), jnp.transpose(cache_v[0], (0, 2, 3, 1)),
        jnp.transpose(cache_logf[0], (0, 2, 1)))
    ya = ya.astype(BF16).reshape(m_s, A_WIDTH)

    t_s = SAMPLE_PAD
    conv_in = jnp.pad(qkb.reshape(dbsz, dseq, 2 * B_WIDTH), ((0, 0), (0, SUBLANES - dseq), (0, 0)))
    conv_init = jnp.pad(state_conv[0], ((0, 0), (SUBLANES - (CONV_W - 1), 0), (0, 0)))
    qkc = _conv(conv_in, conv_init, cw, cb, tc=SUBLANES, nb=min(SAMPLE_SEQS, dbsz))
    tok_valid = (jnp.arange(t_s) < dseq)[None, :, None]
    pad_t = lambda a: jnp.pad(a, ((0, 0), (0, t_s - a.shape[1]), (0, 0)))
    qkc = jnp.where(tok_valid, pad_t(qkc), jnp.zeros((), BF16))
    vb_s = pad_t(vb.reshape(dbsz, dseq, B_WIDTH))
    gates = pad_t(sm[:, A_HEADS:A_HEADS + 2 * B_HEADS].reshape(dbsz, dseq, 2 * B_HEADS))
    neutral = jnp.concatenate([jnp.full((B_HEADS,), NEG_INF, F32), jnp.zeros((B_HEADS,), F32)])
    gcol = jnp.where(tok_valid, gates, neutral)
    grow = jnp.transpose(gcol, (0, 2, 1))
    hb, c_s, n_s, m_sm = _mlstm(qkc, vb_s, gcol, grow, state_C[0], state_n[0][:, :, None, :],
                                state_m[0][:, :, None, None], t=t_s, nb=min(SAMPLE_SEQS, dbsz))
    hb = hb[:, :dseq, :].reshape(m_s, B_WIDTH)
    y_sample = finish(xs2, ya, hb, og, rest, m_s, m_s).reshape(dbsz, dseq, d)

    new_k_sample = head_split(kt).reshape(1, dbsz, dseq, A_HEADS, A_HEAD_DIM)
    new_v_sample = head_split(vt).reshape(1, dbsz, dseq, A_HEADS, A_HEAD_DIM)
    new_logf_sample = lf_s[None]
    qkb3 = qkb.reshape(dbsz, dseq, 2 * B_WIDTH)
    new_conv_sample = jnp.concatenate([state_conv[0], qkb3], axis=1)[:, dseq:, :][None]

    return (y_prompt, y_sample, new_k_prompt, new_v_prompt, new_logf_prompt,
            c_p[None], n_p[:, :, 0, :][None], m_pr[:, :, 0, 0][None], new_conv_prompt,
            new_k_sample, new_v_sample, new_logf_sample,
            c_s[None], n_s[:, :, 0, :][None], m_sm[:, :, 0, 0][None], new_conv_sample)
```

```python
import functools

import jax
import jax.numpy as jnp
from jax import lax
from jax.experimental import pallas as pl
from jax.experimental.pallas import tpu as pltpu

F32 = jnp.float32
BF16 = jnp.bfloat16

A_HEADS = 8
A_HEAD_DIM = 64
A_WIDTH = A_HEADS * A_HEAD_DIM
B_HEADS = 4
B_HEAD_DIM = 128
B_WIDTH = B_HEADS * B_HEAD_DIM
CONV_W = 4
N_MOD = 6
RMS_EPS = 1e-6
NEG_INF = -1e30
PAGE_SIZE = 128
LOG2E = 1.4426950408889634

LANES = 128
SUBLANES = 8
VMEM_LIMIT_BYTES = 56 * 1024 * 1024

TOKEN_TILE = 512
ADALN_COLS = 1024
MLP_CHUNK = 1024
ATTN_Q_BLOCK, ATTN_K_BLOCK = 1024, 512
ATTN_BUILD_CHUNK = 512
CONV_TILE = 1024
MLSTM_CHUNK, MLSTM_SEQS = 128, 4
SAMPLE_PAD = 16
SAMPLE_SEQS = 16
PAGED_SEQS = 2

_NT = (((1,), (1,)), ((), ()))


def _params(*sem):
    return pltpu.CompilerParams(dimension_semantics=sem, vmem_limit_bytes=VMEM_LIMIT_BYTES)


def _resident(shape):
    nd = len(shape)
    return pl.BlockSpec(shape, lambda *_: (0,) * nd, pipeline_mode=pl.Buffered(1))


def _rms(x, g):
    r = lax.rsqrt(jnp.mean(x * x, axis=-1, keepdims=True) + RMS_EPS)
    return (x * r) * g


def _log_sigmoid(x):
    return jnp.minimum(x, 0.0) - jnp.log1p(jnp.exp(-jnp.abs(x)))


def _mod_kernel(c_ref, w_ref, b_ref, o_ref):
    c = c_ref[...]
    a = (c * jax.nn.sigmoid(c)).astype(BF16)
    o_ref[...] = jnp.dot(a, w_ref[...].astype(BF16), preferred_element_type=F32) + b_ref[...]


def _adaln(c, w, b, tn=ADALN_COLS):
    r, d = c.shape
    n = w.shape[1]
    return pl.pallas_call(
        _mod_kernel,
        out_shape=jax.ShapeDtypeStruct((r, n), F32),
        grid=(n // tn,),
        in_specs=[pl.BlockSpec((r, d), lambda j: (0, 0)),
                  pl.BlockSpec((d, tn), lambda j: (0, j)),
                  pl.BlockSpec((1, tn), lambda j: (0, j))],
        out_specs=pl.BlockSpec((r, tn), lambda j: (0, j)),
        compiler_params=_params("arbitrary"),
        name="adaln",
    )(c, w, b)


SHIFT_MIX, SCALE_MIX, GATE_MIX, SHIFT_MLP, SCALE_MLP, GATE_MLP = range(N_MOD)


def _mod_spec(arr, which, tm, tiles_per_group, m):
    g, r, width = arr.shape
    d = width // N_MOD
    if r == 1:
        return pl.BlockSpec((1, 1, d), lambda i: (i // tiles_per_group, 0, which))
    assert g == 1
    return pl.BlockSpec((1, tm * r // m, d), lambda i: (0, i, which))


def _mod_rows(ref, part, tm):
    if ref.shape[1] == 1:
        return ref[0]
    rep = tm // ref.shape[1]
    return jnp.repeat(ref[0, part.start // rep:part.stop // rep, :], rep, axis=0)


TOKEN_PARTS = 2
N_GATES = A_HEADS + 2 * B_HEADS
GATE_ROWS = 16


def _gate_act(z, idx):
    return jnp.where((idx >= A_HEADS) & (idx < A_HEADS + B_HEADS), z, _log_sigmoid(z))


def _in_kernel(x_ref, sc_ref, sh_ref, g_ref, wq_ref, wk_ref, wv_ref, wqk_ref, wbv_ref, wg_ref,
               ws_ref, bs_ref, wst_ref, bst_ref, q_ref, kt_ref, vt_ref, ka_ref, *rest, token_minor_q):
    if not token_minor_q:
        va_ref, *rest = rest
    qkb_ref, vb_ref, og_ref, sm_ref, smt_ref = rest
    tm = x_ref.shape[0]
    parts = [slice(r, r + tm // TOKEN_PARTS) for r in range(0, tm, tm // TOKEN_PARTS)]
    mod = lambda ref, p: _mod_rows(ref, p, tm)
    hb = [(_rms(x_ref[p, :], g_ref[...]) * (1.0 + mod(sc_ref, p)) + mod(sh_ref, p)).astype(BF16) for p in parts]
    proj = lambda w_ref: [jnp.dot(h, w_ref[...], preferred_element_type=F32) for h in hb]
    for p, q in zip(parts, proj(wq_ref)):
        if token_minor_q:
            q_ref[0, :, p] = q.T.astype(BF16)
        else:
            q_ref[p, :] = q.astype(BF16)
    for p, k in zip(parts, proj(wk_ref)):
        kt_ref[0, :, p] = k.T
        ka_ref[p, :] = k.astype(BF16)
    for p, v in zip(parts, proj(wv_ref)):
        vt_ref[0, :, p] = v.T
        if not token_minor_q:
            va_ref[p, :] = v.astype(BF16)
    for p, z in zip(parts, proj(wqk_ref)):
        qkb_ref[p, :] = z
    for p, z in zip(parts, proj(wbv_ref)):
        vb_ref[p, :] = z.astype(BF16)
    for p, z in zip(parts, proj(wg_ref)):
        og_ref[p, :] = jax.nn.sigmoid(z).astype(BF16)
    for p, z in zip(parts, proj(ws_ref)):
        z = z + bs_ref[...]
        sm_ref[p, :] = _gate_act(z, lax.broadcasted_iota(jnp.int32, z.shape, 1))
    for p, h in zip(parts, hb):
        z = lax.dot_general(wst_ref[...], h, _NT, preferred_element_type=F32) + bst_ref[...]
        smt_ref[0, :, p] = _gate_act(z, lax.broadcasted_iota(jnp.int32, z.shape, 0))


def _in_proj(x, mod, g, w, tm, tiles_per_group, token_minor_q):
    m, d = x.shape
    group_tokens = tm * tiles_per_group
    groups = m // group_tokens
    row = lambda n: pl.BlockSpec((tm, n), lambda i: (i, 0))
    col = lambda n: pl.BlockSpec((1, n, tm), lambda i: (i // tiles_per_group, 0, i % tiles_per_group))
    rows_out = lambda n, dt: (jax.ShapeDtypeStruct((m, n), dt), row(n))
    cols_out = lambda n, dt=F32: (jax.ShapeDtypeStruct((groups, n, group_tokens), dt), col(n))
    outs = [cols_out(A_WIDTH, BF16) if token_minor_q else rows_out(A_WIDTH, BF16),
            cols_out(A_WIDTH), cols_out(A_WIDTH), rows_out(A_WIDTH, BF16)]
    if not token_minor_q:
        outs.append(rows_out(A_WIDTH, BF16))
    outs += [rows_out(2 * B_WIDTH, F32), rows_out(B_WIDTH, BF16), rows_out(w["g"].shape[1], BF16),
             rows_out(LANES, F32), cols_out(GATE_ROWS)]
    weights = [w["q"], w["k"], w["v"], w["qk"], w["bv"], w["g"], w["s"], w["bs"], w["st"], w["bst"]]
    return pl.pallas_call(
        functools.partial(_in_kernel, token_minor_q=token_minor_q),
        out_shape=[o[0] for o in outs],
        grid=(m // tm,),
        in_specs=[row(d), _mod_spec(mod, SCALE_MIX, tm, tiles_per_group, m),
                  _mod_spec(mod, SHIFT_MIX, tm, tiles_per_group, m),
                  _resident(g.shape)] + [_resident(a.shape) for a in weights],
        out_specs=[o[1] for o in outs],
        compiler_params=_params("arbitrary"),
        name="in_proj",
    )(x, mod, mod, g, *weights)


MIX_CHUNK = 256


def _mix_kernel(ya_ref, hb_ref, og_ref, x_ref, gt_ref, sc_ref, sh_ref, g1_ref, g2_ref,
                wpa_ref, wpb_ref, wo_ref, x1_ref, h2_ref):
    tm, d = x_ref.shape
    parts = [slice(r, r + tm // TOKEN_PARTS) for r in range(0, tm, tm // TOKEN_PARTS)]
    mod = lambda ref, p: _mod_rows(ref, p, tm)
    merged = []
    for p in parts:
        yb_in = (og_ref[p, :B_WIDTH].astype(F32) * hb_ref[p, :].astype(F32)).astype(BF16)
        cols = []
        for c in range(0, d, MIX_CHUNK):
            a = jnp.dot(ya_ref[p, :], wpa_ref[:, c:c + MIX_CHUNK], preferred_element_type=F32)
            b = jnp.dot(yb_in, wpb_ref[:, c:c + MIX_CHUNK], preferred_element_type=F32)
            ga = og_ref[p, B_WIDTH + c:B_WIDTH + c + MIX_CHUNK].astype(F32)
            gb = og_ref[p, B_WIDTH + d + c:B_WIDTH + d + c + MIX_CHUNK].astype(F32)
            cols.append((ga * a + gb * b).astype(BF16))
        merged.append(jnp.concatenate(cols, axis=1))
    yo = [jnp.dot(y, wo_ref[...], preferred_element_type=F32) for y in merged]
    for p, o in zip(parts, yo):
        x1 = x_ref[p, :] + mod(gt_ref, p) * _rms(o, g1_ref[...])
        x1_ref[p, :] = x1
        h2_ref[p, :] = (_rms(x1, g2_ref[...]) * (1.0 + mod(sc_ref, p)) + mod(sh_ref, p)).astype(BF16)


def _mix_out(ya, hb, og, x, mod, g1, g2, wpa, wpb, wo, tm, tiles_per_group):
    m, d = x.shape
    row = lambda n: pl.BlockSpec((tm, n), lambda i: (i, 0))
    ms = lambda which: _mod_spec(mod, which, tm, tiles_per_group, m)
    return pl.pallas_call(
        _mix_kernel,
        out_shape=[jax.ShapeDtypeStruct((m, d), F32), jax.ShapeDtypeStruct((m, d), BF16)],
        grid=(m // tm,),
        in_specs=[row(ya.shape[1]), row(hb.shape[1]), row(og.shape[1]), row(d),
                  ms(GATE_MIX), ms(SCALE_MLP), ms(SHIFT_MLP),
                  _resident(g1.shape), _resident(g2.shape), _resident(wpa.shape), _resident(wpb.shape),
                  _resident(wo.shape)],
        out_specs=[row(d), row(d)],
        compiler_params=_params("arbitrary"),
        name="mix_out",
    )(ya, hb, og, x, mod, mod, mod, g1, g2, wpa, wpb, wo)


def _mlp_kernel(h2_ref, x1_ref, gt_ref, g_ref, wu_ref, wd_ref, y_ref, *, fc):
    tm = x1_ref.shape[0]
    parts = [slice(r, r + tm // TOKEN_PARTS) for r in range(0, tm, tm // TOKEN_PARTS)]
    mod = lambda ref, p: _mod_rows(ref, p, tm)
    acc = [jnp.zeros((tm // TOKEN_PARTS, x1_ref.shape[1]), F32) for _ in parts]
    for c in range(wu_ref.shape[1] // fc):
        u = [jnp.dot(h2_ref[p, :], wu_ref[:, c * fc:(c + 1) * fc], preferred_element_type=F32) for p in parts]
        u = [jnp.square(jnp.maximum(x, 0.0)).astype(BF16) for x in u]
        acc = [a + jnp.dot(x, wd_ref[c * fc:(c + 1) * fc, :], preferred_element_type=F32) for a, x in zip(acc, u)]
    for p, a in zip(parts, acc):
        y_ref[p, :] = x1_ref[p, :] + mod(gt_ref, p) * _rms(a, g_ref[...])


def _mlp(h2, x1, mod, g, wu, wd, tm, tiles_per_group, fc=MLP_CHUNK):
    m, d = x1.shape
    row = lambda n: pl.BlockSpec((tm, n), lambda i: (i, 0))
    return pl.pallas_call(
        functools.partial(_mlp_kernel, fc=fc),
        out_shape=jax.ShapeDtypeStruct((m, d), F32),
        grid=(m // tm,),
        in_specs=[row(d), row(d), _mod_spec(mod, GATE_MLP, tm, tiles_per_group, m), _resident(g.shape),
                  _resident(wu.shape), _resident(wd.shape)],
        out_specs=row(d),
        compiler_params=_params("arbitrary"),
        name="mlp",
    )(h2, x1, mod, g, wu, wd)


def _lane_scan(x, lane, shifts):
    for s in shifts:
        x = x + jnp.where(lane >= s, pltpu.roll(x, s, axis=1), 0.0)
    return x


def _cumsum_kernel(x_ref, o_ref):
    rows, length = x_ref.shape[1], x_ref.shape[2]
    lane = lax.broadcasted_iota(jnp.int32, (rows, LANES), 1)
    carry = jnp.zeros((rows, 1), F32)
    for c in range(length // LANES):
        x = _lane_scan(x_ref[0, :, c * LANES:(c + 1) * LANES], lane, (1, 2, 4, 8, 16, 32, 64)) + carry
        o_ref[0, :, c * LANES:(c + 1) * LANES] = x
        carry = x[:, LANES - 1:LANES]


def _cumsum_lanes(x):
    b, r, length = x.shape
    spec = pl.BlockSpec((1, r, length), lambda i: (i, 0, 0))
    return pl.pallas_call(
        _cumsum_kernel,
        out_shape=jax.ShapeDtypeStruct(x.shape, F32),
        grid=(b,),
        in_specs=[spec],
        out_specs=spec,
        compiler_params=_params("arbitrary"),
        name="logf_cumsum",
    )(x)


V_ROWS = 80


def _aug_lane0(hh):
    return A_HEAD_DIM if hh == 0 else 0


def _attn_kernel(qt_ref, k_ref, vt_ref, f_ref, o_ref, kaug_ref, vaug_ref, s_ref, *, blk_q, blk_k, chunk):
    pair = pl.program_id(1)
    qi = pl.program_id(2)
    length = k_ref.shape[1]
    lane = lax.broadcasted_iota(jnp.int32, (1, LANES), 1)
    own = (lane < A_HEAD_DIM, lane >= A_HEAD_DIM)
    feat = lax.broadcasted_iota(jnp.int32, (LANES, 1), 0)

    @pl.when(qi == 0)
    def _():
        ones_row = jnp.where(lax.broadcasted_iota(jnp.int32, (V_ROWS - A_HEAD_DIM, chunk), 0) == 0, 1.0, 0.0)

        def build(c, _):
            toks = pl.ds(pl.multiple_of(c * chunk, chunk), chunk)
            k = k_ref[0, toks, :]
            vt = vt_ref[0, :, toks]
            for hh in range(2):
                a0 = _aug_lane0(hh)
                neg_f = f_ref[0, pl.ds(2 * pair + hh, 1), toks] * (-LOG2E)
                hi = neg_f.astype(BF16).astype(F32)
                mid = (neg_f - hi).astype(BF16).astype(F32)
                lo = (neg_f - hi - mid).astype(BF16).astype(F32)
                bias = jnp.concatenate(
                    [jnp.where(feat == a0, hi[:, t], jnp.where(feat == a0 + 1, mid[:, t],
                                                               jnp.where(feat == a0 + 2, lo[:, t], 0.0))).T
                     for t in (slice(t0, t0 + LANES) for t0 in range(0, chunk, LANES))], axis=0)
                kaug_ref[hh, toks, :] = jnp.where(own[hh], k, bias.astype(BF16))
                dims = vt[hh * A_HEAD_DIM:(hh + 1) * A_HEAD_DIM]
                vaug_ref[hh, :, toks] = jnp.concatenate([dims, ones_row], axis=0).astype(BF16)
            return 0
        lax.fori_loop(0, length // chunk, build, 0)

    qt = qt_ref[0]
    qh = []
    for hh in range(2):
        a0 = _aug_lane0(hh)
        ones = jnp.where((feat >= a0) & (feat < a0 + 3), 1.0, 0.0).astype(BF16)
        qh.append(jnp.where((feat < A_HEAD_DIM) if hh == 0 else (feat >= A_HEAD_DIM), qt, ones))

    def keys(j):
        return pl.ds(pl.multiple_of(j * blk_k, blk_k), blk_k)

    def scores(j, slot, c0=0):
        for hh in range(2):
            s_ref[slot, hh, :, c0:] = jnp.dot(kaug_ref[hh, keys(j), :], qh[hh][:, c0:], preferred_element_type=F32)

    def fold(j, slot, carry, causal, c0=0):
        probs, stats = [], []
        for hh in range(2):
            m = carry[hh][0]
            sh = s_ref[slot, hh, :, c0:]
            if causal:
                key = lax.broadcasted_iota(jnp.int32, sh.shape, 0)
                qry = lax.broadcasted_iota(jnp.int32, sh.shape, 1)
                sh = jnp.where(key <= qry, sh, NEG_INF)
            m_new = jnp.maximum(m, jnp.max(sh, axis=0, keepdims=True))
            probs.append(jnp.exp2(sh - m_new).astype(BF16))
            stats.append((m_new, jnp.exp2(m - m_new)))
        return tuple(
            (stats[hh][0], stats[hh][1] * carry[hh][1]
             + jnp.dot(vaug_ref[hh, :, keys(j)], probs[hh], preferred_element_type=F32))
            for hh in range(2))

    assert blk_q == 2 * blk_k
    init = tuple((jnp.full((1, blk_q), NEG_INF, F32), jnp.zeros((V_ROWS, blk_q), F32)) for _ in range(2))

    def body(p, carry):
        scores(2 * p + 1, 1)
        carry = fold(2 * p, 0, carry, False)
        scores(2 * p + 2, 0)
        return fold(2 * p + 1, 1, carry, False)

    scores(0, 0)
    carry = lax.fori_loop(0, qi, body, init)
    scores(2 * qi + 1, 1, blk_k)
    carry = fold(2 * qi, 0, carry, True)
    upper = fold(2 * qi + 1, 1, tuple((m[:, blk_k:], acc[:, blk_k:]) for m, acc in carry), True, blk_k)
    out = []
    for hh in range(2):
        acc = jnp.concatenate([carry[hh][1][:, :blk_k], upper[hh][1]], axis=1)
        out.append(acc[:A_HEAD_DIM] / acc[A_HEAD_DIM:A_HEAD_DIM + 1])
    o_ref[0] = jnp.concatenate(out, axis=0).T.astype(BF16)


def _attention(qt, k, vt, f, blk_q=ATTN_Q_BLOCK, blk_k=ATTN_K_BLOCK):
    b, length, width = k.shape
    pairs = width // LANES
    blk_q, blk_k = min(blk_q, length), min(blk_k, length)
    return pl.pallas_call(
        functools.partial(_attn_kernel, blk_q=blk_q, blk_k=blk_k, chunk=min(ATTN_BUILD_CHUNK, length)),
        out_shape=jax.ShapeDtypeStruct(k.shape, BF16),
        grid=(b, pairs, length // blk_q),
        in_specs=[pl.BlockSpec((1, LANES, blk_q), lambda i, p, t: (i, p, t)),
                  pl.BlockSpec((1, length, LANES), lambda i, p, t: (i, 0, p)),
                  pl.BlockSpec((1, LANES, length), lambda i, p, t: (i, p, 0)),
                  pl.BlockSpec((1,) + f.shape[1:], lambda i, p, t: (i, 0, 0))],
        out_specs=pl.BlockSpec((1, blk_q, LANES), lambda i, p, t: (i, t, p)),
        scratch_shapes=[pltpu.VMEM((2, length, LANES), BF16), pltpu.VMEM((2, V_ROWS, length), BF16),
                        pltpu.VMEM((2, 2, blk_k, blk_q), F32)],
        compiler_params=_params("arbitrary", "arbitrary", "arbitrary"),
        name="fox_attention",
    )(qt, k, vt, f)


def _conv_silu(xp_ref, w_ref, b_ref, tc):
    y = b_ref[...]
    for j in range(CONV_W):
        off = SUBLANES - (CONV_W - 1) + j
        y = y + w_ref[j:j + 1, :] * xp_ref[off:off + tc, :]
    lane = lax.broadcasted_iota(jnp.int32, (1, y.shape[1]), 1)
    return y * jax.nn.sigmoid(y) * jnp.where(lane < B_WIDTH, 1.0, B_HEAD_DIM ** -0.5)


def _conv_kernel(x_ref, prev_ref, init_ref, w_ref, b_ref, o_ref, xp_ref, *, tc):
    for s in range(x_ref.shape[0]):
        xp_ref[s, 0:SUBLANES, :] = jnp.where(pl.program_id(1) == 0, init_ref[s], prev_ref[s])
        xp_ref[s, SUBLANES:SUBLANES + tc, :] = x_ref[s]
        o_ref[s] = _conv_silu(xp_ref.at[s], w_ref, b_ref, tc).astype(BF16)


def _conv(x, init, w, b, tc, nb=1):
    bsz, length, c = x.shape
    tpb = tc // SUBLANES
    return pl.pallas_call(
        functools.partial(_conv_kernel, tc=tc),
        out_shape=jax.ShapeDtypeStruct(x.shape, BF16),
        grid=(bsz // nb, length // tc),
        in_specs=[pl.BlockSpec((nb, tc, c), lambda i, t: (i, t, 0)),
                  pl.BlockSpec((nb, SUBLANES, c), lambda i, t: (i, jnp.maximum(t * tpb - 1, 0), 0)),
                  pl.BlockSpec((nb, SUBLANES, c), lambda i, t: (i, 0, 0)),
                  pl.BlockSpec((CONV_W, c), lambda i, t: (0, 0)),
                  pl.BlockSpec((1, c), lambda i, t: (0, 0))],
        out_specs=pl.BlockSpec((nb, tc, c), lambda i, t: (i, t, 0)),
        scratch_shapes=[pltpu.VMEM((nb, tc + SUBLANES, c), F32)],
        compiler_params=_params("arbitrary", "arbitrary"),
        name="short_conv",
    )(x, x, init, w, b)


_BNT = (((2,), (2,)), ((0,), (0,)))
_BNN = (((2,), (1,)), ((0,), (0,)))
_BTN = (((1,), (1,)), ((0,), (0,)))


def _mlstm_heads(q, k, v, ir, fr, ic, fc, c_st, n_st, m_st):
    n, t, d = q.shape
    row = lax.broadcasted_iota(jnp.int32, (n, t, t), 1)
    col = lax.broadcasted_iota(jnp.int32, (n, t, t), 2)
    tril = col <= row
    spread = lambda x, w: jnp.broadcast_to(x, (n, t, w))
    both = lambda x: (spread(x, t),) * 2 if t == d else (spread(x, t), spread(x, d))
    bcum_c = jnp.sum(jnp.where(tril, fr, 0.0), axis=2, keepdims=True)
    bcum_r = jnp.sum(jnp.where(row <= col, fc, 0.0), axis=1, keepdims=True)
    bcum_t, bcum_d = both(bcum_c)
    dlog = jnp.where(tril, bcum_t - bcum_r + ir, NEG_INF)
    mt_t, mt_d = both(jnp.maximum(bcum_c + m_st, jnp.max(dlog, axis=2, keepdims=True)))
    a = jnp.exp(dlog - mt_t) * lax.dot_general(q, k, _BNT, preferred_element_type=F32)
    si = jnp.exp(bcum_d + m_st - mt_d)
    num = (lax.dot_general(a.astype(BF16), v, _BNN, preferred_element_type=F32)
           + si * lax.dot_general(q, c_st.astype(BF16), _BNN, preferred_element_type=F32))
    den = jnp.sum(a, axis=2, keepdims=True) + si * jnp.sum(q.astype(F32) * n_st, axis=2, keepdims=True)
    h = num / jnp.maximum(jnp.abs(den), jnp.exp(-mt_d))
    bl = bcum_r[:, :, t - 1:t]
    m_new = jnp.maximum(bl + m_st, jnp.max(bl - bcum_r + ir, axis=2, keepdims=True))
    kw = jnp.exp(bl - bcum_d + spread(ic, d) - m_new) * k.astype(F32)
    decay = jnp.exp(bl + m_st - m_new)
    c_new = decay * c_st + lax.dot_general(kw.astype(BF16), v, _BTN, preferred_element_type=F32)
    n_new = decay * n_st + jnp.sum(kw, axis=1, keepdims=True)
    return h, c_new, n_new, m_new


def _mlstm_kernel(q_ref, k_ref, v_ref, gc_ref, gr_ref, c0_ref, n0_ref, m0_ref,
                  h_ref, c_ref, n_ref, m_ref, *, g0):
    @pl.when(pl.program_id(1) == 0)
    def _():
        c_ref[...] = c0_ref[...]
        n_ref[...] = n0_ref[...]
        m_ref[...] = m0_ref[...]

    nb = q_ref.shape[0]
    where = [(s, hd, slice(hd * B_HEAD_DIM, (hd + 1) * B_HEAD_DIM)) for s in range(nb) for hd in range(B_HEADS)]
    heads = lambda ref: jnp.stack([ref[s, :, sl] for s, _, sl in where])
    gc = [gc_ref[s] for s in range(nb)]
    gr = [gr_ref[s] for s in range(nb)]
    gi, gf = g0, g0 + B_HEADS
    h, c_new, n_new, m_new = _mlstm_heads(
        heads(q_ref), heads(k_ref), heads(v_ref),
        jnp.stack([gr[s][gi + hd:gi + hd + 1, :] for s, hd, _ in where]),
        jnp.stack([gr[s][gf + hd:gf + hd + 1, :] for s, hd, _ in where]),
        jnp.stack([gc[s][:, gi + hd:gi + hd + 1] for s, hd, _ in where]),
        jnp.stack([gc[s][:, gf + hd:gf + hd + 1] for s, hd, _ in where]),
        c_ref[...].reshape((nb * B_HEADS,) + c_ref.shape[2:]),
        n_ref[...].reshape((nb * B_HEADS,) + n_ref.shape[2:]),
        m_ref[...].reshape((nb * B_HEADS,) + m_ref.shape[2:]))
    for i, (s, _, sl) in enumerate(where):
        h_ref[s, :, sl] = h[i].astype(BF16)
    c_ref[...] = c_new.reshape(c_ref.shape)
    n_ref[...] = n_new.reshape(n_ref.shape)
    m_ref[...] = m_new.reshape(m_ref.shape)


def _mlstm(qk, v, gcol, grow, c0, n0, m0, t, nb=1, g0=0):
    b, length, _ = v.shape
    st = lambda a: pl.BlockSpec((nb,) + a.shape[1:], lambda i, j: (i, 0, 0, 0))
    return pl.pallas_call(
        functools.partial(_mlstm_kernel, g0=g0),
        out_shape=[jax.ShapeDtypeStruct(v.shape, BF16), jax.ShapeDtypeStruct(c0.shape, F32),
                   jax.ShapeDtypeStruct(n0.shape, F32), jax.ShapeDtypeStruct(m0.shape, F32)],
        grid=(b // nb, length // t),
        in_specs=[pl.BlockSpec((nb, t, B_WIDTH), lambda i, j: (i, j, 0)),
                  pl.BlockSpec((nb, t, B_WIDTH), lambda i, j: (i, j, 1)),
                  pl.BlockSpec((nb, t, B_WIDTH), lambda i, j: (i, j, 0)),
                  pl.BlockSpec((nb, t, gcol.shape[2]), lambda i, j: (i, j, 0)),
                  pl.BlockSpec((nb, grow.shape[1], t), lambda i, j: (i, 0, j)),
                  st(c0), st(n0), st(m0)],
        out_specs=[pl.BlockSpec((nb, t, B_WIDTH), lambda i, j: (i, j, 0)), st(c0), st(n0), st(m0)],
        compiler_params=_params("arbitrary", "arbitrary"),
        name="mlstm",
    )(qk, qk, v, gcol, grow, c0, n0, m0)


def _page_copies(pt_ref, step, slot, ck_ref, cv_ref, kbuf_ref, vbuf_ref, sem_ref, nb, n_pages):
    copies = []
    for s in range(nb):
        for j in range(n_pages):
            page = pt_ref[step * nb + s, j]
            at = s * n_pages + j
            copies.append(pltpu.make_async_copy(ck_ref.at[page], kbuf_ref.at[slot, at], sem_ref.at[slot, 0]))
            copies.append(pltpu.make_async_copy(cv_ref.at[page], vbuf_ref.at[slot, at], sem_ref.at[slot, 1]))
    return copies


def _sattn_kernel(pt_ref, q_ref, kn_ref, vn_ref, lfn_ref, lf_ref, ck_ref, cv_ref, o_ref,
                  kbuf_ref, vbuf_ref, sem_ref, *, n_pages):
    nb, nrow, width = q_ref.shape
    step, n_steps = pl.program_id(0), pl.num_programs(0)
    slot = step % 2
    fetch = functools.partial(_page_copies, pt_ref, ck_ref=ck_ref, cv_ref=cv_ref, kbuf_ref=kbuf_ref,
                              vbuf_ref=vbuf_ref, sem_ref=sem_ref, nb=nb, n_pages=n_pages)

    @pl.when(step == 0)
    def _():
        for c in fetch(0, 0):
            c.start()

    @pl.when(step + 1 < n_steps)
    def _():
        for c in fetch(step + 1, 1 - slot):
            c.start()

    for c in fetch(step, slot):
        c.wait()
    k_refs = [[kbuf_ref.at[slot, s * n_pages + j] for j in range(n_pages)] for s in range(nb)]
    v_refs = [[vbuf_ref.at[slot, s * n_pages + j] for j in range(n_pages)] for s in range(nb)]
    first = step * nb
    nq = nrow // A_HEADS
    n_new = kn_ref.shape[1]
    lane = lax.broadcasted_iota(jnp.int32, (A_HEADS, LANES), 1)
    per_row = lambda f: jnp.concatenate([f] * nq, axis=0)
    scan = lambda x: _lane_scan(x, lane, (1, 2, 4, 8, 16, 32, 64))

    local = [[scan(lf_ref[pt_ref[first + s, j]]) for j in range(n_pages)] for s in range(nb)]
    new_local = [scan(lfn_ref[s]) for s in range(nb)]
    f_keys, f_new = [], []
    for s in range(nb):
        offset = jnp.zeros((A_HEADS, 1), F32)
        f_keys.append([])
        for j in range(n_pages):
            f = local[s][j] + offset
            offset = f[:, LANES - 1:LANES]
            f_keys[s].append(f)
        f_new.append((new_local[s] + offset)[:, :n_new])

    scores = [[jnp.dot(q_ref[s], k_refs[s][j][...].reshape(width, PAGE_SIZE).astype(BF16),
                       preferred_element_type=F32) - per_row(f_keys[s][j] * LOG2E)
               for j in range(n_pages)] for s in range(nb)]
    q_tok = lax.broadcasted_iota(jnp.int32, (nrow, n_new), 0) // A_HEADS
    k_tok = lax.broadcasted_iota(jnp.int32, (nrow, n_new), 1)
    s_new = [jnp.where(k_tok <= q_tok,
                       lax.dot_general(q_ref[s], kn_ref[s], _NT, preferred_element_type=F32)
                       - per_row(f_new[s] * LOG2E), NEG_INF) for s in range(nb)]

    m = [jnp.maximum(jnp.max(functools.reduce(jnp.maximum, scores[s]), axis=1, keepdims=True),
                     jnp.max(s_new[s], axis=1, keepdims=True)) for s in range(nb)]
    probs = [[jnp.exp2(x - m[s]) for x in scores[s]] for s in range(nb)]
    p_new = [jnp.exp2(s_new[s] - m[s]) for s in range(nb)]
    l = [jnp.sum(functools.reduce(jnp.add, probs[s]), axis=1, keepdims=True)
         + jnp.sum(p_new[s], axis=1, keepdims=True) for s in range(nb)]
    acc = [jnp.dot(p_new[s].astype(BF16), vn_ref[s], preferred_element_type=F32) for s in range(nb)]
    for s in range(nb):
        for j in range(n_pages):
            vt = v_refs[s][j][...].reshape(width, PAGE_SIZE).astype(BF16)
            acc[s] = acc[s] + lax.dot_general(probs[s][j].astype(BF16), vt, _NT, preferred_element_type=F32)
    row_head = lax.broadcasted_iota(jnp.int32, (nrow, width), 0) % A_HEADS
    col_head = lax.broadcasted_iota(jnp.int32, (nrow, width), 1) // A_HEAD_DIM
    for s in range(nb):
        out = jnp.where(row_head == col_head, acc[s] / l[s], 0.0)
        o_ref[s] = jnp.sum(out.reshape(nq, A_HEADS, width), axis=1)


def _sample_attention(page_table, q, kn, vn, lfn, cache_kt, cache_vt, cache_lf, nb=PAGED_SEQS):
    b, n_pages = page_table.shape
    nq = q.shape[1] // A_HEADS
    per_seq = lambda a: pl.BlockSpec((nb,) + a.shape[1:], lambda i, pt: (i, 0, 0))
    in_hbm = pl.BlockSpec(memory_space=pl.ANY)
    page_buffers = pltpu.VMEM((2, nb * n_pages) + cache_kt.shape[1:], F32)
    grid_spec = pltpu.PrefetchScalarGridSpec(
        num_scalar_prefetch=1,
        grid=(b // nb,),
        in_specs=[per_seq(q), per_seq(kn), per_seq(vn), per_seq(lfn), _resident(cache_lf.shape), in_hbm, in_hbm],
        out_specs=pl.BlockSpec((nb, nq, A_WIDTH), lambda i, pt: (i, 0, 0)),
        scratch_shapes=[page_buffers, page_buffers, pltpu.SemaphoreType.DMA((2, 2))],
    )
    return pl.pallas_call(
        functools.partial(_sattn_kernel, n_pages=n_pages),
        out_shape=jax.ShapeDtypeStruct((b, nq, A_WIDTH), F32),
        grid_spec=grid_spec,
        compiler_params=_params("arbitrary"),
        name="paged_fox_attention",
    )(page_table, q, kn, vn, lfn, cache_lf, cache_kt, cache_vt)


def _split_weights(w_in, b_fox_f, b_ml_i, b_ml_f):
    d = w_in.shape[0]
    o = 0
    parts = {}
    for name, n in (("aq", A_WIDTH), ("ak", A_WIDTH), ("av", A_WIDTH), ("af", A_HEADS), ("bq", B_WIDTH),
                    ("bk", B_WIDTH), ("bv", B_WIDTH), ("bi", B_HEADS), ("bf", B_HEADS), ("bo", B_WIDTH),
                    ("ga", d), ("gb", d)):
        parts[name] = w_in[:, o:o + n]
        o += n
    assert o == w_in.shape[1]
    small = jnp.concatenate([parts["af"], parts["bi"], parts["bf"]], axis=1)
    bias = jnp.concatenate([b_fox_f, b_ml_i, b_ml_f])
    bf = lambda a: a.astype(BF16)
    return {
        "q": bf(parts["aq"] * (A_HEAD_DIM ** -0.5 * LOG2E)),
        "k": bf(parts["ak"]),
        "v": bf(parts["av"]),
        "qk": bf(jnp.concatenate([parts["bq"], parts["bk"]], axis=1)),
        "bv": bf(parts["bv"]),
        "g": bf(jnp.concatenate([parts["bo"], parts["ga"], parts["gb"]], axis=1)),
        "s": bf(jnp.pad(small, ((0, 0), (0, LANES - N_GATES)))),
        "bs": jnp.pad(bias, (0, LANES - N_GATES)).reshape(1, LANES),
        "st": bf(jnp.pad(small.T, ((0, GATE_ROWS - N_GATES), (0, 0)))),
        "bst": jnp.pad(bias, (0, GATE_ROWS - N_GATES)).reshape(GATE_ROWS, 1),
    }


def kernel(x_prompt, x_sample, cache_k, cache_v, cache_logf, page_table, state_C, state_n, state_m, state_conv,
           c_prompt, c_sample, w_ada, b_ada, g_pre_mix, g_post_mix, g_pre_mlp, g_post_mlp, w_in, b_fox_f,
           b_ml_i, b_ml_f, conv_w, conv_b, w_proj_a, w_proj_b, w_out, w_up, w_down):
    assert w_in.shape[0] == 1, "one trunk layer"
    bsz, seq, d = x_prompt.shape
    dbsz, dseq, _ = x_sample.shape

    w_proj = _split_weights(w_in[0], b_fox_f[0], b_ml_i[0], b_ml_f[0])
    wpa, wpb, wo = w_proj_a[0].astype(BF16), w_proj_b[0].astype(BF16), w_out[0].astype(BF16)
    wu, wd = w_up[0].astype(BF16), w_down[0].astype(BF16)
    g1, g2, g3, g4 = g_pre_mix, g_post_mix, g_pre_mlp, g_post_mlp
    cw, cb = conv_w[0], conv_b

    n_c = bsz + dbsz
    c_all = jnp.concatenate([c_prompt, c_sample, jnp.zeros((-n_c % SUBLANES, d), F32)], axis=0)
    mod = _adaln(c_all, w_ada[0], b_ada)

    def token_path(x2, mods, tm, tiles_per_group, token_minor_q):
        return _in_proj(x2, mods, g1, w_proj, tm, tiles_per_group, token_minor_q), mods

    def head_split(kt):
        g_, _, n_ = kt.shape
        return jnp.transpose(kt.reshape(g_, A_HEADS, A_HEAD_DIM, n_), (0, 3, 1, 2))[None]

    def finish(x2, ya, hb, og, mods, tm, group_tokens):
        x1, h2 = _mix_out(ya, hb, og, x2, mods, g2, g3, wpa, wpb, wo, tm, group_tokens // tm)
        return _mlp(h2, x1, mods, g4, wu, wd, tm, group_tokens // tm)

    tm = min(TOKEN_TILE, seq)
    m_p = bsz * seq
    x2 = x_prompt.reshape(m_p, d)
    mods = mod[:bsz].reshape(bsz, 1, N_MOD * d)
    (qt, kt, vt, ka, qkb, vb, og, sm, smt), rest = token_path(x2, mods, tm, seq // tm, True)

    r3 = lambda a: a.reshape(bsz, seq, a.shape[-1])
    ya = _attention(qt, r3(ka), vt, _cumsum_lanes(smt)).reshape(m_p, A_WIDTH)

    qkc = _conv(r3(qkb), jnp.zeros((bsz, SUBLANES, 2 * B_WIDTH), F32), cw, cb, tc=min(CONV_TILE, seq))
    hb, c_p, n_p, m_pr = _mlstm(qkc, r3(vb), r3(sm), smt,
                                jnp.zeros((bsz, B_HEADS, B_HEAD_DIM, B_HEAD_DIM), F32),
                                jnp.zeros((bsz, B_HEADS, 1, B_HEAD_DIM), F32),
                                jnp.zeros((bsz, B_HEADS, 1, 1), F32), t=min(MLSTM_CHUNK, seq), nb=min(MLSTM_SEQS, bsz), g0=A_HEADS)
    y_prompt = finish(x2, ya, hb.reshape(m_p, B_WIDTH), og, rest, min(2 * tm, seq), seq).reshape(bsz, seq, d)

    new_k_prompt = head_split(kt)
    new_v_prompt = head_split(vt)
    new_logf_prompt = jnp.transpose(smt[:, :A_HEADS, :], (0, 2, 1))[None]
    new_conv_prompt = r3(qkb)[:, seq - (CONV_W - 1):, :][None]

    m_s = dbsz * dseq
    xs2 = x_sample.reshape(m_s, d)
    mods = mod[bsz:n_c].reshape(1, dbsz, N_MOD * d)
    (qa, kt, vt, ka, va, qkb, vb, og, sm, smt), rest = token_path(xs2, mods, m_s, 1, False)

    n_new = SAMPLE_PAD
    new_rows = lambda a: jnp.pad(a.reshape(dbsz, dseq, -1), ((0, 0), (0, n_new - dseq), (0, 0)))
    lf_s = sm[:, :A_HEADS].reshape(dbsz, dseq, A_HEADS)
    lf_new = jnp.pad(jnp.transpose(lf_s, (0, 2, 1)), ((0, 0), (0, 0), (0, LANES - dseq)))
    q_heads = qa.reshape(dbsz, dseq, A_HEADS, 1, A_HEAD_DIM)
    q_bd = (q_heads * jnp.eye(A_HEADS, dtype=BF16)[None, None, :, :, None]).reshape(dbsz, dseq * A_HEADS, A_WIDTH)
    ya = _sample_attention(
        page_table, q_bd, new_rows(ka), new_rows(va), lf_new,
        jnp.transpose(cache_k[0], (0, 2, 3, 1)), jnp.transpose(cache_v[0], (0, 2, 3, 1)),
        jnp.transpose(cache_logf[0], (0, 2, 1)))
    ya = ya.astype(BF16).reshape(m_s, A_WIDTH)

    t_s = SAMPLE_PAD
    conv_in = jnp.pad(qkb.reshape(dbsz, dseq, 2 * B_WIDTH), ((0, 0), (0, SUBLANES - dseq), (0, 0)))
    conv_init = jnp.pad(state_conv[0], ((0, 0), (SUBLANES - (CONV_W - 1), 0), (0, 0)))
    qkc = _conv(conv_in, conv_init, cw, cb, tc=SUBLANES, nb=min(SAMPLE_SEQS, dbsz))
    tok_valid = (jnp.arange(t_s) < dseq)[None, :, None]
    pad_t = lambda a: jnp.pad(a, ((0, 0), (0, t_s - a.shape[1]), (0, 0)))
    qkc = jnp.where(tok_valid, pad_t(qkc), jnp.zeros((), BF16))
    vb_s = pad_t(vb.reshape(dbsz, dseq, B_WIDTH))
    gates = pad_t(sm[:, A_HEADS:A_HEADS + 2 * B_HEADS].reshape(dbsz, dseq, 2 * B_HEADS))
    neutral = jnp.concatenate([jnp.full((B_HEADS,), NEG_INF, F32), jnp.zeros((B_HEADS,), F32)])
    gcol = jnp.where(tok_valid, gates, neutral)
    grow = jnp.transpose(gcol, (0, 2, 1))
    hb, c_s, n_s, m_sm = _mlstm(qkc, vb_s, gcol, grow, state_C[0], state_n[0][:, :, None, :],
                                state_m[0][:, :, None, None], t=t_s, nb=min(SAMPLE_SEQS, dbsz))
    hb = hb[:, :dseq, :].reshape(m_s, B_WIDTH)
    y_sample = finish(xs2, ya, hb, og, rest, m_s, m_s).reshape(dbsz, dseq, d)

    new_k_sample = head_split(kt).reshape(1, dbsz, dseq, A_HEADS, A_HEAD_DIM)
    new_v_sample = head_split(vt).reshape(1, dbsz, dseq, A_HEADS, A_HEAD_DIM)
    new_logf_sample = lf_s[None]
    qkb3 = qkb.reshape(dbsz, dseq, 2 * B_WIDTH)
    new_conv_sample = jnp.concatenate([state_conv[0], qkb3], axis=1)[:, dseq:, :][None]

    return (y_prompt, y_sample, new_k_prompt, new_v_prompt, new_logf_prompt,
            c_p[None], n_p[:, :, 0, :][None], m_pr[:, :, 0, 0][None], new_conv_prompt,
            new_k_sample, new_v_sample, new_logf_sample,
            c_s[None], n_s[:, :, 0, :][None], m_sm[:, :, 0, 0][None], new_conv_sample)
```

```python
import functools

import jax
import jax.numpy as jnp
from jax import lax
from jax.experimental import pallas as pl
from jax.experimental.pallas import tpu as pltpu

F32 = jnp.float32
BF16 = jnp.bfloat16

A_HEADS = 8
A_HEAD_DIM = 64
A_WIDTH = A_HEADS * A_HEAD_DIM
B_HEADS = 4
B_HEAD_DIM = 128
B_WIDTH = B_HEADS * B_HEAD_DIM
CONV_W = 4
N_MOD = 6
RMS_EPS = 1e-6
NEG_INF = -1e30
PAGE_SIZE = 128
LOG2E = 1.4426950408889634

LANES = 128
SUBLANES = 8
VMEM_LIMIT_BYTES = 56 * 1024 * 1024

TOKEN_TILE = 512
ADALN_COLS = 1024
MLP_CHUNK = 1024
ATTN_Q_BLOCK, ATTN_K_BLOCK = 1024, 512
ATTN_BUILD_CHUNK = 512
CONV_TILE = 1024
MLSTM_CHUNK, MLSTM_SEQS = 128, 4
SAMPLE_PAD = 16
SAMPLE_SEQS = 16
PAGED_SEQS = 2

_NT = (((1,), (1,)), ((), ()))


def _params(*sem):
    return pltpu.CompilerParams(dimension_semantics=sem, vmem_limit_bytes=VMEM_LIMIT_BYTES)


def _resident(shape):
    nd = len(shape)
    return pl.BlockSpec(shape, lambda *_: (0,) * nd, pipeline_mode=pl.Buffered(1))


def _rms(x, g):
    r = lax.rsqrt(jnp.mean(x * x, axis=-1, keepdims=True) + RMS_EPS)
    return (x * r) * g


def _log_sigmoid(x):
    return jnp.minimum(x, 0.0) - jnp.log1p(jnp.exp(-jnp.abs(x)))


def _mod_kernel(c_ref, w_ref, b_ref, o_ref):
    c = c_ref[...]
    a = (c * jax.nn.sigmoid(c)).astype(BF16)
    o_ref[...] = jnp.dot(a, w_ref[...].astype(BF16), preferred_element_type=F32) + b_ref[...]


def _adaln(c, w, b, tn=ADALN_COLS):
    r, d = c.shape
    n = w.shape[1]
    return pl.pallas_call(
        _mod_kernel,
        out_shape=jax.ShapeDtypeStruct((r, n), F32),
        grid=(n // tn,),
        in_specs=[pl.BlockSpec((r, d), lambda j: (0, 0)),
                  pl.BlockSpec((d, tn), lambda j: (0, j)),
                  pl.BlockSpec((1, tn), lambda j: (0, j))],
        out_specs=pl.BlockSpec((r, tn), lambda j: (0, j)),
        compiler_params=_params("arbitrary"),
        name="adaln",
    )(c, w, b)


SHIFT_MIX, SCALE_MIX, GATE_MIX, SHIFT_MLP, SCALE_MLP, GATE_MLP = range(N_MOD)


def _mod_spec(arr, which, tm, tiles_per_group, m):
    g, r, width = arr.shape
    d = width // N_MOD
    if r == 1:
        return pl.BlockSpec((1, 1, d), lambda i: (i // tiles_per_group, 0, which))
    assert g == 1
    return pl.BlockSpec((1, tm * r // m, d), lambda i: (0, i, which))


def _mod_rows(ref, part, tm):
    if ref.shape[1] == 1:
        return ref[0]
    rep = tm // ref.shape[1]
    return jnp.repeat(ref[0, part.start // rep:part.stop // rep, :], rep, axis=0)


TOKEN_PARTS = 2
N_GATES = A_HEADS + 2 * B_HEADS
GATE_ROWS = 16


def _gate_act(z, idx):
    return jnp.where((idx >= A_HEADS) & (idx < A_HEADS + B_HEADS), z, _log_sigmoid(z))


def _in_kernel(x_ref, sc_ref, sh_ref, g_ref, wq_ref, wk_ref, wv_ref, wqk_ref, wbv_ref, wg_ref,
               ws_ref, bs_ref, wst_ref, bst_ref, q_ref, kt_ref, vt_ref, ka_ref, *rest, token_minor_q):
    if not token_minor_q:
        va_ref, *rest = rest
    qkb_ref, vb_ref, og_ref, sm_ref, smt_ref = rest
    tm = x_ref.shape[0]
    parts = [slice(r, r + tm // TOKEN_PARTS) for r in range(0, tm, tm // TOKEN_PARTS)]
    mod = lambda ref, p: _mod_rows(ref, p, tm)
    hb = [(_rms(x_ref[p, :], g_ref[...]) * (1.0 + mod(sc_ref, p)) + mod(sh_ref, p)).astype(BF16) for p in parts]
    proj = lambda w_ref: [jnp.dot(h, w_ref[...], preferred_element_type=F32) for h in hb]
    for p, q in zip(parts, proj(wq_ref)):
        if token_minor_q:
            q_ref[0, :, p] = q.T.astype(BF16)
        else:
            q_ref[p, :] = q.astype(BF16)
    for p, k in zip(parts, proj(wk_ref)):
        kt_ref[0, :, p] = k.T
        ka_ref[p, :] = k.astype(BF16)
    for p, v in zip(parts, proj(wv_ref)):
        vt_ref[0, :, p] = v.T
        if not token_minor_q:
            va_ref[p, :] = v.astype(BF16)
    for p, z in zip(parts, proj(wqk_ref)):
        qkb_ref[p, :] = z
    for p, z in zip(parts, proj(wbv_ref)):
        vb_ref[p, :] = z.astype(BF16)
    for p, z in zip(parts, proj(wg_ref)):
        og_ref[p, :] = jax.nn.sigmoid(z).astype(BF16)
    for p, z in zip(parts, proj(ws_ref)):
        z = z + bs_ref[...]
        sm_ref[p, :] = _gate_act(z, lax.broadcasted_iota(jnp.int32, z.shape, 1))
    for p, h in zip(parts, hb):
        z = lax.dot_general(wst_ref[...], h, _NT, preferred_element_type=F32) + bst_ref[...]
        smt_ref[0, :, p] = _gate_act(z, lax.broadcasted_iota(jnp.int32, z.shape, 0))


def _in_proj(x, mod, g, w, tm, tiles_per_group, token_minor_q):
    m, d = x.shape
    group_tokens = tm * tiles_per_group
    groups = m // group_tokens
    row = lambda n: pl.BlockSpec((tm, n), lambda i: (i, 0))
    col = lambda n: pl.BlockSpec((1, n, tm), lambda i: (i // tiles_per_group, 0, i % tiles_per_group))
    rows_out = lambda n, dt: (jax.ShapeDtypeStruct((m, n), dt), row(n))
    cols_out = lambda n, dt=F32: (jax.ShapeDtypeStruct((groups, n, group_tokens), dt), col(n))
    outs = [cols_out(A_WIDTH, BF16) if token_minor_q else rows_out(A_WIDTH, BF16),
            cols_out(A_WIDTH), cols_out(A_WIDTH), rows_out(A_WIDTH, BF16)]
    if not token_minor_q:
        outs.append(rows_out(A_WIDTH, BF16))
    outs += [rows_out(2 * B_WIDTH, F32), rows_out(B_WIDTH, BF16), rows_out(w["g"].shape[1], BF16),
             rows_out(LANES, F32), cols_out(GATE_ROWS)]
    weights = [w["q"], w["k"], w["v"], w["qk"], w["bv"], w["g"], w["s"], w["bs"], w["st"], w["bst"]]
    return pl.pallas_call(
        functools.partial(_in_kernel, token_minor_q=token_minor_q),
        out_shape=[o[0] for o in outs],
        grid=(m // tm,),
        in_specs=[row(d), _mod_spec(mod, SCALE_MIX, tm, tiles_per_group, m),
                  _mod_spec(mod, SHIFT_MIX, tm, tiles_per_group, m),
                  _resident(g.shape)] + [_resident(a.shape) for a in weights],
        out_specs=[o[1] for o in outs],
        compiler_params=_params("arbitrary"),
        name="in_proj",
    )(x, mod, mod, g, *weights)


MIX_CHUNK = 256


def _mix_kernel(ya_ref, hb_ref, og_ref, x_ref, gt_ref, sc_ref, sh_ref, g1_ref, g2_ref,
                wpa_ref, wpb_ref, wo_ref, x1_ref, h2_ref):
    tm, d = x_ref.shape
    parts = [slice(r, r + tm // TOKEN_PARTS) for r in range(0, tm, tm // TOKEN_PARTS)]
    mod = lambda ref, p: _mod_rows(ref, p, tm)
    merged = []
    for p in parts:
        yb_in = (og_ref[p, :B_WIDTH].astype(F32) * hb_ref[p, :].astype(F32)).astype(BF16)
        cols = []
        for c in range(0, d, MIX_CHUNK):
            a = jnp.dot(ya_ref[p, :], wpa_ref[:, c:c + MIX_CHUNK], preferred_element_type=F32)
            b = jnp.dot(yb_in, wpb_ref[:, c:c + MIX_CHUNK], preferred_element_type=F32)
            ga = og_ref[p, B_WIDTH + c:B_WIDTH + c + MIX_CHUNK].astype(F32)
            gb = og_ref[p, B_WIDTH + d + c:B_WIDTH + d + c + MIX_CHUNK].astype(F32)
            cols.append((ga * a + gb * b).astype(BF16))
        merged.append(jnp.concatenate(cols, axis=1))
    yo = [jnp.dot(y, wo_ref[...], preferred_element_type=F32) for y in merged]
    for p, o in zip(parts, yo):
        x1 = x_ref[p, :] + mod(gt_ref, p) * _rms(o, g1_ref[...])
        x1_ref[p, :] = x1
        h2_ref[p, :] = (_rms(x1, g2_ref[...]) * (1.0 + mod(sc_ref, p)) + mod(sh_ref, p)).astype(BF16)


def _mix_out(ya, hb, og, x, mod, g1, g2, wpa, wpb, wo, tm, tiles_per_group):
    m, d = x.shape
    row = lambda n: pl.BlockSpec((tm, n), lambda i: (i, 0))
    ms = lambda which: _mod_spec(mod, which, tm, tiles_per_group, m)
    return pl.pallas_call(
        _mix_kernel,
        out_shape=[jax.ShapeDtypeStruct((m, d), F32), jax.ShapeDtypeStruct((m, d), BF16)],
        grid=(m // tm,),
        in_specs=[row(ya.shape[1]), row(hb.shape[1]), row(og.shape[1]), row(d),
                  ms(GATE_MIX), ms(SCALE_MLP), ms(SHIFT_MLP),
                  _resident(g1.shape), _resident(g2.shape), _resident(wpa.shape), _resident(wpb.shape),
                  _resident(wo.shape)],
        out_specs=[row(d), row(d)],
        compiler_params=_params("arbitrary"),
        name="mix_out",
    )(ya, hb, og, x, mod, mod, mod, g1, g2, wpa, wpb, wo)


def _mlp_kernel(h2_ref, x1_ref, gt_ref, g_ref, wu_ref, wd_ref, y_ref, *, fc):
    tm = x1_ref.shape[0]
    parts = [slice(r, r + tm // TOKEN_PARTS) for r in range(0, tm, tm // TOKEN_PARTS)]
    mod = lambda ref, p: _mod_rows(ref, p, tm)
    acc = [jnp.zeros((tm // TOKEN_PARTS, x1_ref.shape[1]), F32) for _ in parts]
    for c in range(wu_ref.shape[1] // fc):
        u = [jnp.dot(h2_ref[p, :], wu_ref[:, c * fc:(c + 1) * fc], preferred_element_type=F32) for p in parts]
        u = [jnp.square(jnp.maximum(x, 0.0)).astype(BF16) for x in u]
        acc = [a + jnp.dot(x, wd_ref[c * fc:(c + 1) * fc, :], preferred_element_type=F32) for a, x in zip(acc, u)]
    for p, a in zip(parts, acc):
        y_ref[p, :] = x1_ref[p, :] + mod(gt_ref, p) * _rms(a, g_ref[...])


def _mlp(h2, x1, mod, g, wu, wd, tm, tiles_per_group, fc=MLP_CHUNK):
    m, d = x1.shape
    row = lambda n: pl.BlockSpec((tm, n), lambda i: (i, 0))
    return pl.pallas_call(
        functools.partial(_mlp_kernel, fc=fc),
        out_shape=jax.ShapeDtypeStruct((m, d), F32),
        grid=(m // tm,),
        in_specs=[row(d), row(d), _mod_spec(mod, GATE_MLP, tm, tiles_per_group, m), _resident(g.shape),
                  _resident(wu.shape), _resident(wd.shape)],
        out_specs=row(d),
        compiler_params=_params("arbitrary"),
        name="mlp",
    )(h2, x1, mod, g, wu, wd)


def _lane_scan(x, lane, shifts):
    for s in shifts:
        x = x + jnp.where(lane >= s, pltpu.roll(x, s, axis=1), 0.0)
    return x


def _cumsum_kernel(x_ref, o_ref):
    rows, length = x_ref.shape[1], x_ref.shape[2]
    lane = lax.broadcasted_iota(jnp.int32, (rows, LANES), 1)
    carry = jnp.zeros((rows, 1), F32)
    for c in range(length // LANES):
        x = _lane_scan(x_ref[0, :, c * LANES:(c + 1) * LANES], lane, (1, 2, 4, 8, 16, 32, 64)) + carry
        o_ref[0, :, c * LANES:(c + 1) * LANES] = x
        carry = x[:, LANES - 1:LANES]


def _cumsum_lanes(x):
    b, r, length = x.shape
    spec = pl.BlockSpec((1, r, length), lambda i: (i, 0, 0))
    return pl.pallas_call(
        _cumsum_kernel,
        out_shape=jax.ShapeDtypeStruct(x.shape, F32),
        grid=(b,),
        in_specs=[spec],
        out_specs=spec,
        compiler_params=_params("arbitrary"),
        name="logf_cumsum",
    )(x)


V_ROWS = 80


def _aug_lane0(hh):
    return A_HEAD_DIM if hh == 0 else 0


def _attn_kernel(qt_ref, k_ref, vt_ref, f_ref, o_ref, kaug_ref, vaug_ref, s_ref, *, blk_q, blk_k, chunk):
    pair = pl.program_id(1)
    qi = pl.program_id(2)
    length = k_ref.shape[1]
    lane = lax.broadcasted_iota(jnp.int32, (1, LANES), 1)
    own = (lane < A_HEAD_DIM, lane >= A_HEAD_DIM)
    feat = lax.broadcasted_iota(jnp.int32, (LANES, 1), 0)

    @pl.when(qi == 0)
    def _():
        ones_row = jnp.where(lax.broadcasted_iota(jnp.int32, (V_ROWS - A_HEAD_DIM, chunk), 0) == 0, 1.0, 0.0)

        def build(c, _):
            toks = pl.ds(pl.multiple_of(c * chunk, chunk), chunk)
            k = k_ref[0, toks, :]
            vt = vt_ref[0, :, toks]
            for hh in range(2):
                a0 = _aug_lane0(hh)
                neg_f = f_ref[0, pl.ds(2 * pair + hh, 1), toks] * (-LOG2E)
                hi = neg_f.astype(BF16).astype(F32)
                mid = (neg_f - hi).astype(BF16).astype(F32)
                lo = (neg_f - hi - mid).astype(BF16).astype(F32)
                bias = jnp.concatenate(
                    [jnp.where(feat == a0, hi[:, t], jnp.where(feat == a0 + 1, mid[:, t],
                                                               jnp.where(feat == a0 + 2, lo[:, t], 0.0))).T
                     for t in (slice(t0, t0 + LANES) for t0 in range(0, chunk, LANES))], axis=0)
                kaug_ref[hh, toks, :] = jnp.where(own[hh], k, bias.astype(BF16))
                dims = vt[hh * A_HEAD_DIM:(hh + 1) * A_HEAD_DIM]
                vaug_ref[hh, :, toks] = jnp.concatenate([dims, ones_row], axis=0).astype(BF16)
            return 0
        lax.fori_loop(0, length // chunk, build, 0)

    qt = qt_ref[0]
    qh = []
    for hh in range(2):
        a0 = _aug_lane0(hh)
        ones = jnp.where((feat >= a0) & (feat < a0 + 3), 1.0, 0.0).astype(BF16)
        qh.append(jnp.where((feat < A_HEAD_DIM) if hh == 0 else (feat >= A_HEAD_DIM), qt, ones))

    def keys(j):
        return pl.ds(pl.multiple_of(j * blk_k, blk_k), blk_k)

    def scores(j, slot, c0=0):
        for hh in range(2):
            s_ref[slot, hh, :, c0:] = jnp.dot(kaug_ref[hh, keys(j), :], qh[hh][:, c0:], preferred_element_type=F32)

    def fold(j, slot, carry, causal, c0=0):
        probs, stats = [], []
        for hh in range(2):
            m = carry[hh][0]
            sh = s_ref[slot, hh, :, c0:]
            if causal:
                key = lax.broadcasted_iota(jnp.int32, sh.shape, 0)
                qry = lax.broadcasted_iota(jnp.int32, sh.shape, 1)
                sh = jnp.where(key <= qry, sh, NEG_INF)
            m_new = jnp.maximum(m, jnp.max(sh, axis=0, keepdims=True))
            probs.append(jnp.exp2(sh - m_new).astype(BF16))
            stats.append((m_new, jnp.exp2(m - m_new)))
        return tuple(
            (stats[hh][0], stats[hh][1] * carry[hh][1]
             + jnp.dot(vaug_ref[hh, :, keys(j)], probs[hh], preferred_element_type=F32))
            for hh in range(2))

    assert blk_q == 2 * blk_k
    init = tuple((jnp.full((1, blk_q), NEG_INF, F32), jnp.zeros((V_ROWS, blk_q), F32)) for _ in range(2))

    def body(p, carry):
        scores(2 * p + 1, 1)
        carry = fold(2 * p, 0, carry, False)
        scores(2 * p + 2, 0)
        return fold(2 * p + 1, 1, carry, False)

    scores(0, 0)
    carry = lax.fori_loop(0, qi, body, init)
    scores(2 * qi + 1, 1, blk_k)
    carry = fold(2 * qi, 0, carry, True)
    upper = fold(2 * qi + 1, 1, tuple((m[:, blk_k:], acc[:, blk_k:]) for m, acc in carry), True, blk_k)
    out = []
    for hh in range(2):
        acc = jnp.concatenate([carry[hh][1][:, :blk_k], upper[hh][1]], axis=1)
        out.append(acc[:A_HEAD_DIM] / acc[A_HEAD_DIM:A_HEAD_DIM + 1])
    o_ref[0] = jnp.concatenate(out, axis=0).T.astype(BF16)


def _attention(qt, k, vt, f, blk_q=ATTN_Q_BLOCK, blk_k=ATTN_K_BLOCK):
    b, length, width = k.shape
    pairs = width // LANES
    blk_q, blk_k = min(blk_q, length), min(blk_k, length)
    return pl.pallas_call(
        functools.partial(_attn_kernel, blk_q=blk_q, blk_k=blk_k, chunk=min(ATTN_BUILD_CHUNK, length)),
        out_shape=jax.ShapeDtypeStruct(k.shape, BF16),
        grid=(b, pairs, length // blk_q),
        in_specs=[pl.BlockSpec((1, LANES, blk_q), lambda i, p, t: (i, p, t)),
                  pl.BlockSpec((1, length, LANES), lambda i, p, t: (i, 0, p)),
                  pl.BlockSpec((1, LANES, length), lambda i, p, t: (i, p, 0)),
                  pl.BlockSpec((1,) + f.shape[1:], lambda i, p, t: (i, 0, 0))],
        out_specs=pl.BlockSpec((1, blk_q, LANES), lambda i, p, t: (i, t, p)),
        scratch_shapes=[pltpu.VMEM((2, length, LANES), BF16), pltpu.VMEM((2, V_ROWS, length), BF16),
                        pltpu.VMEM((2, 2, blk_k, blk_q), F32)],
        compiler_params=_params("arbitrary", "arbitrary", "arbitrary"),
        name="fox_attention",
    )(qt, k, vt, f)


def _conv_silu(xp_ref, w_ref, b_ref, tc):
    y = b_ref[...]
    for j in range(CONV_W):
        off = SUBLANES - (CONV_W - 1) + j
        y = y + w_ref[j:j + 1, :] * xp_ref[off:off + tc, :]
    lane = lax.broadcasted_iota(jnp.int32, (1, y.shape[1]), 1)
    return y * jax.nn.sigmoid(y) * jnp.where(lane < B_WIDTH, 1.0, B_HEAD_DIM ** -0.5)


def _conv_kernel(x_ref, prev_ref, init_ref, w_ref, b_ref, o_ref, xp_ref, *, tc):
    for s in range(x_ref.shape[0]):
        xp_ref[s, 0:SUBLANES, :] = jnp.where(pl.program_id(1) == 0, init_ref[s], prev_ref[s])
        xp_ref[s, SUBLANES:SUBLANES + tc, :] = x_ref[s]
        o_ref[s] = _conv_silu(xp_ref.at[s], w_ref, b_ref, tc).astype(BF16)


def _conv(x, init, w, b, tc, nb=1):
    bsz, length, c = x.shape
    tpb = tc // SUBLANES
    return pl.pallas_call(
        functools.partial(_conv_kernel, tc=tc),
        out_shape=jax.ShapeDtypeStruct(x.shape, BF16),
        grid=(bsz // nb, length // tc),
        in_specs=[pl.BlockSpec((nb, tc, c), lambda i, t: (i, t, 0)),
                  pl.BlockSpec((nb, SUBLANES, c), lambda i, t: (i, jnp.maximum(t * tpb - 1, 0), 0)),
                  pl.BlockSpec((nb, SUBLANES, c), lambda i, t: (i, 0, 0)),
                  pl.BlockSpec((CONV_W, c), lambda i, t: (0, 0)),
                  pl.BlockSpec((1, c), lambda i, t: (0, 0))],
        out_specs=pl.BlockSpec((nb, tc, c), lambda i, t: (i, t, 0)),
        scratch_shapes=[pltpu.VMEM((nb, tc + SUBLANES, c), F32)],
        compiler_params=_params("arbitrary", "arbitrary"),
        name="short_conv",
    )(x, x, init, w, b)


_BNT = (((2,), (2,)), ((0,), (0,)))
_BNN = (((2,), (1,)), ((0,), (0,)))
_BTN = (((1,), (1,)), ((0,), (0,)))


def _mlstm_heads(q, k, v, ir, fr, ic, fc, c_st, n_st, m_st):
    n, t, d = q.shape
    row = lax.broadcasted_iota(jnp.int32, (n, t, t), 1)
    col = lax.broadcasted_iota(jnp.int32, (n, t, t), 2)
    tril = col <= row
    spread = lambda x, w: jnp.broadcast_to(x, (n, t, w))
    both = lambda x: (spread(x, t),) * 2 if t == d else (spread(x, t), spread(x, d))
    bcum_c = jnp.sum(jnp.where(tril, fr, 0.0), axis=2, keepdims=True)
    bcum_r = jnp.sum(jnp.where(row <= col, fc, 0.0), axis=1, keepdims=True)
    bcum_t, bcum_d = both(bcum_c)
    dlog = jnp.where(tril, bcum_t - bcum_r + ir, NEG_INF)
    mt_t, mt_d = both(jnp.maximum(bcum_c + m_st, jnp.max(dlog, axis=2, keepdims=True)))
    a = jnp.exp(dlog - mt_t) * lax.dot_general(q, k, _BNT, preferred_element_type=F32)
    si = jnp.exp(bcum_d + m_st - mt_d)
    num = (lax.dot_general(a.astype(BF16), v, _BNN, preferred_element_type=F32)
           + si * lax.dot_general(q, c_st.astype(BF16), _BNN, preferred_element_type=F32))
    den = jnp.sum(a, axis=2, keepdims=True) + si * jnp.sum(q.astype(F32) * n_st, axis=2, keepdims=True)
    h = num / jnp.maximum(jnp.abs(den), jnp.exp(-mt_d))
    bl = bcum_r[:, :, t - 1:t]
    m_new = jnp.maximum(bl + m_st, jnp.max(bl - bcum_r + ir, axis=2, keepdims=True))
    kw = jnp.exp(bl - bcum_d + spread(ic, d) - m_new) * k.astype(F32)
    decay = jnp.exp(bl + m_st - m_new)
    c_new = decay * c_st + lax.dot_general(kw.astype(BF16), v, _BTN, preferred_element_type=F32)
    n_new = decay * n_st + jnp.sum(kw, axis=1, keepdims=True)
    return h, c_new, n_new, m_new


def _mlstm_kernel(q_ref, k_ref, v_ref, gc_ref, gr_ref, c0_ref, n0_ref, m0_ref,
                  h_ref, c_ref, n_ref, m_ref, *, g0):
    @pl.when(pl.program_id(1) == 0)
    def _():
        c_ref[...] = c0_ref[...]
        n_ref[...] = n0_ref[...]
        m_ref[...] = m0_ref[...]

    nb = q_ref.shape[0]
    where = [(s, hd, slice(hd * B_HEAD_DIM, (hd + 1) * B_HEAD_DIM)) for s in range(nb) for hd in range(B_HEADS)]
    heads = lambda ref: jnp.stack([ref[s, :, sl] for s, _, sl in where])
    gc = [gc_ref[s] for s in range(nb)]
    gr = [gr_ref[s] for s in range(nb)]
    gi, gf = g0, g0 + B_HEADS
    h, c_new, n_new, m_new = _mlstm_heads(
        heads(q_ref), heads(k_ref), heads(v_ref),
        jnp.stack([gr[s][gi + hd:gi + hd + 1, :] for s, hd, _ in where]),
        jnp.stack([gr[s][gf + hd:gf + hd + 1, :] for s, hd, _ in where]),
        jnp.stack([gc[s][:, gi + hd:gi + hd + 1] for s, hd, _ in where]),
        jnp.stack([gc[s][:, gf + hd:gf + hd + 1] for s, hd, _ in where]),
        c_ref[...].reshape((nb * B_HEADS,) + c_ref.shape[2:]),
        n_ref[...].reshape((nb * B_HEADS,) + n_ref.shape[2:]),
        m_ref[...].reshape((nb * B_HEADS,) + m_ref.shape[2:]))
    for i, (s, _, sl) in enumerate(where):
        h_ref[s, :, sl] = h[i].astype(BF16)
    c_ref[...] = c_new.reshape(c_ref.shape)
    n_ref[...] = n_new.reshape(n_ref.shape)
    m_ref[...] = m_new.reshape(m_ref.shape)


def _mlstm(qk, v, gcol, grow, c0, n0, m0, t, nb=1, g0=0):
    b, length, _ = v.shape
    st = lambda a: pl.BlockSpec((nb,) + a.shape[1:], lambda i, j: (i, 0, 0, 0))
    return pl.pallas_call(
        functools.partial(_mlstm_kernel, g0=g0),
        out_shape=[jax.ShapeDtypeStruct(v.shape, BF16), jax.ShapeDtypeStruct(c0.shape, F32),
                   jax.ShapeDtypeStruct(n0.shape, F32), jax.ShapeDtypeStruct(m0.shape, F32)],
        grid=(b // nb, length // t),
        in_specs=[pl.BlockSpec((nb, t, B_WIDTH), lambda i, j: (i, j, 0)),
                  pl.BlockSpec((nb, t, B_WIDTH), lambda i, j: (i, j, 1)),
                  pl.BlockSpec((nb, t, B_WIDTH), lambda i, j: (i, j, 0)),
                  pl.BlockSpec((nb, t, gcol.shape[2]), lambda i, j: (i, j, 0)),
                  pl.BlockSpec((nb, grow.shape[1], t), lambda i, j: (i, 0, j)),
                  st(c0), st(n0), st(m0)],
        out_specs=[pl.BlockSpec((nb, t, B_WIDTH), lambda i, j: (i, j, 0)), st(c0), st(n0), st(m0)],
        compiler_params=_params("arbitrary", "arbitrary"),
        name="mlstm",
    )(qk, qk, v, gcol, grow, c0, n0, m0)


def _page_copies(pt_ref, step, slot, ck_ref, cv_ref, kbuf_ref, vbuf_ref, sem_ref, nb, n_pages):
    copies = []
    for s in range(nb):
        for j in range(n_pages):
            page = pt_ref[step * nb + s, j]
            at = s * n_pages + j
            copies.append(pltpu.make_async_copy(ck_ref.at[page], kbuf_ref.at[slot, at], sem_ref.at[slot, 0]))
            copies.append(pltpu.make_async_copy(cv_ref.at[page], vbuf_ref.at[slot, at], sem_ref.at[slot, 1]))
    return copies


def _sattn_kernel(pt_ref, q_ref, kn_ref, vn_ref, lfn_ref, lf_ref, ck_ref, cv_ref, o_ref,
                  kbuf_ref, vbuf_ref, sem_ref, *, n_pages):
    nb, nrow, width = q_ref.shape
    step, n_steps = pl.program_id(0), pl.num_programs(0)
    slot = step % 2
    fetch = functools.partial(_page_copies, pt_ref, ck_ref=ck_ref, cv_ref=cv_ref, kbuf_ref=kbuf_ref,
                              vbuf_ref=vbuf_ref, sem_ref=sem_ref, nb=nb, n_pages=n_pages)

    def start_all(copies):
        for n, c in enumerate(copies):
            c.start(priority=n % 2)

    @pl.when(step == 0)
    def _():
        start_all(fetch(0, 0))

    @pl.when(step + 1 < n_steps)
    def _():
        start_all(fetch(step + 1, 1 - slot))

    for c in fetch(step, slot):
        c.wait()
    k_refs = [[kbuf_ref.at[slot, s * n_pages + j] for j in range(n_pages)] for s in range(nb)]
    v_refs = [[vbuf_ref.at[slot, s * n_pages + j] for j in range(n_pages)] for s in range(nb)]
    first = step * nb
    nq = nrow // A_HEADS
    n_new = kn_ref.shape[1]
    lane = lax.broadcasted_iota(jnp.int32, (A_HEADS, LANES), 1)
    per_row = lambda f: jnp.concatenate([f] * nq, axis=0)
    scan = lambda x: _lane_scan(x, lane, (1, 2, 4, 8, 16, 32, 64))

    local = [[scan(lf_ref[pt_ref[first + s, j]]) for j in range(n_pages)] for s in range(nb)]
    new_local = [scan(lfn_ref[s]) for s in range(nb)]
    f_keys, f_new = [], []
    for s in range(nb):
        offset = jnp.zeros((A_HEADS, 1), F32)
        f_keys.append([])
        for j in range(n_pages):
            f = local[s][j] + offset
            offset = f[:, LANES - 1:LANES]
            f_keys[s].append(f)
        f_new.append((new_local[s] + offset)[:, :n_new])

    scores = [[jnp.dot(q_ref[s], k_refs[s][j][...].reshape(width, PAGE_SIZE).astype(BF16),
                       preferred_element_type=F32) - per_row(f_keys[s][j] * LOG2E)
               for j in range(n_pages)] for s in range(nb)]
    q_tok = lax.broadcasted_iota(jnp.int32, (nrow, n_new), 0) // A_HEADS
    k_tok = lax.broadcasted_iota(jnp.int32, (nrow, n_new), 1)
    s_new = [jnp.where(k_tok <= q_tok,
                       lax.dot_general(q_ref[s], kn_ref[s], _NT, preferred_element_type=F32)
                       - per_row(f_new[s] * LOG2E), NEG_INF) for s in range(nb)]

    m = [jnp.maximum(jnp.max(functools.reduce(jnp.maximum, scores[s]), axis=1, keepdims=True),
                     jnp.max(s_new[s], axis=1, keepdims=True)) for s in range(nb)]
    probs = [[jnp.exp2(x - m[s]) for x in scores[s]] for s in range(nb)]
    p_new = [jnp.exp2(s_new[s] - m[s]) for s in range(nb)]
    l = [jnp.sum(functools.reduce(jnp.add, probs[s]), axis=1, keepdims=True)
         + jnp.sum(p_new[s], axis=1, keepdims=True) for s in range(nb)]
    acc = [jnp.dot(p_new[s].astype(BF16), vn_ref[s], preferred_element_type=F32) for s in range(nb)]
    for s in range(nb):
        for j in range(n_pages):
            vt = v_refs[s][j][...].reshape(width, PAGE_SIZE).astype(BF16)
            acc[s] = acc[s] + lax.dot_general(probs[s][j].astype(BF16), vt, _NT, preferred_element_type=F32)
    row_head = lax.broadcasted_iota(jnp.int32, (nrow, width), 0) % A_HEADS
    col_head = lax.broadcasted_iota(jnp.int32, (nrow, width), 1) // A_HEAD_DIM
    for s in range(nb):
        out = jnp.where(row_head == col_head, acc[s] / l[s], 0.0)
        o_ref[s] = jnp.sum(out.reshape(nq, A_HEADS, width), axis=1)


def _sample_attention(page_table, q, kn, vn, lfn, cache_kt, cache_vt, cache_lf, nb=PAGED_SEQS):
    b, n_pages = page_table.shape
    nq = q.shape[1] // A_HEADS
    per_seq = lambda a: pl.BlockSpec((nb,) + a.shape[1:], lambda i, pt: (i, 0, 0))
    in_hbm = pl.BlockSpec(memory_space=pl.ANY)
    page_buffers = pltpu.VMEM((2, nb * n_pages) + cache_kt.shape[1:], F32)
    grid_spec = pltpu.PrefetchScalarGridSpec(
        num_scalar_prefetch=1,
        grid=(b // nb,),
        in_specs=[per_seq(q), per_seq(kn), per_seq(vn), per_seq(lfn), _resident(cache_lf.shape), in_hbm, in_hbm],
        out_specs=pl.BlockSpec((nb, nq, A_WIDTH), lambda i, pt: (i, 0, 0)),
        scratch_shapes=[page_buffers, page_buffers, pltpu.SemaphoreType.DMA((2, 2))],
    )
    return pl.pallas_call(
        functools.partial(_sattn_kernel, n_pages=n_pages),
        out_shape=jax.ShapeDtypeStruct((b, nq, A_WIDTH), F32),
        grid_spec=grid_spec,
        compiler_params=_params("arbitrary"),
        name="paged_fox_attention",
    )(page_table, q, kn, vn, lfn, cache_lf, cache_kt, cache_vt)


def _split_weights(w_in, b_fox_f, b_ml_i, b_ml_f):
    d = w_in.shape[0]
    o = 0
    parts = {}
    for name, n in (("aq", A_WIDTH), ("ak", A_WIDTH), ("av", A_WIDTH), ("af", A_HEADS), ("bq", B_WIDTH),
                    ("bk", B_WIDTH), ("bv", B_WIDTH), ("bi", B_HEADS), ("bf", B_HEADS), ("bo", B_WIDTH),
                    ("ga", d), ("gb", d)):
        parts[name] = w_in[:, o:o + n]
        o += n
    assert o == w_in.shape[1]
    small = jnp.concatenate([parts["af"], parts["bi"], parts["bf"]], axis=1)
    bias = jnp.concatenate([b_fox_f, b_ml_i, b_ml_f])
    bf = lambda a: a.astype(BF16)
    return {
        "q": bf(parts["aq"] * (A_HEAD_DIM ** -0.5 * LOG2E)),
        "k": bf(parts["ak"]),
        "v": bf(parts["av"]),
        "qk": bf(jnp.concatenate([parts["bq"], parts["bk"]], axis=1)),
        "bv": bf(parts["bv"]),
        "g": bf(jnp.concatenate([parts["bo"], parts["ga"], parts["gb"]], axis=1)),
        "s": bf(jnp.pad(small, ((0, 0), (0, LANES - N_GATES)))),
        "bs": jnp.pad(bias, (0, LANES - N_GATES)).reshape(1, LANES),
        "st": bf(jnp.pad(small.T, ((0, GATE_ROWS - N_GATES), (0, 0)))),
        "bst": jnp.pad(bias, (0, GATE_ROWS - N_GATES)).reshape(GATE_ROWS, 1),
    }


def kernel(x_prompt, x_sample, cache_k, cache_v, cache_logf, page_table, state_C, state_n, state_m, state_conv,
           c_prompt, c_sample, w_ada, b_ada, g_pre_mix, g_post_mix, g_pre_mlp, g_post_mlp, w_in, b_fox_f,
           b_ml_i, b_ml_f, conv_w, conv_b, w_proj_a, w_proj_b, w_out, w_up, w_down):
    assert w_in.shape[0] == 1, "one trunk layer"
    bsz, seq, d = x_prompt.shape
    dbsz, dseq, _ = x_sample.shape

    w_proj = _split_weights(w_in[0], b_fox_f[0], b_ml_i[0], b_ml_f[0])
    wpa, wpb, wo = w_proj_a[0].astype(BF16), w_proj_b[0].astype(BF16), w_out[0].astype(BF16)
    wu, wd = w_up[0].astype(BF16), w_down[0].astype(BF16)
    g1, g2, g3, g4 = g_pre_mix, g_post_mix, g_pre_mlp, g_post_mlp
    cw, cb = conv_w[0], conv_b

    n_c = bsz + dbsz
    c_all = jnp.concatenate([c_prompt, c_sample, jnp.zeros((-n_c % SUBLANES, d), F32)], axis=0)
    mod = _adaln(c_all, w_ada[0], b_ada)

    def token_path(x2, mods, tm, tiles_per_group, token_minor_q):
        return _in_proj(x2, mods, g1, w_proj, tm, tiles_per_group, token_minor_q), mods

    def head_split(kt):
        g_, _, n_ = kt.shape
        return jnp.transpose(kt.reshape(g_, A_HEADS, A_HEAD_DIM, n_), (0, 3, 1, 2))[None]

    def finish(x2, ya, hb, og, mods, tm, group_tokens):
        x1, h2 = _mix_out(ya, hb, og, x2, mods, g2, g3, wpa, wpb, wo, tm, group_tokens // tm)
        return _mlp(h2, x1, mods, g4, wu, wd, tm, group_tokens // tm)

    tm = min(TOKEN_TILE, seq)
    m_p = bsz * seq
    x2 = x_prompt.reshape(m_p, d)
    mods = mod[:bsz].reshape(bsz, 1, N_MOD * d)
    (qt, kt, vt, ka, qkb, vb, og, sm, smt), rest = token_path(x2, mods, tm, seq // tm, True)

    r3 = lambda a: a.reshape(bsz, seq, a.shape[-1])
    ya = _attention(qt, r3(ka), vt, _cumsum_lanes(smt)).reshape(m_p, A_WIDTH)

    qkc = _conv(r3(qkb), jnp.zeros((bsz, SUBLANES, 2 * B_WIDTH), F32), cw, cb, tc=min(CONV_TILE, seq))
    hb, c_p, n_p, m_pr = _mlstm(qkc, r3(vb), r3(sm), smt,
                                jnp.zeros((bsz, B_HEADS, B_HEAD_DIM, B_HEAD_DIM), F32),
                                jnp.zeros((bsz, B_HEADS, 1, B_HEAD_DIM), F32),
                                jnp.zeros((bsz, B_HEADS, 1, 1), F32), t=min(MLSTM_CHUNK, seq), nb=min(MLSTM_SEQS, bsz), g0=A_HEADS)
    y_prompt = finish(x2, ya, hb.reshape(m_p, B_WIDTH), og, rest, min(2 * tm, seq), seq).reshape(bsz, seq, d)

    new_k_prompt = head_split(kt)
    new_v_prompt = head_split(vt)
    new_logf_prompt = jnp.transpose(smt[:, :A_HEADS, :], (0, 2, 1))[None]
    new_conv_prompt = r3(qkb)[:, seq - (CONV_W - 1):, :][None]

    m_s = dbsz * dseq
    xs2 = x_sample.reshape(m_s, d)
    mods = mod[bsz:n_c].reshape(1, dbsz, N_MOD * d)
    (qa, kt, vt, ka, va, qkb, vb, og, sm, smt), rest = token_path(xs2, mods, m_s, 1, False)

    n_new = SAMPLE_PAD
    new_rows = lambda a: jnp.pad(a.reshape(dbsz, dseq, -1), ((0, 0), (0, n_new - dseq), (0, 0)))
    lf_s = sm[:, :A_HEADS].reshape(dbsz, dseq, A_HEADS)
    lf_new = jnp.pad(jnp.transpose(lf_s, (0, 2, 1)), ((0, 0), (0, 0), (0, LANES - dseq)))
    q_heads = qa.reshape(dbsz, dseq, A_HEADS, 1, A_HEAD_DIM)
    q_bd = (q_heads * jnp.eye(A_HEADS, dtype=BF16)[None, None, :, :, None]).reshape(dbsz, dseq * A_HEADS, A_WIDTH)
    ya = _sample_attention(
        page_table, q_bd, new_rows(ka), new_rows(va), lf_new,
        jnp.transpose(cache_k[0], (0, 2, 3, 1)), jnp.transpose(cache_v[0], (0, 2, 3, 1)),
        jnp.transpose(cache_logf[0], (0, 2, 1)))
    ya = ya.astype(BF16).reshape(m_s, A_WIDTH)

    t_s = SAMPLE_PAD
    conv_in = jnp.pad(qkb.reshape(dbsz, dseq, 2 * B_WIDTH), ((0, 0), (0, SUBLANES - dseq), (0, 0)))
    conv_init = jnp.pad(state_conv[0], ((0, 0), (SUBLANES - (CONV_W - 1), 0), (0, 0)))
    qkc = _conv(conv_in, conv_init, cw, cb, tc=SUBLANES, nb=min(SAMPLE_SEQS, dbsz))
    tok_valid = (jnp.arange(t_s) < dseq)[None, :, None]
    pad_t = lambda a: jnp.pad(a, ((0, 0), (0, t_s - a.shape[1]), (0, 0)))
    qkc = jnp.where(tok_valid, pad_t(qkc), jnp.zeros((), BF16))
    vb_s = pad_t(vb.reshape(dbsz, dseq, B_WIDTH))
    gates = pad_t(sm[:, A_HEADS:A_HEADS + 2 * B_HEADS].reshape(dbsz, dseq, 2 * B_HEADS))
    neutral = jnp.concatenate([jnp.full((B_HEADS,), NEG_INF, F32), jnp.zeros((B_HEADS,), F32)])
    gcol = jnp.where(tok_valid, gates, neutral)
    grow = jnp.transpose(gcol, (0, 2, 1))
    hb, c_s, n_s, m_sm = _mlstm(qkc, vb_s, gcol, grow, state_C[0], state_n[0][:, :, None, :],
                                state_m[0][:, :, None, None], t=t_s, nb=min(SAMPLE_SEQS, dbsz))
    hb = hb[:, :dseq, :].reshape(m_s, B_WIDTH)
    y_sample = finish(xs2, ya, hb, og, rest, m_s, m_s).reshape(dbsz, dseq, d)

    new_k_sample = head_split(kt).reshape(1, dbsz, dseq, A_HEADS, A_HEAD_DIM)
    new_v_sample = head_split(vt).reshape(1, dbsz, dseq, A_HEADS, A_HEAD_DIM)
    new_logf_sample = lf_s[None]
    qkb3 = qkb.reshape(dbsz, dseq, 2 * B_WIDTH)
    new_conv_sample = jnp.concatenate([state_conv[0], qkb3], axis=1)[:, dseq:, :][None]

    return (y_prompt, y_sample, new_k_prompt, new_v_prompt, new_logf_prompt,
            c_p[None], n_p[:, :, 0, :][None], m_pr[:, :, 0, 0][None], new_conv_prompt,
            new_k_sample, new_v_sample, new_logf_sample,
            c_s[None], n_s[:, :, 0, :][None], m_sm[:, :, 0, 0][None], new_conv_sample)
```
